```python
import math
import jax, jax.numpy as jnp
from jax import lax
import numpy as np

D_MODEL = 2048
BATCH = 4
SEQ = 2048
DEPTH = 4

GRID_W = 64
CTX_LEN = 256
N_MIXERS = 4
CHUNK = 64
CONV_W = 3
D_FF = 4 * D_MODEL
NORM_EPS = 1e-6

SSD_D_INNER = 2 * D_MODEL
SSD_HEAD_DIM = 64
SSD_HEADS = SSD_D_INNER // SSD_HEAD_DIM
SSD_GROUPS = 8
SSD_STATE = 128
SSD_CONV_CH = SSD_D_INNER + 2 * SSD_GROUPS * SSD_STATE
SSD_IN = SSD_D_INNER + SSD_CONV_CH + 2 * SSD_HEADS

RET_HEADS = 8
RET_QK_DIM = D_MODEL // RET_HEADS
RET_V_DIM = 2 * RET_QK_DIM
RET_DV = RET_HEADS * RET_V_DIM
RET_IN = 2 * D_MODEL + 2 * RET_DV
ROPE_BASE = 10000.0

HGRN_EXPAND = 128
HGRN_HEADS = D_MODEL // HGRN_EXPAND
HGRN_IN = 5 * D_MODEL

GDN_HEAD_DIM = 128
GDN_K_HEADS = D_MODEL // GDN_HEAD_DIM
GDN_V_HEADS = 2 * GDN_K_HEADS
GDN_DK = GDN_K_HEADS * GDN_HEAD_DIM
GDN_DV = GDN_V_HEADS * GDN_HEAD_DIM
GDN_CONV_CH = 2 * GDN_DK + GDN_DV
GDN_IN = GDN_CONV_CH + GDN_DV + 4 * GDN_V_HEADS

kernel_name = 'hybrid_bidir_recurrent_dit_block'


def _rmsnorm(x, g, eps=NORM_EPS):
    x32 = x.astype(jnp.float32)
    y = x32 * lax.rsqrt(jnp.mean(x32 * x32, axis=-1, keepdims=True) + eps)
    return y.astype(x.dtype) * g


def _adaln(x, g, shift, scale):
    return _rmsnorm(x, g) * (1 + scale) + shift


def _l2norm(x, eps=1e-6):
    x32 = x.astype(jnp.float32)
    return x32 * lax.rsqrt(jnp.sum(x32 * x32, axis=-1, keepdims=True) + eps)


def _sq_relu_mlp(h, w1, w2):
    return jnp.square(jax.nn.relu(h @ w1)) @ w2


def _dwconv(u, w):
    return lax.conv_general_dilated(u, w[:, None, :], window_strides=(1,),
                                    padding=[(CONV_W // 2, CONV_W // 2)],
                                    dimension_numbers=('NWC', 'WIO', 'NWC'),
                                    feature_group_count=u.shape[-1])


def _conv_split(u, w, lc):
    return jnp.concatenate([_dwconv(u[:, :lc], w), _dwconv(u[:, lc:], w)], axis=1)


def _rev(t, lc):
    return jnp.concatenate([jnp.flip(t[:, :lc], 1), jnp.flip(t[:, lc:], 1)], axis=1)


def _split_out(out, lc, keep_ctx):
    if keep_ctx:
        return out[:, :lc], out[:, lc:]
    return None, out


def _to_chunks(t):
    b, n = t.shape[:2]
    return jnp.moveaxis(t.reshape((b, n // CHUNK, CHUNK) + t.shape[2:]), 1, 0)


def _from_chunks(t):
    nc, b, q = t.shape[:3]
    return jnp.moveaxis(t, 0, 1).reshape((b, nc * q) + t.shape[3:])


def _chunk_masks():
    idx = jnp.arange(CHUNK)
    return idx[:, None] >= idx[None, :], idx[:, None] > idx[None, :]


def _scalar_decay_scan(q, k, v, log_a):
    f32 = jnp.float32
    q, k, v, log_a = (t.astype(f32) for t in (q, k, v, log_a))
    bsz, _, g, n = q.shape
    r, p = v.shape[-2:]
    incl, _ = _chunk_masks()

    def body(s, xs):
        qc, kc, vc, la = xs
        cum = jnp.cumsum(la, axis=1)
        cum_t = jnp.moveaxis(cum, 1, -1)
        seg = cum_t[..., :, None] - cum_t[..., None, :]
        scores = jnp.einsum('btgn,bsgn->bgts', qc, kc)
        attn = scores[:, :, None] * jnp.exp(jnp.where(incl, seg, -jnp.inf))
        y = jnp.einsum('bgrts,bsgrp->btgrp', attn, vc)
        y = y + jnp.einsum('btgn,bgrnp->btgrp', qc, s) * jnp.exp(cum)[..., None]
        to_end = jnp.exp(cum[:, -1:] - cum)
        s = jnp.exp(cum[:, -1])[..., None, None] * s + jnp.einsum('bsgn,bsgr,bsgrp->bgrnp', kc, to_end, vc)
        return s, y

    s0 = jnp.zeros((bsz, g, r, n, p), f32)
    _, y = lax.scan(body, s0, tuple(_to_chunks(t) for t in (q, k, v, log_a)))
    return _from_chunks(y)


def _vector_decay_scan(q, k, v, log_f):
    f32 = jnp.float32
    q, k, v, log_f = (t.astype(f32) for t in (q, k, v, log_f))
    bsz, _, h, kd = q.shape
    vd = v.shape[-1]
    incl, _ = _chunk_masks()

    def body(s, xs):
        qc, kc, vc, lf = xs
        cum = jnp.cumsum(lf, axis=1)
        seg = cum[:, :, None] - cum[:, None, :]
        decay = jnp.exp(jnp.where(incl[:, :, None, None], seg, -jnp.inf))
        attn = jnp.einsum('bthk,bshk,btshk->bhts', qc, kc, decay)
        y = jnp.einsum('bhts,bshv->bthv', attn, vc)
        y = y + jnp.einsum('bthk,bhkv->bthv', qc * jnp.exp(cum), s)
        s = jnp.exp(cum[:, -1])[..., None] * s + jnp.einsum('bshk,bshv->bhkv', kc * jnp.exp(cum[:, -1:] - cum), vc)
        return s, y

    s0 = jnp.zeros((bsz, h, kd, vd), f32)
    _, y = lax.scan(body, s0, tuple(_to_chunks(t) for t in (q, k, v, log_f)))
    return _from_chunks(y)


def _delta_scan(q, k, v, beta, log_a):
    f32 = jnp.float32
    q, k, v, beta, log_a = (t.astype(f32) for t in (q, k, v, beta, log_a))
    bsz, _, g, kd = q.shape
    r, vd = v.shape[-2:]
    incl, strict = _chunk_masks()

    def body(s, xs):
        qc, kc, vc, bc, la = xs
        cum = jnp.cumsum(la, axis=1)
        cum_t = jnp.moveaxis(cum, 1, -1)
        seg = cum_t[..., :, None] - cum_t[..., None, :]
        beta_t = jnp.moveaxis(bc, 1, -1)
        kk = jnp.einsum('btgk,bsgk->bgts', kc, kc)
        lower = beta_t[..., :, None] * kk[:, :, None] * jnp.exp(jnp.where(strict, seg, -jnp.inf))
        rhs_v = jnp.moveaxis(vc * bc[..., None], 1, 3)
        rhs_k = jnp.moveaxis(kc[:, :, :, None, :] * (bc * jnp.exp(cum))[..., None], 1, 3)
        sol = lax.linalg.triangular_solve(lower, jnp.concatenate([rhs_v, rhs_k], axis=-1),
                                          left_side=True, lower=True, unit_diagonal=True)
        u, w = sol[..., :vd], sol[..., vd:]
        v_new = u - jnp.einsum('bgrtk,bgrkv->bgrtv', w, s)
        qk = jnp.einsum('btgk,bsgk->bgts', qc, kc)
        attn = qk[:, :, None] * jnp.exp(jnp.where(incl, seg, -jnp.inf))
        y = jnp.einsum('bgrts,bgrsv->btgrv', attn, v_new)
        y = y + jnp.einsum('btgk,bgrkv->btgrv', qc, s) * jnp.exp(cum)[..., None]
        to_end = jnp.exp(cum_t[..., -1:] - cum_t)
        s = jnp.exp(cum_t[..., -1])[..., None, None] * s + jnp.einsum('bsgk,bgrs,bgrsv->bgrkv', kc, to_end, v_new)
        return s, y

    s0 = jnp.zeros((bsz, g, r, kd, vd), f32)
    _, y = lax.scan(body, s0, tuple(_to_chunks(t) for t in (q, k, v, beta, log_a)))
    return _from_chunks(y)


def _rope_2d(t, rows):
    f32 = jnp.float32
    pos = jnp.arange(rows * GRID_W)
    row = (pos // GRID_W).astype(f32)
    col = (pos % GRID_W).astype(f32)
    half = t.shape[-1] // 2
    inv_freq = ROPE_BASE ** (-jnp.arange(0, half, 2, dtype=f32) / half)

    def rot(u, p):
        ang = p[:, None] * inv_freq
        cos, sin = jnp.cos(ang)[:, None, :], jnp.sin(ang)[:, None, :]
        u1, u2 = jnp.split(u, 2, axis=-1)
        return jnp.concatenate([u1 * cos - u2 * sin, u2 * cos + u1 * sin], axis=-1)

    t32 = t.astype(f32)
    return jnp.concatenate([rot(t32[..., :half], row), rot(t32[..., half:], col)], axis=-1).astype(t.dtype)


def _ssd_mixer(h_ctx, h_lat, w_in, conv_w, conv_b, dt_bias, a_log, d_skip, norm_g, w_out, keep_ctx):
    f32 = jnp.float32
    lc = h_ctx.shape[1]
    u = jnp.concatenate([h_ctx, h_lat], axis=1) @ w_in
    bsz, t = u.shape[:2]
    z, xbc, dt = jnp.split(u, [SSD_D_INNER, SSD_D_INNER + SSD_CONV_CH], axis=-1)
    xbc = jax.nn.silu(_conv_split(xbc, conv_w, lc) + conv_b)
    xs, bm, cm = jnp.split(xbc, [SSD_D_INNER, SSD_D_INNER + SSD_GROUPS * SSD_STATE], axis=-1)
    r = SSD_HEADS // SSD_GROUPS
    xs = xs.reshape(bsz, t, SSD_GROUPS, r, SSD_HEAD_DIM)
    bm = bm.reshape(bsz, t, SSD_GROUPS, SSD_STATE)
    cm = cm.reshape(bsz, t, SSD_GROUPS, SSD_STATE)
    dt = jax.nn.softplus(dt.reshape(bsz, t, 2, SSD_HEADS).astype(f32) + dt_bias.astype(f32))
    log_a = -jnp.exp(a_log.astype(f32)) * dt
    grp = lambda a: a.reshape(bsz, t, SSD_GROUPS, r)
    y = _scalar_decay_scan(cm, bm, xs * grp(dt[:, :, 0])[..., None], grp(log_a[:, :, 0]))
    y = y + _rev(_scalar_decay_scan(_rev(cm, lc), _rev(bm, lc), _rev(xs * grp(dt[:, :, 1])[..., None], lc),
                                    _rev(grp(log_a[:, :, 1]), lc)), lc)
    y = y + d_skip.reshape(SSD_GROUPS, r)[..., None] * xs
    start = 0 if keep_ctx else lc
    n = t - start
    y = y.reshape(bsz, t, SSD_D_INNER)[:, start:].astype(h_lat.dtype) * jax.nn.silu(z[:, start:])
    y = _rmsnorm(y.reshape(bsz, n, SSD_GROUPS, -1), norm_g.reshape(SSD_GROUPS, -1)).reshape(bsz, n, SSD_D_INNER)
    return _split_out(y @ w_out, lc, keep_ctx)


def _retention_mixer(h_ctx, h_lat, w_in, log_decay, w_out, rows, keep_ctx):
    f32 = jnp.float32
    lc = h_ctx.shape[1]
    u = jnp.concatenate([h_ctx, h_lat], axis=1) @ w_in
    bsz, t = u.shape[:2]
    q, k, v, g = jnp.split(u, [D_MODEL, 2 * D_MODEL, 2 * D_MODEL + RET_DV], axis=-1)
    q = q.reshape(bsz, t, RET_HEADS, RET_QK_DIM)
    k = k.reshape(bsz, t, RET_HEADS, RET_QK_DIM) * RET_QK_DIM ** -0.5
    q = jnp.concatenate([q[:, :lc], _rope_2d(q[:, lc:], rows)], axis=1)
    k = jnp.concatenate([k[:, :lc], _rope_2d(k[:, lc:], rows)], axis=1)
    v = v.reshape(bsz, t, RET_HEADS, 1, RET_V_DIM)
    ld_f = jnp.broadcast_to(log_decay[0].astype(f32)[:, None], (bsz, t, RET_HEADS, 1))
    ld_b = jnp.broadcast_to(log_decay[1].astype(f32)[:, None], (bsz, t, RET_HEADS, 1))
    y = _scalar_decay_scan(q, k, v, ld_f)
    y = y + _rev(_scalar_decay_scan(_rev(q, lc), _rev(k, lc), _rev(v, lc), ld_b), lc)
    start = 0 if keep_ctx else lc
    n = t - start
    y = y[:, start:].reshape(bsz, n, RET_HEADS, RET_V_DIM)
    mu = jnp.mean(y, axis=-1, keepdims=True)
    var = jnp.mean(jnp.square(y - mu), axis=-1, keepdims=True)
    y = ((y - mu) * lax.rsqrt(var + NORM_EPS)).reshape(bsz, n, RET_DV).astype(h_lat.dtype)
    y = y * jax.nn.silu(g[:, start:])
    return _split_out(y @ w_out, lc, keep_ctx)


def _lower_bound(lb_logits, layer):
    p = jax.nn.softmax(lb_logits.astype(jnp.float32), axis=0)
    return jnp.cumsum(p, axis=0)[layer] - p[0]


def _hgrn2_mixer(h_ctx, h_lat, w_in, lb, norm_g, w_out, keep_ctx):
    f32 = jnp.float32
    lc = h_ctx.shape[1]
    u = jnp.concatenate([h_ctx, h_lat], axis=1) @ w_in
    bsz, t = u.shape[:2]
    q, f_f, f_b, i, g = jnp.split(u, 5, axis=-1)
    shp = (bsz, t, HGRN_HEADS, HGRN_EXPAND)
    q, i = q.reshape(shp), i.reshape(shp)
    lb = lb.reshape(HGRN_HEADS, HGRN_EXPAND)

    def gates(f):
        f = f.reshape(shp).astype(f32)
        log_f = jnp.logaddexp(jnp.log(lb), jnp.log1p(-lb) + jax.nn.log_sigmoid(f))
        return log_f, (1 - lb) * jax.nn.sigmoid(-f)

    lf_f, k_f = gates(f_f)
    lf_b, k_b = gates(f_b)
    y = _vector_decay_scan(q, k_f, i, lf_f)
    y = y + _rev(_vector_decay_scan(_rev(q, lc), _rev(k_b, lc), _rev(i, lc), _rev(lf_b, lc)), lc)
    start = 0 if keep_ctx else lc
    n = t - start
    y = _rmsnorm(y[:, start:].astype(h_lat.dtype), norm_g.reshape(HGRN_HEADS, HGRN_EXPAND))
    y = (y * jax.nn.silu(g[:, start:].reshape(bsz, n, HGRN_HEADS, HGRN_EXPAND))).reshape(bsz, n, D_MODEL)
    return _split_out(y @ w_out, lc, keep_ctx)


def _gdn_mixer(h_ctx, h_lat, w_in, conv_w, dt_bias, a_log, norm_g, w_out, keep_ctx):
    f32 = jnp.float32
    lc = h_ctx.shape[1]
    u = jnp.concatenate([h_ctx, h_lat], axis=1) @ w_in
    bsz, t = u.shape[:2]
    qkv, z, bt, a = jnp.split(u, [GDN_CONV_CH, GDN_CONV_CH + GDN_DV, GDN_CONV_CH + GDN_DV + 2 * GDN_V_HEADS], axis=-1)
    qkv = jax.nn.silu(_conv_split(qkv, conv_w, lc))
    q, k, v = jnp.split(qkv, [GDN_DK, 2 * GDN_DK], axis=-1)
    r = GDN_V_HEADS // GDN_K_HEADS
    q = _l2norm(q.reshape(bsz, t, GDN_K_HEADS, GDN_HEAD_DIM)) * GDN_HEAD_DIM ** -0.5
    k = _l2norm(k.reshape(bsz, t, GDN_K_HEADS, GDN_HEAD_DIM))
    v = v.reshape(bsz, t, GDN_K_HEADS, r, GDN_HEAD_DIM)
    beta = jax.nn.sigmoid(bt.reshape(bsz, t, 2, GDN_K_HEADS, r).astype(f32))
    log_a = -jnp.exp(a_log.astype(f32)).reshape(2, GDN_K_HEADS, r) * jax.nn.softplus(
        a.reshape(bsz, t, 2, GDN_K_HEADS, r).astype(f32) + dt_bias.astype(f32).reshape(2, GDN_K_HEADS, r))
    y = _delta_scan(q, k, v, beta[:, :, 0], log_a[:, :, 0])
    y = y + _rev(_delta_scan(_rev(q, lc), _rev(k, lc), _rev(v, lc), _rev(beta[:, :, 1], lc),
                             _rev(log_a[:, :, 1], lc)), lc)
    start = 0 if keep_ctx else lc
    n = t - start
    y = y[:, start:].reshape(bsz, n, GDN_V_HEADS, GDN_HEAD_DIM).astype(h_lat.dtype)
    y = _rmsnorm(y, norm_g) * jax.nn.silu(z[:, start:].reshape(bsz, n, GDN_V_HEADS, GDN_HEAD_DIM))
    return _split_out(y.reshape(bsz, n, GDN_DV) @ w_out, lc, keep_ctx)


def _n_occ(m):
    return len(range(m, DEPTH, N_MIXERS))


def setup_inputs(seed: int = 0) -> dict:
    key = jax.random.key(seed)
    ks = iter(jax.random.split(key, 48))
    f32 = jnp.float32
    d = D_MODEL

    def nrm(shape, std):
        return std * jax.random.normal(next(ks), shape, f32)

    def gain(shape):
        return 1.0 + nrm(shape, 0.02)

    def dt_bias(shape):
        dt = jnp.exp(jax.random.uniform(next(ks), shape, f32, math.log(1e-3), math.log(1e-1)))
        return dt + jnp.log(-jnp.expm1(-dt))

    def a_log(shape):
        return jnp.log(jax.random.uniform(next(ks), shape, f32, 1.0, 16.0))

    n_a, n_b, n_c, n_d = (_n_occ(m) for m in range(N_MIXERS))
    ret_base = jnp.log1p(-(2.0 ** (-5.0 - jnp.arange(RET_HEADS, dtype=f32))))
    return {
        'x': nrm((BATCH, SEQ, d), 1.0),
        'c': nrm((BATCH, d), 1.0),
        'ctx': nrm((BATCH, CTX_LEN, d), 1.0),
        'c_ctx': nrm((d,), 1.0),
        'ada_w': nrm((DEPTH, d, 6 * d), 0.5 * d ** -0.5),
        'ada_b': nrm((DEPTH, 6 * d), 0.02),
        'norm_g': gain((DEPTH, 2, d)),
        'mlp_w1': nrm((DEPTH, d, D_FF), d ** -0.5),
        'mlp_w2': nrm((DEPTH, D_FF, d), D_FF ** -0.5),
        'final_g': gain((d,)),
        'ssd_w_in': nrm((n_a, d, SSD_IN), d ** -0.5),
        'ssd_conv_w': nrm((n_a, CONV_W, SSD_CONV_CH), CONV_W ** -0.5),
        'ssd_conv_b': nrm((n_a, SSD_CONV_CH), 0.02),
        'ssd_dt_bias': dt_bias((n_a, 2, SSD_HEADS)),
        'ssd_a_log': a_log((n_a, 2, SSD_HEADS)),
        'ssd_d': gain((n_a, SSD_HEADS)),
        'ssd_norm_g': gain((n_a, SSD_D_INNER)),
        'ssd_w_out': nrm((n_a, SSD_D_INNER, d), SSD_D_INNER ** -0.5),
        'ret_w_in': nrm((n_b, d, RET_IN), d ** -0.5),
        'ret_log_decay': ret_base * jnp.exp(nrm((n_b, 2, RET_HEADS), 0.1)),
        'ret_w_out': nrm((n_b, RET_DV, d), RET_DV ** -0.5),
        'hgrn_w_in': nrm((n_c, d, HGRN_IN), d ** -0.5),
        'hgrn_lb_logits': nrm((DEPTH, d), 0.1),
        'hgrn_norm_g': gain((n_c, d)),
        'hgrn_w_out': nrm((n_c, d, d), d ** -0.5),
        'gdn_w_in': nrm((n_d, d, GDN_IN), d ** -0.5),
        'gdn_conv_w': nrm((n_d, CONV_W, GDN_CONV_CH), CONV_W ** -0.5),
        'gdn_dt_bias': dt_bias((n_d, 2, GDN_V_HEADS)),
        'gdn_a_log': a_log((n_d, 2, GDN_V_HEADS)),
        'gdn_norm_g': gain((n_d, GDN_HEAD_DIM)),
        'gdn_w_out': nrm((n_d, GDN_DV, d), GDN_DV ** -0.5),
    }


def reference(x, c, ctx, c_ctx, ada_w, ada_b, norm_g, mlp_w1, mlp_w2, final_g,
              ssd_w_in, ssd_conv_w, ssd_conv_b, ssd_dt_bias, ssd_a_log, ssd_d, ssd_norm_g, ssd_w_out,
              ret_w_in, ret_log_decay, ret_w_out,
              hgrn_w_in, hgrn_lb_logits, hgrn_norm_g, hgrn_w_out,
              gdn_w_in, gdn_conv_w, gdn_dt_bias, gdn_a_log, gdn_norm_g, gdn_w_out):
    bsz = x.shape[0]
    rows = x.shape[1] // GRID_W
    x_lat, x_ctx = x, ctx
    cond = jax.nn.silu(jnp.concatenate([c, c_ctx[None]], axis=0))
    for i in range(DEPTH):
        mixer, occ = i % N_MIXERS, i // N_MIXERS
        keep_ctx = i < DEPTH - 1
        mod = cond @ ada_w[i] + ada_b[i]
        sh1, sc1, g1, sh2, sc2, g2 = jnp.split(mod, 6, axis=-1)
        h_lat = _adaln(x_lat, norm_g[i, 0], sh1[:bsz, None], sc1[:bsz, None])
        h_ctx = _adaln(x_ctx, norm_g[i, 0], sh1[bsz], sc1[bsz])
        if mixer == 0:
            y_ctx, y_lat = _ssd_mixer(h_ctx, h_lat, ssd_w_in[occ], ssd_conv_w[occ], ssd_conv_b[occ], ssd_dt_bias[occ],
                                      ssd_a_log[occ], ssd_d[occ], ssd_norm_g[occ], ssd_w_out[occ], keep_ctx)
        elif mixer == 1:
            y_ctx, y_lat = _retention_mixer(h_ctx, h_lat, ret_w_in[occ], ret_log_decay[occ], ret_w_out[occ],
                                            rows, keep_ctx)
        elif mixer == 2:
            y_ctx, y_lat = _hgrn2_mixer(h_ctx, h_lat, hgrn_w_in[occ], _lower_bound(hgrn_lb_logits, i),
                                        hgrn_norm_g[occ], hgrn_w_out[occ], keep_ctx)
        else:
            y_ctx, y_lat = _gdn_mixer(h_ctx, h_lat, gdn_w_in[occ], gdn_conv_w[occ], gdn_dt_bias[occ],
                                      gdn_a_log[occ], gdn_norm_g[occ], gdn_w_out[occ], keep_ctx)
        x_lat = x_lat + g1[:bsz, None] * y_lat
        h_lat = _adaln(x_lat, norm_g[i, 1], sh2[:bsz, None], sc2[:bsz, None])
        x_lat = x_lat + g2[:bsz, None] * _sq_relu_mlp(h_lat, mlp_w1[i], mlp_w2[i])
        if keep_ctx:
            x_ctx = x_ctx + g1[bsz] * y_ctx
            h_ctx = _adaln(x_ctx, norm_g[i, 1], sh2[bsz], sc2[bsz])
            x_ctx = x_ctx + g2[bsz] * _sq_relu_mlp(h_ctx, mlp_w1[i], mlp_w2[i])
    return _rmsnorm(x_lat, final_g)
```

```python
import functools
import math

import jax
import jax.numpy as jnp
import numpy as np
from jax import lax
from jax.experimental import pallas as pl
from jax.experimental.pallas import tpu as pltpu

F32 = jnp.float32
BF16 = jnp.bfloat16

GRID_W = 64
CHUNK = 64
CONV_W = 3
NORM_EPS = 1e-6
ROPE_BASE = 10000.0
SSD_HEAD_DIM = 64
SSD_GROUPS = 8
SSD_STATE = 128
RET_HEADS = 8
HGRN_EXPAND = 128
GDN_HEAD_DIM = 128

ROW_TILE = 512
VMEM_LIMIT = 56 * 1024 * 1024


def _params(*sem):
    return pltpu.CompilerParams(dimension_semantics=sem, vmem_limit_bytes=VMEM_LIMIT)


def _col_tile(n, cap=1536):
    best = 128
    for t in range(128, cap + 1, 128):
        if n % t == 0:
            best = t
    return best


def _ada_kernel(c_ref, w_ref, b_ref, o_ref):
    c = c_ref[...]
    c = (c * jax.nn.sigmoid(c)).astype(BF16)
    o_ref[0] = jnp.dot(c, w_ref[0].astype(BF16), preferred_element_type=F32) + b_ref[0]


def _ada_mod(cond_pad, ada_w, ada_b):
    depth, d, n = ada_w.shape
    tn = 1024
    return pl.pallas_call(
        _ada_kernel,
        grid=(depth, n // tn),
        in_specs=[pl.BlockSpec((8, d), lambda l, j: (0, 0)),
                  pl.BlockSpec((1, d, tn), lambda l, j: (l, 0, j)),
                  pl.BlockSpec((1, 1, tn), lambda l, j: (l, 0, j))],
        out_specs=pl.BlockSpec((1, 8, tn), lambda l, j: (l, 0, j)),
        out_shape=jax.ShapeDtypeStruct((depth, 8, n), F32),
        compiler_params=_params("parallel", "parallel"),
        name="ada_mod",
    )(cond_pad, ada_w, ada_b.reshape(depth, 1, n))


def _adaln_rows(x, g, sh, sc):
    y = x * lax.rsqrt(jnp.mean(x * x, axis=-1, keepdims=True) + NORM_EPS)
    return y * g * (1.0 + sc) + sh


def _ln_mm_kernel(x_ref, g_ref, sh_ref, sc_ref, w_ref, o_ref, h_ref):
    @pl.when(pl.program_id(1) == 0)
    def _():
        h_ref[...] = _adaln_rows(x_ref[...], g_ref[...], sh_ref[0], sc_ref[0]).astype(BF16)

    o_ref[...] = jnp.dot(h_ref[...], w_ref[...], preferred_element_type=F32).astype(o_ref.dtype)


def _ln_mm(x, g, mod_t, sh_col, sc_col, w, out_dtype=F32):
    m, d = x.shape
    n = w.shape[1]
    tm, tn = ROW_TILE, _col_tile(n)
    return pl.pallas_call(
        _ln_mm_kernel,
        grid=(m // tm, n // tn),
        in_specs=[pl.BlockSpec((tm, d), lambda i, j: (i, 0)),
                  pl.BlockSpec((1, d), lambda i, j: (0, 0)),
                  pl.BlockSpec((1, 1, d), lambda i, j: (i, 0, sh_col)),
                  pl.BlockSpec((1, 1, d), lambda i, j: (i, 0, sc_col)),
                  pl.BlockSpec((d, tn), lambda i, j: (0, j))],
        out_specs=pl.BlockSpec((tm, tn), lambda i, j: (i, j)),
        out_shape=jax.ShapeDtypeStruct((m, n), out_dtype),
        scratch_shapes=[pltpu.VMEM((tm, d), BF16)],
        compiler_params=_params("parallel", "arbitrary"),
        name="adaln_in_proj",
    )(x, g, mod_t, mod_t, w)


def _out_kernel(y_ref, w_ref, x_ref, gate_ref, o_ref):
    o_ref[...] = x_ref[...] + gate_ref[0] * jnp.dot(y_ref[...], w_ref[...], preferred_element_type=F32)


def _out_proj(y, w, x, mod_t, gate_col):
    m, k = y.shape
    d = w.shape[1]
    tm, tn = ROW_TILE, 512
    return pl.pallas_call(
        _out_kernel,
        grid=(m // tm, d // tn),
        in_specs=[pl.BlockSpec((tm, k), lambda i, j: (i, 0)),
                  pl.BlockSpec((k, tn), lambda i, j: (0, j)),
                  pl.BlockSpec((tm, tn), lambda i, j: (i, j)),
                  pl.BlockSpec((1, 1, tn), lambda i, j: (i, 0, gate_col * (d // tn) + j))],
        out_specs=pl.BlockSpec((tm, tn), lambda i, j: (i, j)),
        out_shape=jax.ShapeDtypeStruct((m, d), F32),
        compiler_params=_params("parallel", "arbitrary"),
        name="out_proj",
    )(y, w, x, mod_t)


def _mlp_kernel(x_ref, g_ref, sh_ref, sc_ref, gate_ref, w1_ref, w2_ref, fg_ref, o_ref, h_ref, acc_ref, *, final):
    f = pl.program_id(1)

    @pl.when(f == 0)
    def _():
        h_ref[...] = _adaln_rows(x_ref[...], g_ref[...], sh_ref[0], sc_ref[0]).astype(BF16)
        acc_ref[...] = jnp.zeros_like(acc_ref)

    a = jnp.dot(h_ref[...], w1_ref[...], preferred_element_type=F32)
    a = jnp.square(jnp.maximum(a, 0.0)).astype(BF16)
    acc_ref[...] += jnp.dot(a, w2_ref[...], preferred_element_type=F32)

    @pl.when(f == pl.num_programs(1) - 1)
    def _():
        out = x_ref[...] + gate_ref[0] * acc_ref[...]
        if final:
            out = out * lax.rsqrt(jnp.mean(out * out, axis=-1, keepdims=True) + NORM_EPS) * fg_ref[...]
        o_ref[...] = out


def _mlp(x, g, mod_t, w1, w2, final_g, final):
    m, d = x.shape
    ff = w1.shape[1]
    tm, tf = ROW_TILE, 512
    return pl.pallas_call(
        functools.partial(_mlp_kernel, final=final),
        grid=(m // tm, ff // tf),
        in_specs=[pl.BlockSpec((tm, d), lambda i, f: (i, 0)),
                  pl.BlockSpec((1, d), lambda i, f: (0, 0)),
                  pl.BlockSpec((1, 1, d), lambda i, f: (i, 0, 3)),
                  pl.BlockSpec((1, 1, d), lambda i, f: (i, 0, 4)),
                  pl.BlockSpec((1, 1, d), lambda i, f: (i, 0, 5)),
                  pl.BlockSpec((d, tf), lambda i, f: (0, f)),
                  pl.BlockSpec((tf, d), lambda i, f: (f, 0)),
                  pl.BlockSpec((1, d), lambda i, f: (0, 0))],
        out_specs=pl.BlockSpec((tm, d), lambda i, f: (i, 0)),
        out_shape=jax.ShapeDtypeStruct((m, d), F32),
        scratch_shapes=[pltpu.VMEM((tm, d), BF16), pltpu.VMEM((tm, d), F32)],
        compiler_params=_params("parallel", "arbitrary"),
        name="adaln_mlp",
    )(x, g, mod_t, mod_t, mod_t, w1, w2, final_g)


SCAN_CHUNK = 256


def _scan_geometry(bsz, lc, seq):
    assert lc % SCAN_CHUNK == 0 and seq % SCAN_CHUNK == 0
    return lc // SCAN_CHUNK, seq // SCAN_CHUNK


def _seg_chunk(c, n_ctx_c, n_lat_c, reverse):
    if reverse:
        return c >= n_ctx_c, jnp.where(c < n_ctx_c, n_ctx_c - 1 - c, n_lat_c - 1 - (c - n_ctx_c))
    return c >= n_ctx_c, jnp.where(c < n_ctx_c, c, c - n_ctx_c)


def _row_block(b, c, bsz, n_ctx_c, n_lat_c, reverse):
    is_lat, j = _seg_chunk(c, n_ctx_c, n_lat_c, reverse)
    return jnp.where(is_lat, bsz * n_ctx_c + b * n_lat_c + j, b * n_ctx_c + j)


def _time_iotas(q):
    t = lax.broadcasted_iota(jnp.int32, (q, 1), 0).astype(F32)
    s = lax.broadcasted_iota(jnp.int32, (1, q), 1).astype(F32)
    return t, s


def _dot_t(a, b):
    return lax.dot_general(a, b, (((0,), (0,)), ((), ())), preferred_element_type=F32)


def _dot_nt(a, b):
    return lax.dot_general(a, b, (((1,), (1,)), ((), ())), preferred_element_type=F32)


def _ret_kernel(*refs, reverse, n_ctx_c, n_heads, k_scale):
    if reverse:
        ld_ref, q_ref, k_ref, v_ref, cos_ref, sin_ref, o_ref, s_ref = refs
    else:
        ld_ref, q_ref, k_ref, v_ref, cos_ref, sin_ref, yb_ref, g_ref, o_ref, s_ref = refs
    h, c = pl.program_id(1), pl.program_id(2)
    qn = SCAN_CHUNK

    @pl.when(c == 0)
    def _():
        s_ref[...] = jnp.zeros_like(s_ref)

    lg = ld_ref[(n_heads if reverse else 0) + h]
    t, s = _time_iotas(qn)
    is_lat = c >= n_ctx_c
    cos = jnp.where(is_lat, cos_ref[...], 1.0)
    sin = jnp.where(is_lat, sin_ref[...], 0.0)

    def rope(x):
        half = x.shape[1] // 2
        swapped = jnp.concatenate([pltpu.roll(x[:, :half], half // 2, 1), pltpu.roll(x[:, half:], half // 2, 1)], axis=1)
        return x * cos + swapped * sin

    q = rope(q_ref[...]).astype(BF16)
    k = rope(k_ref[...]) * k_scale
    v = v_ref[...].astype(BF16)
    if reverse:
        dmat = jnp.where(s >= t, jnp.exp((s - t) * lg), 0.0)
        q_dec, k_dec = jnp.exp((qn - t) * lg), jnp.exp(t * lg)
    else:
        dmat = jnp.where(t >= s, jnp.exp((t - s) * lg), 0.0)
        q_dec, k_dec = jnp.exp((t + 1.0) * lg), jnp.exp((qn - 1.0 - t) * lg)
    attn = (_dot_nt(q, k.astype(BF16)) * dmat).astype(BF16)
    state = s_ref[...]
    y = jnp.dot(attn, v, preferred_element_type=F32)
    y = y + jnp.dot(q, state.astype(BF16), preferred_element_type=F32) * q_dec
    chunk_dec = jnp.exp(jnp.full((1, 1), qn, F32) * lg)
    s_ref[...] = state * chunk_dec + _dot_t((k * k_dec).astype(BF16), v)
    if reverse:
        o_ref[...] = y
    else:
        y = y + yb_ref[...]
        mu = jnp.mean(y, axis=-1, keepdims=True)
        yc = y - mu
        var = jnp.mean(yc * yc, axis=-1, keepdims=True)
        g = g_ref[...]
        o_ref[...] = (yc * lax.rsqrt(var + NORM_EPS) * (g * jax.nn.sigmoid(g))).astype(o_ref.dtype)


def _ret_mixer(u, log_decay, bsz, lc, seq, d_model):
    n_ctx_c, n_lat_c = _scan_geometry(bsz, lc, seq)
    n_steps = n_ctx_c + n_lat_c
    qk_dim, v_dim = d_model // RET_HEADS, 2 * d_model // RET_HEADS
    rows = u.shape[0]
    cos_r, sin_r, cos_c, sin_c = _rope_tables(seq // GRID_W, qk_dim // 2)
    cos_t = jnp.asarray(np.concatenate([cos_r, cos_r, cos_c, cos_c], axis=1))
    sin_t = jnp.asarray(np.concatenate([-sin_r, sin_r, -sin_c, sin_c], axis=1))
    ld = log_decay.reshape(-1).astype(F32)
    y_b = None
    for reverse in (True, False):
        rb = functools.partial(_row_block, bsz=bsz, n_ctx_c=n_ctx_c, n_lat_c=n_lat_c, reverse=reverse)

        def tab(b, h, c, reverse=reverse):
            is_lat, j = _seg_chunk(c, n_ctx_c, n_lat_c, reverse)
            return (jnp.where(is_lat, j, 0), 0)

        in_specs = [pl.BlockSpec(memory_space=pltpu.SMEM),
                    pl.BlockSpec((SCAN_CHUNK, qk_dim), lambda b, h, c, rb=rb: (rb(b, c), h)),
                    pl.BlockSpec((SCAN_CHUNK, qk_dim), lambda b, h, c, rb=rb: (rb(b, c), RET_HEADS + h)),
                    pl.BlockSpec((SCAN_CHUNK, v_dim), lambda b, h, c, rb=rb: (rb(b, c), RET_HEADS + h)),
                    pl.BlockSpec((SCAN_CHUNK, qk_dim), tab),
                    pl.BlockSpec((SCAN_CHUNK, qk_dim), tab)]
        args = [ld, u, u, u, cos_t, sin_t]
        if not reverse:
            in_specs += [pl.BlockSpec((SCAN_CHUNK, v_dim), lambda b, h, c, rb=rb: (rb(b, c), h)),
                         pl.BlockSpec((SCAN_CHUNK, v_dim), lambda b, h, c, rb=rb: (rb(b, c), 2 * RET_HEADS + h))]
            args += [y_b, u]
        out = pl.pallas_call(
            functools.partial(_ret_kernel, reverse=reverse, n_ctx_c=n_ctx_c, n_heads=RET_HEADS, k_scale=qk_dim ** -0.5),
            grid=(bsz, RET_HEADS, n_steps),
            in_specs=in_specs,
            out_specs=pl.BlockSpec((SCAN_CHUNK, v_dim), lambda b, h, c, rb=rb: (rb(b, c), h)),
            out_shape=jax.ShapeDtypeStruct((rows, RET_HEADS * v_dim), F32 if reverse else BF16),
            scratch_shapes=[pltpu.VMEM((qk_dim, v_dim), F32)],
            compiler_params=_params("parallel", "parallel", "arbitrary"),
            name="retention_bwd" if reverse else "retention_fwd",
        )(*args)
        if reverse:
            y_b = out
    return out


def _rmsnorm(x, g, eps=NORM_EPS):
    y = x * lax.rsqrt(jnp.mean(x * x, axis=-1, keepdims=True) + eps)
    return y * g


def _l2norm(x, eps=1e-6):
    return x * lax.rsqrt(jnp.sum(x * x, axis=-1, keepdims=True) + eps)


def _dwconv(u, w):
    return lax.conv_general_dilated(u, w[:, None, :], window_strides=(1,), padding=[(CONV_W // 2, CONV_W // 2)],
                                    dimension_numbers=('NWC', 'WIO', 'NWC'), feature_group_count=u.shape[-1])


def _conv_split(u, w, lc):
    return jnp.concatenate([_dwconv(u[:, :lc], w), _dwconv(u[:, lc:], w)], axis=1)


def _rev(t, lc):
    return jnp.concatenate([jnp.flip(t[:, :lc], 1), jnp.flip(t[:, lc:], 1)], axis=1)


def _to_chunks(t):
    b, n = t.shape[:2]
    return jnp.moveaxis(t.reshape((b, n // CHUNK, CHUNK) + t.shape[2:]), 1, 0)


def _from_chunks(t):
    nc, b, q = t.shape[:3]
    return jnp.moveaxis(t, 0, 1).reshape((b, nc * q) + t.shape[3:])


def _chunk_masks():
    idx = jnp.arange(CHUNK)
    return idx[:, None] >= idx[None, :], idx[:, None] > idx[None, :]


def _scalar_decay_scan(q, k, v, log_a):
    bsz, _, g, n = q.shape
    r, p = v.shape[-2:]
    incl, _ = _chunk_masks()

    def body(s, xs):
        qc, kc, vc, la = xs
        cum = jnp.cumsum(la, axis=1)
        cum_t = jnp.moveaxis(cum, 1, -1)
        seg = cum_t[..., :, None] - cum_t[..., None, :]
        scores = jnp.einsum('btgn,bsgn->bgts', qc, kc)
        attn = scores[:, :, None] * jnp.exp(jnp.where(incl, seg, -jnp.inf))
        y = jnp.einsum('bgrts,bsgrp->btgrp', attn, vc)
        y = y + jnp.einsum('btgn,bgrnp->btgrp', qc, s) * jnp.exp(cum)[..., None]
        to_end = jnp.exp(cum[:, -1:] - cum)
        s = jnp.exp(cum[:, -1])[..., None, None] * s + jnp.einsum('bsgn,bsgr,bsgrp->bgrnp', kc, to_end, vc)
        return s, y

    s0 = jnp.zeros((bsz, g, r, n, p), F32)
    _, y = lax.scan(body, s0, tuple(_to_chunks(t) for t in (q, k, v, log_a)))
    return _from_chunks(y)


def _vector_decay_scan(q, k, v, log_f):
    bsz, _, h, kd = q.shape
    vd = v.shape[-1]
    incl, _ = _chunk_masks()

    def body(s, xs):
        qc, kc, vc, lf = xs
        cum = jnp.cumsum(lf, axis=1)
        seg = cum[:, :, None] - cum[:, None, :]
        decay = jnp.exp(jnp.where(incl[:, :, None, None], seg, -jnp.inf))
        attn = jnp.einsum('bthk,bshk,btshk->bhts', qc, kc, decay)
        y = jnp.einsum('bhts,bshv->bthv', attn, vc)
        y = y + jnp.einsum('bthk,bhkv->bthv', qc * jnp.exp(cum), s)
        s = jnp.exp(cum[:, -1])[..., None] * s + jnp.einsum('bshk,bshv->bhkv', kc * jnp.exp(cum[:, -1:] - cum), vc)
        return s, y

    s0 = jnp.zeros((bsz, h, kd, vd), F32)
    _, y = lax.scan(body, s0, tuple(_to_chunks(t) for t in (q, k, v, log_f)))
    return _from_chunks(y)


def _delta_scan(q, k, v, beta, log_a):
    bsz, _, g, kd = q.shape
    r, vd = v.shape[-2:]
    incl, strict = _chunk_masks()

    def body(s, xs):
        qc, kc, vc, bc, la = xs
        cum = jnp.cumsum(la, axis=1)
        cum_t = jnp.moveaxis(cum, 1, -1)
        seg = cum_t[..., :, None] - cum_t[..., None, :]
        beta_t = jnp.moveaxis(bc, 1, -1)
        kk = jnp.einsum('btgk,bsgk->bgts', kc, kc)
        lower = beta_t[..., :, None] * kk[:, :, None] * jnp.exp(jnp.where(strict, seg, -jnp.inf))
        rhs_v = jnp.moveaxis(vc * bc[..., None], 1, 3)
        rhs_k = jnp.moveaxis(kc[:, :, :, None, :] * (bc * jnp.exp(cum))[..., None], 1, 3)
        sol = lax.linalg.triangular_solve(lower, jnp.concatenate([rhs_v, rhs_k], axis=-1),
                                          left_side=True, lower=True, unit_diagonal=True)
        u, w = sol[..., :vd], sol[..., vd:]
        v_new = u - jnp.einsum('bgrtk,bgrkv->bgrtv', w, s)
        qk = jnp.einsum('btgk,bsgk->bgts', qc, kc)
        attn = qk[:, :, None] * jnp.exp(jnp.where(incl, seg, -jnp.inf))
        y = jnp.einsum('bgrts,bgrsv->btgrv', attn, v_new)
        y = y + jnp.einsum('btgk,bgrkv->btgrv', qc, s) * jnp.exp(cum)[..., None]
        to_end = jnp.exp(cum_t[..., -1:] - cum_t)
        s = jnp.exp(cum_t[..., -1])[..., None, None] * s + jnp.einsum('bsgk,bgrs,bgrsv->bgrkv', kc, to_end, v_new)
        return s, y

    s0 = jnp.zeros((bsz, g, r, kd, vd), F32)
    _, y = lax.scan(body, s0, tuple(_to_chunks(t) for t in (q, k, v, beta, log_a)))
    return _from_chunks(y)


def _rope_tables(rows, half):
    pos = np.arange(rows * GRID_W)
    inv_freq = np.float32(ROPE_BASE) ** (-(np.arange(0, half, 2, dtype=np.float32) / np.float32(half)))
    out = []
    for p in ((pos // GRID_W).astype(np.float32), (pos % GRID_W).astype(np.float32)):
        ang = (p[:, None] * inv_freq.astype(np.float32)).astype(np.float32).astype(np.float64)
        out += [np.cos(ang).astype(np.float32), np.sin(ang).astype(np.float32)]
    return out


def _rope_2d(t, rows):
    half = t.shape[-1] // 2
    cos_r, sin_r, cos_c, sin_c = (jnp.asarray(a)[:, None, :] for a in _rope_tables(rows, half))

    def rot(u, cos, sin):
        u1, u2 = jnp.split(u, 2, axis=-1)
        return jnp.concatenate([u1 * cos - u2 * sin, u2 * cos + u1 * sin], axis=-1)

    return jnp.concatenate([rot(t[..., :half], cos_r, sin_r), rot(t[..., half:], cos_c, sin_c)], axis=-1)


def _ssd_core(u, lc, conv_w, conv_b, dt_bias, a_log, d_skip, norm_g, start):
    bsz, t = u.shape[:2]
    d_inner = norm_g.shape[0]
    heads = d_inner // SSD_HEAD_DIM
    conv_ch = d_inner + 2 * SSD_GROUPS * SSD_STATE
    z, xbc, dt = jnp.split(u, [d_inner, d_inner + conv_ch], axis=-1)
    xbc = jax.nn.silu(_conv_split(xbc, conv_w, lc) + conv_b)
    xs, bm, cm = jnp.split(xbc, [d_inner, d_inner + SSD_GROUPS * SSD_STATE], axis=-1)
    r = heads // SSD_GROUPS
    xs = xs.reshape(bsz, t, SSD_GROUPS, r, SSD_HEAD_DIM)
    bm = bm.reshape(bsz, t, SSD_GROUPS, SSD_STATE)
    cm = cm.reshape(bsz, t, SSD_GROUPS, SSD_STATE)
    dt = jax.nn.softplus(dt.reshape(bsz, t, 2, heads) + dt_bias)
    log_a = -jnp.exp(a_log) * dt
    grp = lambda a: a.reshape(bsz, t, SSD_GROUPS, r)
    y = _scalar_decay_scan(cm, bm, xs * grp(dt[:, :, 0])[..., None], grp(log_a[:, :, 0]))
    y = y + _rev(_scalar_decay_scan(_rev(cm, lc), _rev(bm, lc), _rev(xs * grp(dt[:, :, 1])[..., None], lc),
                                    _rev(grp(log_a[:, :, 1]), lc)), lc)
    y = y + d_skip.reshape(SSD_GROUPS, r)[..., None] * xs
    n = t - start
    y = y.reshape(bsz, t, d_inner)[:, start:] * jax.nn.silu(z[:, start:])
    return _rmsnorm(y.reshape(bsz, n, SSD_GROUPS, -1), norm_g.reshape(SSD_GROUPS, -1)).reshape(bsz, n, d_inner)


def _ret_core(u, lc, log_decay, d_model, rows, start):
    bsz, t = u.shape[:2]
    dv = 2 * d_model
    qk_dim = d_model // RET_HEADS
    v_dim = 2 * qk_dim
    q, k, v, g = jnp.split(u, [d_model, 2 * d_model, 2 * d_model + dv], axis=-1)
    q = q.reshape(bsz, t, RET_HEADS, qk_dim)
    k = k.reshape(bsz, t, RET_HEADS, qk_dim) * qk_dim ** -0.5
    q = jnp.concatenate([q[:, :lc], _rope_2d(q[:, lc:], rows)], axis=1)
    k = jnp.concatenate([k[:, :lc], _rope_2d(k[:, lc:], rows)], axis=1)
    v = v.reshape(bsz, t, RET_HEADS, 1, v_dim)
    ld_f = jnp.broadcast_to(log_decay[0][:, None], (bsz, t, RET_HEADS, 1))
    ld_b = jnp.broadcast_to(log_decay[1][:, None], (bsz, t, RET_HEADS, 1))
    y = _scalar_decay_scan(q, k, v, ld_f)
    y = y + _rev(_scalar_decay_scan(_rev(q, lc), _rev(k, lc), _rev(v, lc), ld_b), lc)
    n = t - start
    y = y[:, start:].reshape(bsz, n, RET_HEADS, v_dim)
    mu = jnp.mean(y, axis=-1, keepdims=True)
    var = jnp.mean(jnp.square(y - mu), axis=-1, keepdims=True)
    y = ((y - mu) * lax.rsqrt(var + NORM_EPS)).reshape(bsz, n, dv)
    return y * jax.nn.silu(g[:, start:])


def _lower_bound(lb_logits, layer):
    p = jax.nn.softmax(lb_logits.astype(F32), axis=0)
    return jnp.cumsum(p, axis=0)[layer] - p[0]


def _hgrn_core(u, lc, lb, norm_g, start):
    bsz, t = u.shape[:2]
    d_model = norm_g.shape[0]
    heads = d_model // HGRN_EXPAND
    q, f_f, f_b, i, g = jnp.split(u, 5, axis=-1)
    shp = (bsz, t, heads, HGRN_EXPAND)
    q, i = q.reshape(shp), i.reshape(shp)
    lb = lb.reshape(heads, HGRN_EXPAND)

    def gates(f):
        f = f.reshape(shp)
        log_f = jnp.logaddexp(jnp.log(lb), jnp.log1p(-lb) + jax.nn.log_sigmoid(f))
        return log_f, (1 - lb) * jax.nn.sigmoid(-f)

    lf_f, k_f = gates(f_f)
    lf_b, k_b = gates(f_b)
    y = _vector_decay_scan(q, k_f, i, lf_f)
    y = y + _rev(_vector_decay_scan(_rev(q, lc), _rev(k_b, lc), _rev(i, lc), _rev(lf_b, lc)), lc)
    n = t - start
    y = _rmsnorm(y[:, start:], norm_g.reshape(heads, HGRN_EXPAND))
    return (y * jax.nn.silu(g[:, start:].reshape(bsz, n, heads, HGRN_EXPAND))).reshape(bsz, n, d_model)


def _gdn_core(u, lc, conv_w, dt_bias, a_log, norm_g, d_model, start):
    bsz, t = u.shape[:2]
    k_heads = d_model // GDN_HEAD_DIM
    v_heads = 2 * k_heads
    dk, dv = d_model, 2 * d_model
    conv_ch = 2 * dk + dv
    qkv, z, bt, a = jnp.split(u, [conv_ch, conv_ch + dv, conv_ch + dv + 2 * v_heads], axis=-1)
    qkv = jax.nn.silu(_conv_split(qkv, conv_w, lc))
    q, k, v = jnp.split(qkv, [dk, 2 * dk], axis=-1)
    r = v_heads // k_heads
    q = _l2norm(q.reshape(bsz, t, k_heads, GDN_HEAD_DIM)) * GDN_HEAD_DIM ** -0.5
    k = _l2norm(k.reshape(bsz, t, k_heads, GDN_HEAD_DIM))
    v = v.reshape(bsz, t, k_heads, r, GDN_HEAD_DIM)
    beta = jax.nn.sigmoid(bt.reshape(bsz, t, 2, k_heads, r))
    log_a = -jnp.exp(a_log).reshape(2, k_heads, r) * jax.nn.softplus(
        a.reshape(bsz, t, 2, k_heads, r) + dt_bias.reshape(2, k_heads, r))
    y = _delta_scan(q, k, v, beta[:, :, 0], log_a[:, :, 0])
    y = y + _rev(_delta_scan(_rev(q, lc), _rev(k, lc), _rev(v, lc), _rev(beta[:, :, 1], lc),
                             _rev(log_a[:, :, 1], lc)), lc)
    n = t - start
    y = y[:, start:].reshape(bsz, n, v_heads, GDN_HEAD_DIM)
    y = _rmsnorm(y, norm_g) * jax.nn.silu(z[:, start:].reshape(bsz, n, v_heads, GDN_HEAD_DIM))
    return y.reshape(bsz, n, dv)


def _rows_to_seq(u, bsz, lc):
    n_ctx = bsz * lc
    return jnp.concatenate([u[:n_ctx].reshape(bsz, lc, -1), u[n_ctx:].reshape(bsz, -1, u.shape[-1])], axis=1)


def _seq_to_rows(y, lc, start):
    bsz = y.shape[0]
    if start:
        return y.reshape(bsz * y.shape[1], -1)
    return jnp.concatenate([y[:, :lc].reshape(bsz * lc, -1), y[:, lc:].reshape(-1, y.shape[-1])], axis=0)


def kernel(x, c, ctx, c_ctx, ada_w, ada_b, norm_g, mlp_w1, mlp_w2, final_g, ssd_w_in, ssd_conv_w, ssd_conv_b,
           ssd_dt_bias, ssd_a_log, ssd_d, ssd_norm_g, ssd_w_out, ret_w_in, ret_log_decay, ret_w_out, hgrn_w_in,
           hgrn_lb_logits, hgrn_norm_g, hgrn_w_out, gdn_w_in, gdn_conv_w, gdn_dt_bias, gdn_a_log, gdn_norm_g,
           gdn_w_out):
    bsz, seq, d = x.shape
    lc = ctx.shape[1]
    depth = ada_w.shape[0]
    rows_grid = seq // GRID_W
    n_ctx = bsz * lc
    assert lc % ROW_TILE == 0 or ROW_TILE % lc == 0 and n_ctx % ROW_TILE == 0
    assert seq % ROW_TILE == 0 and bsz + 1 <= 8

    cond_pad = jnp.concatenate([c, c_ctx[None], jnp.zeros((8 - bsz - 1, d), F32)], axis=0)
    mod = _ada_mod(cond_pad, ada_w, ada_b)

    tile_row = [bsz] * (n_ctx // ROW_TILE) + [b for b in range(bsz) for _ in range(seq // ROW_TILE)]
    tile_row = jnp.asarray(tile_row, jnp.int32)
    n_ctx_tiles = n_ctx // ROW_TILE

    xr = jnp.concatenate([ctx.reshape(n_ctx, d), x.reshape(bsz * seq, d)], axis=0)

    for i in range(depth):
        mixer, occ = i % 4, i // 4
        keep_ctx = i < depth - 1
        start = 0 if keep_ctx else lc
        mod_t = mod[i][tile_row][:, None, :]
        g0, g1 = norm_g[i, 0][None], norm_g[i, 1][None]
        if mixer == 0:
            w_in, w_out = ssd_w_in[occ], ssd_w_out[occ]
        elif mixer == 1:
            w_in, w_out = ret_w_in[occ], ret_w_out[occ]
        elif mixer == 2:
            w_in, w_out = hgrn_w_in[occ], hgrn_w_out[occ]
        else:
            w_in, w_out = gdn_w_in[occ], gdn_w_out[occ]
        n_in = w_in.shape[1]
        n_main = (n_in // 1024) * 1024 if n_in % 1024 else n_in
        w_in = w_in.astype(BF16)
        u = _ln_mm(xr, g0, mod_t, 0, 1, w_in[:, :n_main])
        if n_main != n_in:
            u = jnp.concatenate([u, _ln_mm(xr, g0, mod_t, 0, 1, w_in[:, n_main:])], axis=1)
        first_row = 0 if keep_ctx else n_ctx
        if mixer == 1:
            yr = _ret_mixer(u, ret_log_decay[occ], bsz, lc, seq, d)[first_row:]
        else:
            useq = _rows_to_seq(u, bsz, lc)
            if mixer == 0:
                y = _ssd_core(useq, lc, ssd_conv_w[occ], ssd_conv_b[occ], ssd_dt_bias[occ], ssd_a_log[occ],
                              ssd_d[occ], ssd_norm_g[occ], start)
            elif mixer == 2:
                y = _hgrn_core(useq, lc, _lower_bound(hgrn_lb_logits, i), hgrn_norm_g[occ], start)
            else:
                y = _gdn_core(useq, lc, gdn_conv_w[occ], gdn_dt_bias[occ], gdn_a_log[occ], gdn_norm_g[occ], d,
                              start)
            yr = _seq_to_rows(y, lc, start).astype(BF16)
        if not keep_ctx:
            xr = xr[n_ctx:]
            mod_t = mod_t[n_ctx_tiles:]
        xr = _out_proj(yr, w_out.astype(BF16), xr, mod_t, 2)
        xr = _mlp(xr, g1, mod_t, mlp_w1[i].astype(BF16), mlp_w2[i].astype(BF16), final_g[None], final=not keep_ctx)
    return xr.reshape(bsz, seq, d)
```

```python
import functools
import math

import jax
import jax.numpy as jnp
import numpy as np
from jax import lax
from jax.experimental import pallas as pl
from jax.experimental.pallas import tpu as pltpu

F32 = jnp.float32
BF16 = jnp.bfloat16

GRID_W = 64
CHUNK = 64
CONV_W = 3
NORM_EPS = 1e-6
ROPE_BASE = 10000.0
SSD_HEAD_DIM = 64
SSD_GROUPS = 8
SSD_STATE = 128
RET_HEADS = 8
HGRN_EXPAND = 128
GDN_HEAD_DIM = 128

ROW_TILE = 512
VMEM_LIMIT = 56 * 1024 * 1024


def _params(*sem):
    return pltpu.CompilerParams(dimension_semantics=sem, vmem_limit_bytes=VMEM_LIMIT)


def _col_tile(n, cap=1536):
    best = 128
    for t in range(128, cap + 1, 128):
        if n % t == 0:
            best = t
    return best


def _ada_kernel(c_ref, w_ref, b_ref, o_ref):
    c = c_ref[...]
    c = (c * jax.nn.sigmoid(c)).astype(BF16)
    o_ref[0] = jnp.dot(c, w_ref[0].astype(BF16), preferred_element_type=F32) + b_ref[0]


def _ada_mod(cond_pad, ada_w, ada_b):
    depth, d, n = ada_w.shape
    tn = 1024
    return pl.pallas_call(
        _ada_kernel,
        grid=(depth, n // tn),
        in_specs=[pl.BlockSpec((8, d), lambda l, j: (0, 0)),
                  pl.BlockSpec((1, d, tn), lambda l, j: (l, 0, j)),
                  pl.BlockSpec((1, 1, tn), lambda l, j: (l, 0, j))],
        out_specs=pl.BlockSpec((1, 8, tn), lambda l, j: (l, 0, j)),
        out_shape=jax.ShapeDtypeStruct((depth, 8, n), F32),
        compiler_params=_params("parallel", "parallel"),
        name="ada_mod",
    )(cond_pad, ada_w, ada_b.reshape(depth, 1, n))


def _adaln_rows(x, g, sh, sc):
    y = x * lax.rsqrt(jnp.mean(x * x, axis=-1, keepdims=True) + NORM_EPS)
    return y * g * (1.0 + sc) + sh


def _ln_mm_kernel(x_ref, g_ref, sh_ref, sc_ref, w_ref, o_ref, h_ref):
    @pl.when(pl.program_id(1) == 0)
    def _():
        h_ref[...] = _adaln_rows(x_ref[...], g_ref[...], sh_ref[0], sc_ref[0]).astype(BF16)

    o_ref[...] = jnp.dot(h_ref[...], w_ref[...], preferred_element_type=F32).astype(o_ref.dtype)


def _ln_mm(x, g, mod_t, sh_col, sc_col, w, out_dtype=F32):
    m, d = x.shape
    n = w.shape[1]
    tm, tn = ROW_TILE, _col_tile(n)
    return pl.pallas_call(
        _ln_mm_kernel,
        grid=(m // tm, n // tn),
        in_specs=[pl.BlockSpec((tm, d), lambda i, j: (i, 0)),
                  pl.BlockSpec((1, d), lambda i, j: (0, 0)),
                  pl.BlockSpec((1, 1, d), lambda i, j: (i, 0, sh_col)),
                  pl.BlockSpec((1, 1, d), lambda i, j: (i, 0, sc_col)),
                  pl.BlockSpec((d, tn), lambda i, j: (0, j))],
        out_specs=pl.BlockSpec((tm, tn), lambda i, j: (i, j)),
        out_shape=jax.ShapeDtypeStruct((m, n), out_dtype),
        scratch_shapes=[pltpu.VMEM((tm, d), BF16)],
        compiler_params=_params("parallel", "arbitrary"),
        name="adaln_in_proj",
    )(x, g, mod_t, mod_t, w)


def _out_kernel(y_ref, w_ref, x_ref, gate_ref, o_ref):
    o_ref[...] = x_ref[...] + gate_ref[0] * jnp.dot(y_ref[...], w_ref[...], preferred_element_type=F32)


def _out_proj(y, w, x, mod_t, gate_col):
    m, k = y.shape
    d = w.shape[1]
    tm, tn = ROW_TILE, 512
    return pl.pallas_call(
        _out_kernel,
        grid=(m // tm, d // tn),
        in_specs=[pl.BlockSpec((tm, k), lambda i, j: (i, 0)),
                  pl.BlockSpec((k, tn), lambda i, j: (0, j)),
                  pl.BlockSpec((tm, tn), lambda i, j: (i, j)),
                  pl.BlockSpec((1, 1, tn), lambda i, j: (i, 0, gate_col * (d // tn) + j))],
        out_specs=pl.BlockSpec((tm, tn), lambda i, j: (i, j)),
        out_shape=jax.ShapeDtypeStruct((m, d), F32),
        compiler_params=_params("parallel", "arbitrary"),
        name="out_proj",
    )(y, w, x, mod_t)


def _mlp_kernel(x_ref, g_ref, sh_ref, sc_ref, gate_ref, w1_ref, w2_ref, fg_ref, o_ref, h_ref, acc_ref, *, final):
    f = pl.program_id(1)

    @pl.when(f == 0)
    def _():
        h_ref[...] = _adaln_rows(x_ref[...], g_ref[...], sh_ref[0], sc_ref[0]).astype(BF16)
        acc_ref[...] = jnp.zeros_like(acc_ref)

    a = jnp.dot(h_ref[...], w1_ref[...], preferred_element_type=F32)
    a = jnp.square(jnp.maximum(a, 0.0)).astype(BF16)
    acc_ref[...] += jnp.dot(a, w2_ref[...], preferred_element_type=F32)

    @pl.when(f == pl.num_programs(1) - 1)
    def _():
        out = x_ref[...] + gate_ref[0] * acc_ref[...]
        if final:
            out = out * lax.rsqrt(jnp.mean(out * out, axis=-1, keepdims=True) + NORM_EPS) * fg_ref[...]
        o_ref[...] = out


def _mlp(x, g, mod_t, w1, w2, final_g, final):
    m, d = x.shape
    ff = w1.shape[1]
    tm, tf = ROW_TILE, 512
    return pl.pallas_call(
        functools.partial(_mlp_kernel, final=final),
        grid=(m // tm, ff // tf),
        in_specs=[pl.BlockSpec((tm, d), lambda i, f: (i, 0)),
                  pl.BlockSpec((1, d), lambda i, f: (0, 0)),
                  pl.BlockSpec((1, 1, d), lambda i, f: (i, 0, 3)),
                  pl.BlockSpec((1, 1, d), lambda i, f: (i, 0, 4)),
                  pl.BlockSpec((1, 1, d), lambda i, f: (i, 0, 5)),
                  pl.BlockSpec((d, tf), lambda i, f: (0, f)),
                  pl.BlockSpec((tf, d), lambda i, f: (f, 0)),
                  pl.BlockSpec((1, d), lambda i, f: (0, 0))],
        out_specs=pl.BlockSpec((tm, d), lambda i, f: (i, 0)),
        out_shape=jax.ShapeDtypeStruct((m, d), F32),
        scratch_shapes=[pltpu.VMEM((tm, d), BF16), pltpu.VMEM((tm, d), F32)],
        compiler_params=_params("parallel", "arbitrary"),
        name="adaln_mlp",
    )(x, g, mod_t, mod_t, mod_t, w1, w2, final_g)


SCAN_CHUNK = 256


def _scan_geometry(bsz, lc, seq):
    assert lc % SCAN_CHUNK == 0 and seq % SCAN_CHUNK == 0
    return lc // SCAN_CHUNK, seq // SCAN_CHUNK


def _seg_chunk(c, n_ctx_c, n_lat_c, reverse):
    if reverse:
        return c >= n_ctx_c, jnp.where(c < n_ctx_c, n_ctx_c - 1 - c, n_lat_c - 1 - (c - n_ctx_c))
    return c >= n_ctx_c, jnp.where(c < n_ctx_c, c, c - n_ctx_c)


def _row_block(b, c, bsz, n_ctx_c, n_lat_c, reverse):
    is_lat, j = _seg_chunk(c, n_ctx_c, n_lat_c, reverse)
    return jnp.where(is_lat, bsz * n_ctx_c + b * n_lat_c + j, b * n_ctx_c + j)


def _time_iotas(q):
    t = lax.broadcasted_iota(jnp.int32, (q, 1), 0).astype(F32)
    s = lax.broadcasted_iota(jnp.int32, (1, q), 1).astype(F32)
    return t, s


def _dot_t(a, b):
    return lax.dot_general(a, b, (((0,), (0,)), ((), ())), preferred_element_type=F32)


def _dot_nt(a, b):
    return lax.dot_general(a, b, (((1,), (1,)), ((), ())), preferred_element_type=F32)


def _ret_kernel(*refs, reverse, n_ctx_c, n_heads, k_scale):
    if reverse:
        ld_ref, q_ref, k_ref, v_ref, cos_ref, sin_ref, o_ref, s_ref = refs
    else:
        ld_ref, q_ref, k_ref, v_ref, cos_ref, sin_ref, yb_ref, g_ref, o_ref, s_ref = refs
    h, c = pl.program_id(1), pl.program_id(2)
    qn = SCAN_CHUNK

    @pl.when(c == 0)
    def _():
        s_ref[...] = jnp.zeros_like(s_ref)

    lg = ld_ref[(n_heads if reverse else 0) + h]
    t, s = _time_iotas(qn)
    is_lat = c >= n_ctx_c
    cos = jnp.where(is_lat, cos_ref[...], 1.0)
    sin = jnp.where(is_lat, sin_ref[...], 0.0)

    def rope(x):
        half = x.shape[1] // 2
        swapped = jnp.concatenate([pltpu.roll(x[:, :half], half // 2, 1), pltpu.roll(x[:, half:], half // 2, 1)], axis=1)
        return x * cos + swapped * sin

    q = rope(q_ref[...]).astype(BF16)
    k = rope(k_ref[...]) * k_scale
    v = v_ref[...].astype(BF16)
    if reverse:
        dmat = jnp.where(s >= t, jnp.exp((s - t) * lg), 0.0)
        q_dec, k_dec = jnp.exp((qn - t) * lg), jnp.exp(t * lg)
    else:
        dmat = jnp.where(t >= s, jnp.exp((t - s) * lg), 0.0)
        q_dec, k_dec = jnp.exp((t + 1.0) * lg), jnp.exp((qn - 1.0 - t) * lg)
    attn = (_dot_nt(q, k.astype(BF16)) * dmat).astype(BF16)
    state = s_ref[...]
    y = jnp.dot(attn, v, preferred_element_type=F32)
    y = y + jnp.dot(q, state.astype(BF16), preferred_element_type=F32) * q_dec
    chunk_dec = jnp.exp(jnp.full((1, 1), qn, F32) * lg)
    s_ref[...] = state * chunk_dec + _dot_t((k * k_dec).astype(BF16), v)
    if reverse:
        o_ref[...] = y
    else:
        y = y + yb_ref[...]
        mu = jnp.mean(y, axis=-1, keepdims=True)
        yc = y - mu
        var = jnp.mean(yc * yc, axis=-1, keepdims=True)
        g = g_ref[...]
        o_ref[...] = (yc * lax.rsqrt(var + NORM_EPS) * (g * jax.nn.sigmoid(g))).astype(o_ref.dtype)


def _ret_mixer(u, log_decay, bsz, lc, seq, d_model):
    n_ctx_c, n_lat_c = _scan_geometry(bsz, lc, seq)
    n_steps = n_ctx_c + n_lat_c
    qk_dim, v_dim = d_model // RET_HEADS, 2 * d_model // RET_HEADS
    rows = u.shape[0]
    cos_r, sin_r, cos_c, sin_c = _rope_tables(seq // GRID_W, qk_dim // 2)
    cos_t = jnp.asarray(np.concatenate([cos_r, cos_r, cos_c, cos_c], axis=1))
    sin_t = jnp.asarray(np.concatenate([-sin_r, sin_r, -sin_c, sin_c], axis=1))
    ld = log_decay.reshape(-1).astype(F32)
    y_b = None
    for reverse in (True, False):
        rb = functools.partial(_row_block, bsz=bsz, n_ctx_c=n_ctx_c, n_lat_c=n_lat_c, reverse=reverse)

        def tab(b, h, c, reverse=reverse):
            is_lat, j = _seg_chunk(c, n_ctx_c, n_lat_c, reverse)
            return (jnp.where(is_lat, j, 0), 0)

        in_specs = [pl.BlockSpec(memory_space=pltpu.SMEM),
                    pl.BlockSpec((SCAN_CHUNK, qk_dim), lambda b, h, c, rb=rb: (rb(b, c), h)),
                    pl.BlockSpec((SCAN_CHUNK, qk_dim), lambda b, h, c, rb=rb: (rb(b, c), RET_HEADS + h)),
                    pl.BlockSpec((SCAN_CHUNK, v_dim), lambda b, h, c, rb=rb: (rb(b, c), RET_HEADS + h)),
                    pl.BlockSpec((SCAN_CHUNK, qk_dim), tab),
                    pl.BlockSpec((SCAN_CHUNK, qk_dim), tab)]
        args = [ld, u, u, u, cos_t, sin_t]
        if not reverse:
            in_specs += [pl.BlockSpec((SCAN_CHUNK, v_dim), lambda b, h, c, rb=rb: (rb(b, c), h)),
                         pl.BlockSpec((SCAN_CHUNK, v_dim), lambda b, h, c, rb=rb: (rb(b, c), 2 * RET_HEADS + h))]
            args += [y_b, u]
        out = pl.pallas_call(
            functools.partial(_ret_kernel, reverse=reverse, n_ctx_c=n_ctx_c, n_heads=RET_HEADS, k_scale=qk_dim ** -0.5),
            grid=(bsz, RET_HEADS, n_steps),
            in_specs=in_specs,
            out_specs=pl.BlockSpec((SCAN_CHUNK, v_dim), lambda b, h, c, rb=rb: (rb(b, c), h)),
            out_shape=jax.ShapeDtypeStruct((rows, RET_HEADS * v_dim), F32 if reverse else BF16),
            scratch_shapes=[pltpu.VMEM((qk_dim, v_dim), F32)],
            compiler_params=_params("parallel", "parallel", "arbitrary"),
            name="retention_bwd" if reverse else "retention_fwd",
        )(*args)
        if reverse:
            y_b = out
    return out


CONV_COLS = 512


def _conv_kernel(x_ref, prev_ref, next_ref, w_ref, b_ref, o_ref, *, n_ctx_c, n_lat_c, bsz, n_scaled, n_normed, scale,
                 head_dim):
    rb, cb = pl.program_id(0), pl.program_id(1)
    is_lat = rb >= bsz * n_ctx_c
    j = jnp.where(is_lat, (rb - bsz * n_ctx_c) % n_lat_c, rb % n_ctx_c)
    last = jnp.where(is_lat, n_lat_c - 1, n_ctx_c - 1)
    x = x_ref[...]
    rows = x.shape[0]
    row = lax.broadcasted_iota(jnp.int32, (rows, 1), 0)
    halo_prev = jnp.where(j > 0, prev_ref[7:8, :], 0.0)
    halo_next = jnp.where(j < last, next_ref[0:1, :], 0.0)
    x_prev = jnp.where(row == 0, halo_prev, pltpu.roll(x, 1, 0))
    x_next = jnp.where(row == rows - 1, halo_next, pltpu.roll(x, rows - 1, 0))
    y = w_ref[0:1, :] * x_prev + w_ref[1:2, :] * x + w_ref[2:3, :] * x_next + b_ref[...]
    y = y * jax.nn.sigmoid(y)

    def normed(mult):
        parts = []
        for i in range(y.shape[1] // head_dim):
            p = y[:, i * head_dim:(i + 1) * head_dim]
            parts.append(p * (lax.rsqrt(jnp.sum(p * p, axis=-1, keepdims=True) + 1e-6) * mult))
        return jnp.concatenate(parts, axis=1)

    if n_normed == 0:
        o_ref[...] = y
    else:
        @pl.when(cb < n_scaled)
        def _():
            o_ref[...] = normed(scale)

        @pl.when(jnp.logical_and(cb >= n_scaled, cb < n_normed))
        def _():
            o_ref[...] = normed(1.0)

        @pl.when(cb >= n_normed)
        def _():
            o_ref[...] = y


def _conv_silu(u, w, b, n_cols, bsz, lc, seq, n_scaled=0, n_normed=0, scale=1.0, head_dim=128):
    n_ctx_c, n_lat_c = _scan_geometry(bsz, lc, seq)
    rows = u.shape[0]
    n_rb = rows // SCAN_CHUNK
    sub = SCAN_CHUNK // 8
    n_halo = rows // 8
    return pl.pallas_call(
        functools.partial(_conv_kernel, n_ctx_c=n_ctx_c, n_lat_c=n_lat_c, bsz=bsz, n_scaled=n_scaled,
                          n_normed=n_normed, scale=scale, head_dim=head_dim),
        grid=(n_rb, n_cols // CONV_COLS),
        in_specs=[pl.BlockSpec((SCAN_CHUNK, CONV_COLS), lambda r, c: (r, c)),
                  pl.BlockSpec((8, CONV_COLS), lambda r, c: (jnp.maximum(r * sub - 1, 0), c)),
                  pl.BlockSpec((8, CONV_COLS), lambda r, c: (jnp.minimum(r * sub + sub, n_halo - 1), c)),
                  pl.BlockSpec((CONV_W, CONV_COLS), lambda r, c: (0, c)),
                  pl.BlockSpec((1, CONV_COLS), lambda r, c: (0, c))],
        out_specs=pl.BlockSpec((SCAN_CHUNK, CONV_COLS), lambda r, c: (r, c)),
        out_shape=jax.ShapeDtypeStruct((rows, n_cols), F32),
        compiler_params=_params("parallel", "parallel"),
        name="conv_silu",
    )(u, u, u, w, b)


GDN_SUB = 64


def _mm_exact(a, b):
    return jnp.dot(a, b, preferred_element_type=F32, precision=lax.Precision.HIGHEST)


def _unit_tri_inverse(a, eye):
    p = _mm_exact(a, a)
    inv = (eye - a) + _mm_exact(eye - a, p)
    steps = int(math.log2(GDN_SUB)) - 2
    for _ in range(steps):
        p = _mm_exact(p, p)
        inv = inv + _mm_exact(inv, p)
    return inv


def _softplus(x):
    return jnp.maximum(x, 0.0) + jnp.log(1.0 + jnp.exp(-jnp.abs(x)))


def _gdn_kernel(*refs, reverse, n_r, hd):
    if reverse:
        q_ref, k_ref, v_ref, tail_ref, prow_ref, arow_ref, o_ref, s_ref = refs
    else:
        q_ref, k_ref, v_ref, tail_ref, prow_ref, arow_ref, yb_ref, z_ref, ng_ref, o_ref, s_ref = refs
    g, c = pl.program_id(1), pl.program_id(2)
    qn = SCAN_CHUNK
    n_sub = qn // GDN_SUB
    n_gate = tail_ref.shape[1] // 2

    @pl.when(c == 0)
    def _():
        s_ref[...] = jnp.zeros_like(s_ref)

    tail = tail_ref[...]
    lane = lax.broadcasted_iota(jnp.int32, (1, 2 * n_gate), 1)
    gates = jnp.where(lane < n_gate, jax.nn.sigmoid(tail), -jnp.exp(arow_ref[...]) * _softplus(tail + prow_ref[...]))
    li = lax.broadcasted_iota(jnp.int32, (2 * n_gate, 2 * n_r), 0)
    ji = lax.broadcasted_iota(jnp.int32, (2 * n_gate, 2 * n_r), 1)
    col = jnp.where(ji < n_r, ji, n_gate + ji - n_r) + (n_gate // 2 if reverse else 0) + g * n_r
    gsel = _mm_exact(gates, (li == col).astype(F32))

    ti = lax.broadcasted_iota(jnp.int32, (qn, 1), 0)
    si = lax.broadcasted_iota(jnp.int32, (1, qn), 1)
    shift = int(math.log2(GDN_SUB))
    same = lax.shift_right_logical(ti, shift) == lax.shift_right_logical(si, shift)
    if reverse:
        incl, strict = same & (si >= ti), same & (si > ti)
    else:
        incl, strict = same & (si <= ti), same & (si < ti)
    eye = (ti == si).astype(F32)
    cs = incl.astype(F32)
    cum = _mm_exact(cs, gsel)
    cum_t = lax.dot_general(gsel, cs, (((0,), (1,)), ((), ())), preferred_element_type=F32,
                            precision=lax.Precision.HIGHEST)

    q, k = q_ref[...], k_ref[...]
    qb, kb = q.astype(BF16), k.astype(BF16)
    kk = _dot_nt(kb, kb)
    qk = _dot_nt(qb, kb)
    order = range(n_sub - 1, -1, -1) if reverse else range(n_sub)
    outs = []
    for r in range(n_r):
        beta = gsel[:, r:r + 1]
        cum_c = cum[:, n_r + r:n_r + r + 1]
        cum_r = cum_t[n_r + r:n_r + r + 1, :]
        decay = jnp.exp(jnp.where(incl, cum_c - cum_r, -jnp.inf))
        a = jnp.where(strict, beta * kk * decay, 0.0)
        inv = _unit_tri_inverse(a, eye)
        v = v_ref[:, r * hd:(r + 1) * hd]
        rhs = jnp.concatenate([v * beta, k * (beta * jnp.exp(cum_c))], axis=1)
        sol = _mm_exact(inv, rhs)
        u_sol, w_sol = sol[:, :hd], sol[:, hd:]
        state = s_ref[r]
        v_new, inter = [None] * n_sub, [None] * n_sub
        for i in order:
            sl = slice(i * GDN_SUB, (i + 1) * GDN_SUB)
            sb = state.astype(BF16)
            v_new[i] = u_sol[sl] - jnp.dot(w_sol[sl].astype(BF16), sb, preferred_element_type=F32)
            inter[i] = jnp.dot(qb[sl], sb, preferred_element_type=F32)
            end = i * GDN_SUB if reverse else (i + 1) * GDN_SUB - 1
            cum_end = cum_c[end:end + 1, :]
            to_end = jnp.exp(cum_end - cum_c[sl])
            state = jnp.exp(cum_end) * state + _dot_t((k[sl] * to_end).astype(BF16), v_new[i].astype(BF16))
        s_ref[r] = state
        v_new = jnp.concatenate(v_new, axis=0)
        inter = jnp.concatenate(inter, axis=0)
        y = jnp.dot((qk * decay).astype(BF16), v_new.astype(BF16), preferred_element_type=F32) + inter * jnp.exp(cum_c)
        if not reverse:
            y = y + yb_ref[:, r * hd:(r + 1) * hd]
            y = y * lax.rsqrt(jnp.mean(y * y, axis=-1, keepdims=True) + NORM_EPS) * ng_ref[...]
            z = z_ref[:, r * hd:(r + 1) * hd]
            y = y * (z * jax.nn.sigmoid(z))
        outs.append(y)
    o_ref[...] = jnp.concatenate(outs, axis=1).astype(o_ref.dtype)


def _gdn_mixer(u, tail, conv_w, dt_bias, a_log, norm_g, bsz, lc, seq, d_model):
    n_ctx_c, n_lat_c = _scan_geometry(bsz, lc, seq)
    n_steps = n_ctx_c + n_lat_c
    hd = GDN_HEAD_DIM
    k_heads = d_model // hd
    n_r = 2
    rows = u.shape[0]
    conv_ch = 4 * d_model
    nq = d_model // CONV_COLS
    qkv = _conv_silu(u, conv_w, jnp.zeros((1, conv_ch), F32), conv_ch, bsz, lc, seq, n_scaled=nq, n_normed=2 * nq,
                     scale=hd ** -0.5, head_dim=hd)
    n_gate = tail.shape[1] // 2
    prow = jnp.concatenate([jnp.zeros((1, n_gate), F32), dt_bias.reshape(1, n_gate)], axis=1)
    arow = jnp.concatenate([jnp.zeros((1, n_gate), F32), a_log.reshape(1, n_gate)], axis=1)
    y_b = None
    for reverse in (True, False):
        rb = functools.partial(_row_block, bsz=bsz, n_ctx_c=n_ctx_c, n_lat_c=n_lat_c, reverse=reverse)
        const = lambda b, g, c: (0, 0)
        in_specs = [pl.BlockSpec((SCAN_CHUNK, hd), lambda b, g, c, rb=rb: (rb(b, c), g)),
                    pl.BlockSpec((SCAN_CHUNK, hd), lambda b, g, c, rb=rb: (rb(b, c), k_heads + g)),
                    pl.BlockSpec((SCAN_CHUNK, n_r * hd), lambda b, g, c, rb=rb: (rb(b, c), k_heads + g)),
                    pl.BlockSpec((SCAN_CHUNK, 2 * n_gate), lambda b, g, c, rb=rb: (rb(b, c), 0)),
                    pl.BlockSpec((1, 2 * n_gate), const),
                    pl.BlockSpec((1, 2 * n_gate), const)]
        args = [qkv, qkv, qkv, tail, prow, arow]
        if not reverse:
            in_specs += [pl.BlockSpec((SCAN_CHUNK, n_r * hd), lambda b, g, c, rb=rb: (rb(b, c), g)),
                         pl.BlockSpec((SCAN_CHUNK, n_r * hd), lambda b, g, c, rb=rb: (rb(b, c), 2 * k_heads + g)),
                         pl.BlockSpec((1, hd), const)]
            args += [y_b, u, norm_g.reshape(1, hd)]
        out = pl.pallas_call(
            functools.partial(_gdn_kernel, reverse=reverse, n_r=n_r, hd=hd),
            grid=(bsz, k_heads, n_steps),
            in_specs=in_specs,
            out_specs=pl.BlockSpec((SCAN_CHUNK, n_r * hd), lambda b, g, c, rb=rb: (rb(b, c), g)),
            out_shape=jax.ShapeDtypeStruct((rows, k_heads * n_r * hd), F32 if reverse else BF16),
            scratch_shapes=[pltpu.VMEM((n_r, hd, hd), F32)],
            compiler_params=_params("parallel", "parallel", "arbitrary"),
            name="gdn_bwd" if reverse else "gdn_fwd",
        )(*args)
        if reverse:
            y_b = out
    return out


def _rmsnorm(x, g, eps=NORM_EPS):
    y = x * lax.rsqrt(jnp.mean(x * x, axis=-1, keepdims=True) + eps)
    return y * g


def _l2norm(x, eps=1e-6):
    return x * lax.rsqrt(jnp.sum(x * x, axis=-1, keepdims=True) + eps)


def _dwconv(u, w):
    return lax.conv_general_dilated(u, w[:, None, :], window_strides=(1,), padding=[(CONV_W // 2, CONV_W // 2)],
                                    dimension_numbers=('NWC', 'WIO', 'NWC'), feature_group_count=u.shape[-1])


def _conv_split(u, w, lc):
    return jnp.concatenate([_dwconv(u[:, :lc], w), _dwconv(u[:, lc:], w)], axis=1)


def _rev(t, lc):
    return jnp.concatenate([jnp.flip(t[:, :lc], 1), jnp.flip(t[:, lc:], 1)], axis=1)


def _to_chunks(t):
    b, n = t.shape[:2]
    return jnp.moveaxis(t.reshape((b, n // CHUNK, CHUNK) + t.shape[2:]), 1, 0)


def _from_chunks(t):
    nc, b, q = t.shape[:3]
    return jnp.moveaxis(t, 0, 1).reshape((b, nc * q) + t.shape[3:])


def _chunk_masks():
    idx = jnp.arange(CHUNK)
    return idx[:, None] >= idx[None, :], idx[:, None] > idx[None, :]


def _scalar_decay_scan(q, k, v, log_a):
    bsz, _, g, n = q.shape
    r, p = v.shape[-2:]
    incl, _ = _chunk_masks()

    def body(s, xs):
        qc, kc, vc, la = xs
        cum = jnp.cumsum(la, axis=1)
        cum_t = jnp.moveaxis(cum, 1, -1)
        seg = cum_t[..., :, None] - cum_t[..., None, :]
        scores = jnp.einsum('btgn,bsgn->bgts', qc, kc)
        attn = scores[:, :, None] * jnp.exp(jnp.where(incl, seg, -jnp.inf))
        y = jnp.einsum('bgrts,bsgrp->btgrp', attn, vc)
        y = y + jnp.einsum('btgn,bgrnp->btgrp', qc, s) * jnp.exp(cum)[..., None]
        to_end = jnp.exp(cum[:, -1:] - cum)
        s = jnp.exp(cum[:, -1])[..., None, None] * s + jnp.einsum('bsgn,bsgr,bsgrp->bgrnp', kc, to_end, vc)
        return s, y

    s0 = jnp.zeros((bsz, g, r, n, p), F32)
    _, y = lax.scan(body, s0, tuple(_to_chunks(t) for t in (q, k, v, log_a)))
    return _from_chunks(y)


def _vector_decay_scan(q, k, v, log_f):
    bsz, _, h, kd = q.shape
    vd = v.shape[-1]
    incl, _ = _chunk_masks()

    def body(s, xs):
        qc, kc, vc, lf = xs
        cum = jnp.cumsum(lf, axis=1)
        seg = cum[:, :, None] - cum[:, None, :]
        decay = jnp.exp(jnp.where(incl[:, :, None, None], seg, -jnp.inf))
        attn = jnp.einsum('bthk,bshk,btshk->bhts', qc, kc, decay)
        y = jnp.einsum('bhts,bshv->bthv', attn, vc)
        y = y + jnp.einsum('bthk,bhkv->bthv', qc * jnp.exp(cum), s)
        s = jnp.exp(cum[:, -1])[..., None] * s + jnp.einsum('bshk,bshv->bhkv', kc * jnp.exp(cum[:, -1:] - cum), vc)
        return s, y

    s0 = jnp.zeros((bsz, h, kd, vd), F32)
    _, y = lax.scan(body, s0, tuple(_to_chunks(t) for t in (q, k, v, log_f)))
    return _from_chunks(y)


def _delta_scan(q, k, v, beta, log_a):
    bsz, _, g, kd = q.shape
    r, vd = v.shape[-2:]
    incl, strict = _chunk_masks()

    def body(s, xs):
        qc, kc, vc, bc, la = xs
        cum = jnp.cumsum(la, axis=1)
        cum_t = jnp.moveaxis(cum, 1, -1)
        seg = cum_t[..., :, None] - cum_t[..., None, :]
        beta_t = jnp.moveaxis(bc, 1, -1)
        kk = jnp.einsum('btgk,bsgk->bgts', kc, kc)
        lower = beta_t[..., :, None] * kk[:, :, None] * jnp.exp(jnp.where(strict, seg, -jnp.inf))
        rhs_v = jnp.moveaxis(vc * bc[..., None], 1, 3)
        rhs_k = jnp.moveaxis(kc[:, :, :, None, :] * (bc * jnp.exp(cum))[..., None], 1, 3)
        sol = lax.linalg.triangular_solve(lower, jnp.concatenate([rhs_v, rhs_k], axis=-1),
                                          left_side=True, lower=True, unit_diagonal=True)
        u, w = sol[..., :vd], sol[..., vd:]
        v_new = u - jnp.einsum('bgrtk,bgrkv->bgrtv', w, s)
        qk = jnp.einsum('btgk,bsgk->bgts', qc, kc)
        attn = qk[:, :, None] * jnp.exp(jnp.where(incl, seg, -jnp.inf))
        y = jnp.einsum('bgrts,bgrsv->btgrv', attn, v_new)
        y = y + jnp.einsum('btgk,bgrkv->btgrv', qc, s) * jnp.exp(cum)[..., None]
        to_end = jnp.exp(cum_t[..., -1:] - cum_t)
        s = jnp.exp(cum_t[..., -1])[..., None, None] * s + jnp.einsum('bsgk,bgrs,bgrsv->bgrkv', kc, to_end, v_new)
        return s, y

    s0 = jnp.zeros((bsz, g, r, kd, vd), F32)
    _, y = lax.scan(body, s0, tuple(_to_chunks(t) for t in (q, k, v, beta, log_a)))
    return _from_chunks(y)


def _rope_tables(rows, half):
    pos = np.arange(rows * GRID_W)
    inv_freq = np.float32(ROPE_BASE) ** (-(np.arange(0, half, 2, dtype=np.float32) / np.float32(half)))
    out = []
    for p in ((pos // GRID_W).astype(np.float32), (pos % GRID_W).astype(np.float32)):
        ang = (p[:, None] * inv_freq.astype(np.float32)).astype(np.float32).astype(np.float64)
        out += [np.cos(ang).astype(np.float32), np.sin(ang).astype(np.float32)]
    return out


def _rope_2d(t, rows):
    half = t.shape[-1] // 2
    cos_r, sin_r, cos_c, sin_c = (jnp.asarray(a)[:, None, :] for a in _rope_tables(rows, half))

    def rot(u, cos, sin):
        u1, u2 = jnp.split(u, 2, axis=-1)
        return jnp.concatenate([u1 * cos - u2 * sin, u2 * cos + u1 * sin], axis=-1)

    return jnp.concatenate([rot(t[..., :half], cos_r, sin_r), rot(t[..., half:], cos_c, sin_c)], axis=-1)


def _ssd_core(u, lc, conv_w, conv_b, dt_bias, a_log, d_skip, norm_g, start):
    bsz, t = u.shape[:2]
    d_inner = norm_g.shape[0]
    heads = d_inner // SSD_HEAD_DIM
    conv_ch = d_inner + 2 * SSD_GROUPS * SSD_STATE
    z, xbc, dt = jnp.split(u, [d_inner, d_inner + conv_ch], axis=-1)
    xbc = jax.nn.silu(_conv_split(xbc, conv_w, lc) + conv_b)
    xs, bm, cm = jnp.split(xbc, [d_inner, d_inner + SSD_GROUPS * SSD_STATE], axis=-1)
    r = heads // SSD_GROUPS
    xs = xs.reshape(bsz, t, SSD_GROUPS, r, SSD_HEAD_DIM)
    bm = bm.reshape(bsz, t, SSD_GROUPS, SSD_STATE)
    cm = cm.reshape(bsz, t, SSD_GROUPS, SSD_STATE)
    dt = jax.nn.softplus(dt.reshape(bsz, t, 2, heads) + dt_bias)
    log_a = -jnp.exp(a_log) * dt
    grp = lambda a: a.reshape(bsz, t, SSD_GROUPS, r)
    y = _scalar_decay_scan(cm, bm, xs * grp(dt[:, :, 0])[..., None], grp(log_a[:, :, 0]))
    y = y + _rev(_scalar_decay_scan(_rev(cm, lc), _rev(bm, lc), _rev(xs * grp(dt[:, :, 1])[..., None], lc),
                                    _rev(grp(log_a[:, :, 1]), lc)), lc)
    y = y + d_skip.reshape(SSD_GROUPS, r)[..., None] * xs
    n = t - start
    y = y.reshape(bsz, t, d_inner)[:, start:] * jax.nn.silu(z[:, start:])
    return _rmsnorm(y.reshape(bsz, n, SSD_GROUPS, -1), norm_g.reshape(SSD_GROUPS, -1)).reshape(bsz, n, d_inner)


def _ret_core(u, lc, log_decay, d_model, rows, start):
    bsz, t = u.shape[:2]
    dv = 2 * d_model
    qk_dim = d_model // RET_HEADS
    v_dim = 2 * qk_dim
    q, k, v, g = jnp.split(u, [d_model, 2 * d_model, 2 * d_model + dv], axis=-1)
    q = q.reshape(bsz, t, RET_HEADS, qk_dim)
    k = k.reshape(bsz, t, RET_HEADS, qk_dim) * qk_dim ** -0.5
    q = jnp.concatenate([q[:, :lc], _rope_2d(q[:, lc:], rows)], axis=1)
    k = jnp.concatenate([k[:, :lc], _rope_2d(k[:, lc:], rows)], axis=1)
    v = v.reshape(bsz, t, RET_HEADS, 1, v_dim)
    ld_f = jnp.broadcast_to(log_decay[0][:, None], (bsz, t, RET_HEADS, 1))
    ld_b = jnp.broadcast_to(log_decay[1][:, None], (bsz, t, RET_HEADS, 1))
    y = _scalar_decay_scan(q, k, v, ld_f)
    y = y + _rev(_scalar_decay_scan(_rev(q, lc), _rev(k, lc), _rev(v, lc), ld_b), lc)
    n = t - start
    y = y[:, start:].reshape(bsz, n, RET_HEADS, v_dim)
    mu = jnp.mean(y, axis=-1, keepdims=True)
    var = jnp.mean(jnp.square(y - mu), axis=-1, keepdims=True)
    y = ((y - mu) * lax.rsqrt(var + NORM_EPS)).reshape(bsz, n, dv)
    return y * jax.nn.silu(g[:, start:])


def _lower_bound(lb_logits, layer):
    p = jax.nn.softmax(lb_logits.astype(F32), axis=0)
    return jnp.cumsum(p, axis=0)[layer] - p[0]


def _hgrn_core(u, lc, lb, norm_g, start):
    bsz, t = u.shape[:2]
    d_model = norm_g.shape[0]
    heads = d_model // HGRN_EXPAND
    q, f_f, f_b, i, g = jnp.split(u, 5, axis=-1)
    shp = (bsz, t, heads, HGRN_EXPAND)
    q, i = q.reshape(shp), i.reshape(shp)
    lb = lb.reshape(heads, HGRN_EXPAND)

    def gates(f):
        f = f.reshape(shp)
        log_f = jnp.logaddexp(jnp.log(lb), jnp.log1p(-lb) + jax.nn.log_sigmoid(f))
        return log_f, (1 - lb) * jax.nn.sigmoid(-f)

    lf_f, k_f = gates(f_f)
    lf_b, k_b = gates(f_b)
    y = _vector_decay_scan(q, k_f, i, lf_f)
    y = y + _rev(_vector_decay_scan(_rev(q, lc), _rev(k_b, lc), _rev(i, lc), _rev(lf_b, lc)), lc)
    n = t - start
    y = _rmsnorm(y[:, start:], norm_g.reshape(heads, HGRN_EXPAND))
    return (y * jax.nn.silu(g[:, start:].reshape(bsz, n, heads, HGRN_EXPAND))).reshape(bsz, n, d_model)


def _gdn_core(u, lc, conv_w, dt_bias, a_log, norm_g, d_model, start):
    bsz, t = u.shape[:2]
    k_heads = d_model // GDN_HEAD_DIM
    v_heads = 2 * k_heads
    dk, dv = d_model, 2 * d_model
    conv_ch = 2 * dk + dv
    qkv, z, bt, a = jnp.split(u, [conv_ch, conv_ch + dv, conv_ch + dv + 2 * v_heads], axis=-1)
    qkv = jax.nn.silu(_conv_split(qkv, conv_w, lc))
    q, k, v = jnp.split(qkv, [dk, 2 * dk], axis=-1)
    r = v_heads // k_heads
    q = _l2norm(q.reshape(bsz, t, k_heads, GDN_HEAD_DIM)) * GDN_HEAD_DIM ** -0.5
    k = _l2norm(k.reshape(bsz, t, k_heads, GDN_HEAD_DIM))
    v = v.reshape(bsz, t, k_heads, r, GDN_HEAD_DIM)
    beta = jax.nn.sigmoid(bt.reshape(bsz, t, 2, k_heads, r))
    log_a = -jnp.exp(a_log).reshape(2, k_heads, r) * jax.nn.softplus(
        a.reshape(bsz, t, 2, k_heads, r) + dt_bias.reshape(2, k_heads, r))
    y = _delta_scan(q, k, v, beta[:, :, 0], log_a[:, :, 0])
    y = y + _rev(_delta_scan(_rev(q, lc), _rev(k, lc), _rev(v, lc), _rev(beta[:, :, 1], lc),
                             _rev(log_a[:, :, 1], lc)), lc)
    n = t - start
    y = y[:, start:].reshape(bsz, n, v_heads, GDN_HEAD_DIM)
    y = _rmsnorm(y, norm_g) * jax.nn.silu(z[:, start:].reshape(bsz, n, v_heads, GDN_HEAD_DIM))
    return y.reshape(bsz, n, dv)


def _rows_to_seq(u, bsz, lc):
    n_ctx = bsz * lc
    return jnp.concatenate([u[:n_ctx].reshape(bsz, lc, -1), u[n_ctx:].reshape(bsz, -1, u.shape[-1])], axis=1)


def _seq_to_rows(y, lc, start):
    bsz = y.shape[0]
    if start:
        return y.reshape(bsz * y.shape[1], -1)
    return jnp.concatenate([y[:, :lc].reshape(bsz * lc, -1), y[:, lc:].reshape(-1, y.shape[-1])], axis=0)


def kernel(x, c, ctx, c_ctx, ada_w, ada_b, norm_g, mlp_w1, mlp_w2, final_g, ssd_w_in, ssd_conv_w, ssd_conv_b,
           ssd_dt_bias, ssd_a_log, ssd_d, ssd_norm_g, ssd_w_out, ret_w_in, ret_log_decay, ret_w_out, hgrn_w_in,
           hgrn_lb_logits, hgrn_norm_g, hgrn_w_out, gdn_w_in, gdn_conv_w, gdn_dt_bias, gdn_a_log, gdn_norm_g,
           gdn_w_out):
    bsz, seq, d = x.shape
    lc = ctx.shape[1]
    depth = ada_w.shape[0]
    rows_grid = seq // GRID_W
    n_ctx = bsz * lc
    assert lc % ROW_TILE == 0 or ROW_TILE % lc == 0 and n_ctx % ROW_TILE == 0
    assert seq % ROW_TILE == 0 and bsz + 1 <= 8

    cond_pad = jnp.concatenate([c, c_ctx[None], jnp.zeros((8 - bsz - 1, d), F32)], axis=0)
    mod = _ada_mod(cond_pad, ada_w, ada_b)

    tile_row = [bsz] * (n_ctx // ROW_TILE) + [b for b in range(bsz) for _ in range(seq // ROW_TILE)]
    tile_row = jnp.asarray(tile_row, jnp.int32)
    n_ctx_tiles = n_ctx // ROW_TILE

    xr = jnp.concatenate([ctx.reshape(n_ctx, d), x.reshape(bsz * seq, d)], axis=0)

    for i in range(depth):
        mixer, occ = i % 4, i // 4
        keep_ctx = i < depth - 1
        start = 0 if keep_ctx else lc
        mod_t = mod[i][tile_row][:, None, :]
        g0, g1 = norm_g[i, 0][None], norm_g[i, 1][None]
        if mixer == 0:
            w_in, w_out = ssd_w_in[occ], ssd_w_out[occ]
        elif mixer == 1:
            w_in, w_out = ret_w_in[occ], ret_w_out[occ]
        elif mixer == 2:
            w_in, w_out = hgrn_w_in[occ], hgrn_w_out[occ]
        else:
            w_in, w_out = gdn_w_in[occ], gdn_w_out[occ]
        n_in = w_in.shape[1]
        n_main = (n_in // 1024) * 1024 if n_in % 1024 else n_in
        w_in = w_in.astype(BF16)
        u = _ln_mm(xr, g0, mod_t, 0, 1, w_in[:, :n_main])
        tail = _ln_mm(xr, g0, mod_t, 0, 1, w_in[:, n_main:]) if n_main != n_in else None
        first_row = 0 if keep_ctx else n_ctx
        if mixer == 1:
            yr = _ret_mixer(u, ret_log_decay[occ], bsz, lc, seq, d)[first_row:]
        elif mixer == 3:
            yr = _gdn_mixer(u, tail, gdn_conv_w[occ], gdn_dt_bias[occ], gdn_a_log[occ], gdn_norm_g[occ], bsz, lc,
                            seq, d)[first_row:]
        else:
            if tail is not None:
                u = jnp.concatenate([u, tail], axis=1)
            useq = _rows_to_seq(u, bsz, lc)
            if mixer == 0:
                y = _ssd_core(useq, lc, ssd_conv_w[occ], ssd_conv_b[occ], ssd_dt_bias[occ], ssd_a_log[occ],
                              ssd_d[occ], ssd_norm_g[occ], start)
            elif mixer == 2:
                y = _hgrn_core(useq, lc, _lower_bound(hgrn_lb_logits, i), hgrn_norm_g[occ], start)
            else:
                y = _gdn_core(useq, lc, gdn_conv_w[occ], gdn_dt_bias[occ], gdn_a_log[occ], gdn_norm_g[occ], d,
                              start)
            yr = _seq_to_rows(y, lc, start).astype(BF16)
        if not keep_ctx:
            xr = xr[n_ctx:]
            mod_t = mod_t[n_ctx_tiles:]
        xr = _out_proj(yr, w_out.astype(BF16), xr, mod_t, 2)
        xr = _mlp(xr, g1, mod_t, mlp_w1[i].astype(BF16), mlp_w2[i].astype(BF16), final_g[None], final=not keep_ctx)
    return xr.reshape(bsz, seq, d)
```

```python
import functools
import math

import jax
import jax.numpy as jnp
import numpy as np
from jax import lax
from jax.experimental import pallas as pl
from jax.experimental.pallas import tpu as pltpu

F32 = jnp.float32
BF16 = jnp.bfloat16

GRID_W = 64
CHUNK = 64
CONV_W = 3
NORM_EPS = 1e-6
ROPE_BASE = 10000.0
SSD_HEAD_DIM = 64
SSD_GROUPS = 8
SSD_STATE = 128
RET_HEADS = 8
HGRN_EXPAND = 128
GDN_HEAD_DIM = 128

ROW_TILE = 512
VMEM_LIMIT = 56 * 1024 * 1024


def _params(*sem):
    return pltpu.CompilerParams(dimension_semantics=sem, vmem_limit_bytes=VMEM_LIMIT)


def _col_tile(n, cap=1536):
    best = 128
    for t in range(128, cap + 1, 128):
        if n % t == 0:
            best = t
    return best


def _ada_kernel(c_ref, w_ref, b_ref, o_ref):
    c = c_ref[...]
    c = (c * jax.nn.sigmoid(c)).astype(BF16)
    o_ref[0] = jnp.dot(c, w_ref[0].astype(BF16), preferred_element_type=F32) + b_ref[0]


def _ada_mod(cond_pad, ada_w, ada_b):
    depth, d, n = ada_w.shape
    tn = 1024
    return pl.pallas_call(
        _ada_kernel,
        grid=(depth, n // tn),
        in_specs=[pl.BlockSpec((8, d), lambda l, j: (0, 0)),
                  pl.BlockSpec((1, d, tn), lambda l, j: (l, 0, j)),
                  pl.BlockSpec((1, 1, tn), lambda l, j: (l, 0, j))],
        out_specs=pl.BlockSpec((1, 8, tn), lambda l, j: (l, 0, j)),
        out_shape=jax.ShapeDtypeStruct((depth, 8, n), F32),
        compiler_params=_params("parallel", "parallel"),
        name="ada_mod",
    )(cond_pad, ada_w, ada_b.reshape(depth, 1, n))


def _adaln_rows(x, g, sh, sc):
    y = x * lax.rsqrt(jnp.mean(x * x, axis=-1, keepdims=True) + NORM_EPS)
    return y * g * (1.0 + sc) + sh


def _ln_mm_kernel(x_ref, g_ref, sh_ref, sc_ref, w_ref, o_ref, h_ref):
    @pl.when(pl.program_id(1) == 0)
    def _():
        h_ref[...] = _adaln_rows(x_ref[...], g_ref[...], sh_ref[0], sc_ref[0]).astype(BF16)

    o_ref[...] = jnp.dot(h_ref[...], w_ref[...], preferred_element_type=F32).astype(o_ref.dtype)


def _ln_mm(x, g, mod_t, sh_col, sc_col, w, out_dtype=F32):
    m, d = x.shape
    n = w.shape[1]
    tm, tn = ROW_TILE, _col_tile(n)
    return pl.pallas_call(
        _ln_mm_kernel,
        grid=(m // tm, n // tn),
        in_specs=[pl.BlockSpec((tm, d), lambda i, j: (i, 0)),
                  pl.BlockSpec((1, d), lambda i, j: (0, 0)),
                  pl.BlockSpec((1, 1, d), lambda i, j: (i, 0, sh_col)),
                  pl.BlockSpec((1, 1, d), lambda i, j: (i, 0, sc_col)),
                  pl.BlockSpec((d, tn), lambda i, j: (0, j))],
        out_specs=pl.BlockSpec((tm, tn), lambda i, j: (i, j)),
        out_shape=jax.ShapeDtypeStruct((m, n), out_dtype),
        scratch_shapes=[pltpu.VMEM((tm, d), BF16)],
        compiler_params=_params("parallel", "arbitrary"),
        name="adaln_in_proj",
    )(x, g, mod_t, mod_t, w)


def _out_kernel(y_ref, w_ref, x_ref, gate_ref, o_ref):
    o_ref[...] = x_ref[...] + gate_ref[0] * jnp.dot(y_ref[...], w_ref[...], preferred_element_type=F32)


def _out_proj(y, w, x, mod_t, gate_col):
    m, k = y.shape
    d = w.shape[1]
    tm, tn = ROW_TILE, 512
    return pl.pallas_call(
        _out_kernel,
        grid=(m // tm, d // tn),
        in_specs=[pl.BlockSpec((tm, k), lambda i, j: (i, 0)),
                  pl.BlockSpec((k, tn), lambda i, j: (0, j)),
                  pl.BlockSpec((tm, tn), lambda i, j: (i, j)),
                  pl.BlockSpec((1, 1, tn), lambda i, j: (i, 0, gate_col * (d // tn) + j))],
        out_specs=pl.BlockSpec((tm, tn), lambda i, j: (i, j)),
        out_shape=jax.ShapeDtypeStruct((m, d), F32),
        compiler_params=_params("parallel", "arbitrary"),
        name="out_proj",
    )(y, w, x, mod_t)


def _mlp_kernel(x_ref, g_ref, sh_ref, sc_ref, gate_ref, w1_ref, w2_ref, fg_ref, o_ref, h_ref, acc_ref, *, final):
    f = pl.program_id(1)

    @pl.when(f == 0)
    def _():
        h_ref[...] = _adaln_rows(x_ref[...], g_ref[...], sh_ref[0], sc_ref[0]).astype(BF16)
        acc_ref[...] = jnp.zeros_like(acc_ref)

    a = jnp.dot(h_ref[...], w1_ref[...], preferred_element_type=F32)
    a = jnp.square(jnp.maximum(a, 0.0)).astype(BF16)
    acc_ref[...] += jnp.dot(a, w2_ref[...], preferred_element_type=F32)

    @pl.when(f == pl.num_programs(1) - 1)
    def _():
        out = x_ref[...] + gate_ref[0] * acc_ref[...]
        if final:
            out = out * lax.rsqrt(jnp.mean(out * out, axis=-1, keepdims=True) + NORM_EPS) * fg_ref[...]
        o_ref[...] = out


def _mlp(x, g, mod_t, w1, w2, final_g, final):
    m, d = x.shape
    ff = w1.shape[1]
    tm, tf = ROW_TILE, 512
    return pl.pallas_call(
        functools.partial(_mlp_kernel, final=final),
        grid=(m // tm, ff // tf),
        in_specs=[pl.BlockSpec((tm, d), lambda i, f: (i, 0)),
                  pl.BlockSpec((1, d), lambda i, f: (0, 0)),
                  pl.BlockSpec((1, 1, d), lambda i, f: (i, 0, 3)),
                  pl.BlockSpec((1, 1, d), lambda i, f: (i, 0, 4)),
                  pl.BlockSpec((1, 1, d), lambda i, f: (i, 0, 5)),
                  pl.BlockSpec((d, tf), lambda i, f: (0, f)),
                  pl.BlockSpec((tf, d), lambda i, f: (f, 0)),
                  pl.BlockSpec((1, d), lambda i, f: (0, 0))],
        out_specs=pl.BlockSpec((tm, d), lambda i, f: (i, 0)),
        out_shape=jax.ShapeDtypeStruct((m, d), F32),
        scratch_shapes=[pltpu.VMEM((tm, d), BF16), pltpu.VMEM((tm, d), F32)],
        compiler_params=_params("parallel", "arbitrary"),
        name="adaln_mlp",
    )(x, g, mod_t, mod_t, mod_t, w1, w2, final_g)


SCAN_CHUNK = 256


def _scan_geometry(bsz, lc, seq):
    assert lc % SCAN_CHUNK == 0 and seq % SCAN_CHUNK == 0
    return lc // SCAN_CHUNK, seq // SCAN_CHUNK


def _seg_chunk(c, n_ctx_c, n_lat_c, reverse):
    if reverse:
        return c >= n_ctx_c, jnp.where(c < n_ctx_c, n_ctx_c - 1 - c, n_lat_c - 1 - (c - n_ctx_c))
    return c >= n_ctx_c, jnp.where(c < n_ctx_c, c, c - n_ctx_c)


def _row_block(b, c, bsz, n_ctx_c, n_lat_c, reverse):
    is_lat, j = _seg_chunk(c, n_ctx_c, n_lat_c, reverse)
    return jnp.where(is_lat, bsz * n_ctx_c + b * n_lat_c + j, b * n_ctx_c + j)


def _time_iotas(q):
    t = lax.broadcasted_iota(jnp.int32, (q, 1), 0).astype(F32)
    s = lax.broadcasted_iota(jnp.int32, (1, q), 1).astype(F32)
    return t, s


def _dot_t(a, b):
    return lax.dot_general(a, b, (((0,), (0,)), ((), ())), preferred_element_type=F32)


def _dot_nt(a, b):
    return lax.dot_general(a, b, (((1,), (1,)), ((), ())), preferred_element_type=F32)


def _ret_kernel(*refs, reverse, n_ctx_c, n_heads, k_scale):
    if reverse:
        ld_ref, q_ref, k_ref, v_ref, cos_ref, sin_ref, o_ref, s_ref = refs
    else:
        ld_ref, q_ref, k_ref, v_ref, cos_ref, sin_ref, yb_ref, g_ref, o_ref, s_ref = refs
    h, c = pl.program_id(1), pl.program_id(2)
    qn = SCAN_CHUNK

    @pl.when(c == 0)
    def _():
        s_ref[...] = jnp.zeros_like(s_ref)

    lg = ld_ref[(n_heads if reverse else 0) + h]
    t, s = _time_iotas(qn)
    is_lat = c >= n_ctx_c
    cos = jnp.where(is_lat, cos_ref[...], 1.0)
    sin = jnp.where(is_lat, sin_ref[...], 0.0)

    def rope(x):
        half = x.shape[1] // 2
        swapped = jnp.concatenate([pltpu.roll(x[:, :half], half // 2, 1), pltpu.roll(x[:, half:], half // 2, 1)], axis=1)
        return x * cos + swapped * sin

    q = rope(q_ref[...]).astype(BF16)
    k = rope(k_ref[...]) * k_scale
    v = v_ref[...].astype(BF16)
    if reverse:
        dmat = jnp.where(s >= t, jnp.exp((s - t) * lg), 0.0)
        q_dec, k_dec = jnp.exp((qn - t) * lg), jnp.exp(t * lg)
    else:
        dmat = jnp.where(t >= s, jnp.exp((t - s) * lg), 0.0)
        q_dec, k_dec = jnp.exp((t + 1.0) * lg), jnp.exp((qn - 1.0 - t) * lg)
    attn = (_dot_nt(q, k.astype(BF16)) * dmat).astype(BF16)
    state = s_ref[...]
    y = jnp.dot(attn, v, preferred_element_type=F32)
    y = y + jnp.dot(q, state.astype(BF16), preferred_element_type=F32) * q_dec
    chunk_dec = jnp.exp(jnp.full((1, 1), qn, F32) * lg)
    s_ref[...] = state * chunk_dec + _dot_t((k * k_dec).astype(BF16), v)
    if reverse:
        o_ref[...] = y
    else:
        y = y + yb_ref[...]
        mu = jnp.mean(y, axis=-1, keepdims=True)
        yc = y - mu
        var = jnp.mean(yc * yc, axis=-1, keepdims=True)
        g = g_ref[...]
        o_ref[...] = (yc * lax.rsqrt(var + NORM_EPS) * (g * jax.nn.sigmoid(g))).astype(o_ref.dtype)


def _ret_mixer(u, log_decay, bsz, lc, seq, d_model):
    n_ctx_c, n_lat_c = _scan_geometry(bsz, lc, seq)
    n_steps = n_ctx_c + n_lat_c
    qk_dim, v_dim = d_model // RET_HEADS, 2 * d_model // RET_HEADS
    rows = u.shape[0]
    cos_r, sin_r, cos_c, sin_c = _rope_tables(seq // GRID_W, qk_dim // 2)
    cos_t = jnp.asarray(np.concatenate([cos_r, cos_r, cos_c, cos_c], axis=1))
    sin_t = jnp.asarray(np.concatenate([-sin_r, sin_r, -sin_c, sin_c], axis=1))
    ld = log_decay.reshape(-1).astype(F32)
    y_b = None
    for reverse in (True, False):
        rb = functools.partial(_row_block, bsz=bsz, n_ctx_c=n_ctx_c, n_lat_c=n_lat_c, reverse=reverse)

        def tab(b, h, c, reverse=reverse):
            is_lat, j = _seg_chunk(c, n_ctx_c, n_lat_c, reverse)
            return (jnp.where(is_lat, j, 0), 0)

        in_specs = [pl.BlockSpec(memory_space=pltpu.SMEM),
                    pl.BlockSpec((SCAN_CHUNK, qk_dim), lambda b, h, c, rb=rb: (rb(b, c), h)),
                    pl.BlockSpec((SCAN_CHUNK, qk_dim), lambda b, h, c, rb=rb: (rb(b, c), RET_HEADS + h)),
                    pl.BlockSpec((SCAN_CHUNK, v_dim), lambda b, h, c, rb=rb: (rb(b, c), RET_HEADS + h)),
                    pl.BlockSpec((SCAN_CHUNK, qk_dim), tab),
                    pl.BlockSpec((SCAN_CHUNK, qk_dim), tab)]
        args = [ld, u, u, u, cos_t, sin_t]
        if not reverse:
            in_specs += [pl.BlockSpec((SCAN_CHUNK, v_dim), lambda b, h, c, rb=rb: (rb(b, c), h)),
                         pl.BlockSpec((SCAN_CHUNK, v_dim), lambda b, h, c, rb=rb: (rb(b, c), 2 * RET_HEADS + h))]
            args += [y_b, u]
        out = pl.pallas_call(
            functools.partial(_ret_kernel, reverse=reverse, n_ctx_c=n_ctx_c, n_heads=RET_HEADS, k_scale=qk_dim ** -0.5),
            grid=(bsz, RET_HEADS, n_steps),
            in_specs=in_specs,
            out_specs=pl.BlockSpec((SCAN_CHUNK, v_dim), lambda b, h, c, rb=rb: (rb(b, c), h)),
            out_shape=jax.ShapeDtypeStruct((rows, RET_HEADS * v_dim), F32 if reverse else BF16),
            scratch_shapes=[pltpu.VMEM((qk_dim, v_dim), F32)],
            compiler_params=_params("parallel", "parallel", "arbitrary"),
            name="retention_bwd" if reverse else "retention_fwd",
        )(*args)
        if reverse:
            y_b = out
    return out


CONV_COLS = 512


def _conv_kernel(x_ref, prev_ref, next_ref, w_ref, b_ref, o_ref, *, n_ctx_c, n_lat_c, bsz, n_scaled, n_normed, scale,
                 head_dim):
    rb, cb = pl.program_id(0), pl.program_id(1)
    is_lat = rb >= bsz * n_ctx_c
    j = jnp.where(is_lat, (rb - bsz * n_ctx_c) % n_lat_c, rb % n_ctx_c)
    last = jnp.where(is_lat, n_lat_c - 1, n_ctx_c - 1)
    x = x_ref[...]
    rows = x.shape[0]
    row = lax.broadcasted_iota(jnp.int32, (rows, 1), 0)
    halo_prev = jnp.where(j > 0, prev_ref[7:8, :], 0.0)
    halo_next = jnp.where(j < last, next_ref[0:1, :], 0.0)
    x_prev = jnp.where(row == 0, halo_prev, pltpu.roll(x, 1, 0))
    x_next = jnp.where(row == rows - 1, halo_next, pltpu.roll(x, rows - 1, 0))
    y = w_ref[0:1, :] * x_prev + w_ref[1:2, :] * x + w_ref[2:3, :] * x_next + b_ref[...]
    y = y * jax.nn.sigmoid(y)

    def normed(mult):
        parts = []
        for i in range(y.shape[1] // head_dim):
            p = y[:, i * head_dim:(i + 1) * head_dim]
            parts.append(p * (lax.rsqrt(jnp.sum(p * p, axis=-1, keepdims=True) + 1e-6) * mult))
        return jnp.concatenate(parts, axis=1)

    if n_normed == 0:
        o_ref[...] = y
    else:
        @pl.when(cb < n_scaled)
        def _():
            o_ref[...] = normed(scale)

        @pl.when(jnp.logical_and(cb >= n_scaled, cb < n_normed))
        def _():
            o_ref[...] = normed(1.0)

        @pl.when(cb >= n_normed)
        def _():
            o_ref[...] = y


def _conv_silu(u, w, b, n_cols, bsz, lc, seq, n_scaled=0, n_normed=0, scale=1.0, head_dim=128):
    n_ctx_c, n_lat_c = _scan_geometry(bsz, lc, seq)
    rows = u.shape[0]
    n_rb = rows // SCAN_CHUNK
    sub = SCAN_CHUNK // 8
    n_halo = rows // 8
    return pl.pallas_call(
        functools.partial(_conv_kernel, n_ctx_c=n_ctx_c, n_lat_c=n_lat_c, bsz=bsz, n_scaled=n_scaled,
                          n_normed=n_normed, scale=scale, head_dim=head_dim),
        grid=(n_rb, n_cols // CONV_COLS),
        in_specs=[pl.BlockSpec((SCAN_CHUNK, CONV_COLS), lambda r, c: (r, c)),
                  pl.BlockSpec((8, CONV_COLS), lambda r, c: (jnp.maximum(r * sub - 1, 0), c)),
                  pl.BlockSpec((8, CONV_COLS), lambda r, c: (jnp.minimum(r * sub + sub, n_halo - 1), c)),
                  pl.BlockSpec((CONV_W, CONV_COLS), lambda r, c: (0, c)),
                  pl.BlockSpec((1, CONV_COLS), lambda r, c: (0, c))],
        out_specs=pl.BlockSpec((SCAN_CHUNK, CONV_COLS), lambda r, c: (r, c)),
        out_shape=jax.ShapeDtypeStruct((rows, n_cols), F32),
        compiler_params=_params("parallel", "parallel"),
        name="conv_silu",
    )(u, u, u, w, b)


GDN_SUB = 64


def _mm_exact(a, b):
    return jnp.dot(a, b, preferred_element_type=F32, precision=lax.Precision.HIGHEST)


def _unit_tri_inverse(a, eye):
    p = _mm_exact(a, a)
    inv = (eye - a) + _mm_exact(eye - a, p)
    steps = int(math.log2(GDN_SUB)) - 2
    for _ in range(steps):
        p = _mm_exact(p, p)
        inv = inv + _mm_exact(inv, p)
    return inv


def _softplus(x):
    return jnp.maximum(x, 0.0) + jnp.log(1.0 + jnp.exp(-jnp.abs(x)))


def _gdn_kernel(*refs, reverse, n_r, hd):
    if reverse:
        q_ref, k_ref, v_ref, tail_ref, prow_ref, arow_ref, o_ref, s_ref = refs
    else:
        q_ref, k_ref, v_ref, tail_ref, prow_ref, arow_ref, yb_ref, z_ref, ng_ref, o_ref, s_ref = refs
    g, c = pl.program_id(1), pl.program_id(2)
    qn = SCAN_CHUNK
    n_sub = qn // GDN_SUB
    n_gate = tail_ref.shape[1] // 2

    @pl.when(c == 0)
    def _():
        s_ref[...] = jnp.zeros_like(s_ref)

    tail = tail_ref[...]
    lane = lax.broadcasted_iota(jnp.int32, (1, 2 * n_gate), 1)
    gates = jnp.where(lane < n_gate, jax.nn.sigmoid(tail), -jnp.exp(arow_ref[...]) * _softplus(tail + prow_ref[...]))
    li = lax.broadcasted_iota(jnp.int32, (2 * n_gate, 2 * n_r), 0)
    ji = lax.broadcasted_iota(jnp.int32, (2 * n_gate, 2 * n_r), 1)
    col = jnp.where(ji < n_r, ji, n_gate + ji - n_r) + (n_gate // 2 if reverse else 0) + g * n_r
    gsel = _mm_exact(gates, (li == col).astype(F32))

    ti = lax.broadcasted_iota(jnp.int32, (qn, 1), 0)
    si = lax.broadcasted_iota(jnp.int32, (1, qn), 1)
    shift = int(math.log2(GDN_SUB))
    same = lax.shift_right_logical(ti, shift) == lax.shift_right_logical(si, shift)
    if reverse:
        incl, strict = same & (si >= ti), same & (si > ti)
    else:
        incl, strict = same & (si <= ti), same & (si < ti)
    eye = (ti == si).astype(F32)
    cs = incl.astype(F32)
    cum = _mm_exact(cs, gsel)
    cum_t = lax.dot_general(gsel, cs, (((0,), (1,)), ((), ())), preferred_element_type=F32,
                            precision=lax.Precision.HIGHEST)

    q, k = q_ref[...], k_ref[...]
    qb, kb = q.astype(BF16), k.astype(BF16)
    kk = _dot_nt(kb, kb)
    qk = _dot_nt(qb, kb)
    order = range(n_sub - 1, -1, -1) if reverse else range(n_sub)
    outs = []
    for r in range(n_r):
        beta = gsel[:, r:r + 1]
        cum_c = cum[:, n_r + r:n_r + r + 1]
        cum_r = cum_t[n_r + r:n_r + r + 1, :]
        decay = jnp.exp(jnp.where(incl, cum_c - cum_r, -jnp.inf))
        a = jnp.where(strict, beta * kk * decay, 0.0)
        inv = _unit_tri_inverse(a, eye)
        v = v_ref[:, r * hd:(r + 1) * hd]
        rhs = jnp.concatenate([v * beta, k * (beta * jnp.exp(cum_c))], axis=1)
        sol = _mm_exact(inv, rhs)
        u_sol, w_sol = sol[:, :hd], sol[:, hd:]
        state = s_ref[r]
        v_new, inter = [None] * n_sub, [None] * n_sub
        for i in order:
            sl = slice(i * GDN_SUB, (i + 1) * GDN_SUB)
            sb = state.astype(BF16)
            v_new[i] = u_sol[sl] - jnp.dot(w_sol[sl].astype(BF16), sb, preferred_element_type=F32)
            inter[i] = jnp.dot(qb[sl], sb, preferred_element_type=F32)
            end = i * GDN_SUB if reverse else (i + 1) * GDN_SUB - 1
            cum_end = cum_c[end:end + 1, :]
            to_end = jnp.exp(cum_end - cum_c[sl])
            state = jnp.exp(cum_end) * state + _dot_t((k[sl] * to_end).astype(BF16), v_new[i].astype(BF16))
        s_ref[r] = state
        v_new = jnp.concatenate(v_new, axis=0)
        inter = jnp.concatenate(inter, axis=0)
        y = jnp.dot((qk * decay).astype(BF16), v_new.astype(BF16), preferred_element_type=F32) + inter * jnp.exp(cum_c)
        if not reverse:
            y = y + yb_ref[:, r * hd:(r + 1) * hd]
            y = y * lax.rsqrt(jnp.mean(y * y, axis=-1, keepdims=True) + NORM_EPS) * ng_ref[...]
            z = z_ref[:, r * hd:(r + 1) * hd]
            y = y * (z * jax.nn.sigmoid(z))
        outs.append(y)
    o_ref[...] = jnp.concatenate(outs, axis=1).astype(o_ref.dtype)


def _gdn_mixer(u, tail, conv_w, dt_bias, a_log, norm_g, bsz, lc, seq, d_model):
    n_ctx_c, n_lat_c = _scan_geometry(bsz, lc, seq)
    n_steps = n_ctx_c + n_lat_c
    hd = GDN_HEAD_DIM
    k_heads = d_model // hd
    n_r = 2
    rows = u.shape[0]
    conv_ch = 4 * d_model
    nq = d_model // CONV_COLS
    qkv = _conv_silu(u, conv_w, jnp.zeros((1, conv_ch), F32), conv_ch, bsz, lc, seq, n_scaled=nq, n_normed=2 * nq,
                     scale=hd ** -0.5, head_dim=hd)
    n_gate = tail.shape[1] // 2
    prow = jnp.concatenate([jnp.zeros((1, n_gate), F32), dt_bias.reshape(1, n_gate)], axis=1)
    arow = jnp.concatenate([jnp.zeros((1, n_gate), F32), a_log.reshape(1, n_gate)], axis=1)
    y_b = None
    for reverse in (True, False):
        rb = functools.partial(_row_block, bsz=bsz, n_ctx_c=n_ctx_c, n_lat_c=n_lat_c, reverse=reverse)
        const = lambda b, g, c: (0, 0)
        in_specs = [pl.BlockSpec((SCAN_CHUNK, hd), lambda b, g, c, rb=rb: (rb(b, c), g)),
                    pl.BlockSpec((SCAN_CHUNK, hd), lambda b, g, c, rb=rb: (rb(b, c), k_heads + g)),
                    pl.BlockSpec((SCAN_CHUNK, n_r * hd), lambda b, g, c, rb=rb: (rb(b, c), k_heads + g)),
                    pl.BlockSpec((SCAN_CHUNK, 2 * n_gate), lambda b, g, c, rb=rb: (rb(b, c), 0)),
                    pl.BlockSpec((1, 2 * n_gate), const),
                    pl.BlockSpec((1, 2 * n_gate), const)]
        args = [qkv, qkv, qkv, tail, prow, arow]
        if not reverse:
            in_specs += [pl.BlockSpec((SCAN_CHUNK, n_r * hd), lambda b, g, c, rb=rb: (rb(b, c), g)),
                         pl.BlockSpec((SCAN_CHUNK, n_r * hd), lambda b, g, c, rb=rb: (rb(b, c), 2 * k_heads + g)),
                         pl.BlockSpec((1, hd), const)]
            args += [y_b, u, norm_g.reshape(1, hd)]
        out = pl.pallas_call(
            functools.partial(_gdn_kernel, reverse=reverse, n_r=n_r, hd=hd),
            grid=(bsz, k_heads, n_steps),
            in_specs=in_specs,
            out_specs=pl.BlockSpec((SCAN_CHUNK, n_r * hd), lambda b, g, c, rb=rb: (rb(b, c), g)),
            out_shape=jax.ShapeDtypeStruct((rows, k_heads * n_r * hd), F32 if reverse else BF16),
            scratch_shapes=[pltpu.VMEM((n_r, hd, hd), F32)],
            compiler_params=_params("parallel", "parallel", "arbitrary"),
            name="gdn_bwd" if reverse else "gdn_fwd",
        )(*args)
        if reverse:
            y_b = out
    return out


HG_SUB = 64
HG_BLK = 16


def _log1p(x):
    return jnp.log(1.0 + x)


def _hgrn_kernel(*refs, reverse):
    if reverse:
        q_ref, f_ref, i_ref, lb_ref, o_ref, st_ref = refs
    else:
        q_ref, f_ref, i_ref, lb_ref, yb_ref, g_ref, ng_ref, o_ref, st_ref = refs
    c = pl.program_id(2)
    qn = SCAN_CHUNK
    hd = q_ref.shape[1]
    n_sub, n_blk = qn // HG_SUB, HG_SUB // HG_BLK

    @pl.when(c == 0)
    def _():
        st_ref[...] = jnp.zeros_like(st_ref)

    q, f, v, lb = q_ref[...], f_ref[...], i_ref[...], lb_ref[...]
    log_sig = jnp.minimum(f, 0.0) - _log1p(jnp.exp(-jnp.abs(f)))
    ga, gb = jnp.log(lb), _log1p(-lb) + log_sig
    log_f = jnp.maximum(ga, gb) + _log1p(jnp.exp(-jnp.abs(ga - gb)))
    k = (1.0 - lb) * jax.nn.sigmoid(-f)

    ti = lax.broadcasted_iota(jnp.int32, (qn, 1), 0)
    si = lax.broadcasted_iota(jnp.int32, (1, qn), 1)
    before = (si >= ti) if reverse else (si <= ti)
    same = lambda n: lax.shift_right_logical(ti, int(math.log2(n))) == lax.shift_right_logical(si, int(math.log2(n)))
    cum_sub = _mm_exact((same(HG_SUB) & before).astype(F32), log_f)
    cum_blk = _mm_exact((same(HG_BLK) & before).astype(F32), log_f)
    q_blk = q * jnp.exp(cum_blk)
    q_sub = (q * jnp.exp(cum_sub)).astype(BF16)

    ones = jnp.ones((hd, hd), BF16)
    lane = lax.broadcasted_iota(jnp.int32, (1, hd), 1)
    row_sub = lax.broadcasted_iota(jnp.int32, (HG_SUB, 1), 0)
    row_blk = lax.broadcasted_iota(jnp.int32, (HG_BLK, 1), 0)
    zeros_sub = jnp.zeros((hd - HG_SUB, hd), F32)
    y_intra, kvt, chunk_dec = [], [], []
    for i in range(n_sub):
        r0 = i * HG_SUB
        sl = slice(r0, r0 + HG_SUB)
        cs, ks, vs = cum_sub[sl], k[sl], v[sl]
        end = r0 if reverse else r0 + HG_SUB - 1
        cum_end = cum_sub[end:end + 1]
        kvt.append(_dot_t(vs.astype(BF16), (ks * jnp.exp(cum_end - cs)).astype(BF16)))
        chunk_dec.append(jnp.exp(cum_end))
        a_rows = []
        for a in range(n_blk):
            b0 = r0 + a * HG_BLK
            bl = slice(b0, b0 + HG_BLK)
            qb, kb, cb = q[bl], k[bl], cum_blk[bl]
            tiles = []
            for s in range(HG_BLK):
                ok = (row_blk <= s) if reverse else (row_blk >= s)
                e = jnp.exp(jnp.where(ok, cb - cb[s:s + 1], -jnp.inf))
                tiles.append(qb * e * kb[s:s + 1])
            sums = jnp.dot(jnp.concatenate(tiles, axis=0).astype(BF16), ones, preferred_element_type=F32)
            acc = jnp.zeros((HG_BLK, hd), F32)
            for s in range(HG_BLK):
                acc = acc + jnp.where(lane == a * HG_BLK + s, sums[s * HG_BLK:(s + 1) * HG_BLK], 0.0)
            has_earlier = (a < n_blk - 1) if reverse else (a > 0)
            if has_earlier:
                ref_row = b0 + HG_BLK if reverse else b0 - 1
                earlier = (row_sub >= (a + 1) * HG_BLK) if reverse else (row_sub < a * HG_BLK)
                kt = ks * jnp.exp(jnp.where(earlier, cum_sub[ref_row:ref_row + 1] - cs, -jnp.inf))
                kt = jnp.concatenate([kt, zeros_sub], axis=0).astype(BF16)
                acc = acc + _dot_nt(q_blk[bl].astype(BF16), kt)
            a_rows.append(acc)
        attn = jnp.concatenate(a_rows, axis=0).astype(BF16)
        v_pad = jnp.concatenate([vs, zeros_sub], axis=0).astype(BF16)
        y_intra.append(jnp.dot(attn, v_pad, preferred_element_type=F32))

    state = st_ref[...]
    ys = [None] * n_sub
    for i in (range(n_sub - 1, -1, -1) if reverse else range(n_sub)):
        ys[i] = y_intra[i] + _dot_nt(q_sub[i * HG_SUB:(i + 1) * HG_SUB], state.astype(BF16))
        state = state * chunk_dec[i] + kvt[i]
    st_ref[...] = state
    y = jnp.concatenate(ys, axis=0)
    if reverse:
        o_ref[...] = y
    else:
        y = y + yb_ref[...]
        y = y * lax.rsqrt(jnp.mean(y * y, axis=-1, keepdims=True) + NORM_EPS) * ng_ref[...]
        g = g_ref[...]
        o_ref[...] = (y * (g * jax.nn.sigmoid(g))).astype(o_ref.dtype)


def _hgrn_mixer(u, lb, norm_g, bsz, lc, seq):
    n_ctx_c, n_lat_c = _scan_geometry(bsz, lc, seq)
    n_steps = n_ctx_c + n_lat_c
    hd = HGRN_EXPAND
    d_model = lb.shape[0]
    heads = d_model // hd
    rows = u.shape[0]
    lb2, ng2 = lb.reshape(1, d_model), norm_g.reshape(1, d_model)
    y_b = None
    for reverse in (True, False):
        rb = functools.partial(_row_block, bsz=bsz, n_ctx_c=n_ctx_c, n_lat_c=n_lat_c, reverse=reverse)
        col = lambda seg: (lambda b, h, c, rb=rb: (rb(b, c), seg * heads + h))
        par = lambda b, h, c: (0, h)
        in_specs = [pl.BlockSpec((SCAN_CHUNK, hd), col(0)),
                    pl.BlockSpec((SCAN_CHUNK, hd), col(2 if reverse else 1)),
                    pl.BlockSpec((SCAN_CHUNK, hd), col(3)),
                    pl.BlockSpec((1, hd), par)]
        args = [u, u, u, lb2]
        if not reverse:
            in_specs += [pl.BlockSpec((SCAN_CHUNK, hd), col(0)), pl.BlockSpec((SCAN_CHUNK, hd), col(4)),
                         pl.BlockSpec((1, hd), par)]
            args += [y_b, u, ng2]
        out = pl.pallas_call(
            functools.partial(_hgrn_kernel, reverse=reverse),
            grid=(bsz, heads, n_steps),
            in_specs=in_specs,
            out_specs=pl.BlockSpec((SCAN_CHUNK, hd), col(0)),
            out_shape=jax.ShapeDtypeStruct((rows, d_model), F32 if reverse else BF16),
            scratch_shapes=[pltpu.VMEM((hd, hd), F32)],
            compiler_params=_params("parallel", "parallel", "arbitrary"),
            name="hgrn_bwd" if reverse else "hgrn_fwd",
        )(*args)
        if reverse:
            y_b = out
    return out


def _rmsnorm(x, g, eps=NORM_EPS):
    y = x * lax.rsqrt(jnp.mean(x * x, axis=-1, keepdims=True) + eps)
    return y * g


def _l2norm(x, eps=1e-6):
    return x * lax.rsqrt(jnp.sum(x * x, axis=-1, keepdims=True) + eps)


def _dwconv(u, w):
    return lax.conv_general_dilated(u, w[:, None, :], window_strides=(1,), padding=[(CONV_W // 2, CONV_W // 2)],
                                    dimension_numbers=('NWC', 'WIO', 'NWC'), feature_group_count=u.shape[-1])


def _conv_split(u, w, lc):
    return jnp.concatenate([_dwconv(u[:, :lc], w), _dwconv(u[:, lc:], w)], axis=1)


def _rev(t, lc):
    return jnp.concatenate([jnp.flip(t[:, :lc], 1), jnp.flip(t[:, lc:], 1)], axis=1)


def _to_chunks(t):
    b, n = t.shape[:2]
    return jnp.moveaxis(t.reshape((b, n // CHUNK, CHUNK) + t.shape[2:]), 1, 0)


def _from_chunks(t):
    nc, b, q = t.shape[:3]
    return jnp.moveaxis(t, 0, 1).reshape((b, nc * q) + t.shape[3:])


def _chunk_masks():
    idx = jnp.arange(CHUNK)
    return idx[:, None] >= idx[None, :], idx[:, None] > idx[None, :]


def _scalar_decay_scan(q, k, v, log_a):
    bsz, _, g, n = q.shape
    r, p = v.shape[-2:]
    incl, _ = _chunk_masks()

    def body(s, xs):
        qc, kc, vc, la = xs
        cum = jnp.cumsum(la, axis=1)
        cum_t = jnp.moveaxis(cum, 1, -1)
        seg = cum_t[..., :, None] - cum_t[..., None, :]
        scores = jnp.einsum('btgn,bsgn->bgts', qc, kc)
        attn = scores[:, :, None] * jnp.exp(jnp.where(incl, seg, -jnp.inf))
        y = jnp.einsum('bgrts,bsgrp->btgrp', attn, vc)
        y = y + jnp.einsum('btgn,bgrnp->btgrp', qc, s) * jnp.exp(cum)[..., None]
        to_end = jnp.exp(cum[:, -1:] - cum)
        s = jnp.exp(cum[:, -1])[..., None, None] * s + jnp.einsum('bsgn,bsgr,bsgrp->bgrnp', kc, to_end, vc)
        return s, y

    s0 = jnp.zeros((bsz, g, r, n, p), F32)
    _, y = lax.scan(body, s0, tuple(_to_chunks(t) for t in (q, k, v, log_a)))
    return _from_chunks(y)


def _vector_decay_scan(q, k, v, log_f):
    bsz, _, h, kd = q.shape
    vd = v.shape[-1]
    incl, _ = _chunk_masks()

    def body(s, xs):
        qc, kc, vc, lf = xs
        cum = jnp.cumsum(lf, axis=1)
        seg = cum[:, :, None] - cum[:, None, :]
        decay = jnp.exp(jnp.where(incl[:, :, None, None], seg, -jnp.inf))
        attn = jnp.einsum('bthk,bshk,btshk->bhts', qc, kc, decay)
        y = jnp.einsum('bhts,bshv->bthv', attn, vc)
        y = y + jnp.einsum('bthk,bhkv->bthv', qc * jnp.exp(cum), s)
        s = jnp.exp(cum[:, -1])[..., None] * s + jnp.einsum('bshk,bshv->bhkv', kc * jnp.exp(cum[:, -1:] - cum), vc)
        return s, y

    s0 = jnp.zeros((bsz, h, kd, vd), F32)
    _, y = lax.scan(body, s0, tuple(_to_chunks(t) for t in (q, k, v, log_f)))
    return _from_chunks(y)


def _delta_scan(q, k, v, beta, log_a):
    bsz, _, g, kd = q.shape
    r, vd = v.shape[-2:]
    incl, strict = _chunk_masks()

    def body(s, xs):
        qc, kc, vc, bc, la = xs
        cum = jnp.cumsum(la, axis=1)
        cum_t = jnp.moveaxis(cum, 1, -1)
        seg = cum_t[..., :, None] - cum_t[..., None, :]
        beta_t = jnp.moveaxis(bc, 1, -1)
        kk = jnp.einsum('btgk,bsgk->bgts', kc, kc)
        lower = beta_t[..., :, None] * kk[:, :, None] * jnp.exp(jnp.where(strict, seg, -jnp.inf))
        rhs_v = jnp.moveaxis(vc * bc[..., None], 1, 3)
        rhs_k = jnp.moveaxis(kc[:, :, :, None, :] * (bc * jnp.exp(cum))[..., None], 1, 3)
        sol = lax.linalg.triangular_solve(lower, jnp.concatenate([rhs_v, rhs_k], axis=-1),
                                          left_side=True, lower=True, unit_diagonal=True)
        u, w = sol[..., :vd], sol[..., vd:]
        v_new = u - jnp.einsum('bgrtk,bgrkv->bgrtv', w, s)
        qk = jnp.einsum('btgk,bsgk->bgts', qc, kc)
        attn = qk[:, :, None] * jnp.exp(jnp.where(incl, seg, -jnp.inf))
        y = jnp.einsum('bgrts,bgrsv->btgrv', attn, v_new)
        y = y + jnp.einsum('btgk,bgrkv->btgrv', qc, s) * jnp.exp(cum)[..., None]
        to_end = jnp.exp(cum_t[..., -1:] - cum_t)
        s = jnp.exp(cum_t[..., -1])[..., None, None] * s + jnp.einsum('bsgk,bgrs,bgrsv->bgrkv', kc, to_end, v_new)
        return s, y

    s0 = jnp.zeros((bsz, g, r, kd, vd), F32)
    _, y = lax.scan(body, s0, tuple(_to_chunks(t) for t in (q, k, v, beta, log_a)))
    return _from_chunks(y)


def _rope_tables(rows, half):
    pos = np.arange(rows * GRID_W)
    inv_freq = np.float32(ROPE_BASE) ** (-(np.arange(0, half, 2, dtype=np.float32) / np.float32(half)))
    out = []
    for p in ((pos // GRID_W).astype(np.float32), (pos % GRID_W).astype(np.float32)):
        ang = (p[:, None] * inv_freq.astype(np.float32)).astype(np.float32).astype(np.float64)
        out += [np.cos(ang).astype(np.float32), np.sin(ang).astype(np.float32)]
    return out


def _rope_2d(t, rows):
    half = t.shape[-1] // 2
    cos_r, sin_r, cos_c, sin_c = (jnp.asarray(a)[:, None, :] for a in _rope_tables(rows, half))

    def rot(u, cos, sin):
        u1, u2 = jnp.split(u, 2, axis=-1)
        return jnp.concatenate([u1 * cos - u2 * sin, u2 * cos + u1 * sin], axis=-1)

    return jnp.concatenate([rot(t[..., :half], cos_r, sin_r), rot(t[..., half:], cos_c, sin_c)], axis=-1)


def _ssd_core(u, lc, conv_w, conv_b, dt_bias, a_log, d_skip, norm_g, start):
    bsz, t = u.shape[:2]
    d_inner = norm_g.shape[0]
    heads = d_inner // SSD_HEAD_DIM
    conv_ch = d_inner + 2 * SSD_GROUPS * SSD_STATE
    z, xbc, dt = jnp.split(u, [d_inner, d_inner + conv_ch], axis=-1)
    xbc = jax.nn.silu(_conv_split(xbc, conv_w, lc) + conv_b)
    xs, bm, cm = jnp.split(xbc, [d_inner, d_inner + SSD_GROUPS * SSD_STATE], axis=-1)
    r = heads // SSD_GROUPS
    xs = xs.reshape(bsz, t, SSD_GROUPS, r, SSD_HEAD_DIM)
    bm = bm.reshape(bsz, t, SSD_GROUPS, SSD_STATE)
    cm = cm.reshape(bsz, t, SSD_GROUPS, SSD_STATE)
    dt = jax.nn.softplus(dt.reshape(bsz, t, 2, heads) + dt_bias)
    log_a = -jnp.exp(a_log) * dt
    grp = lambda a: a.reshape(bsz, t, SSD_GROUPS, r)
    y = _scalar_decay_scan(cm, bm, xs * grp(dt[:, :, 0])[..., None], grp(log_a[:, :, 0]))
    y = y + _rev(_scalar_decay_scan(_rev(cm, lc), _rev(bm, lc), _rev(xs * grp(dt[:, :, 1])[..., None], lc),
                                    _rev(grp(log_a[:, :, 1]), lc)), lc)
    y = y + d_skip.reshape(SSD_GROUPS, r)[..., None] * xs
    n = t - start
    y = y.reshape(bsz, t, d_inner)[:, start:] * jax.nn.silu(z[:, start:])
    return _rmsnorm(y.reshape(bsz, n, SSD_GROUPS, -1), norm_g.reshape(SSD_GROUPS, -1)).reshape(bsz, n, d_inner)


def _ret_core(u, lc, log_decay, d_model, rows, start):
    bsz, t = u.shape[:2]
    dv = 2 * d_model
    qk_dim = d_model // RET_HEADS
    v_dim = 2 * qk_dim
    q, k, v, g = jnp.split(u, [d_model, 2 * d_model, 2 * d_model + dv], axis=-1)
    q = q.reshape(bsz, t, RET_HEADS, qk_dim)
    k = k.reshape(bsz, t, RET_HEADS, qk_dim) * qk_dim ** -0.5
    q = jnp.concatenate([q[:, :lc], _rope_2d(q[:, lc:], rows)], axis=1)
    k = jnp.concatenate([k[:, :lc], _rope_2d(k[:, lc:], rows)], axis=1)
    v = v.reshape(bsz, t, RET_HEADS, 1, v_dim)
    ld_f = jnp.broadcast_to(log_decay[0][:, None], (bsz, t, RET_HEADS, 1))
    ld_b = jnp.broadcast_to(log_decay[1][:, None], (bsz, t, RET_HEADS, 1))
    y = _scalar_decay_scan(q, k, v, ld_f)
    y = y + _rev(_scalar_decay_scan(_rev(q, lc), _rev(k, lc), _rev(v, lc), ld_b), lc)
    n = t - start
    y = y[:, start:].reshape(bsz, n, RET_HEADS, v_dim)
    mu = jnp.mean(y, axis=-1, keepdims=True)
    var = jnp.mean(jnp.square(y - mu), axis=-1, keepdims=True)
    y = ((y - mu) * lax.rsqrt(var + NORM_EPS)).reshape(bsz, n, dv)
    return y * jax.nn.silu(g[:, start:])


def _lower_bound(lb_logits, layer):
    p = jax.nn.softmax(lb_logits.astype(F32), axis=0)
    return jnp.cumsum(p, axis=0)[layer] - p[0]


def _hgrn_core(u, lc, lb, norm_g, start):
    bsz, t = u.shape[:2]
    d_model = norm_g.shape[0]
    heads = d_model // HGRN_EXPAND
    q, f_f, f_b, i, g = jnp.split(u, 5, axis=-1)
    shp = (bsz, t, heads, HGRN_EXPAND)
    q, i = q.reshape(shp), i.reshape(shp)
    lb = lb.reshape(heads, HGRN_EXPAND)

    def gates(f):
        f = f.reshape(shp)
        log_f = jnp.logaddexp(jnp.log(lb), jnp.log1p(-lb) + jax.nn.log_sigmoid(f))
        return log_f, (1 - lb) * jax.nn.sigmoid(-f)

    lf_f, k_f = gates(f_f)
    lf_b, k_b = gates(f_b)
    y = _vector_decay_scan(q, k_f, i, lf_f)
    y = y + _rev(_vector_decay_scan(_rev(q, lc), _rev(k_b, lc), _rev(i, lc), _rev(lf_b, lc)), lc)
    n = t - start
    y = _rmsnorm(y[:, start:], norm_g.reshape(heads, HGRN_EXPAND))
    return (y * jax.nn.silu(g[:, start:].reshape(bsz, n, heads, HGRN_EXPAND))).reshape(bsz, n, d_model)


def _gdn_core(u, lc, conv_w, dt_bias, a_log, norm_g, d_model, start):
    bsz, t = u.shape[:2]
    k_heads = d_model // GDN_HEAD_DIM
    v_heads = 2 * k_heads
    dk, dv = d_model, 2 * d_model
    conv_ch = 2 * dk + dv
    qkv, z, bt, a = jnp.split(u, [conv_ch, conv_ch + dv, conv_ch + dv + 2 * v_heads], axis=-1)
    qkv = jax.nn.silu(_conv_split(qkv, conv_w, lc))
    q, k, v = jnp.split(qkv, [dk, 2 * dk], axis=-1)
    r = v_heads // k_heads
    q = _l2norm(q.reshape(bsz, t, k_heads, GDN_HEAD_DIM)) * GDN_HEAD_DIM ** -0.5
    k = _l2norm(k.reshape(bsz, t, k_heads, GDN_HEAD_DIM))
    v = v.reshape(bsz, t, k_heads, r, GDN_HEAD_DIM)
    beta = jax.nn.sigmoid(bt.reshape(bsz, t, 2, k_heads, r))
    log_a = -jnp.exp(a_log).reshape(2, k_heads, r) * jax.nn.softplus(
        a.reshape(bsz, t, 2, k_heads, r) + dt_bias.reshape(2, k_heads, r))
    y = _delta_scan(q, k, v, beta[:, :, 0], log_a[:, :, 0])
    y = y + _rev(_delta_scan(_rev(q, lc), _rev(k, lc), _rev(v, lc), _rev(beta[:, :, 1], lc),
                             _rev(log_a[:, :, 1], lc)), lc)
    n = t - start
    y = y[:, start:].reshape(bsz, n, v_heads, GDN_HEAD_DIM)
    y = _rmsnorm(y, norm_g) * jax.nn.silu(z[:, start:].reshape(bsz, n, v_heads, GDN_HEAD_DIM))
    return y.reshape(bsz, n, dv)


def _rows_to_seq(u, bsz, lc):
    n_ctx = bsz * lc
    return jnp.concatenate([u[:n_ctx].reshape(bsz, lc, -1), u[n_ctx:].reshape(bsz, -1, u.shape[-1])], axis=1)


def _seq_to_rows(y, lc, start):
    bsz = y.shape[0]
    if start:
        return y.reshape(bsz * y.shape[1], -1)
    return jnp.concatenate([y[:, :lc].reshape(bsz * lc, -1), y[:, lc:].reshape(-1, y.shape[-1])], axis=0)


def kernel(x, c, ctx, c_ctx, ada_w, ada_b, norm_g, mlp_w1, mlp_w2, final_g, ssd_w_in, ssd_conv_w, ssd_conv_b,
           ssd_dt_bias, ssd_a_log, ssd_d, ssd_norm_g, ssd_w_out, ret_w_in, ret_log_decay, ret_w_out, hgrn_w_in,
           hgrn_lb_logits, hgrn_norm_g, hgrn_w_out, gdn_w_in, gdn_conv_w, gdn_dt_bias, gdn_a_log, gdn_norm_g,
           gdn_w_out):
    bsz, seq, d = x.shape
    lc = ctx.shape[1]
    depth = ada_w.shape[0]
    rows_grid = seq // GRID_W
    n_ctx = bsz * lc
    assert lc % ROW_TILE == 0 or ROW_TILE % lc == 0 and n_ctx % ROW_TILE == 0
    assert seq % ROW_TILE == 0 and bsz + 1 <= 8

    cond_pad = jnp.concatenate([c, c_ctx[None], jnp.zeros((8 - bsz - 1, d), F32)], axis=0)
    mod = _ada_mod(cond_pad, ada_w, ada_b)

    tile_row = [bsz] * (n_ctx // ROW_TILE) + [b for b in range(bsz) for _ in range(seq // ROW_TILE)]
    tile_row = jnp.asarray(tile_row, jnp.int32)
    n_ctx_tiles = n_ctx // ROW_TILE

    xr = jnp.concatenate([ctx.reshape(n_ctx, d), x.reshape(bsz * seq, d)], axis=0)

    for i in range(depth):
        mixer, occ = i % 4, i // 4
        keep_ctx = i < depth - 1
        start = 0 if keep_ctx else lc
        mod_t = mod[i][tile_row][:, None, :]
        g0, g1 = norm_g[i, 0][None], norm_g[i, 1][None]
        if mixer == 0:
            w_in, w_out = ssd_w_in[occ], ssd_w_out[occ]
        elif mixer == 1:
            w_in, w_out = ret_w_in[occ], ret_w_out[occ]
        elif mixer == 2:
            w_in, w_out = hgrn_w_in[occ], hgrn_w_out[occ]
        else:
            w_in, w_out = gdn_w_in[occ], gdn_w_out[occ]
        n_in = w_in.shape[1]
        n_main = (n_in // 1024) * 1024 if n_in % 1024 else n_in
        w_in = w_in.astype(BF16)
        u = _ln_mm(xr, g0, mod_t, 0, 1, w_in[:, :n_main])
        tail = _ln_mm(xr, g0, mod_t, 0, 1, w_in[:, n_main:]) if n_main != n_in else None
        first_row = 0 if keep_ctx else n_ctx
        if mixer == 1:
            yr = _ret_mixer(u, ret_log_decay[occ], bsz, lc, seq, d)[first_row:]
        elif mixer == 2:
            yr = _hgrn_mixer(u, _lower_bound(hgrn_lb_logits, i), hgrn_norm_g[occ], bsz, lc, seq)[first_row:]
        elif mixer == 3:
            yr = _gdn_mixer(u, tail, gdn_conv_w[occ], gdn_dt_bias[occ], gdn_a_log[occ], gdn_norm_g[occ], bsz, lc,
                            seq, d)[first_row:]
        else:
            if tail is not None:
                u = jnp.concatenate([u, tail], axis=1)
            useq = _rows_to_seq(u, bsz, lc)
            if mixer == 0:
                y = _ssd_core(useq, lc, ssd_conv_w[occ], ssd_conv_b[occ], ssd_dt_bias[occ], ssd_a_log[occ],
                              ssd_d[occ], ssd_norm_g[occ], start)
            elif mixer == 2:
                y = _hgrn_core(useq, lc, _lower_bound(hgrn_lb_logits, i), hgrn_norm_g[occ], start)
            else:
                y = _gdn_core(useq, lc, gdn_conv_w[occ], gdn_dt_bias[occ], gdn_a_log[occ], gdn_norm_g[occ], d,
                              start)
            yr = _seq_to_rows(y, lc, start).astype(BF16)
        if not keep_ctx:
            xr = xr[n_ctx:]
            mod_t = mod_t[n_ctx_tiles:]
        xr = _out_proj(yr, w_out.astype(BF16), xr, mod_t, 2)
        xr = _mlp(xr, g1, mod_t, mlp_w1[i].astype(BF16), mlp_w2[i].astype(BF16), final_g[None], final=not keep_ctx)
    return xr.reshape(bsz, seq, d)
```

```python
import functools
import math

import jax
import jax.numpy as jnp
import numpy as np
from jax import lax
from jax.experimental import pallas as pl
from jax.experimental.pallas import tpu as pltpu

F32 = jnp.float32
BF16 = jnp.bfloat16

GRID_W = 64
CHUNK = 64
CONV_W = 3
NORM_EPS = 1e-6
ROPE_BASE = 10000.0
SSD_HEAD_DIM = 64
SSD_GROUPS = 8
SSD_STATE = 128
RET_HEADS = 8
HGRN_EXPAND = 128
GDN_HEAD_DIM = 128

ROW_TILE = 512
VMEM_LIMIT = 56 * 1024 * 1024


def _params(*sem):
    return pltpu.CompilerParams(dimension_semantics=sem, vmem_limit_bytes=VMEM_LIMIT)


def _col_tile(n, cap=1536):
    best = 128
    for t in range(128, cap + 1, 128):
        if n % t == 0:
            best = t
    return best


def _ada_kernel(c_ref, w_ref, b_ref, o_ref):
    c = c_ref[...]
    c = (c * jax.nn.sigmoid(c)).astype(BF16)
    o_ref[0] = jnp.dot(c, w_ref[0].astype(BF16), preferred_element_type=F32) + b_ref[0]


def _ada_mod(cond_pad, ada_w, ada_b):
    depth, d, n = ada_w.shape
    tn = 1024
    return pl.pallas_call(
        _ada_kernel,
        grid=(depth, n // tn),
        in_specs=[pl.BlockSpec((8, d), lambda l, j: (0, 0)),
                  pl.BlockSpec((1, d, tn), lambda l, j: (l, 0, j)),
                  pl.BlockSpec((1, 1, tn), lambda l, j: (l, 0, j))],
        out_specs=pl.BlockSpec((1, 8, tn), lambda l, j: (l, 0, j)),
        out_shape=jax.ShapeDtypeStruct((depth, 8, n), F32),
        compiler_params=_params("parallel", "parallel"),
        name="ada_mod",
    )(cond_pad, ada_w, ada_b.reshape(depth, 1, n))


def _adaln_rows(x, g, sh, sc):
    y = x * lax.rsqrt(jnp.mean(x * x, axis=-1, keepdims=True) + NORM_EPS)
    return y * g * (1.0 + sc) + sh


def _ln_mm_kernel(x_ref, g_ref, sh_ref, sc_ref, w_ref, o_ref, h_ref):
    @pl.when(pl.program_id(1) == 0)
    def _():
        h_ref[...] = _adaln_rows(x_ref[...], g_ref[...], sh_ref[0], sc_ref[0]).astype(BF16)

    o_ref[...] = jnp.dot(h_ref[...], w_ref[...], preferred_element_type=F32).astype(o_ref.dtype)


def _ln_mm(x, g, mod_t, sh_col, sc_col, w, out_dtype=F32):
    m, d = x.shape
    n = w.shape[1]
    tm, tn = ROW_TILE, _col_tile(n)
    return pl.pallas_call(
        _ln_mm_kernel,
        grid=(m // tm, n // tn),
        in_specs=[pl.BlockSpec((tm, d), lambda i, j: (i, 0)),
                  pl.BlockSpec((1, d), lambda i, j: (0, 0)),
                  pl.BlockSpec((1, 1, d), lambda i, j: (i, 0, sh_col)),
                  pl.BlockSpec((1, 1, d), lambda i, j: (i, 0, sc_col)),
                  pl.BlockSpec((d, tn), lambda i, j: (0, j))],
        out_specs=pl.BlockSpec((tm, tn), lambda i, j: (i, j)),
        out_shape=jax.ShapeDtypeStruct((m, n), out_dtype),
        scratch_shapes=[pltpu.VMEM((tm, d), BF16)],
        compiler_params=_params("parallel", "arbitrary"),
        name="adaln_in_proj",
    )(x, g, mod_t, mod_t, w)


def _out_kernel(y_ref, w_ref, x_ref, gate_ref, o_ref):
    o_ref[...] = x_ref[...] + gate_ref[0] * jnp.dot(y_ref[...], w_ref[...], preferred_element_type=F32)


def _out_proj(y, w, x, mod_t, gate_col):
    m, k = y.shape
    d = w.shape[1]
    tm, tn = ROW_TILE, 512
    return pl.pallas_call(
        _out_kernel,
        grid=(m // tm, d // tn),
        in_specs=[pl.BlockSpec((tm, k), lambda i, j: (i, 0)),
                  pl.BlockSpec((k, tn), lambda i, j: (0, j)),
                  pl.BlockSpec((tm, tn), lambda i, j: (i, j)),
                  pl.BlockSpec((1, 1, tn), lambda i, j: (i, 0, gate_col * (d // tn) + j))],
        out_specs=pl.BlockSpec((tm, tn), lambda i, j: (i, j)),
        out_shape=jax.ShapeDtypeStruct((m, d), F32),
        compiler_params=_params("parallel", "arbitrary"),
        name="out_proj",
    )(y, w, x, mod_t)


def _mlp_kernel(x_ref, g_ref, sh_ref, sc_ref, gate_ref, w1_ref, w2_ref, fg_ref, o_ref, h_ref, acc_ref, *, final):
    f = pl.program_id(1)

    @pl.when(f == 0)
    def _():
        h_ref[...] = _adaln_rows(x_ref[...], g_ref[...], sh_ref[0], sc_ref[0]).astype(BF16)
        acc_ref[...] = jnp.zeros_like(acc_ref)

    a = jnp.dot(h_ref[...], w1_ref[...], preferred_element_type=F32)
    a = jnp.square(jnp.maximum(a, 0.0)).astype(BF16)
    acc_ref[...] += jnp.dot(a, w2_ref[...], preferred_element_type=F32)

    @pl.when(f == pl.num_programs(1) - 1)
    def _():
        out = x_ref[...] + gate_ref[0] * acc_ref[...]
        if final:
            out = out * lax.rsqrt(jnp.mean(out * out, axis=-1, keepdims=True) + NORM_EPS) * fg_ref[...]
        o_ref[...] = out


def _mlp(x, g, mod_t, w1, w2, final_g, final):
    m, d = x.shape
    ff = w1.shape[1]
    tm, tf = ROW_TILE, 512
    return pl.pallas_call(
        functools.partial(_mlp_kernel, final=final),
        grid=(m // tm, ff // tf),
        in_specs=[pl.BlockSpec((tm, d), lambda i, f: (i, 0)),
                  pl.BlockSpec((1, d), lambda i, f: (0, 0)),
                  pl.BlockSpec((1, 1, d), lambda i, f: (i, 0, 3)),
                  pl.BlockSpec((1, 1, d), lambda i, f: (i, 0, 4)),
                  pl.BlockSpec((1, 1, d), lambda i, f: (i, 0, 5)),
                  pl.BlockSpec((d, tf), lambda i, f: (0, f)),
                  pl.BlockSpec((tf, d), lambda i, f: (f, 0)),
                  pl.BlockSpec((1, d), lambda i, f: (0, 0))],
        out_specs=pl.BlockSpec((tm, d), lambda i, f: (i, 0)),
        out_shape=jax.ShapeDtypeStruct((m, d), F32),
        scratch_shapes=[pltpu.VMEM((tm, d), BF16), pltpu.VMEM((tm, d), F32)],
        compiler_params=_params("parallel", "arbitrary"),
        name="adaln_mlp",
    )(x, g, mod_t, mod_t, mod_t, w1, w2, final_g)


SCAN_CHUNK = 256


def _scan_geometry(bsz, lc, seq):
    assert lc % SCAN_CHUNK == 0 and seq % SCAN_CHUNK == 0
    return lc // SCAN_CHUNK, seq // SCAN_CHUNK


def _seg_chunk(c, n_ctx_c, n_lat_c, reverse):
    if reverse:
        return c >= n_ctx_c, jnp.where(c < n_ctx_c, n_ctx_c - 1 - c, n_lat_c - 1 - (c - n_ctx_c))
    return c >= n_ctx_c, jnp.where(c < n_ctx_c, c, c - n_ctx_c)


def _row_block(b, c, bsz, n_ctx_c, n_lat_c, reverse):
    is_lat, j = _seg_chunk(c, n_ctx_c, n_lat_c, reverse)
    return jnp.where(is_lat, bsz * n_ctx_c + b * n_lat_c + j, b * n_ctx_c + j)


def _time_iotas(q):
    t = lax.broadcasted_iota(jnp.int32, (q, 1), 0).astype(F32)
    s = lax.broadcasted_iota(jnp.int32, (1, q), 1).astype(F32)
    return t, s


def _dot_t(a, b):
    return lax.dot_general(a, b, (((0,), (0,)), ((), ())), preferred_element_type=F32)


def _dot_nt(a, b):
    return lax.dot_general(a, b, (((1,), (1,)), ((), ())), preferred_element_type=F32)


def _ret_kernel(*refs, reverse, n_ctx_c, n_heads, k_scale):
    if reverse:
        ld_ref, q_ref, k_ref, v_ref, cos_ref, sin_ref, o_ref, s_ref = refs
    else:
        ld_ref, q_ref, k_ref, v_ref, cos_ref, sin_ref, yb_ref, g_ref, o_ref, s_ref = refs
    h, c = pl.program_id(1), pl.program_id(2)
    qn = SCAN_CHUNK

    @pl.when(c == 0)
    def _():
        s_ref[...] = jnp.zeros_like(s_ref)

    lg = ld_ref[(n_heads if reverse else 0) + h]
    t, s = _time_iotas(qn)
    is_lat = c >= n_ctx_c
    cos = jnp.where(is_lat, cos_ref[...], 1.0)
    sin = jnp.where(is_lat, sin_ref[...], 0.0)

    def rope(x):
        half = x.shape[1] // 2
        swapped = jnp.concatenate([pltpu.roll(x[:, :half], half // 2, 1), pltpu.roll(x[:, half:], half // 2, 1)], axis=1)
        return x * cos + swapped * sin

    q = rope(q_ref[...]).astype(BF16)
    k = rope(k_ref[...]) * k_scale
    v = v_ref[...].astype(BF16)
    if reverse:
        dmat = jnp.where(s >= t, jnp.exp((s - t) * lg), 0.0)
        q_dec, k_dec = jnp.exp((qn - t) * lg), jnp.exp(t * lg)
    else:
        dmat = jnp.where(t >= s, jnp.exp((t - s) * lg), 0.0)
        q_dec, k_dec = jnp.exp((t + 1.0) * lg), jnp.exp((qn - 1.0 - t) * lg)
    attn = (_dot_nt(q, k.astype(BF16)) * dmat).astype(BF16)
    state = s_ref[...]
    y = jnp.dot(attn, v, preferred_element_type=F32)
    y = y + jnp.dot(q, state.astype(BF16), preferred_element_type=F32) * q_dec
    chunk_dec = jnp.exp(jnp.full((1, 1), qn, F32) * lg)
    s_ref[...] = state * chunk_dec + _dot_t((k * k_dec).astype(BF16), v)
    if reverse:
        o_ref[...] = y
    else:
        y = y + yb_ref[...]
        mu = jnp.mean(y, axis=-1, keepdims=True)
        yc = y - mu
        var = jnp.mean(yc * yc, axis=-1, keepdims=True)
        g = g_ref[...]
        o_ref[...] = (yc * lax.rsqrt(var + NORM_EPS) * (g * jax.nn.sigmoid(g))).astype(o_ref.dtype)


def _ret_mixer(u, log_decay, bsz, lc, seq, d_model):
    n_ctx_c, n_lat_c = _scan_geometry(bsz, lc, seq)
    n_steps = n_ctx_c + n_lat_c
    qk_dim, v_dim = d_model // RET_HEADS, 2 * d_model // RET_HEADS
    rows = u.shape[0]
    cos_r, sin_r, cos_c, sin_c = _rope_tables(seq // GRID_W, qk_dim // 2)
    cos_t = jnp.asarray(np.concatenate([cos_r, cos_r, cos_c, cos_c], axis=1))
    sin_t = jnp.asarray(np.concatenate([-sin_r, sin_r, -sin_c, sin_c], axis=1))
    ld = log_decay.reshape(-1).astype(F32)
    y_b = None
    for reverse in (True, False):
        rb = functools.partial(_row_block, bsz=bsz, n_ctx_c=n_ctx_c, n_lat_c=n_lat_c, reverse=reverse)

        def tab(b, h, c, reverse=reverse):
            is_lat, j = _seg_chunk(c, n_ctx_c, n_lat_c, reverse)
            return (jnp.where(is_lat, j, 0), 0)

        in_specs = [pl.BlockSpec(memory_space=pltpu.SMEM),
                    pl.BlockSpec((SCAN_CHUNK, qk_dim), lambda b, h, c, rb=rb: (rb(b, c), h)),
                    pl.BlockSpec((SCAN_CHUNK, qk_dim), lambda b, h, c, rb=rb: (rb(b, c), RET_HEADS + h)),
                    pl.BlockSpec((SCAN_CHUNK, v_dim), lambda b, h, c, rb=rb: (rb(b, c), RET_HEADS + h)),
                    pl.BlockSpec((SCAN_CHUNK, qk_dim), tab),
                    pl.BlockSpec((SCAN_CHUNK, qk_dim), tab)]
        args = [ld, u, u, u, cos_t, sin_t]
        if not reverse:
            in_specs += [pl.BlockSpec((SCAN_CHUNK, v_dim), lambda b, h, c, rb=rb: (rb(b, c), h)),
                         pl.BlockSpec((SCAN_CHUNK, v_dim), lambda b, h, c, rb=rb: (rb(b, c), 2 * RET_HEADS + h))]
            args += [y_b, u]
        out = pl.pallas_call(
            functools.partial(_ret_kernel, reverse=reverse, n_ctx_c=n_ctx_c, n_heads=RET_HEADS, k_scale=qk_dim ** -0.5),
            grid=(bsz, RET_HEADS, n_steps),
            in_specs=in_specs,
            out_specs=pl.BlockSpec((SCAN_CHUNK, v_dim), lambda b, h, c, rb=rb: (rb(b, c), h)),
            out_shape=jax.ShapeDtypeStruct((rows, RET_HEADS * v_dim), F32 if reverse else BF16),
            scratch_shapes=[pltpu.VMEM((qk_dim, v_dim), F32)],
            compiler_params=_params("parallel", "parallel", "arbitrary"),
            name="retention_bwd" if reverse else "retention_fwd",
        )(*args)
        if reverse:
            y_b = out
    return out


CONV_COLS = 512


def _conv_kernel(x_ref, prev_ref, next_ref, w_ref, b_ref, o_ref, *, n_ctx_c, n_lat_c, bsz, n_scaled, n_normed, scale,
                 head_dim):
    rb, cb = pl.program_id(0), pl.program_id(1)
    is_lat = rb >= bsz * n_ctx_c
    j = jnp.where(is_lat, (rb - bsz * n_ctx_c) % n_lat_c, rb % n_ctx_c)
    last = jnp.where(is_lat, n_lat_c - 1, n_ctx_c - 1)
    x = x_ref[...]
    rows = x.shape[0]
    row = lax.broadcasted_iota(jnp.int32, (rows, 1), 0)
    halo_prev = jnp.where(j > 0, prev_ref[7:8, :], 0.0)
    halo_next = jnp.where(j < last, next_ref[0:1, :], 0.0)
    x_prev = jnp.where(row == 0, halo_prev, pltpu.roll(x, 1, 0))
    x_next = jnp.where(row == rows - 1, halo_next, pltpu.roll(x, rows - 1, 0))
    y = w_ref[0:1, :] * x_prev + w_ref[1:2, :] * x + w_ref[2:3, :] * x_next + b_ref[...]
    y = y * jax.nn.sigmoid(y)

    def normed(mult):
        parts = []
        for i in range(y.shape[1] // head_dim):
            p = y[:, i * head_dim:(i + 1) * head_dim]
            parts.append(p * (lax.rsqrt(jnp.sum(p * p, axis=-1, keepdims=True) + 1e-6) * mult))
        return jnp.concatenate(parts, axis=1)

    if n_normed == 0:
        o_ref[...] = y
    else:
        @pl.when(cb < n_scaled)
        def _():
            o_ref[...] = normed(scale)

        @pl.when(jnp.logical_and(cb >= n_scaled, cb < n_normed))
        def _():
            o_ref[...] = normed(1.0)

        @pl.when(cb >= n_normed)
        def _():
            o_ref[...] = y


def _conv_silu(u, w, b, n_cols, bsz, lc, seq, n_scaled=0, n_normed=0, scale=1.0, head_dim=128, col0=0):
    n_ctx_c, n_lat_c = _scan_geometry(bsz, lc, seq)
    rows = u.shape[0]
    n_rb = rows // SCAN_CHUNK
    sub = SCAN_CHUNK // 8
    n_halo = rows // 8
    return pl.pallas_call(
        functools.partial(_conv_kernel, n_ctx_c=n_ctx_c, n_lat_c=n_lat_c, bsz=bsz, n_scaled=n_scaled,
                          n_normed=n_normed, scale=scale, head_dim=head_dim),
        grid=(n_rb, n_cols // CONV_COLS),
        in_specs=[pl.BlockSpec((SCAN_CHUNK, CONV_COLS), lambda r, c: (r, c + col0)),
                  pl.BlockSpec((8, CONV_COLS), lambda r, c: (jnp.maximum(r * sub - 1, 0), c + col0)),
                  pl.BlockSpec((8, CONV_COLS), lambda r, c: (jnp.minimum(r * sub + sub, n_halo - 1), c + col0)),
                  pl.BlockSpec((CONV_W, CONV_COLS), lambda r, c: (0, c)),
                  pl.BlockSpec((1, CONV_COLS), lambda r, c: (0, c))],
        out_specs=pl.BlockSpec((SCAN_CHUNK, CONV_COLS), lambda r, c: (r, c)),
        out_shape=jax.ShapeDtypeStruct((rows, n_cols), F32),
        compiler_params=_params("parallel", "parallel"),
        name="conv_silu",
    )(u, u, u, w, b)


GDN_SUB = 64


def _mm_exact(a, b):
    return jnp.dot(a, b, preferred_element_type=F32, precision=lax.Precision.HIGHEST)


def _unit_tri_inverse(a, eye):
    p = _mm_exact(a, a)
    inv = (eye - a) + _mm_exact(eye - a, p)
    steps = int(math.log2(GDN_SUB)) - 2
    for _ in range(steps):
        p = _mm_exact(p, p)
        inv = inv + _mm_exact(inv, p)
    return inv


def _softplus(x):
    return jnp.maximum(x, 0.0) + jnp.log(1.0 + jnp.exp(-jnp.abs(x)))


def _gdn_kernel(*refs, reverse, n_r, hd):
    if reverse:
        q_ref, k_ref, v_ref, tail_ref, prow_ref, arow_ref, o_ref, s_ref = refs
    else:
        q_ref, k_ref, v_ref, tail_ref, prow_ref, arow_ref, yb_ref, z_ref, ng_ref, o_ref, s_ref = refs
    g, c = pl.program_id(1), pl.program_id(2)
    qn = SCAN_CHUNK
    n_sub = qn // GDN_SUB
    n_gate = tail_ref.shape[1] // 2

    @pl.when(c == 0)
    def _():
        s_ref[...] = jnp.zeros_like(s_ref)

    tail = tail_ref[...]
    lane = lax.broadcasted_iota(jnp.int32, (1, 2 * n_gate), 1)
    gates = jnp.where(lane < n_gate, jax.nn.sigmoid(tail), -jnp.exp(arow_ref[...]) * _softplus(tail + prow_ref[...]))
    li = lax.broadcasted_iota(jnp.int32, (2 * n_gate, 2 * n_r), 0)
    ji = lax.broadcasted_iota(jnp.int32, (2 * n_gate, 2 * n_r), 1)
    col = jnp.where(ji < n_r, ji, n_gate + ji - n_r) + (n_gate // 2 if reverse else 0) + g * n_r
    gsel = _mm_exact(gates, (li == col).astype(F32))

    ti = lax.broadcasted_iota(jnp.int32, (qn, 1), 0)
    si = lax.broadcasted_iota(jnp.int32, (1, qn), 1)
    shift = int(math.log2(GDN_SUB))
    same = lax.shift_right_logical(ti, shift) == lax.shift_right_logical(si, shift)
    if reverse:
        incl, strict = same & (si >= ti), same & (si > ti)
    else:
        incl, strict = same & (si <= ti), same & (si < ti)
    eye = (ti == si).astype(F32)
    cs = incl.astype(F32)
    cum = _mm_exact(cs, gsel)
    cum_t = lax.dot_general(gsel, cs, (((0,), (1,)), ((), ())), preferred_element_type=F32,
                            precision=lax.Precision.HIGHEST)

    q, k = q_ref[...], k_ref[...]
    qb, kb = q.astype(BF16), k.astype(BF16)
    kk = _dot_nt(kb, kb)
    qk = _dot_nt(qb, kb)
    order = range(n_sub - 1, -1, -1) if reverse else range(n_sub)
    outs = []
    for r in range(n_r):
        beta = gsel[:, r:r + 1]
        cum_c = cum[:, n_r + r:n_r + r + 1]
        cum_r = cum_t[n_r + r:n_r + r + 1, :]
        decay = jnp.exp(jnp.where(incl, cum_c - cum_r, -jnp.inf))
        a = jnp.where(strict, beta * kk * decay, 0.0)
        inv = _unit_tri_inverse(a, eye)
        v = v_ref[:, r * hd:(r + 1) * hd]
        rhs = jnp.concatenate([v * beta, k * (beta * jnp.exp(cum_c))], axis=1)
        sol = _mm_exact(inv, rhs)
        u_sol, w_sol = sol[:, :hd], sol[:, hd:]
        state = s_ref[r]
        v_new, inter = [None] * n_sub, [None] * n_sub
        for i in order:
            sl = slice(i * GDN_SUB, (i + 1) * GDN_SUB)
            sb = state.astype(BF16)
            v_new[i] = u_sol[sl] - jnp.dot(w_sol[sl].astype(BF16), sb, preferred_element_type=F32)
            inter[i] = jnp.dot(qb[sl], sb, preferred_element_type=F32)
            end = i * GDN_SUB if reverse else (i + 1) * GDN_SUB - 1
            cum_end = cum_c[end:end + 1, :]
            to_end = jnp.exp(cum_end - cum_c[sl])
            state = jnp.exp(cum_end) * state + _dot_t((k[sl] * to_end).astype(BF16), v_new[i].astype(BF16))
        s_ref[r] = state
        v_new = jnp.concatenate(v_new, axis=0)
        inter = jnp.concatenate(inter, axis=0)
        y = jnp.dot((qk * decay).astype(BF16), v_new.astype(BF16), preferred_element_type=F32) + inter * jnp.exp(cum_c)
        if not reverse:
            y = y + yb_ref[:, r * hd:(r + 1) * hd]
            y = y * lax.rsqrt(jnp.mean(y * y, axis=-1, keepdims=True) + NORM_EPS) * ng_ref[...]
            z = z_ref[:, r * hd:(r + 1) * hd]
            y = y * (z * jax.nn.sigmoid(z))
        outs.append(y)
    o_ref[...] = jnp.concatenate(outs, axis=1).astype(o_ref.dtype)


def _gdn_mixer(u, tail, conv_w, dt_bias, a_log, norm_g, bsz, lc, seq, d_model):
    n_ctx_c, n_lat_c = _scan_geometry(bsz, lc, seq)
    n_steps = n_ctx_c + n_lat_c
    hd = GDN_HEAD_DIM
    k_heads = d_model // hd
    n_r = 2
    rows = u.shape[0]
    conv_ch = 4 * d_model
    nq = d_model // CONV_COLS
    qkv = _conv_silu(u, conv_w, jnp.zeros((1, conv_ch), F32), conv_ch, bsz, lc, seq, n_scaled=nq, n_normed=2 * nq,
                     scale=hd ** -0.5, head_dim=hd)
    n_gate = tail.shape[1] // 2
    prow = jnp.concatenate([jnp.zeros((1, n_gate), F32), dt_bias.reshape(1, n_gate)], axis=1)
    arow = jnp.concatenate([jnp.zeros((1, n_gate), F32), a_log.reshape(1, n_gate)], axis=1)
    y_b = None
    for reverse in (True, False):
        rb = functools.partial(_row_block, bsz=bsz, n_ctx_c=n_ctx_c, n_lat_c=n_lat_c, reverse=reverse)
        const = lambda b, g, c: (0, 0)
        in_specs = [pl.BlockSpec((SCAN_CHUNK, hd), lambda b, g, c, rb=rb: (rb(b, c), g)),
                    pl.BlockSpec((SCAN_CHUNK, hd), lambda b, g, c, rb=rb: (rb(b, c), k_heads + g)),
                    pl.BlockSpec((SCAN_CHUNK, n_r * hd), lambda b, g, c, rb=rb: (rb(b, c), k_heads + g)),
                    pl.BlockSpec((SCAN_CHUNK, 2 * n_gate), lambda b, g, c, rb=rb: (rb(b, c), 0)),
                    pl.BlockSpec((1, 2 * n_gate), const),
                    pl.BlockSpec((1, 2 * n_gate), const)]
        args = [qkv, qkv, qkv, tail, prow, arow]
        if not reverse:
            in_specs += [pl.BlockSpec((SCAN_CHUNK, n_r * hd), lambda b, g, c, rb=rb: (rb(b, c), g)),
                         pl.BlockSpec((SCAN_CHUNK, n_r * hd), lambda b, g, c, rb=rb: (rb(b, c), 2 * k_heads + g)),
                         pl.BlockSpec((1, hd), const)]
            args += [y_b, u, norm_g.reshape(1, hd)]
        out = pl.pallas_call(
            functools.partial(_gdn_kernel, reverse=reverse, n_r=n_r, hd=hd),
            grid=(bsz, k_heads, n_steps),
            in_specs=in_specs,
            out_specs=pl.BlockSpec((SCAN_CHUNK, n_r * hd), lambda b, g, c, rb=rb: (rb(b, c), g)),
            out_shape=jax.ShapeDtypeStruct((rows, k_heads * n_r * hd), F32 if reverse else BF16),
            scratch_shapes=[pltpu.VMEM((n_r, hd, hd), F32)],
            compiler_params=_params("parallel", "parallel", "arbitrary"),
            name="gdn_bwd" if reverse else "gdn_fwd",
        )(*args)
        if reverse:
            y_b = out
    return out


def _ssd_kernel(*refs, reverse, n_r, hd):
    if reverse:
        x_ref, b_ref, c_ref, tail_ref, dtb_ref, alog_ref, o_ref, s_ref = refs
    else:
        x_ref, b_ref, c_ref, tail_ref, dtb_ref, alog_ref, yb_ref, z_ref, d_ref, ng_ref, o_ref, s_ref = refs
    g, c = pl.program_id(1), pl.program_id(2)
    qn = SCAN_CHUNK
    n_lane = tail_ref.shape[1]
    width = n_r * hd

    @pl.when(c == 0)
    def _():
        s_ref[...] = jnp.zeros_like(s_ref)

    dt_all = _softplus(tail_ref[...] + dtb_ref[...])
    la_all = -jnp.exp(alog_ref[...]) * dt_all
    li = lax.broadcasted_iota(jnp.int32, (n_lane, n_r), 0)
    ji = lax.broadcasted_iota(jnp.int32, (n_lane, n_r), 1)
    sel = (li == ji + (n_lane // 2 if reverse else 0) + g * n_r).astype(F32)
    dt, la = _mm_exact(dt_all, sel), _mm_exact(la_all, sel)

    ti = lax.broadcasted_iota(jnp.int32, (qn, 1), 0)
    si = lax.broadcasted_iota(jnp.int32, (1, qn), 1)
    before = (si >= ti) if reverse else (si <= ti)
    cs = before.astype(F32)
    cum = _mm_exact(cs, la)
    cum_t = lax.dot_general(la, cs, (((0,), (1,)), ((), ())), preferred_element_type=F32,
                            precision=lax.Precision.HIGHEST)
    ei = lax.broadcasted_iota(jnp.int32, (n_r, width), 0)
    el = lax.broadcasted_iota(jnp.int32, (n_r, width), 1)
    expand = (lax.shift_right_logical(el, int(math.log2(hd))) == ei).astype(F32)
    dt_x, cum_x = _mm_exact(dt, expand), _mm_exact(cum, expand)

    xs = x_ref[...]
    v = xs * dt_x
    bm, cm = b_ref[...].astype(BF16), c_ref[...].astype(BF16)
    scores = _dot_nt(cm, bm)
    lane = lax.broadcasted_iota(jnp.int32, (1, 2 * hd), 1)
    tiles = []
    for p in range(n_r // 2):
        vt = v[:, 2 * p * hd:2 * (p + 1) * hd]
        acc = None
        for h in (2 * p, 2 * p + 1):
            decay = jnp.exp(jnp.where(before, cum[:, h:h + 1] - cum_t[h:h + 1, :], -jnp.inf))
            vh = jnp.where((lane >= hd) if h % 2 else (lane < hd), vt, 0.0).astype(BF16)
            part = jnp.dot((scores * decay).astype(BF16), vh, preferred_element_type=F32)
            acc = part if acc is None else acc + part
        tiles.append(acc)
    state = s_ref[...]
    y = jnp.concatenate(tiles, axis=1) + jnp.dot(cm, state.astype(BF16), preferred_element_type=F32) * jnp.exp(cum_x)
    end = 0 if reverse else qn - 1
    cum_end = cum_x[end:end + 1]
    s_ref[...] = state * jnp.exp(cum_end) + _dot_t(bm, (v * jnp.exp(cum_end - cum_x)).astype(BF16))
    if reverse:
        o_ref[...] = y
    else:
        y = y + yb_ref[...] + d_ref[...] * xs
        z = z_ref[...]
        y = y * (z * jax.nn.sigmoid(z))
        y = y * lax.rsqrt(jnp.mean(y * y, axis=-1, keepdims=True) + NORM_EPS) * ng_ref[...]
        o_ref[...] = y.astype(o_ref.dtype)


def _ssd_mixer(u, tail, conv_w, conv_b, dt_bias, a_log, d_skip, norm_g, bsz, lc, seq):
    n_ctx_c, n_lat_c = _scan_geometry(bsz, lc, seq)
    n_steps = n_ctx_c + n_lat_c
    d_inner = norm_g.shape[0]
    hd, st = SSD_HEAD_DIM, SSD_STATE
    heads = d_inner // hd
    n_r = heads // SSD_GROUPS
    width = n_r * hd
    rows = u.shape[0]
    conv_ch = d_inner + 2 * SSD_GROUPS * st
    xbc = _conv_silu(u, conv_w, conv_b.reshape(1, conv_ch), conv_ch, bsz, lc, seq, col0=d_inner // CONV_COLS)
    d_x = jnp.repeat(d_skip, hd).reshape(1, d_inner)
    y_b = None
    for reverse in (True, False):
        rb = functools.partial(_row_block, bsz=bsz, n_ctx_c=n_ctx_c, n_lat_c=n_lat_c, reverse=reverse)
        const = lambda b, g, c: (0, 0)
        grp = lambda b, g, c: (0, g)
        wide = lambda b, g, c, rb=rb: (rb(b, c), g)
        in_specs = [pl.BlockSpec((SCAN_CHUNK, width), wide),
                    pl.BlockSpec((SCAN_CHUNK, st), lambda b, g, c, rb=rb: (rb(b, c), d_inner // st + g)),
                    pl.BlockSpec((SCAN_CHUNK, st), lambda b, g, c, rb=rb: (rb(b, c), d_inner // st + SSD_GROUPS + g)),
                    pl.BlockSpec((SCAN_CHUNK, 2 * heads), lambda b, g, c, rb=rb: (rb(b, c), 0)),
                    pl.BlockSpec((1, 2 * heads), const),
                    pl.BlockSpec((1, 2 * heads), const)]
        args = [xbc, xbc, xbc, tail, dt_bias.reshape(1, 2 * heads), a_log.reshape(1, 2 * heads)]
        if not reverse:
            in_specs += [pl.BlockSpec((SCAN_CHUNK, width), wide), pl.BlockSpec((SCAN_CHUNK, width), wide),
                         pl.BlockSpec((1, width), grp), pl.BlockSpec((1, width), grp)]
            args += [y_b, u, d_x, norm_g.reshape(1, d_inner)]
        out = pl.pallas_call(
            functools.partial(_ssd_kernel, reverse=reverse, n_r=n_r, hd=hd),
            grid=(bsz, SSD_GROUPS, n_steps),
            in_specs=in_specs,
            out_specs=pl.BlockSpec((SCAN_CHUNK, width), wide),
            out_shape=jax.ShapeDtypeStruct((rows, d_inner), F32 if reverse else BF16),
            scratch_shapes=[pltpu.VMEM((st, width), F32)],
            compiler_params=_params("parallel", "parallel", "arbitrary"),
            name="ssd_bwd" if reverse else "ssd_fwd",
        )(*args)
        if reverse:
            y_b = out
    return out


HG_SUB = 64
HG_BLK = 16


def _log1p(x):
    return jnp.log(1.0 + x)


def _hgrn_kernel(*refs, reverse):
    if reverse:
        q_ref, f_ref, i_ref, lb_ref, o_ref, st_ref = refs
    else:
        q_ref, f_ref, i_ref, lb_ref, yb_ref, g_ref, ng_ref, o_ref, st_ref = refs
    c = pl.program_id(2)
    qn = SCAN_CHUNK
    hd = q_ref.shape[1]
    n_sub, n_blk = qn // HG_SUB, HG_SUB // HG_BLK

    @pl.when(c == 0)
    def _():
        st_ref[...] = jnp.zeros_like(st_ref)

    q, f, v, lb = q_ref[...], f_ref[...], i_ref[...], lb_ref[...]
    log_sig = jnp.minimum(f, 0.0) - _log1p(jnp.exp(-jnp.abs(f)))
    ga, gb = jnp.log(lb), _log1p(-lb) + log_sig
    log_f = jnp.maximum(ga, gb) + _log1p(jnp.exp(-jnp.abs(ga - gb)))
    k = (1.0 - lb) * jax.nn.sigmoid(-f)

    ti = lax.broadcasted_iota(jnp.int32, (qn, 1), 0)
    si = lax.broadcasted_iota(jnp.int32, (1, qn), 1)
    before = (si >= ti) if reverse else (si <= ti)
    same = lambda n: lax.shift_right_logical(ti, int(math.log2(n))) == lax.shift_right_logical(si, int(math.log2(n)))
    cum_sub = _mm_exact((same(HG_SUB) & before).astype(F32), log_f)
    cum_blk = _mm_exact((same(HG_BLK) & before).astype(F32), log_f)
    q_blk = q * jnp.exp(cum_blk)
    q_sub = (q * jnp.exp(cum_sub)).astype(BF16)

    ones = jnp.ones((hd, hd), BF16)
    lane = lax.broadcasted_iota(jnp.int32, (1, hd), 1)
    row_sub = lax.broadcasted_iota(jnp.int32, (HG_SUB, 1), 0)
    row_blk = lax.broadcasted_iota(jnp.int32, (HG_BLK, 1), 0)
    zeros_sub = jnp.zeros((hd - HG_SUB, hd), F32)
    y_intra, kvt, chunk_dec = [], [], []
    for i in range(n_sub):
        r0 = i * HG_SUB
        sl = slice(r0, r0 + HG_SUB)
        cs, ks, vs = cum_sub[sl], k[sl], v[sl]
        end = r0 if reverse else r0 + HG_SUB - 1
        cum_end = cum_sub[end:end + 1]
        kvt.append(_dot_t(vs.astype(BF16), (ks * jnp.exp(cum_end - cs)).astype(BF16)))
        chunk_dec.append(jnp.exp(cum_end))
        a_rows = []
        for a in range(n_blk):
            b0 = r0 + a * HG_BLK
            bl = slice(b0, b0 + HG_BLK)
            qb, kb, cb = q[bl], k[bl], cum_blk[bl]
            tiles = []
            for s in range(HG_BLK):
                ok = (row_blk <= s) if reverse else (row_blk >= s)
                e = jnp.exp(jnp.where(ok, cb - cb[s:s + 1], -jnp.inf))
                tiles.append(qb * e * kb[s:s + 1])
            sums = jnp.dot(jnp.concatenate(tiles, axis=0).astype(BF16), ones, preferred_element_type=F32)
            acc = jnp.zeros((HG_BLK, hd), F32)
            for s in range(HG_BLK):
                acc = acc + jnp.where(lane == a * HG_BLK + s, sums[s * HG_BLK:(s + 1) * HG_BLK], 0.0)
            has_earlier = (a < n_blk - 1) if reverse else (a > 0)
            if has_earlier:
                ref_row = b0 + HG_BLK if reverse else b0 - 1
                earlier = (row_sub >= (a + 1) * HG_BLK) if reverse else (row_sub < a * HG_BLK)
                kt = ks * jnp.exp(jnp.where(earlier, cum_sub[ref_row:ref_row + 1] - cs, -jnp.inf))
                kt = jnp.concatenate([kt, zeros_sub], axis=0).astype(BF16)
                acc = acc + _dot_nt(q_blk[bl].astype(BF16), kt)
            a_rows.append(acc)
        attn = jnp.concatenate(a_rows, axis=0).astype(BF16)
        v_pad = jnp.concatenate([vs, zeros_sub], axis=0).astype(BF16)
        y_intra.append(jnp.dot(attn, v_pad, preferred_element_type=F32))

    state = st_ref[...]
    ys = [None] * n_sub
    for i in (range(n_sub - 1, -1, -1) if reverse else range(n_sub)):
        ys[i] = y_intra[i] + _dot_nt(q_sub[i * HG_SUB:(i + 1) * HG_SUB], state.astype(BF16))
        state = state * chunk_dec[i] + kvt[i]
    st_ref[...] = state
    y = jnp.concatenate(ys, axis=0)
    if reverse:
        o_ref[...] = y
    else:
        y = y + yb_ref[...]
        y = y * lax.rsqrt(jnp.mean(y * y, axis=-1, keepdims=True) + NORM_EPS) * ng_ref[...]
        g = g_ref[...]
        o_ref[...] = (y * (g * jax.nn.sigmoid(g))).astype(o_ref.dtype)


def _hgrn_mixer(u, lb, norm_g, bsz, lc, seq):
    n_ctx_c, n_lat_c = _scan_geometry(bsz, lc, seq)
    n_steps = n_ctx_c + n_lat_c
    hd = HGRN_EXPAND
    d_model = lb.shape[0]
    heads = d_model // hd
    rows = u.shape[0]
    lb2, ng2 = lb.reshape(1, d_model), norm_g.reshape(1, d_model)
    y_b = None
    for reverse in (True, False):
        rb = functools.partial(_row_block, bsz=bsz, n_ctx_c=n_ctx_c, n_lat_c=n_lat_c, reverse=reverse)
        col = lambda seg: (lambda b, h, c, rb=rb: (rb(b, c), seg * heads + h))
        par = lambda b, h, c: (0, h)
        in_specs = [pl.BlockSpec((SCAN_CHUNK, hd), col(0)),
                    pl.BlockSpec((SCAN_CHUNK, hd), col(2 if reverse else 1)),
                    pl.BlockSpec((SCAN_CHUNK, hd), col(3)),
                    pl.BlockSpec((1, hd), par)]
        args = [u, u, u, lb2]
        if not reverse:
            in_specs += [pl.BlockSpec((SCAN_CHUNK, hd), col(0)), pl.BlockSpec((SCAN_CHUNK, hd), col(4)),
                         pl.BlockSpec((1, hd), par)]
            args += [y_b, u, ng2]
        out = pl.pallas_call(
            functools.partial(_hgrn_kernel, reverse=reverse),
            grid=(bsz, heads, n_steps),
            in_specs=in_specs,
            out_specs=pl.BlockSpec((SCAN_CHUNK, hd), col(0)),
            out_shape=jax.ShapeDtypeStruct((rows, d_model), F32 if reverse else BF16),
            scratch_shapes=[pltpu.VMEM((hd, hd), F32)],
            compiler_params=_params("parallel", "parallel", "arbitrary"),
            name="hgrn_bwd" if reverse else "hgrn_fwd",
        )(*args)
        if reverse:
            y_b = out
    return out


def _rmsnorm(x, g, eps=NORM_EPS):
    y = x * lax.rsqrt(jnp.mean(x * x, axis=-1, keepdims=True) + eps)
    return y * g


def _l2norm(x, eps=1e-6):
    return x * lax.rsqrt(jnp.sum(x * x, axis=-1, keepdims=True) + eps)


def _dwconv(u, w):
    return lax.conv_general_dilated(u, w[:, None, :], window_strides=(1,), padding=[(CONV_W // 2, CONV_W // 2)],
                                    dimension_numbers=('NWC', 'WIO', 'NWC'), feature_group_count=u.shape[-1])


def _conv_split(u, w, lc):
    return jnp.concatenate([_dwconv(u[:, :lc], w), _dwconv(u[:, lc:], w)], axis=1)


def _rev(t, lc):
    return jnp.concatenate([jnp.flip(t[:, :lc], 1), jnp.flip(t[:, lc:], 1)], axis=1)


def _to_chunks(t):
    b, n = t.shape[:2]
    return jnp.moveaxis(t.reshape((b, n // CHUNK, CHUNK) + t.shape[2:]), 1, 0)


def _from_chunks(t):
    nc, b, q = t.shape[:3]
    return jnp.moveaxis(t, 0, 1).reshape((b, nc * q) + t.shape[3:])


def _chunk_masks():
    idx = jnp.arange(CHUNK)
    return idx[:, None] >= idx[None, :], idx[:, None] > idx[None, :]


def _scalar_decay_scan(q, k, v, log_a):
    bsz, _, g, n = q.shape
    r, p = v.shape[-2:]
    incl, _ = _chunk_masks()

    def body(s, xs):
        qc, kc, vc, la = xs
        cum = jnp.cumsum(la, axis=1)
        cum_t = jnp.moveaxis(cum, 1, -1)
        seg = cum_t[..., :, None] - cum_t[..., None, :]
        scores = jnp.einsum('btgn,bsgn->bgts', qc, kc)
        attn = scores[:, :, None] * jnp.exp(jnp.where(incl, seg, -jnp.inf))
        y = jnp.einsum('bgrts,bsgrp->btgrp', attn, vc)
        y = y + jnp.einsum('btgn,bgrnp->btgrp', qc, s) * jnp.exp(cum)[..., None]
        to_end = jnp.exp(cum[:, -1:] - cum)
        s = jnp.exp(cum[:, -1])[..., None, None] * s + jnp.einsum('bsgn,bsgr,bsgrp->bgrnp', kc, to_end, vc)
        return s, y

    s0 = jnp.zeros((bsz, g, r, n, p), F32)
    _, y = lax.scan(body, s0, tuple(_to_chunks(t) for t in (q, k, v, log_a)))
    return _from_chunks(y)


def _vector_decay_scan(q, k, v, log_f):
    bsz, _, h, kd = q.shape
    vd = v.shape[-1]
    incl, _ = _chunk_masks()

    def body(s, xs):
        qc, kc, vc, lf = xs
        cum = jnp.cumsum(lf, axis=1)
        seg = cum[:, :, None] - cum[:, None, :]
        decay = jnp.exp(jnp.where(incl[:, :, None, None], seg, -jnp.inf))
        attn = jnp.einsum('bthk,bshk,btshk->bhts', qc, kc, decay)
        y = jnp.einsum('bhts,bshv->bthv', attn, vc)
        y = y + jnp.einsum('bthk,bhkv->bthv', qc * jnp.exp(cum), s)
        s = jnp.exp(cum[:, -1])[..., None] * s + jnp.einsum('bshk,bshv->bhkv', kc * jnp.exp(cum[:, -1:] - cum), vc)
        return s, y

    s0 = jnp.zeros((bsz, h, kd, vd), F32)
    _, y = lax.scan(body, s0, tuple(_to_chunks(t) for t in (q, k, v, log_f)))
    return _from_chunks(y)


def _delta_scan(q, k, v, beta, log_a):
    bsz, _, g, kd = q.shape
    r, vd = v.shape[-2:]
    incl, strict = _chunk_masks()

    def body(s, xs):
        qc, kc, vc, bc, la = xs
        cum = jnp.cumsum(la, axis=1)
        cum_t = jnp.moveaxis(cum, 1, -1)
        seg = cum_t[..., :, None] - cum_t[..., None, :]
        beta_t = jnp.moveaxis(bc, 1, -1)
        kk = jnp.einsum('btgk,bsgk->bgts', kc, kc)
        lower = beta_t[..., :, None] * kk[:, :, None] * jnp.exp(jnp.where(strict, seg, -jnp.inf))
        rhs_v = jnp.moveaxis(vc * bc[..., None], 1, 3)
        rhs_k = jnp.moveaxis(kc[:, :, :, None, :] * (bc * jnp.exp(cum))[..., None], 1, 3)
        sol = lax.linalg.triangular_solve(lower, jnp.concatenate([rhs_v, rhs_k], axis=-1),
                                          left_side=True, lower=True, unit_diagonal=True)
        u, w = sol[..., :vd], sol[..., vd:]
        v_new = u - jnp.einsum('bgrtk,bgrkv->bgrtv', w, s)
        qk = jnp.einsum('btgk,bsgk->bgts', qc, kc)
        attn = qk[:, :, None] * jnp.exp(jnp.where(incl, seg, -jnp.inf))
        y = jnp.einsum('bgrts,bgrsv->btgrv', attn, v_new)
        y = y + jnp.einsum('btgk,bgrkv->btgrv', qc, s) * jnp.exp(cum)[..., None]
        to_end = jnp.exp(cum_t[..., -1:] - cum_t)
        s = jnp.exp(cum_t[..., -1])[..., None, None] * s + jnp.einsum('bsgk,bgrs,bgrsv->bgrkv', kc, to_end, v_new)
        return s, y

    s0 = jnp.zeros((bsz, g, r, kd, vd), F32)
    _, y = lax.scan(body, s0, tuple(_to_chunks(t) for t in (q, k, v, beta, log_a)))
    return _from_chunks(y)


def _rope_tables(rows, half):
    pos = np.arange(rows * GRID_W)
    inv_freq = np.float32(ROPE_BASE) ** (-(np.arange(0, half, 2, dtype=np.float32) / np.float32(half)))
    out = []
    for p in ((pos // GRID_W).astype(np.float32), (pos % GRID_W).astype(np.float32)):
        ang = (p[:, None] * inv_freq.astype(np.float32)).astype(np.float32).astype(np.float64)
        out += [np.cos(ang).astype(np.float32), np.sin(ang).astype(np.float32)]
    return out


def _rope_2d(t, rows):
    half = t.shape[-1] // 2
    cos_r, sin_r, cos_c, sin_c = (jnp.asarray(a)[:, None, :] for a in _rope_tables(rows, half))

    def rot(u, cos, sin):
        u1, u2 = jnp.split(u, 2, axis=-1)
        return jnp.concatenate([u1 * cos - u2 * sin, u2 * cos + u1 * sin], axis=-1)

    return jnp.concatenate([rot(t[..., :half], cos_r, sin_r), rot(t[..., half:], cos_c, sin_c)], axis=-1)


def _ssd_core(u, lc, conv_w, conv_b, dt_bias, a_log, d_skip, norm_g, start):
    bsz, t = u.shape[:2]
    d_inner = norm_g.shape[0]
    heads = d_inner // SSD_HEAD_DIM
    conv_ch = d_inner + 2 * SSD_GROUPS * SSD_STATE
    z, xbc, dt = jnp.split(u, [d_inner, d_inner + conv_ch], axis=-1)
    xbc = jax.nn.silu(_conv_split(xbc, conv_w, lc) + conv_b)
    xs, bm, cm = jnp.split(xbc, [d_inner, d_inner + SSD_GROUPS * SSD_STATE], axis=-1)
    r = heads // SSD_GROUPS
    xs = xs.reshape(bsz, t, SSD_GROUPS, r, SSD_HEAD_DIM)
    bm = bm.reshape(bsz, t, SSD_GROUPS, SSD_STATE)
    cm = cm.reshape(bsz, t, SSD_GROUPS, SSD_STATE)
    dt = jax.nn.softplus(dt.reshape(bsz, t, 2, heads) + dt_bias)
    log_a = -jnp.exp(a_log) * dt
    grp = lambda a: a.reshape(bsz, t, SSD_GROUPS, r)
    y = _scalar_decay_scan(cm, bm, xs * grp(dt[:, :, 0])[..., None], grp(log_a[:, :, 0]))
    y = y + _rev(_scalar_decay_scan(_rev(cm, lc), _rev(bm, lc), _rev(xs * grp(dt[:, :, 1])[..., None], lc),
                                    _rev(grp(log_a[:, :, 1]), lc)), lc)
    y = y + d_skip.reshape(SSD_GROUPS, r)[..., None] * xs
    n = t - start
    y = y.reshape(bsz, t, d_inner)[:, start:] * jax.nn.silu(z[:, start:])
    return _rmsnorm(y.reshape(bsz, n, SSD_GROUPS, -1), norm_g.reshape(SSD_GROUPS, -1)).reshape(bsz, n, d_inner)


def _ret_core(u, lc, log_decay, d_model, rows, start):
    bsz, t = u.shape[:2]
    dv = 2 * d_model
    qk_dim = d_model // RET_HEADS
    v_dim = 2 * qk_dim
    q, k, v, g = jnp.split(u, [d_model, 2 * d_model, 2 * d_model + dv], axis=-1)
    q = q.reshape(bsz, t, RET_HEADS, qk_dim)
    k = k.reshape(bsz, t, RET_HEADS, qk_dim) * qk_dim ** -0.5
    q = jnp.concatenate([q[:, :lc], _rope_2d(q[:, lc:], rows)], axis=1)
    k = jnp.concatenate([k[:, :lc], _rope_2d(k[:, lc:], rows)], axis=1)
    v = v.reshape(bsz, t, RET_HEADS, 1, v_dim)
    ld_f = jnp.broadcast_to(log_decay[0][:, None], (bsz, t, RET_HEADS, 1))
    ld_b = jnp.broadcast_to(log_decay[1][:, None], (bsz, t, RET_HEADS, 1))
    y = _scalar_decay_scan(q, k, v, ld_f)
    y = y + _rev(_scalar_decay_scan(_rev(q, lc), _rev(k, lc), _rev(v, lc), ld_b), lc)
    n = t - start
    y = y[:, start:].reshape(bsz, n, RET_HEADS, v_dim)
    mu = jnp.mean(y, axis=-1, keepdims=True)
    var = jnp.mean(jnp.square(y - mu), axis=-1, keepdims=True)
    y = ((y - mu) * lax.rsqrt(var + NORM_EPS)).reshape(bsz, n, dv)
    return y * jax.nn.silu(g[:, start:])


def _lower_bound(lb_logits, layer):
    p = jax.nn.softmax(lb_logits.astype(F32), axis=0)
    return jnp.cumsum(p, axis=0)[layer] - p[0]


def _hgrn_core(u, lc, lb, norm_g, start):
    bsz, t = u.shape[:2]
    d_model = norm_g.shape[0]
    heads = d_model // HGRN_EXPAND
    q, f_f, f_b, i, g = jnp.split(u, 5, axis=-1)
    shp = (bsz, t, heads, HGRN_EXPAND)
    q, i = q.reshape(shp), i.reshape(shp)
    lb = lb.reshape(heads, HGRN_EXPAND)

    def gates(f):
        f = f.reshape(shp)
        log_f = jnp.logaddexp(jnp.log(lb), jnp.log1p(-lb) + jax.nn.log_sigmoid(f))
        return log_f, (1 - lb) * jax.nn.sigmoid(-f)

    lf_f, k_f = gates(f_f)
    lf_b, k_b = gates(f_b)
    y = _vector_decay_scan(q, k_f, i, lf_f)
    y = y + _rev(_vector_decay_scan(_rev(q, lc), _rev(k_b, lc), _rev(i, lc), _rev(lf_b, lc)), lc)
    n = t - start
    y = _rmsnorm(y[:, start:], norm_g.reshape(heads, HGRN_EXPAND))
    return (y * jax.nn.silu(g[:, start:].reshape(bsz, n, heads, HGRN_EXPAND))).reshape(bsz, n, d_model)


def _gdn_core(u, lc, conv_w, dt_bias, a_log, norm_g, d_model, start):
    bsz, t = u.shape[:2]
    k_heads = d_model // GDN_HEAD_DIM
    v_heads = 2 * k_heads
    dk, dv = d_model, 2 * d_model
    conv_ch = 2 * dk + dv
    qkv, z, bt, a = jnp.split(u, [conv_ch, conv_ch + dv, conv_ch + dv + 2 * v_heads], axis=-1)
    qkv = jax.nn.silu(_conv_split(qkv, conv_w, lc))
    q, k, v = jnp.split(qkv, [dk, 2 * dk], axis=-1)
    r = v_heads // k_heads
    q = _l2norm(q.reshape(bsz, t, k_heads, GDN_HEAD_DIM)) * GDN_HEAD_DIM ** -0.5
    k = _l2norm(k.reshape(bsz, t, k_heads, GDN_HEAD_DIM))
    v = v.reshape(bsz, t, k_heads, r, GDN_HEAD_DIM)
    beta = jax.nn.sigmoid(bt.reshape(bsz, t, 2, k_heads, r))
    log_a = -jnp.exp(a_log).reshape(2, k_heads, r) * jax.nn.softplus(
        a.reshape(bsz, t, 2, k_heads, r) + dt_bias.reshape(2, k_heads, r))
    y = _delta_scan(q, k, v, beta[:, :, 0], log_a[:, :, 0])
    y = y + _rev(_delta_scan(_rev(q, lc), _rev(k, lc), _rev(v, lc), _rev(beta[:, :, 1], lc),
                             _rev(log_a[:, :, 1], lc)), lc)
    n = t - start
    y = y[:, start:].reshape(bsz, n, v_heads, GDN_HEAD_DIM)
    y = _rmsnorm(y, norm_g) * jax.nn.silu(z[:, start:].reshape(bsz, n, v_heads, GDN_HEAD_DIM))
    return y.reshape(bsz, n, dv)


def _rows_to_seq(u, bsz, lc):
    n_ctx = bsz * lc
    return jnp.concatenate([u[:n_ctx].reshape(bsz, lc, -1), u[n_ctx:].reshape(bsz, -1, u.shape[-1])], axis=1)


def _seq_to_rows(y, lc, start):
    bsz = y.shape[0]
    if start:
        return y.reshape(bsz * y.shape[1], -1)
    return jnp.concatenate([y[:, :lc].reshape(bsz * lc, -1), y[:, lc:].reshape(-1, y.shape[-1])], axis=0)


def kernel(x, c, ctx, c_ctx, ada_w, ada_b, norm_g, mlp_w1, mlp_w2, final_g, ssd_w_in, ssd_conv_w, ssd_conv_b,
           ssd_dt_bias, ssd_a_log, ssd_d, ssd_norm_g, ssd_w_out, ret_w_in, ret_log_decay, ret_w_out, hgrn_w_in,
           hgrn_lb_logits, hgrn_norm_g, hgrn_w_out, gdn_w_in, gdn_conv_w, gdn_dt_bias, gdn_a_log, gdn_norm_g,
           gdn_w_out):
    bsz, seq, d = x.shape
    lc = ctx.shape[1]
    depth = ada_w.shape[0]
    rows_grid = seq // GRID_W
    n_ctx = bsz * lc
    assert lc % ROW_TILE == 0 or ROW_TILE % lc == 0 and n_ctx % ROW_TILE == 0
    assert seq % ROW_TILE == 0 and bsz + 1 <= 8

    cond_pad = jnp.concatenate([c, c_ctx[None], jnp.zeros((8 - bsz - 1, d), F32)], axis=0)
    mod = _ada_mod(cond_pad, ada_w, ada_b)

    tile_row = [bsz] * (n_ctx // ROW_TILE) + [b for b in range(bsz) for _ in range(seq // ROW_TILE)]
    tile_row = jnp.asarray(tile_row, jnp.int32)
    n_ctx_tiles = n_ctx // ROW_TILE

    xr = jnp.concatenate([ctx.reshape(n_ctx, d), x.reshape(bsz * seq, d)], axis=0)

    for i in range(depth):
        mixer, occ = i % 4, i // 4
        keep_ctx = i < depth - 1
        start = 0 if keep_ctx else lc
        mod_t = mod[i][tile_row][:, None, :]
        g0, g1 = norm_g[i, 0][None], norm_g[i, 1][None]
        if mixer == 0:
            w_in, w_out = ssd_w_in[occ], ssd_w_out[occ]
        elif mixer == 1:
            w_in, w_out = ret_w_in[occ], ret_w_out[occ]
        elif mixer == 2:
            w_in, w_out = hgrn_w_in[occ], hgrn_w_out[occ]
        else:
            w_in, w_out = gdn_w_in[occ], gdn_w_out[occ]
        n_in = w_in.shape[1]
        n_main = (n_in // 1024) * 1024 if n_in % 1024 else n_in
        w_in = w_in.astype(BF16)
        u = _ln_mm(xr, g0, mod_t, 0, 1, w_in[:, :n_main])
        tail = _ln_mm(xr, g0, mod_t, 0, 1, w_in[:, n_main:]) if n_main != n_in else None
        first_row = 0 if keep_ctx else n_ctx
        if mixer == 0:
            yr = _ssd_mixer(u, tail, ssd_conv_w[occ], ssd_conv_b[occ], ssd_dt_bias[occ], ssd_a_log[occ], ssd_d[occ],
                            ssd_norm_g[occ], bsz, lc, seq)[first_row:]
        elif mixer == 1:
            yr = _ret_mixer(u, ret_log_decay[occ], bsz, lc, seq, d)[first_row:]
        elif mixer == 2:
            yr = _hgrn_mixer(u, _lower_bound(hgrn_lb_logits, i), hgrn_norm_g[occ], bsz, lc, seq)[first_row:]
        elif mixer == 3:
            yr = _gdn_mixer(u, tail, gdn_conv_w[occ], gdn_dt_bias[occ], gdn_a_log[occ], gdn_norm_g[occ], bsz, lc,
                            seq, d)[first_row:]
        else:
            if tail is not None:
                u = jnp.concatenate([u, tail], axis=1)
            useq = _rows_to_seq(u, bsz, lc)
            if mixer == 0:
                y = _ssd_core(useq, lc, ssd_conv_w[occ], ssd_conv_b[occ], ssd_dt_bias[occ], ssd_a_log[occ],
                              ssd_d[occ], ssd_norm_g[occ], start)
            elif mixer == 2:
                y = _hgrn_core(useq, lc, _lower_bound(hgrn_lb_logits, i), hgrn_norm_g[occ], start)
            else:
                y = _gdn_core(useq, lc, gdn_conv_w[occ], gdn_dt_bias[occ], gdn_a_log[occ], gdn_norm_g[occ], d,
                              start)
            yr = _seq_to_rows(y, lc, start).astype(BF16)
        if not keep_ctx:
            xr = xr[n_ctx:]
            mod_t = mod_t[n_ctx_tiles:]
        xr = _out_proj(yr, w_out.astype(BF16), xr, mod_t, 2)
        xr = _mlp(xr, g1, mod_t, mlp_w1[i].astype(BF16), mlp_w2[i].astype(BF16), final_g[None], final=not keep_ctx)
    return xr.reshape(bsz, seq, d)
```

```python
import functools
import math

import jax
import jax.numpy as jnp
import numpy as np
from jax import lax
from jax.experimental import pallas as pl
from jax.experimental.pallas import tpu as pltpu

F32 = jnp.float32
BF16 = jnp.bfloat16

GRID_W = 64
CHUNK = 64
CONV_W = 3
NORM_EPS = 1e-6
ROPE_BASE = 10000.0
SSD_HEAD_DIM = 64
SSD_GROUPS = 8
SSD_STATE = 128
RET_HEADS = 8
HGRN_EXPAND = 128
GDN_HEAD_DIM = 128

ROW_TILE = 512
VMEM_LIMIT = 56 * 1024 * 1024


def _params(*sem):
    return pltpu.CompilerParams(dimension_semantics=sem, vmem_limit_bytes=VMEM_LIMIT)


def _col_tile(n, cap=1536):
    best = 128
    for t in range(128, cap + 1, 128):
        if n % t == 0:
            best = t
    return best


def _ada_kernel(c_ref, w_ref, b_ref, o_ref):
    c = c_ref[...]
    c = (c * jax.nn.sigmoid(c)).astype(BF16)
    o_ref[0] = jnp.dot(c, w_ref[0].astype(BF16), preferred_element_type=F32) + b_ref[0]


def _ada_mod(cond_pad, ada_w, ada_b):
    depth, d, n = ada_w.shape
    tn = 1024
    return pl.pallas_call(
        _ada_kernel,
        grid=(depth, n // tn),
        in_specs=[pl.BlockSpec((8, d), lambda l, j: (0, 0)),
                  pl.BlockSpec((1, d, tn), lambda l, j: (l, 0, j)),
                  pl.BlockSpec((1, 1, tn), lambda l, j: (l, 0, j))],
        out_specs=pl.BlockSpec((1, 8, tn), lambda l, j: (l, 0, j)),
        out_shape=jax.ShapeDtypeStruct((depth, 8, n), F32),
        compiler_params=_params("parallel", "parallel"),
        name="ada_mod",
    )(cond_pad, ada_w, ada_b.reshape(depth, 1, n))


def _adaln_rows(x, g, sh, sc):
    y = x * lax.rsqrt(jnp.mean(x * x, axis=-1, keepdims=True) + NORM_EPS)
    return y * g * (1.0 + sc) + sh


def _ln_mm_kernel(x_ref, g_ref, sh_ref, sc_ref, w_ref, o_ref, h_ref):
    @pl.when(pl.program_id(1) == 0)
    def _():
        h_ref[...] = _adaln_rows(x_ref[...], g_ref[...], sh_ref[0], sc_ref[0]).astype(BF16)

    o_ref[...] = jnp.dot(h_ref[...], w_ref[...], preferred_element_type=F32).astype(o_ref.dtype)


def _ln_mm(x, g, mod_t, sh_col, sc_col, w, out_dtype=F32):
    m, d = x.shape
    n = w.shape[1]
    tm, tn = ROW_TILE, _col_tile(n)
    return pl.pallas_call(
        _ln_mm_kernel,
        grid=(m // tm, n // tn),
        in_specs=[pl.BlockSpec((tm, d), lambda i, j: (i, 0)),
                  pl.BlockSpec((1, d), lambda i, j: (0, 0)),
                  pl.BlockSpec((1, 1, d), lambda i, j: (i, 0, sh_col)),
                  pl.BlockSpec((1, 1, d), lambda i, j: (i, 0, sc_col)),
                  pl.BlockSpec((d, tn), lambda i, j: (0, j))],
        out_specs=pl.BlockSpec((tm, tn), lambda i, j: (i, j)),
        out_shape=jax.ShapeDtypeStruct((m, n), out_dtype),
        scratch_shapes=[pltpu.VMEM((tm, d), BF16)],
        compiler_params=_params("parallel", "arbitrary"),
        name="adaln_in_proj",
    )(x, g, mod_t, mod_t, w)


def _out_kernel(y_ref, w_ref, x_ref, gate_ref, o_ref):
    o_ref[...] = x_ref[...] + gate_ref[0] * jnp.dot(y_ref[...], w_ref[...], preferred_element_type=F32)


def _out_proj(y, w, x, mod_t, gate_col):
    m, k = y.shape
    d = w.shape[1]
    tm, tn = ROW_TILE, 512
    return pl.pallas_call(
        _out_kernel,
        grid=(m // tm, d // tn),
        in_specs=[pl.BlockSpec((tm, k), lambda i, j: (i, 0)),
                  pl.BlockSpec((k, tn), lambda i, j: (0, j)),
                  pl.BlockSpec((tm, tn), lambda i, j: (i, j)),
                  pl.BlockSpec((1, 1, tn), lambda i, j: (i, 0, gate_col * (d // tn) + j))],
        out_specs=pl.BlockSpec((tm, tn), lambda i, j: (i, j)),
        out_shape=jax.ShapeDtypeStruct((m, d), F32),
        compiler_params=_params("parallel", "arbitrary"),
        name="out_proj",
    )(y, w, x, mod_t)


def _mlp_kernel(x_ref, g_ref, sh_ref, sc_ref, gate_ref, w1_ref, w2_ref, fg_ref, o_ref, h_ref, acc_ref, *, final):
    f = pl.program_id(1)

    @pl.when(f == 0)
    def _():
        h_ref[...] = _adaln_rows(x_ref[...], g_ref[...], sh_ref[0], sc_ref[0]).astype(BF16)
        acc_ref[...] = jnp.zeros_like(acc_ref)

    a = jnp.dot(h_ref[...], w1_ref[...], preferred_element_type=F32)
    a = jnp.square(jnp.maximum(a, 0.0)).astype(BF16)
    acc_ref[...] += jnp.dot(a, w2_ref[...], preferred_element_type=F32)

    @pl.when(f == pl.num_programs(1) - 1)
    def _():
        out = x_ref[...] + gate_ref[0] * acc_ref[...]
        if final:
            out = out * lax.rsqrt(jnp.mean(out * out, axis=-1, keepdims=True) + NORM_EPS) * fg_ref[...]
        o_ref[...] = out


def _mlp(x, g, mod_t, w1, w2, final_g, final):
    m, d = x.shape
    ff = w1.shape[1]
    tm, tf = ROW_TILE, 512
    return pl.pallas_call(
        functools.partial(_mlp_kernel, final=final),
        grid=(m // tm, ff // tf),
        in_specs=[pl.BlockSpec((tm, d), lambda i, f: (i, 0)),
                  pl.BlockSpec((1, d), lambda i, f: (0, 0)),
                  pl.BlockSpec((1, 1, d), lambda i, f: (i, 0, 3)),
                  pl.BlockSpec((1, 1, d), lambda i, f: (i, 0, 4)),
                  pl.BlockSpec((1, 1, d), lambda i, f: (i, 0, 5)),
                  pl.BlockSpec((d, tf), lambda i, f: (0, f)),
                  pl.BlockSpec((tf, d), lambda i, f: (f, 0)),
                  pl.BlockSpec((1, d), lambda i, f: (0, 0))],
        out_specs=pl.BlockSpec((tm, d), lambda i, f: (i, 0)),
        out_shape=jax.ShapeDtypeStruct((m, d), F32),
        scratch_shapes=[pltpu.VMEM((tm, d), BF16), pltpu.VMEM((tm, d), F32)],
        compiler_params=_params("parallel", "arbitrary"),
        name="adaln_mlp",
    )(x, g, mod_t, mod_t, mod_t, w1, w2, final_g)


SCAN_CHUNK = 256


def _scan_geometry(bsz, lc, seq):
    assert lc % SCAN_CHUNK == 0 and seq % SCAN_CHUNK == 0
    return lc // SCAN_CHUNK, seq // SCAN_CHUNK


def _seg_chunk(c, n_ctx_c, n_lat_c, reverse):
    if reverse:
        return c >= n_ctx_c, jnp.where(c < n_ctx_c, n_ctx_c - 1 - c, n_lat_c - 1 - (c - n_ctx_c))
    return c >= n_ctx_c, jnp.where(c < n_ctx_c, c, c - n_ctx_c)


def _row_block(b, c, bsz, n_ctx_c, n_lat_c, reverse):
    is_lat, j = _seg_chunk(c, n_ctx_c, n_lat_c, reverse)
    return jnp.where(is_lat, bsz * n_ctx_c + b * n_lat_c + j, b * n_ctx_c + j)


def _time_iotas(q):
    t = lax.broadcasted_iota(jnp.int32, (q, 1), 0).astype(F32)
    s = lax.broadcasted_iota(jnp.int32, (1, q), 1).astype(F32)
    return t, s


def _dot_t(a, b):
    return lax.dot_general(a, b, (((0,), (0,)), ((), ())), preferred_element_type=F32)


def _dot_nt(a, b):
    return lax.dot_general(a, b, (((1,), (1,)), ((), ())), preferred_element_type=F32)


def _ret_kernel(*refs, reverse, n_ctx_c, n_heads, k_scale):
    if reverse:
        ld_ref, q_ref, k_ref, v_ref, cos_ref, sin_ref, o_ref, s_ref = refs
    else:
        ld_ref, q_ref, k_ref, v_ref, cos_ref, sin_ref, yb_ref, g_ref, o_ref, s_ref = refs
    h, c = pl.program_id(1), pl.program_id(2)
    qn = SCAN_CHUNK

    @pl.when(c == 0)
    def _():
        s_ref[...] = jnp.zeros_like(s_ref)

    lg = ld_ref[(n_heads if reverse else 0) + h]
    t, s = _time_iotas(qn)
    is_lat = c >= n_ctx_c
    cos = jnp.where(is_lat, cos_ref[...], 1.0)
    sin = jnp.where(is_lat, sin_ref[...], 0.0)

    def rope(x):
        half = x.shape[1] // 2
        swapped = jnp.concatenate([pltpu.roll(x[:, :half], half // 2, 1), pltpu.roll(x[:, half:], half // 2, 1)], axis=1)
        return x * cos + swapped * sin

    q = rope(q_ref[...]).astype(BF16)
    k = rope(k_ref[...]) * k_scale
    v = v_ref[...].astype(BF16)
    if reverse:
        dmat = jnp.where(s >= t, jnp.exp((s - t) * lg), 0.0)
        q_dec, k_dec = jnp.exp((qn - t) * lg), jnp.exp(t * lg)
    else:
        dmat = jnp.where(t >= s, jnp.exp((t - s) * lg), 0.0)
        q_dec, k_dec = jnp.exp((t + 1.0) * lg), jnp.exp((qn - 1.0 - t) * lg)
    attn = (_dot_nt(q, k.astype(BF16)) * dmat).astype(BF16)
    state = s_ref[...]
    y = jnp.dot(attn, v, preferred_element_type=F32)
    y = y + jnp.dot(q, state.astype(BF16), preferred_element_type=F32) * q_dec
    chunk_dec = jnp.exp(jnp.full((1, 1), qn, F32) * lg)
    s_ref[...] = state * chunk_dec + _dot_t((k * k_dec).astype(BF16), v)
    if reverse:
        o_ref[...] = y
    else:
        y = y + yb_ref[...]
        mu = jnp.mean(y, axis=-1, keepdims=True)
        yc = y - mu
        var = jnp.mean(yc * yc, axis=-1, keepdims=True)
        g = g_ref[...]
        o_ref[...] = (yc * lax.rsqrt(var + NORM_EPS) * (g * jax.nn.sigmoid(g))).astype(o_ref.dtype)


def _ret_mixer(u, log_decay, bsz, lc, seq, d_model):
    n_ctx_c, n_lat_c = _scan_geometry(bsz, lc, seq)
    n_steps = n_ctx_c + n_lat_c
    qk_dim, v_dim = d_model // RET_HEADS, 2 * d_model // RET_HEADS
    rows = u.shape[0]
    cos_r, sin_r, cos_c, sin_c = _rope_tables(seq // GRID_W, qk_dim // 2)
    cos_t = jnp.asarray(np.concatenate([cos_r, cos_r, cos_c, cos_c], axis=1))
    sin_t = jnp.asarray(np.concatenate([-sin_r, sin_r, -sin_c, sin_c], axis=1))
    ld = log_decay.reshape(-1).astype(F32)
    y_b = None
    for reverse in (True, False):
        rb = functools.partial(_row_block, bsz=bsz, n_ctx_c=n_ctx_c, n_lat_c=n_lat_c, reverse=reverse)

        def tab(b, h, c, reverse=reverse):
            is_lat, j = _seg_chunk(c, n_ctx_c, n_lat_c, reverse)
            return (jnp.where(is_lat, j, 0), 0)

        in_specs = [pl.BlockSpec(memory_space=pltpu.SMEM),
                    pl.BlockSpec((SCAN_CHUNK, qk_dim), lambda b, h, c, rb=rb: (rb(b, c), h)),
                    pl.BlockSpec((SCAN_CHUNK, qk_dim), lambda b, h, c, rb=rb: (rb(b, c), RET_HEADS + h)),
                    pl.BlockSpec((SCAN_CHUNK, v_dim), lambda b, h, c, rb=rb: (rb(b, c), RET_HEADS + h)),
                    pl.BlockSpec((SCAN_CHUNK, qk_dim), tab),
                    pl.BlockSpec((SCAN_CHUNK, qk_dim), tab)]
        args = [ld, u, u, u, cos_t, sin_t]
        if not reverse:
            in_specs += [pl.BlockSpec((SCAN_CHUNK, v_dim), lambda b, h, c, rb=rb: (rb(b, c), h)),
                         pl.BlockSpec((SCAN_CHUNK, v_dim), lambda b, h, c, rb=rb: (rb(b, c), 2 * RET_HEADS + h))]
            args += [y_b, u]
        out = pl.pallas_call(
            functools.partial(_ret_kernel, reverse=reverse, n_ctx_c=n_ctx_c, n_heads=RET_HEADS, k_scale=qk_dim ** -0.5),
            grid=(bsz, RET_HEADS, n_steps),
            in_specs=in_specs,
            out_specs=pl.BlockSpec((SCAN_CHUNK, v_dim), lambda b, h, c, rb=rb: (rb(b, c), h)),
            out_shape=jax.ShapeDtypeStruct((rows, RET_HEADS * v_dim), F32 if reverse else BF16),
            scratch_shapes=[pltpu.VMEM((qk_dim, v_dim), F32)],
            compiler_params=_params("parallel", "parallel", "arbitrary"),
            name="retention_bwd" if reverse else "retention_fwd",
        )(*args)
        if reverse:
            y_b = out
    return out


CONV_COLS = 512


def _conv_kernel(x_ref, prev_ref, next_ref, w_ref, b_ref, o_ref, *, n_ctx_c, n_lat_c, bsz, n_scaled, n_normed, scale,
                 head_dim):
    rb, cb = pl.program_id(0), pl.program_id(1)
    is_lat = rb >= bsz * n_ctx_c
    j = jnp.where(is_lat, (rb - bsz * n_ctx_c) % n_lat_c, rb % n_ctx_c)
    last = jnp.where(is_lat, n_lat_c - 1, n_ctx_c - 1)
    x = x_ref[...]
    rows = x.shape[0]
    row = lax.broadcasted_iota(jnp.int32, (rows, 1), 0)
    halo_prev = jnp.where(j > 0, prev_ref[7:8, :], 0.0)
    halo_next = jnp.where(j < last, next_ref[0:1, :], 0.0)
    x_prev = jnp.where(row == 0, halo_prev, pltpu.roll(x, 1, 0))
    x_next = jnp.where(row == rows - 1, halo_next, pltpu.roll(x, rows - 1, 0))
    y = w_ref[0:1, :] * x_prev + w_ref[1:2, :] * x + w_ref[2:3, :] * x_next + b_ref[...]
    y = y * jax.nn.sigmoid(y)

    def normed(mult):
        parts = []
        for i in range(y.shape[1] // head_dim):
            p = y[:, i * head_dim:(i + 1) * head_dim]
            parts.append(p * (lax.rsqrt(jnp.sum(p * p, axis=-1, keepdims=True) + 1e-6) * mult))
        return jnp.concatenate(parts, axis=1)

    if n_normed == 0:
        o_ref[...] = y
    else:
        @pl.when(cb < n_scaled)
        def _():
            o_ref[...] = normed(scale)

        @pl.when(jnp.logical_and(cb >= n_scaled, cb < n_normed))
        def _():
            o_ref[...] = normed(1.0)

        @pl.when(cb >= n_normed)
        def _():
            o_ref[...] = y


def _conv_silu(u, w, b, n_cols, bsz, lc, seq, n_scaled=0, n_normed=0, scale=1.0, head_dim=128, col0=0):
    n_ctx_c, n_lat_c = _scan_geometry(bsz, lc, seq)
    rows = u.shape[0]
    n_rb = rows // SCAN_CHUNK
    sub = SCAN_CHUNK // 8
    n_halo = rows // 8
    return pl.pallas_call(
        functools.partial(_conv_kernel, n_ctx_c=n_ctx_c, n_lat_c=n_lat_c, bsz=bsz, n_scaled=n_scaled,
                          n_normed=n_normed, scale=scale, head_dim=head_dim),
        grid=(n_rb, n_cols // CONV_COLS),
        in_specs=[pl.BlockSpec((SCAN_CHUNK, CONV_COLS), lambda r, c: (r, c + col0)),
                  pl.BlockSpec((8, CONV_COLS), lambda r, c: (jnp.maximum(r * sub - 1, 0), c + col0)),
                  pl.BlockSpec((8, CONV_COLS), lambda r, c: (jnp.minimum(r * sub + sub, n_halo - 1), c + col0)),
                  pl.BlockSpec((CONV_W, CONV_COLS), lambda r, c: (0, c)),
                  pl.BlockSpec((1, CONV_COLS), lambda r, c: (0, c))],
        out_specs=pl.BlockSpec((SCAN_CHUNK, CONV_COLS), lambda r, c: (r, c)),
        out_shape=jax.ShapeDtypeStruct((rows, n_cols), F32),
        compiler_params=_params("parallel", "parallel"),
        name="conv_silu",
    )(u, u, u, w, b)


GDN_SUB = 64


GDN_BASE = 8


def _split3(x):
    x1 = x.astype(BF16)
    r1 = x - x1.astype(F32)
    x2 = r1.astype(BF16)
    return x1, x2, (r1 - x2.astype(F32)).astype(BF16)


def _sel_rows(m, x):
    mb = m.astype(BF16)
    return sum(jnp.dot(mb, p, preferred_element_type=F32) for p in _split3(x))


def _sel_cols(x, m):
    mb = m.astype(BF16)
    return sum(jnp.dot(p, mb, preferred_element_type=F32) for p in _split3(x))


def _sel_rows_t(x, m):
    mb = m.astype(BF16)
    return sum(lax.dot_general(p, mb, (((0,), (1,)), ((), ())), preferred_element_type=F32) for p in _split3(x))


def _mm_bf16(a, b):
    return jnp.dot(a.astype(BF16), b.astype(BF16), preferred_element_type=F32)


def _unit_tri_inverse_minus_eye(a, ti, si):
    def same(n):
        s = int(math.log2(n))
        return lax.shift_right_logical(ti, s) == lax.shift_right_logical(si, s)

    ab = jnp.where(same(GDN_BASE), a, 0.0)
    n = -ab
    p = ab
    for _ in range(int(math.log2(GDN_BASE)) - 1):
        p = _mm_bf16(p, p)
        n = n + p + _mm_bf16(n, p)
    b = GDN_BASE
    while b < GDN_SUB:
        cmat = jnp.where(same(2 * b) & jnp.logical_not(same(b)), a, 0.0)
        m = cmat + _mm_bf16(cmat, n)
        n = n - (m + _mm_bf16(n, m))
        b *= 2
    return n


def _softplus(x):
    return jnp.maximum(x, 0.0) + jnp.log(1.0 + jnp.exp(-jnp.abs(x)))


def _gdn_kernel(*refs, reverse, n_r, hd):
    if reverse:
        q_ref, k_ref, v_ref, tail_ref, prow_ref, arow_ref, o_ref, s_ref = refs
    else:
        q_ref, k_ref, v_ref, tail_ref, prow_ref, arow_ref, yb_ref, z_ref, ng_ref, o_ref, s_ref = refs
    g, c = pl.program_id(1), pl.program_id(2)
    qn = SCAN_CHUNK
    n_sub = qn // GDN_SUB
    n_gate = tail_ref.shape[1] // 2

    @pl.when(c == 0)
    def _():
        s_ref[...] = jnp.zeros_like(s_ref)

    tail = tail_ref[...]
    lane = lax.broadcasted_iota(jnp.int32, (1, 2 * n_gate), 1)
    gates = jnp.where(lane < n_gate, jax.nn.sigmoid(tail), -jnp.exp(arow_ref[...]) * _softplus(tail + prow_ref[...]))
    li = lax.broadcasted_iota(jnp.int32, (2 * n_gate, 2 * n_r), 0)
    ji = lax.broadcasted_iota(jnp.int32, (2 * n_gate, 2 * n_r), 1)
    col = jnp.where(ji < n_r, ji, n_gate + ji - n_r) + (n_gate // 2 if reverse else 0) + g * n_r
    gsel = _sel_cols(gates, (li == col).astype(F32))

    ti = lax.broadcasted_iota(jnp.int32, (qn, 1), 0)
    si = lax.broadcasted_iota(jnp.int32, (1, qn), 1)
    shift = int(math.log2(GDN_SUB))
    same = lax.shift_right_logical(ti, shift) == lax.shift_right_logical(si, shift)
    if reverse:
        incl, strict = same & (si >= ti), same & (si > ti)
    else:
        incl, strict = same & (si <= ti), same & (si < ti)
    cs = incl.astype(F32)
    cum = _sel_rows(cs, gsel)
    cum_t = _sel_rows_t(gsel, cs)

    q, k = q_ref[...], k_ref[...]
    qb, kb = q.astype(BF16), k.astype(BF16)
    kk = _dot_nt(kb, kb)
    qk = _dot_nt(qb, kb)
    order = range(n_sub - 1, -1, -1) if reverse else range(n_sub)
    outs = []
    for r in range(n_r):
        beta = gsel[:, r:r + 1]
        cum_c = cum[:, n_r + r:n_r + r + 1]
        cum_r = cum_t[n_r + r:n_r + r + 1, :]
        decay = jnp.exp(jnp.where(incl, cum_c - cum_r, -jnp.inf))
        a = jnp.where(strict, beta * kk * decay, 0.0)
        inv_off = _unit_tri_inverse_minus_eye(a, ti, si)
        v = v_ref[:, r * hd:(r + 1) * hd]
        rhs = jnp.concatenate([v * beta, k * (beta * jnp.exp(cum_c))], axis=1)
        sol = rhs + _mm_bf16(inv_off, rhs)
        u_sol, w_sol = sol[:, :hd], sol[:, hd:]
        state = s_ref[r]
        v_new, inter = [None] * n_sub, [None] * n_sub
        for i in order:
            sl = slice(i * GDN_SUB, (i + 1) * GDN_SUB)
            sb = state.astype(BF16)
            v_new[i] = u_sol[sl] - jnp.dot(w_sol[sl].astype(BF16), sb, preferred_element_type=F32)
            inter[i] = jnp.dot(qb[sl], sb, preferred_element_type=F32)
            end = i * GDN_SUB if reverse else (i + 1) * GDN_SUB - 1
            cum_end = cum_c[end:end + 1, :]
            to_end = jnp.exp(cum_end - cum_c[sl])
            state = jnp.exp(cum_end) * state + _dot_t((k[sl] * to_end).astype(BF16), v_new[i].astype(BF16))
        s_ref[r] = state
        v_new = jnp.concatenate(v_new, axis=0)
        inter = jnp.concatenate(inter, axis=0)
        y = jnp.dot((qk * decay).astype(BF16), v_new.astype(BF16), preferred_element_type=F32) + inter * jnp.exp(cum_c)
        if not reverse:
            y = y + yb_ref[:, r * hd:(r + 1) * hd]
            y = y * lax.rsqrt(jnp.mean(y * y, axis=-1, keepdims=True) + NORM_EPS) * ng_ref[...]
            z = z_ref[:, r * hd:(r + 1) * hd]
            y = y * (z * jax.nn.sigmoid(z))
        outs.append(y)
    o_ref[...] = jnp.concatenate(outs, axis=1).astype(o_ref.dtype)


def _gdn_mixer(u, tail, conv_w, dt_bias, a_log, norm_g, bsz, lc, seq, d_model):
    n_ctx_c, n_lat_c = _scan_geometry(bsz, lc, seq)
    n_steps = n_ctx_c + n_lat_c
    hd = GDN_HEAD_DIM
    k_heads = d_model // hd
    n_r = 2
    rows = u.shape[0]
    conv_ch = 4 * d_model
    nq = d_model // CONV_COLS
    qkv = _conv_silu(u, conv_w, jnp.zeros((1, conv_ch), F32), conv_ch, bsz, lc, seq, n_scaled=nq, n_normed=2 * nq,
                     scale=hd ** -0.5, head_dim=hd)
    n_gate = tail.shape[1] // 2
    prow = jnp.concatenate([jnp.zeros((1, n_gate), F32), dt_bias.reshape(1, n_gate)], axis=1)
    arow = jnp.concatenate([jnp.zeros((1, n_gate), F32), a_log.reshape(1, n_gate)], axis=1)
    y_b = None
    for reverse in (True, False):
        rb = functools.partial(_row_block, bsz=bsz, n_ctx_c=n_ctx_c, n_lat_c=n_lat_c, reverse=reverse)
        const = lambda b, g, c: (0, 0)
        in_specs = [pl.BlockSpec((SCAN_CHUNK, hd), lambda b, g, c, rb=rb: (rb(b, c), g)),
                    pl.BlockSpec((SCAN_CHUNK, hd), lambda b, g, c, rb=rb: (rb(b, c), k_heads + g)),
                    pl.BlockSpec((SCAN_CHUNK, n_r * hd), lambda b, g, c, rb=rb: (rb(b, c), k_heads + g)),
                    pl.BlockSpec((SCAN_CHUNK, 2 * n_gate), lambda b, g, c, rb=rb: (rb(b, c), 0)),
                    pl.BlockSpec((1, 2 * n_gate), const),
                    pl.BlockSpec((1, 2 * n_gate), const)]
        args = [qkv, qkv, qkv, tail, prow, arow]
        if not reverse:
            in_specs += [pl.BlockSpec((SCAN_CHUNK, n_r * hd), lambda b, g, c, rb=rb: (rb(b, c), g)),
                         pl.BlockSpec((SCAN_CHUNK, n_r * hd), lambda b, g, c, rb=rb: (rb(b, c), 2 * k_heads + g)),
                         pl.BlockSpec((1, hd), const)]
            args += [y_b, u, norm_g.reshape(1, hd)]
        out = pl.pallas_call(
            functools.partial(_gdn_kernel, reverse=reverse, n_r=n_r, hd=hd),
            grid=(bsz, k_heads, n_steps),
            in_specs=in_specs,
            out_specs=pl.BlockSpec((SCAN_CHUNK, n_r * hd), lambda b, g, c, rb=rb: (rb(b, c), g)),
            out_shape=jax.ShapeDtypeStruct((rows, k_heads * n_r * hd), F32 if reverse else BF16),
            scratch_shapes=[pltpu.VMEM((n_r, hd, hd), F32)],
            compiler_params=_params("parallel", "parallel", "arbitrary"),
            name="gdn_bwd" if reverse else "gdn_fwd",
        )(*args)
        if reverse:
            y_b = out
    return out


def _ssd_kernel(*refs, reverse, n_r, hd):
    if reverse:
        x_ref, b_ref, c_ref, tail_ref, dtb_ref, alog_ref, o_ref, s_ref = refs
    else:
        x_ref, b_ref, c_ref, tail_ref, dtb_ref, alog_ref, yb_ref, z_ref, d_ref, ng_ref, o_ref, s_ref = refs
    g, c = pl.program_id(1), pl.program_id(2)
    qn = SCAN_CHUNK
    n_lane = tail_ref.shape[1]
    width = n_r * hd

    @pl.when(c == 0)
    def _():
        s_ref[...] = jnp.zeros_like(s_ref)

    dt_all = _softplus(tail_ref[...] + dtb_ref[...])
    la_all = -jnp.exp(alog_ref[...]) * dt_all
    li = lax.broadcasted_iota(jnp.int32, (n_lane, n_r), 0)
    ji = lax.broadcasted_iota(jnp.int32, (n_lane, n_r), 1)
    sel = (li == ji + (n_lane // 2 if reverse else 0) + g * n_r).astype(F32)
    dt, la = _sel_cols(dt_all, sel), _sel_cols(la_all, sel)

    ti = lax.broadcasted_iota(jnp.int32, (qn, 1), 0)
    si = lax.broadcasted_iota(jnp.int32, (1, qn), 1)
    before = (si >= ti) if reverse else (si <= ti)
    cs = before.astype(F32)
    cum = _sel_rows(cs, la)
    cum_t = _sel_rows_t(la, cs)
    ei = lax.broadcasted_iota(jnp.int32, (n_r, width), 0)
    el = lax.broadcasted_iota(jnp.int32, (n_r, width), 1)
    expand = (lax.shift_right_logical(el, int(math.log2(hd))) == ei).astype(F32)
    dt_x, cum_x = _sel_cols(dt, expand), _sel_cols(cum, expand)

    xs = x_ref[...]
    v = xs * dt_x
    bm, cm = b_ref[...].astype(BF16), c_ref[...].astype(BF16)
    scores = _dot_nt(cm, bm)
    lane = lax.broadcasted_iota(jnp.int32, (1, 2 * hd), 1)
    tiles = []
    for p in range(n_r // 2):
        vt = v[:, 2 * p * hd:2 * (p + 1) * hd]
        acc = None
        for h in (2 * p, 2 * p + 1):
            decay = jnp.exp(jnp.where(before, cum[:, h:h + 1] - cum_t[h:h + 1, :], -jnp.inf))
            vh = jnp.where((lane >= hd) if h % 2 else (lane < hd), vt, 0.0).astype(BF16)
            part = jnp.dot((scores * decay).astype(BF16), vh, preferred_element_type=F32)
            acc = part if acc is None else acc + part
        tiles.append(acc)
    state = s_ref[...]
    y = jnp.concatenate(tiles, axis=1) + jnp.dot(cm, state.astype(BF16), preferred_element_type=F32) * jnp.exp(cum_x)
    end = 0 if reverse else qn - 1
    cum_end = cum_x[end:end + 1]
    s_ref[...] = state * jnp.exp(cum_end) + _dot_t(bm, (v * jnp.exp(cum_end - cum_x)).astype(BF16))
    if reverse:
        o_ref[...] = y
    else:
        y = y + yb_ref[...] + d_ref[...] * xs
        z = z_ref[...]
        y = y * (z * jax.nn.sigmoid(z))
        y = y * lax.rsqrt(jnp.mean(y * y, axis=-1, keepdims=True) + NORM_EPS) * ng_ref[...]
        o_ref[...] = y.astype(o_ref.dtype)


def _ssd_mixer(u, tail, conv_w, conv_b, dt_bias, a_log, d_skip, norm_g, bsz, lc, seq):
    n_ctx_c, n_lat_c = _scan_geometry(bsz, lc, seq)
    n_steps = n_ctx_c + n_lat_c
    d_inner = norm_g.shape[0]
    hd, st = SSD_HEAD_DIM, SSD_STATE
    heads = d_inner // hd
    n_r = heads // SSD_GROUPS
    width = n_r * hd
    rows = u.shape[0]
    conv_ch = d_inner + 2 * SSD_GROUPS * st
    xbc = _conv_silu(u, conv_w, conv_b.reshape(1, conv_ch), conv_ch, bsz, lc, seq, col0=d_inner // CONV_COLS)
    d_x = jnp.repeat(d_skip, hd).reshape(1, d_inner)
    y_b = None
    for reverse in (True, False):
        rb = functools.partial(_row_block, bsz=bsz, n_ctx_c=n_ctx_c, n_lat_c=n_lat_c, reverse=reverse)
        const = lambda b, g, c: (0, 0)
        grp = lambda b, g, c: (0, g)
        wide = lambda b, g, c, rb=rb: (rb(b, c), g)
        in_specs = [pl.BlockSpec((SCAN_CHUNK, width), wide),
                    pl.BlockSpec((SCAN_CHUNK, st), lambda b, g, c, rb=rb: (rb(b, c), d_inner // st + g)),
                    pl.BlockSpec((SCAN_CHUNK, st), lambda b, g, c, rb=rb: (rb(b, c), d_inner // st + SSD_GROUPS + g)),
                    pl.BlockSpec((SCAN_CHUNK, 2 * heads), lambda b, g, c, rb=rb: (rb(b, c), 0)),
                    pl.BlockSpec((1, 2 * heads), const),
                    pl.BlockSpec((1, 2 * heads), const)]
        args = [xbc, xbc, xbc, tail, dt_bias.reshape(1, 2 * heads), a_log.reshape(1, 2 * heads)]
        if not reverse:
            in_specs += [pl.BlockSpec((SCAN_CHUNK, width), wide), pl.BlockSpec((SCAN_CHUNK, width), wide),
                         pl.BlockSpec((1, width), grp), pl.BlockSpec((1, width), grp)]
            args += [y_b, u, d_x, norm_g.reshape(1, d_inner)]
        out = pl.pallas_call(
            functools.partial(_ssd_kernel, reverse=reverse, n_r=n_r, hd=hd),
            grid=(bsz, SSD_GROUPS, n_steps),
            in_specs=in_specs,
            out_specs=pl.BlockSpec((SCAN_CHUNK, width), wide),
            out_shape=jax.ShapeDtypeStruct((rows, d_inner), F32 if reverse else BF16),
            scratch_shapes=[pltpu.VMEM((st, width), F32)],
            compiler_params=_params("parallel", "parallel", "arbitrary"),
            name="ssd_bwd" if reverse else "ssd_fwd",
        )(*args)
        if reverse:
            y_b = out
    return out


HG_SUB = 64
HG_BLK = 16


def _log1p(x):
    return jnp.log(1.0 + x)


def _hgrn_kernel(*refs, reverse):
    if reverse:
        q_ref, f_ref, i_ref, lb_ref, o_ref, st_ref = refs
    else:
        q_ref, f_ref, i_ref, lb_ref, yb_ref, g_ref, ng_ref, o_ref, st_ref = refs
    c = pl.program_id(2)
    qn = SCAN_CHUNK
    hd = q_ref.shape[1]
    n_sub, n_blk = qn // HG_SUB, HG_SUB // HG_BLK

    @pl.when(c == 0)
    def _():
        st_ref[...] = jnp.zeros_like(st_ref)

    q, f, v, lb = q_ref[...], f_ref[...], i_ref[...], lb_ref[...]
    log_sig = jnp.minimum(f, 0.0) - _log1p(jnp.exp(-jnp.abs(f)))
    ga, gb = jnp.log(lb), _log1p(-lb) + log_sig
    log_f = jnp.maximum(ga, gb) + _log1p(jnp.exp(-jnp.abs(ga - gb)))
    k = (1.0 - lb) * jax.nn.sigmoid(-f)

    ti = lax.broadcasted_iota(jnp.int32, (qn, 1), 0)
    si = lax.broadcasted_iota(jnp.int32, (1, qn), 1)
    before = (si >= ti) if reverse else (si <= ti)
    same = lambda n: lax.shift_right_logical(ti, int(math.log2(n))) == lax.shift_right_logical(si, int(math.log2(n)))
    cum_sub = _sel_rows((same(HG_SUB) & before).astype(F32), log_f)
    cum_blk = _sel_rows((same(HG_BLK) & before).astype(F32), log_f)
    q_blk = q * jnp.exp(cum_blk)
    q_sub = (q * jnp.exp(cum_sub)).astype(BF16)

    ones = jnp.ones((hd, hd), BF16)
    lane = lax.broadcasted_iota(jnp.int32, (1, hd), 1)
    row_sub = lax.broadcasted_iota(jnp.int32, (HG_SUB, 1), 0)
    row_blk = lax.broadcasted_iota(jnp.int32, (HG_BLK, 1), 0)
    zeros_sub = jnp.zeros((hd - HG_SUB, hd), F32)
    y_intra, kvt, chunk_dec = [], [], []
    for i in range(n_sub):
        r0 = i * HG_SUB
        sl = slice(r0, r0 + HG_SUB)
        cs, ks, vs = cum_sub[sl], k[sl], v[sl]
        end = r0 if reverse else r0 + HG_SUB - 1
        cum_end = cum_sub[end:end + 1]
        kvt.append(_dot_t(vs.astype(BF16), (ks * jnp.exp(cum_end - cs)).astype(BF16)))
        chunk_dec.append(jnp.exp(cum_end))
        a_rows = []
        for a in range(n_blk):
            b0 = r0 + a * HG_BLK
            bl = slice(b0, b0 + HG_BLK)
            qb, kb, cb = q[bl], k[bl], cum_blk[bl]
            tiles = []
            for s in range(HG_BLK):
                ok = (row_blk <= s) if reverse else (row_blk >= s)
                e = jnp.exp(jnp.where(ok, cb - cb[s:s + 1], -jnp.inf))
                tiles.append(qb * e * kb[s:s + 1])
            sums = jnp.dot(jnp.concatenate(tiles, axis=0).astype(BF16), ones, preferred_element_type=F32)
            acc = jnp.zeros((HG_BLK, hd), F32)
            for s in range(HG_BLK):
                acc = acc + jnp.where(lane == a * HG_BLK + s, sums[s * HG_BLK:(s + 1) * HG_BLK], 0.0)
            has_earlier = (a < n_blk - 1) if reverse else (a > 0)
            if has_earlier:
                ref_row = b0 + HG_BLK if reverse else b0 - 1
                earlier = (row_sub >= (a + 1) * HG_BLK) if reverse else (row_sub < a * HG_BLK)
                kt = ks * jnp.exp(jnp.where(earlier, cum_sub[ref_row:ref_row + 1] - cs, -jnp.inf))
                kt = jnp.concatenate([kt, zeros_sub], axis=0).astype(BF16)
                acc = acc + _dot_nt(q_blk[bl].astype(BF16), kt)
            a_rows.append(acc)
        attn = jnp.concatenate(a_rows, axis=0).astype(BF16)
        v_pad = jnp.concatenate([vs, zeros_sub], axis=0).astype(BF16)
        y_intra.append(jnp.dot(attn, v_pad, preferred_element_type=F32))

    state = st_ref[...]
    ys = [None] * n_sub
    for i in (range(n_sub - 1, -1, -1) if reverse else range(n_sub)):
        ys[i] = y_intra[i] + _dot_nt(q_sub[i * HG_SUB:(i + 1) * HG_SUB], state.astype(BF16))
        state = state * chunk_dec[i] + kvt[i]
    st_ref[...] = state
    y = jnp.concatenate(ys, axis=0)
    if reverse:
        o_ref[...] = y
    else:
        y = y + yb_ref[...]
        y = y * lax.rsqrt(jnp.mean(y * y, axis=-1, keepdims=True) + NORM_EPS) * ng_ref[...]
        g = g_ref[...]
        o_ref[...] = (y * (g * jax.nn.sigmoid(g))).astype(o_ref.dtype)


def _hgrn_mixer(u, lb, norm_g, bsz, lc, seq):
    n_ctx_c, n_lat_c = _scan_geometry(bsz, lc, seq)
    n_steps = n_ctx_c + n_lat_c
    hd = HGRN_EXPAND
    d_model = lb.shape[0]
    heads = d_model // hd
    rows = u.shape[0]
    lb2, ng2 = lb.reshape(1, d_model), norm_g.reshape(1, d_model)
    y_b = None
    for reverse in (True, False):
        rb = functools.partial(_row_block, bsz=bsz, n_ctx_c=n_ctx_c, n_lat_c=n_lat_c, reverse=reverse)
        col = lambda seg: (lambda b, h, c, rb=rb: (rb(b, c), seg * heads + h))
        par = lambda b, h, c: (0, h)
        in_specs = [pl.BlockSpec((SCAN_CHUNK, hd), col(0)),
                    pl.BlockSpec((SCAN_CHUNK, hd), col(2 if reverse else 1)),
                    pl.BlockSpec((SCAN_CHUNK, hd), col(3)),
                    pl.BlockSpec((1, hd), par)]
        args = [u, u, u, lb2]
        if not reverse:
            in_specs += [pl.BlockSpec((SCAN_CHUNK, hd), col(0)), pl.BlockSpec((SCAN_CHUNK, hd), col(4)),
                         pl.BlockSpec((1, hd), par)]
            args += [y_b, u, ng2]
        out = pl.pallas_call(
            functools.partial(_hgrn_kernel, reverse=reverse),
            grid=(bsz, heads, n_steps),
            in_specs=in_specs,
            out_specs=pl.BlockSpec((SCAN_CHUNK, hd), col(0)),
            out_shape=jax.ShapeDtypeStruct((rows, d_model), F32 if reverse else BF16),
            scratch_shapes=[pltpu.VMEM((hd, hd), F32)],
            compiler_params=_params("parallel", "parallel", "arbitrary"),
            name="hgrn_bwd" if reverse else "hgrn_fwd",
        )(*args)
        if reverse:
            y_b = out
    return out


def _rmsnorm(x, g, eps=NORM_EPS):
    y = x * lax.rsqrt(jnp.mean(x * x, axis=-1, keepdims=True) + eps)
    return y * g


def _l2norm(x, eps=1e-6):
    return x * lax.rsqrt(jnp.sum(x * x, axis=-1, keepdims=True) + eps)


def _dwconv(u, w):
    return lax.conv_general_dilated(u, w[:, None, :], window_strides=(1,), padding=[(CONV_W // 2, CONV_W // 2)],
                                    dimension_numbers=('NWC', 'WIO', 'NWC'), feature_group_count=u.shape[-1])


def _conv_split(u, w, lc):
    return jnp.concatenate([_dwconv(u[:, :lc], w), _dwconv(u[:, lc:], w)], axis=1)


def _rev(t, lc):
    return jnp.concatenate([jnp.flip(t[:, :lc], 1), jnp.flip(t[:, lc:], 1)], axis=1)


def _to_chunks(t):
    b, n = t.shape[:2]
    return jnp.moveaxis(t.reshape((b, n // CHUNK, CHUNK) + t.shape[2:]), 1, 0)


def _from_chunks(t):
    nc, b, q = t.shape[:3]
    return jnp.moveaxis(t, 0, 1).reshape((b, nc * q) + t.shape[3:])


def _chunk_masks():
    idx = jnp.arange(CHUNK)
    return idx[:, None] >= idx[None, :], idx[:, None] > idx[None, :]


def _scalar_decay_scan(q, k, v, log_a):
    bsz, _, g, n = q.shape
    r, p = v.shape[-2:]
    incl, _ = _chunk_masks()

    def body(s, xs):
        qc, kc, vc, la = xs
        cum = jnp.cumsum(la, axis=1)
        cum_t = jnp.moveaxis(cum, 1, -1)
        seg = cum_t[..., :, None] - cum_t[..., None, :]
        scores = jnp.einsum('btgn,bsgn->bgts', qc, kc)
        attn = scores[:, :, None] * jnp.exp(jnp.where(incl, seg, -jnp.inf))
        y = jnp.einsum('bgrts,bsgrp->btgrp', attn, vc)
        y = y + jnp.einsum('btgn,bgrnp->btgrp', qc, s) * jnp.exp(cum)[..., None]
        to_end = jnp.exp(cum[:, -1:] - cum)
        s = jnp.exp(cum[:, -1])[..., None, None] * s + jnp.einsum('bsgn,bsgr,bsgrp->bgrnp', kc, to_end, vc)
        return s, y

    s0 = jnp.zeros((bsz, g, r, n, p), F32)
    _, y = lax.scan(body, s0, tuple(_to_chunks(t) for t in (q, k, v, log_a)))
    return _from_chunks(y)


def _vector_decay_scan(q, k, v, log_f):
    bsz, _, h, kd = q.shape
    vd = v.shape[-1]
    incl, _ = _chunk_masks()

    def body(s, xs):
        qc, kc, vc, lf = xs
        cum = jnp.cumsum(lf, axis=1)
        seg = cum[:, :, None] - cum[:, None, :]
        decay = jnp.exp(jnp.where(incl[:, :, None, None], seg, -jnp.inf))
        attn = jnp.einsum('bthk,bshk,btshk->bhts', qc, kc, decay)
        y = jnp.einsum('bhts,bshv->bthv', attn, vc)
        y = y + jnp.einsum('bthk,bhkv->bthv', qc * jnp.exp(cum), s)
        s = jnp.exp(cum[:, -1])[..., None] * s + jnp.einsum('bshk,bshv->bhkv', kc * jnp.exp(cum[:, -1:] - cum), vc)
        return s, y

    s0 = jnp.zeros((bsz, h, kd, vd), F32)
    _, y = lax.scan(body, s0, tuple(_to_chunks(t) for t in (q, k, v, log_f)))
    return _from_chunks(y)


def _delta_scan(q, k, v, beta, log_a):
    bsz, _, g, kd = q.shape
    r, vd = v.shape[-2:]
    incl, strict = _chunk_masks()

    def body(s, xs):
        qc, kc, vc, bc, la = xs
        cum = jnp.cumsum(la, axis=1)
        cum_t = jnp.moveaxis(cum, 1, -1)
        seg = cum_t[..., :, None] - cum_t[..., None, :]
        beta_t = jnp.moveaxis(bc, 1, -1)
        kk = jnp.einsum('btgk,bsgk->bgts', kc, kc)
        lower = beta_t[..., :, None] * kk[:, :, None] * jnp.exp(jnp.where(strict, seg, -jnp.inf))
        rhs_v = jnp.moveaxis(vc * bc[..., None], 1, 3)
        rhs_k = jnp.moveaxis(kc[:, :, :, None, :] * (bc * jnp.exp(cum))[..., None], 1, 3)
        sol = lax.linalg.triangular_solve(lower, jnp.concatenate([rhs_v, rhs_k], axis=-1),
                                          left_side=True, lower=True, unit_diagonal=True)
        u, w = sol[..., :vd], sol[..., vd:]
        v_new = u - jnp.einsum('bgrtk,bgrkv->bgrtv', w, s)
        qk = jnp.einsum('btgk,bsgk->bgts', qc, kc)
        attn = qk[:, :, None] * jnp.exp(jnp.where(incl, seg, -jnp.inf))
        y = jnp.einsum('bgrts,bgrsv->btgrv', attn, v_new)
        y = y + jnp.einsum('btgk,bgrkv->btgrv', qc, s) * jnp.exp(cum)[..., None]
        to_end = jnp.exp(cum_t[..., -1:] - cum_t)
        s = jnp.exp(cum_t[..., -1])[..., None, None] * s + jnp.einsum('bsgk,bgrs,bgrsv->bgrkv', kc, to_end, v_new)
        return s, y

    s0 = jnp.zeros((bsz, g, r, kd, vd), F32)
    _, y = lax.scan(body, s0, tuple(_to_chunks(t) for t in (q, k, v, beta, log_a)))
    return _from_chunks(y)


def _rope_tables(rows, half):
    pos = np.arange(rows * GRID_W)
    inv_freq = np.float32(ROPE_BASE) ** (-(np.arange(0, half, 2, dtype=np.float32) / np.float32(half)))
    out = []
    for p in ((pos // GRID_W).astype(np.float32), (pos % GRID_W).astype(np.float32)):
        ang = (p[:, None] * inv_freq.astype(np.float32)).astype(np.float32).astype(np.float64)
        out += [np.cos(ang).astype(np.float32), np.sin(ang).astype(np.float32)]
    return out


def _rope_2d(t, rows):
    half = t.shape[-1] // 2
    cos_r, sin_r, cos_c, sin_c = (jnp.asarray(a)[:, None, :] for a in _rope_tables(rows, half))

    def rot(u, cos, sin):
        u1, u2 = jnp.split(u, 2, axis=-1)
        return jnp.concatenate([u1 * cos - u2 * sin, u2 * cos + u1 * sin], axis=-1)

    return jnp.concatenate([rot(t[..., :half], cos_r, sin_r), rot(t[..., half:], cos_c, sin_c)], axis=-1)


def _ssd_core(u, lc, conv_w, conv_b, dt_bias, a_log, d_skip, norm_g, start):
    bsz, t = u.shape[:2]
    d_inner = norm_g.shape[0]
    heads = d_inner // SSD_HEAD_DIM
    conv_ch = d_inner + 2 * SSD_GROUPS * SSD_STATE
    z, xbc, dt = jnp.split(u, [d_inner, d_inner + conv_ch], axis=-1)
    xbc = jax.nn.silu(_conv_split(xbc, conv_w, lc) + conv_b)
    xs, bm, cm = jnp.split(xbc, [d_inner, d_inner + SSD_GROUPS * SSD_STATE], axis=-1)
    r = heads // SSD_GROUPS
    xs = xs.reshape(bsz, t, SSD_GROUPS, r, SSD_HEAD_DIM)
    bm = bm.reshape(bsz, t, SSD_GROUPS, SSD_STATE)
    cm = cm.reshape(bsz, t, SSD_GROUPS, SSD_STATE)
    dt = jax.nn.softplus(dt.reshape(bsz, t, 2, heads) + dt_bias)
    log_a = -jnp.exp(a_log) * dt
    grp = lambda a: a.reshape(bsz, t, SSD_GROUPS, r)
    y = _scalar_decay_scan(cm, bm, xs * grp(dt[:, :, 0])[..., None], grp(log_a[:, :, 0]))
    y = y + _rev(_scalar_decay_scan(_rev(cm, lc), _rev(bm, lc), _rev(xs * grp(dt[:, :, 1])[..., None], lc),
                                    _rev(grp(log_a[:, :, 1]), lc)), lc)
    y = y + d_skip.reshape(SSD_GROUPS, r)[..., None] * xs
    n = t - start
    y = y.reshape(bsz, t, d_inner)[:, start:] * jax.nn.silu(z[:, start:])
    return _rmsnorm(y.reshape(bsz, n, SSD_GROUPS, -1), norm_g.reshape(SSD_GROUPS, -1)).reshape(bsz, n, d_inner)


def _ret_core(u, lc, log_decay, d_model, rows, start):
    bsz, t = u.shape[:2]
    dv = 2 * d_model
    qk_dim = d_model // RET_HEADS
    v_dim = 2 * qk_dim
    q, k, v, g = jnp.split(u, [d_model, 2 * d_model, 2 * d_model + dv], axis=-1)
    q = q.reshape(bsz, t, RET_HEADS, qk_dim)
    k = k.reshape(bsz, t, RET_HEADS, qk_dim) * qk_dim ** -0.5
    q = jnp.concatenate([q[:, :lc], _rope_2d(q[:, lc:], rows)], axis=1)
    k = jnp.concatenate([k[:, :lc], _rope_2d(k[:, lc:], rows)], axis=1)
    v = v.reshape(bsz, t, RET_HEADS, 1, v_dim)
    ld_f = jnp.broadcast_to(log_decay[0][:, None], (bsz, t, RET_HEADS, 1))
    ld_b = jnp.broadcast_to(log_decay[1][:, None], (bsz, t, RET_HEADS, 1))
    y = _scalar_decay_scan(q, k, v, ld_f)
    y = y + _rev(_scalar_decay_scan(_rev(q, lc), _rev(k, lc), _rev(v, lc), ld_b), lc)
    n = t - start
    y = y[:, start:].reshape(bsz, n, RET_HEADS, v_dim)
    mu = jnp.mean(y, axis=-1, keepdims=True)
    var = jnp.mean(jnp.square(y - mu), axis=-1, keepdims=True)
    y = ((y - mu) * lax.rsqrt(var + NORM_EPS)).reshape(bsz, n, dv)
    return y * jax.nn.silu(g[:, start:])


def _lower_bound(lb_logits, layer):
    p = jax.nn.softmax(lb_logits.astype(F32), axis=0)
    return jnp.cumsum(p, axis=0)[layer] - p[0]


def _hgrn_core(u, lc, lb, norm_g, start):
    bsz, t = u.shape[:2]
    d_model = norm_g.shape[0]
    heads = d_model // HGRN_EXPAND
    q, f_f, f_b, i, g = jnp.split(u, 5, axis=-1)
    shp = (bsz, t, heads, HGRN_EXPAND)
    q, i = q.reshape(shp), i.reshape(shp)
    lb = lb.reshape(heads, HGRN_EXPAND)

    def gates(f):
        f = f.reshape(shp)
        log_f = jnp.logaddexp(jnp.log(lb), jnp.log1p(-lb) + jax.nn.log_sigmoid(f))
        return log_f, (1 - lb) * jax.nn.sigmoid(-f)

    lf_f, k_f = gates(f_f)
    lf_b, k_b = gates(f_b)
    y = _vector_decay_scan(q, k_f, i, lf_f)
    y = y + _rev(_vector_decay_scan(_rev(q, lc), _rev(k_b, lc), _rev(i, lc), _rev(lf_b, lc)), lc)
    n = t - start
    y = _rmsnorm(y[:, start:], norm_g.reshape(heads, HGRN_EXPAND))
    return (y * jax.nn.silu(g[:, start:].reshape(bsz, n, heads, HGRN_EXPAND))).reshape(bsz, n, d_model)


def _gdn_core(u, lc, conv_w, dt_bias, a_log, norm_g, d_model, start):
    bsz, t = u.shape[:2]
    k_heads = d_model // GDN_HEAD_DIM
    v_heads = 2 * k_heads
    dk, dv = d_model, 2 * d_model
    conv_ch = 2 * dk + dv
    qkv, z, bt, a = jnp.split(u, [conv_ch, conv_ch + dv, conv_ch + dv + 2 * v_heads], axis=-1)
    qkv = jax.nn.silu(_conv_split(qkv, conv_w, lc))
    q, k, v = jnp.split(qkv, [dk, 2 * dk], axis=-1)
    r = v_heads // k_heads
    q = _l2norm(q.reshape(bsz, t, k_heads, GDN_HEAD_DIM)) * GDN_HEAD_DIM ** -0.5
    k = _l2norm(k.reshape(bsz, t, k_heads, GDN_HEAD_DIM))
    v = v.reshape(bsz, t, k_heads, r, GDN_HEAD_DIM)
    beta = jax.nn.sigmoid(bt.reshape(bsz, t, 2, k_heads, r))
    log_a = -jnp.exp(a_log).reshape(2, k_heads, r) * jax.nn.softplus(
        a.reshape(bsz, t, 2, k_heads, r) + dt_bias.reshape(2, k_heads, r))
    y = _delta_scan(q, k, v, beta[:, :, 0], log_a[:, :, 0])
    y = y + _rev(_delta_scan(_rev(q, lc), _rev(k, lc), _rev(v, lc), _rev(beta[:, :, 1], lc),
                             _rev(log_a[:, :, 1], lc)), lc)
    n = t - start
    y = y[:, start:].reshape(bsz, n, v_heads, GDN_HEAD_DIM)
    y = _rmsnorm(y, norm_g) * jax.nn.silu(z[:, start:].reshape(bsz, n, v_heads, GDN_HEAD_DIM))
    return y.reshape(bsz, n, dv)


def _rows_to_seq(u, bsz, lc):
    n_ctx = bsz * lc
    return jnp.concatenate([u[:n_ctx].reshape(bsz, lc, -1), u[n_ctx:].reshape(bsz, -1, u.shape[-1])], axis=1)


def _seq_to_rows(y, lc, start):
    bsz = y.shape[0]
    if start:
        return y.reshape(bsz * y.shape[1], -1)
    return jnp.concatenate([y[:, :lc].reshape(bsz * lc, -1), y[:, lc:].reshape(-1, y.shape[-1])], axis=0)


def kernel(x, c, ctx, c_ctx, ada_w, ada_b, norm_g, mlp_w1, mlp_w2, final_g, ssd_w_in, ssd_conv_w, ssd_conv_b,
           ssd_dt_bias, ssd_a_log, ssd_d, ssd_norm_g, ssd_w_out, ret_w_in, ret_log_decay, ret_w_out, hgrn_w_in,
           hgrn_lb_logits, hgrn_norm_g, hgrn_w_out, gdn_w_in, gdn_conv_w, gdn_dt_bias, gdn_a_log, gdn_norm_g,
           gdn_w_out):
    bsz, seq, d = x.shape
    lc = ctx.shape[1]
    depth = ada_w.shape[0]
    rows_grid = seq // GRID_W
    n_ctx = bsz * lc
    assert lc % ROW_TILE == 0 or ROW_TILE % lc == 0 and n_ctx % ROW_TILE == 0
    assert seq % ROW_TILE == 0 and bsz + 1 <= 8

    cond_pad = jnp.concatenate([c, c_ctx[None], jnp.zeros((8 - bsz - 1, d), F32)], axis=0)
    mod = _ada_mod(cond_pad, ada_w, ada_b)

    tile_row = [bsz] * (n_ctx // ROW_TILE) + [b for b in range(bsz) for _ in range(seq // ROW_TILE)]
    tile_row = jnp.asarray(tile_row, jnp.int32)
    n_ctx_tiles = n_ctx // ROW_TILE

    xr = jnp.concatenate([ctx.reshape(n_ctx, d), x.reshape(bsz * seq, d)], axis=0)

    for i in range(depth):
        mixer, occ = i % 4, i // 4
        keep_ctx = i < depth - 1
        start = 0 if keep_ctx else lc
        mod_t = mod[i][tile_row][:, None, :]
        g0, g1 = norm_g[i, 0][None], norm_g[i, 1][None]
        if mixer == 0:
            w_in, w_out = ssd_w_in[occ], ssd_w_out[occ]
        elif mixer == 1:
            w_in, w_out = ret_w_in[occ], ret_w_out[occ]
        elif mixer == 2:
            w_in, w_out = hgrn_w_in[occ], hgrn_w_out[occ]
        else:
            w_in, w_out = gdn_w_in[occ], gdn_w_out[occ]
        n_in = w_in.shape[1]
        n_main = (n_in // 1024) * 1024 if n_in % 1024 else n_in
        w_in = w_in.astype(BF16)
        u = _ln_mm(xr, g0, mod_t, 0, 1, w_in[:, :n_main])
        tail = _ln_mm(xr, g0, mod_t, 0, 1, w_in[:, n_main:]) if n_main != n_in else None
        first_row = 0 if keep_ctx else n_ctx
        if mixer == 0:
            yr = _ssd_mixer(u, tail, ssd_conv_w[occ], ssd_conv_b[occ], ssd_dt_bias[occ], ssd_a_log[occ], ssd_d[occ],
                            ssd_norm_g[occ], bsz, lc, seq)[first_row:]
        elif mixer == 1:
            yr = _ret_mixer(u, ret_log_decay[occ], bsz, lc, seq, d)[first_row:]
        elif mixer == 2:
            yr = _hgrn_mixer(u, _lower_bound(hgrn_lb_logits, i), hgrn_norm_g[occ], bsz, lc, seq)[first_row:]
        elif mixer == 3:
            yr = _gdn_mixer(u, tail, gdn_conv_w[occ], gdn_dt_bias[occ], gdn_a_log[occ], gdn_norm_g[occ], bsz, lc,
                            seq, d)[first_row:]
        else:
            if tail is not None:
                u = jnp.concatenate([u, tail], axis=1)
            useq = _rows_to_seq(u, bsz, lc)
            if mixer == 0:
                y = _ssd_core(useq, lc, ssd_conv_w[occ], ssd_conv_b[occ], ssd_dt_bias[occ], ssd_a_log[occ],
                              ssd_d[occ], ssd_norm_g[occ], start)
            elif mixer == 2:
                y = _hgrn_core(useq, lc, _lower_bound(hgrn_lb_logits, i), hgrn_norm_g[occ], start)
            else:
                y = _gdn_core(useq, lc, gdn_conv_w[occ], gdn_dt_bias[occ], gdn_a_log[occ], gdn_norm_g[occ], d,
                              start)
            yr = _seq_to_rows(y, lc, start).astype(BF16)
        if not keep_ctx:
            xr = xr[n_ctx:]
            mod_t = mod_t[n_ctx_tiles:]
        xr = _out_proj(yr, w_out.astype(BF16), xr, mod_t, 2)
        xr = _mlp(xr, g1, mod_t, mlp_w1[i].astype(BF16), mlp_w2[i].astype(BF16), final_g[None], final=not keep_ctx)
    return xr.reshape(bsz, seq, d)
```

```python
import functools
import math

import jax
import jax.numpy as jnp
import numpy as np
from jax import lax
from jax.experimental import pallas as pl
from jax.experimental.pallas import tpu as pltpu

F32 = jnp.float32
BF16 = jnp.bfloat16

GRID_W = 64
CHUNK = 64
CONV_W = 3
NORM_EPS = 1e-6
ROPE_BASE = 10000.0
SSD_HEAD_DIM = 64
SSD_GROUPS = 8
SSD_STATE = 128
RET_HEADS = 8
HGRN_EXPAND = 128
GDN_HEAD_DIM = 128

ROW_TILE = 512
VMEM_LIMIT = 56 * 1024 * 1024


def _params(*sem):
    return pltpu.CompilerParams(dimension_semantics=sem, vmem_limit_bytes=VMEM_LIMIT)


def _col_tile(n, cap=1536):
    best = 128
    for t in range(128, cap + 1, 128):
        if n % t == 0:
            best = t
    return best


def _ada_kernel(c_ref, w_ref, b_ref, o_ref):
    c = c_ref[...]
    c = (c * jax.nn.sigmoid(c)).astype(BF16)
    o_ref[0] = jnp.dot(c, w_ref[0].astype(BF16), preferred_element_type=F32) + b_ref[0]


def _ada_mod(cond_pad, ada_w, ada_b):
    depth, d, n = ada_w.shape
    tn = 1024
    return pl.pallas_call(
        _ada_kernel,
        grid=(depth, n // tn),
        in_specs=[pl.BlockSpec((8, d), lambda l, j: (0, 0)),
                  pl.BlockSpec((1, d, tn), lambda l, j: (l, 0, j)),
                  pl.BlockSpec((1, 1, tn), lambda l, j: (l, 0, j))],
        out_specs=pl.BlockSpec((1, 8, tn), lambda l, j: (l, 0, j)),
        out_shape=jax.ShapeDtypeStruct((depth, 8, n), F32),
        compiler_params=_params("parallel", "parallel"),
        name="ada_mod",
    )(cond_pad, ada_w, ada_b.reshape(depth, 1, n))


def _adaln_rows(x, g, sh, sc):
    y = x * lax.rsqrt(jnp.mean(x * x, axis=-1, keepdims=True) + NORM_EPS)
    return y * g * (1.0 + sc) + sh


def _ln_mm_kernel(x_ref, g_ref, sh_ref, sc_ref, w_ref, o_ref, h_ref):
    @pl.when(pl.program_id(1) == 0)
    def _():
        h_ref[...] = _adaln_rows(x_ref[...], g_ref[...], sh_ref[0], sc_ref[0]).astype(BF16)

    o_ref[...] = jnp.dot(h_ref[...], w_ref[...], preferred_element_type=F32).astype(o_ref.dtype)


def _ln_mm(x, g, mod_t, sh_col, sc_col, w, out_dtype=F32):
    m, d = x.shape
    n = w.shape[1]
    tm, tn = ROW_TILE, _col_tile(n)
    return pl.pallas_call(
        _ln_mm_kernel,
        grid=(m // tm, n // tn),
        in_specs=[pl.BlockSpec((tm, d), lambda i, j: (i, 0)),
                  pl.BlockSpec((1, d), lambda i, j: (0, 0)),
                  pl.BlockSpec((1, 1, d), lambda i, j: (i, 0, sh_col)),
                  pl.BlockSpec((1, 1, d), lambda i, j: (i, 0, sc_col)),
                  pl.BlockSpec((d, tn), lambda i, j: (0, j))],
        out_specs=pl.BlockSpec((tm, tn), lambda i, j: (i, j)),
        out_shape=jax.ShapeDtypeStruct((m, n), out_dtype),
        scratch_shapes=[pltpu.VMEM((tm, d), BF16)],
        compiler_params=_params("parallel", "arbitrary"),
        name="adaln_in_proj",
    )(x, g, mod_t, mod_t, w)


def _out_kernel(y_ref, w_ref, x_ref, gate_ref, o_ref):
    o_ref[...] = x_ref[...] + gate_ref[0] * jnp.dot(y_ref[...], w_ref[...], preferred_element_type=F32)


def _out_proj(y, w, x, mod_t, gate_col):
    m, k = y.shape
    d = w.shape[1]
    tm, tn = ROW_TILE, 512
    return pl.pallas_call(
        _out_kernel,
        grid=(m // tm, d // tn),
        in_specs=[pl.BlockSpec((tm, k), lambda i, j: (i, 0)),
                  pl.BlockSpec((k, tn), lambda i, j: (0, j)),
                  pl.BlockSpec((tm, tn), lambda i, j: (i, j)),
                  pl.BlockSpec((1, 1, tn), lambda i, j: (i, 0, gate_col * (d // tn) + j))],
        out_specs=pl.BlockSpec((tm, tn), lambda i, j: (i, j)),
        out_shape=jax.ShapeDtypeStruct((m, d), F32),
        compiler_params=_params("parallel", "arbitrary"),
        name="out_proj",
    )(y, w, x, mod_t)


def _mlp_kernel(x_ref, g_ref, sh_ref, sc_ref, gate_ref, w1_ref, w2_ref, fg_ref, o_ref, h_ref, acc_ref, *, final):
    f = pl.program_id(1)

    @pl.when(f == 0)
    def _():
        h_ref[...] = _adaln_rows(x_ref[...], g_ref[...], sh_ref[0], sc_ref[0]).astype(BF16)
        acc_ref[...] = jnp.zeros_like(acc_ref)

    a = jnp.dot(h_ref[...], w1_ref[...], preferred_element_type=F32)
    a = jnp.square(jnp.maximum(a, 0.0)).astype(BF16)
    acc_ref[...] += jnp.dot(a, w2_ref[...], preferred_element_type=F32)

    @pl.when(f == pl.num_programs(1) - 1)
    def _():
        out = x_ref[...] + gate_ref[0] * acc_ref[...]
        if final:
            out = out * lax.rsqrt(jnp.mean(out * out, axis=-1, keepdims=True) + NORM_EPS) * fg_ref[...]
        o_ref[...] = out


def _mlp(x, g, mod_t, w1, w2, final_g, final):
    m, d = x.shape
    ff = w1.shape[1]
    tm, tf = ROW_TILE, 512
    return pl.pallas_call(
        functools.partial(_mlp_kernel, final=final),
        grid=(m // tm, ff // tf),
        in_specs=[pl.BlockSpec((tm, d), lambda i, f: (i, 0)),
                  pl.BlockSpec((1, d), lambda i, f: (0, 0)),
                  pl.BlockSpec((1, 1, d), lambda i, f: (i, 0, 3)),
                  pl.BlockSpec((1, 1, d), lambda i, f: (i, 0, 4)),
                  pl.BlockSpec((1, 1, d), lambda i, f: (i, 0, 5)),
                  pl.BlockSpec((d, tf), lambda i, f: (0, f)),
                  pl.BlockSpec((tf, d), lambda i, f: (f, 0)),
                  pl.BlockSpec((1, d), lambda i, f: (0, 0))],
        out_specs=pl.BlockSpec((tm, d), lambda i, f: (i, 0)),
        out_shape=jax.ShapeDtypeStruct((m, d), F32),
        scratch_shapes=[pltpu.VMEM((tm, d), BF16), pltpu.VMEM((tm, d), F32)],
        compiler_params=_params("parallel", "arbitrary"),
        name="adaln_mlp",
    )(x, g, mod_t, mod_t, mod_t, w1, w2, final_g)


SCAN_CHUNK = 256


def _scan_geometry(bsz, lc, seq):
    assert lc % SCAN_CHUNK == 0 and seq % SCAN_CHUNK == 0
    return lc // SCAN_CHUNK, seq // SCAN_CHUNK


def _seg_chunk(c, n_ctx_c, n_lat_c, reverse):
    if reverse:
        return c >= n_ctx_c, jnp.where(c < n_ctx_c, n_ctx_c - 1 - c, n_lat_c - 1 - (c - n_ctx_c))
    return c >= n_ctx_c, jnp.where(c < n_ctx_c, c, c - n_ctx_c)


def _row_block(b, c, bsz, n_ctx_c, n_lat_c, reverse):
    is_lat, j = _seg_chunk(c, n_ctx_c, n_lat_c, reverse)
    return jnp.where(is_lat, bsz * n_ctx_c + b * n_lat_c + j, b * n_ctx_c + j)


def _time_iotas(q):
    t = lax.broadcasted_iota(jnp.int32, (q, 1), 0).astype(F32)
    s = lax.broadcasted_iota(jnp.int32, (1, q), 1).astype(F32)
    return t, s


def _dot_t(a, b):
    return lax.dot_general(a, b, (((0,), (0,)), ((), ())), preferred_element_type=F32)


def _dot_nt(a, b):
    return lax.dot_general(a, b, (((1,), (1,)), ((), ())), preferred_element_type=F32)


def _ret_kernel(*refs, reverse, n_ctx_c, n_heads, k_scale):
    if reverse:
        ld_ref, q_ref, k_ref, v_ref, cos_ref, sin_ref, o_ref, s_ref = refs
    else:
        ld_ref, q_ref, k_ref, v_ref, cos_ref, sin_ref, yb_ref, g_ref, o_ref, s_ref = refs
    h, c = pl.program_id(1), pl.program_id(2)
    qn = SCAN_CHUNK

    @pl.when(c == 0)
    def _():
        s_ref[...] = jnp.zeros_like(s_ref)

    lg = ld_ref[(n_heads if reverse else 0) + h]
    t, s = _time_iotas(qn)
    is_lat = c >= n_ctx_c
    cos = jnp.where(is_lat, cos_ref[...], 1.0)
    sin = jnp.where(is_lat, sin_ref[...], 0.0)

    def rope(x):
        half = x.shape[1] // 2
        swapped = jnp.concatenate([pltpu.roll(x[:, :half], half // 2, 1), pltpu.roll(x[:, half:], half // 2, 1)], axis=1)
        return x * cos + swapped * sin

    q = rope(q_ref[...]).astype(BF16)
    k = rope(k_ref[...]) * k_scale
    v = v_ref[...].astype(BF16)
    if reverse:
        dmat = jnp.where(s >= t, jnp.exp((s - t) * lg), 0.0)
        q_dec, k_dec = jnp.exp((qn - t) * lg), jnp.exp(t * lg)
    else:
        dmat = jnp.where(t >= s, jnp.exp((t - s) * lg), 0.0)
        q_dec, k_dec = jnp.exp((t + 1.0) * lg), jnp.exp((qn - 1.0 - t) * lg)
    attn = (_dot_nt(q, k.astype(BF16)) * dmat).astype(BF16)
    state = s_ref[...]
    y = jnp.dot(attn, v, preferred_element_type=F32)
    y = y + jnp.dot(q, state.astype(BF16), preferred_element_type=F32) * q_dec
    chunk_dec = jnp.exp(jnp.full((1, 1), qn, F32) * lg)
    s_ref[...] = state * chunk_dec + _dot_t((k * k_dec).astype(BF16), v)
    if reverse:
        o_ref[...] = y
    else:
        y = y + yb_ref[...]
        mu = jnp.mean(y, axis=-1, keepdims=True)
        yc = y - mu
        var = jnp.mean(yc * yc, axis=-1, keepdims=True)
        g = g_ref[...]
        o_ref[...] = (yc * lax.rsqrt(var + NORM_EPS) * (g * jax.nn.sigmoid(g))).astype(o_ref.dtype)


def _ret_mixer(u, log_decay, bsz, lc, seq, d_model):
    n_ctx_c, n_lat_c = _scan_geometry(bsz, lc, seq)
    n_steps = n_ctx_c + n_lat_c
    qk_dim, v_dim = d_model // RET_HEADS, 2 * d_model // RET_HEADS
    rows = u.shape[0]
    cos_r, sin_r, cos_c, sin_c = _rope_tables(seq // GRID_W, qk_dim // 2)
    cos_t = jnp.asarray(np.concatenate([cos_r, cos_r, cos_c, cos_c], axis=1))
    sin_t = jnp.asarray(np.concatenate([-sin_r, sin_r, -sin_c, sin_c], axis=1))
    ld = log_decay.reshape(-1).astype(F32)
    y_b = None
    for reverse in (True, False):
        rb = functools.partial(_row_block, bsz=bsz, n_ctx_c=n_ctx_c, n_lat_c=n_lat_c, reverse=reverse)

        def tab(b, h, c, reverse=reverse):
            is_lat, j = _seg_chunk(c, n_ctx_c, n_lat_c, reverse)
            return (jnp.where(is_lat, j, 0), 0)

        in_specs = [pl.BlockSpec(memory_space=pltpu.SMEM),
                    pl.BlockSpec((SCAN_CHUNK, qk_dim), lambda b, h, c, rb=rb: (rb(b, c), h)),
                    pl.BlockSpec((SCAN_CHUNK, qk_dim), lambda b, h, c, rb=rb: (rb(b, c), RET_HEADS + h)),
                    pl.BlockSpec((SCAN_CHUNK, v_dim), lambda b, h, c, rb=rb: (rb(b, c), RET_HEADS + h)),
                    pl.BlockSpec((SCAN_CHUNK, qk_dim), tab),
                    pl.BlockSpec((SCAN_CHUNK, qk_dim), tab)]
        args = [ld, u, u, u, cos_t, sin_t]
        if not reverse:
            in_specs += [pl.BlockSpec((SCAN_CHUNK, v_dim), lambda b, h, c, rb=rb: (rb(b, c), h)),
                         pl.BlockSpec((SCAN_CHUNK, v_dim), lambda b, h, c, rb=rb: (rb(b, c), 2 * RET_HEADS + h))]
            args += [y_b, u]
        out = pl.pallas_call(
            functools.partial(_ret_kernel, reverse=reverse, n_ctx_c=n_ctx_c, n_heads=RET_HEADS, k_scale=qk_dim ** -0.5),
            grid=(bsz, RET_HEADS, n_steps),
            in_specs=in_specs,
            out_specs=pl.BlockSpec((SCAN_CHUNK, v_dim), lambda b, h, c, rb=rb: (rb(b, c), h)),
            out_shape=jax.ShapeDtypeStruct((rows, RET_HEADS * v_dim), F32 if reverse else BF16),
            scratch_shapes=[pltpu.VMEM((qk_dim, v_dim), F32)],
            compiler_params=_params("parallel", "parallel", "arbitrary"),
            name="retention_bwd" if reverse else "retention_fwd",
        )(*args)
        if reverse:
            y_b = out
    return out


CONV_COLS = 512


def _conv_kernel(x_ref, prev_ref, next_ref, w_ref, b_ref, o_ref, *, n_ctx_c, n_lat_c, bsz, n_scaled, n_normed, scale,
                 head_dim):
    rb, cb = pl.program_id(0), pl.program_id(1)
    is_lat = rb >= bsz * n_ctx_c
    j = jnp.where(is_lat, (rb - bsz * n_ctx_c) % n_lat_c, rb % n_ctx_c)
    last = jnp.where(is_lat, n_lat_c - 1, n_ctx_c - 1)
    x = x_ref[...]
    rows = x.shape[0]
    row = lax.broadcasted_iota(jnp.int32, (rows, 1), 0)
    halo_prev = jnp.where(j > 0, prev_ref[7:8, :], 0.0)
    halo_next = jnp.where(j < last, next_ref[0:1, :], 0.0)
    x_prev = jnp.where(row == 0, halo_prev, pltpu.roll(x, 1, 0))
    x_next = jnp.where(row == rows - 1, halo_next, pltpu.roll(x, rows - 1, 0))
    y = w_ref[0:1, :] * x_prev + w_ref[1:2, :] * x + w_ref[2:3, :] * x_next + b_ref[...]
    y = y * jax.nn.sigmoid(y)

    def normed(mult):
        parts = []
        for i in range(y.shape[1] // head_dim):
            p = y[:, i * head_dim:(i + 1) * head_dim]
            parts.append(p * (lax.rsqrt(jnp.sum(p * p, axis=-1, keepdims=True) + 1e-6) * mult))
        return jnp.concatenate(parts, axis=1)

    if n_normed == 0:
        o_ref[...] = y
    else:
        @pl.when(cb < n_scaled)
        def _():
            o_ref[...] = normed(scale)

        @pl.when(jnp.logical_and(cb >= n_scaled, cb < n_normed))
        def _():
            o_ref[...] = normed(1.0)

        @pl.when(cb >= n_normed)
        def _():
            o_ref[...] = y


def _conv_silu(u, w, b, n_cols, bsz, lc, seq, n_scaled=0, n_normed=0, scale=1.0, head_dim=128, col0=0):
    n_ctx_c, n_lat_c = _scan_geometry(bsz, lc, seq)
    rows = u.shape[0]
    n_rb = rows // SCAN_CHUNK
    sub = SCAN_CHUNK // 8
    n_halo = rows // 8
    return pl.pallas_call(
        functools.partial(_conv_kernel, n_ctx_c=n_ctx_c, n_lat_c=n_lat_c, bsz=bsz, n_scaled=n_scaled,
                          n_normed=n_normed, scale=scale, head_dim=head_dim),
        grid=(n_rb, n_cols // CONV_COLS),
        in_specs=[pl.BlockSpec((SCAN_CHUNK, CONV_COLS), lambda r, c: (r, c + col0)),
                  pl.BlockSpec((8, CONV_COLS), lambda r, c: (jnp.maximum(r * sub - 1, 0), c + col0)),
                  pl.BlockSpec((8, CONV_COLS), lambda r, c: (jnp.minimum(r * sub + sub, n_halo - 1), c + col0)),
                  pl.BlockSpec((CONV_W, CONV_COLS), lambda r, c: (0, c)),
                  pl.BlockSpec((1, CONV_COLS), lambda r, c: (0, c))],
        out_specs=pl.BlockSpec((SCAN_CHUNK, CONV_COLS), lambda r, c: (r, c)),
        out_shape=jax.ShapeDtypeStruct((rows, n_cols), F32),
        compiler_params=_params("parallel", "parallel"),
        name="conv_silu",
    )(u, u, u, w, b)


GDN_SUB = 64


GDN_BASE = 8
GDN_HEADS_PER_STEP = 4


def _split3(x):
    x1 = x.astype(BF16)
    r1 = x - x1.astype(F32)
    x2 = r1.astype(BF16)
    return x1, x2, (r1 - x2.astype(F32)).astype(BF16)


def _sel_rows(m, x):
    mb = m.astype(BF16)
    return sum(jnp.dot(mb, p, preferred_element_type=F32) for p in _split3(x))


def _sel_cols(x, m):
    mb = m.astype(BF16)
    return sum(jnp.dot(p, mb, preferred_element_type=F32) for p in _split3(x))


def _sel_rows_t(x, m):
    mb = m.astype(BF16)
    return sum(lax.dot_general(p, mb, (((0,), (1,)), ((), ())), preferred_element_type=F32) for p in _split3(x))


def _mm_bf16(a, b):
    return jnp.dot(a.astype(BF16), b.astype(BF16), preferred_element_type=F32)


def _unit_tri_inverse_minus_eye(mats, ti, si):
    def same(n):
        s = int(math.log2(n))
        return lax.shift_right_logical(ti, s) == lax.shift_right_logical(si, s)

    base = same(GDN_BASE)
    ps = [jnp.where(base, a, 0.0) for a in mats]
    ns = [-p for p in ps]
    for _ in range(int(math.log2(GDN_BASE)) - 1):
        ps = [_mm_bf16(p, p) for p in ps]
        ns = [n + p + _mm_bf16(n, p) for n, p in zip(ns, ps)]
    b = GDN_BASE
    while b < GDN_SUB:
        join = same(2 * b) & jnp.logical_not(same(b))
        cs = [jnp.where(join, a, 0.0) for a in mats]
        ms = [c + _mm_bf16(c, n) for c, n in zip(cs, ns)]
        ns = [n - (m + _mm_bf16(n, m)) for n, m in zip(ns, ms)]
        b *= 2
    return ns


def _softplus(x):
    return jnp.maximum(x, 0.0) + jnp.log(1.0 + jnp.exp(-jnp.abs(x)))


def _gdn_kernel(*refs, reverse, n_kh, n_r, hd):
    if reverse:
        q_ref, k_ref, v_ref, tail_ref, prow_ref, arow_ref, o_ref, s_ref = refs
    else:
        q_ref, k_ref, v_ref, tail_ref, prow_ref, arow_ref, yb_ref, z_ref, ng_ref, o_ref, s_ref = refs
    g, c = pl.program_id(1), pl.program_id(2)
    qn = SCAN_CHUNK
    n_sub = qn // GDN_SUB
    n_gate = tail_ref.shape[1] // 2

    @pl.when(c == 0)
    def _():
        s_ref[...] = jnp.zeros_like(s_ref)

    tail = tail_ref[...]
    lane = lax.broadcasted_iota(jnp.int32, (1, 2 * n_gate), 1)
    gates = jnp.where(lane < n_gate, jax.nn.sigmoid(tail), -jnp.exp(arow_ref[...]) * _softplus(tail + prow_ref[...]))
    nh = n_kh * n_r
    li = lax.broadcasted_iota(jnp.int32, (2 * n_gate, 2 * nh), 0)
    ji = lax.broadcasted_iota(jnp.int32, (2 * n_gate, 2 * nh), 1)
    col = jnp.where(ji < nh, ji, n_gate + ji - nh) + (n_gate // 2 if reverse else 0) + g * nh
    gsel = _sel_cols(gates, (li == col).astype(F32))
    lw = lax.broadcasted_iota(jnp.int32, (2 * n_gate, 2 * nh * hd), 0)
    jw = lax.shift_right_logical(lax.broadcasted_iota(jnp.int32, (2 * n_gate, 2 * nh * hd), 1), int(math.log2(hd)))
    colw = jnp.where(jw < nh, jw, n_gate + jw - nh) + (n_gate // 2 if reverse else 0) + g * nh
    gwide = _sel_cols(gates, (lw == colw).astype(F32))

    ti = lax.broadcasted_iota(jnp.int32, (qn, 1), 0)
    si = lax.broadcasted_iota(jnp.int32, (1, qn), 1)
    shift = int(math.log2(GDN_SUB))
    same = lax.shift_right_logical(ti, shift) == lax.shift_right_logical(si, shift)
    if reverse:
        incl, strict = same & (si >= ti), same & (si > ti)
    else:
        incl, strict = same & (si <= ti), same & (si < ti)
    cs = incl.astype(F32)
    cum_w = _sel_rows(cs, gwide[:, nh * hd:])
    cum_t = _sel_rows_t(gsel, cs)
    reps = qn // hd

    order = range(n_sub - 1, -1, -1) if reverse else range(n_sub)
    outs = []
    shared = []
    for kh in range(n_kh):
        k = k_ref[:, kh * hd:(kh + 1) * hd]
        qb, kb = q_ref[:, kh * hd:(kh + 1) * hd].astype(BF16), k.astype(BF16)
        shared.append((k, qb, _dot_nt(kb, kb), _dot_nt(qb, kb)))
    heads = range(nh)
    beta = [gwide[:, j * hd:(j + 1) * hd] for j in heads]
    cum_c = [cum_w[:, j * hd:(j + 1) * hd] for j in heads]
    decay = [jnp.exp(jnp.where(incl, jnp.concatenate([cum_c[j]] * reps, axis=1) - cum_t[nh + j:nh + j + 1, :],
                               -jnp.inf)) for j in heads]
    amat = [jnp.where(strict, jnp.concatenate([beta[j]] * reps, axis=1) * shared[j // n_r][2] * decay[j], 0.0)
            for j in heads]
    inv_off = _unit_tri_inverse_minus_eye(amat, ti, si)
    rhs = [jnp.concatenate([v_ref[:, j * hd:(j + 1) * hd] * beta[j],
                            shared[j // n_r][0] * (beta[j] * jnp.exp(cum_c[j]))], axis=1) for j in heads]
    sol = [rhs[j] + _mm_bf16(inv_off[j], rhs[j]) for j in heads]
    state = [s_ref[j] for j in heads]
    v_new = [[None] * n_sub for _ in heads]
    inter = [[None] * n_sub for _ in heads]
    for i in order:
        sl = slice(i * GDN_SUB, (i + 1) * GDN_SUB)
        end = i * GDN_SUB if reverse else (i + 1) * GDN_SUB - 1
        sb = [state[j].astype(BF16) for j in heads]
        for j in heads:
            v_new[j][i] = sol[j][sl, :hd] - jnp.dot(sol[j][sl, hd:].astype(BF16), sb[j], preferred_element_type=F32)
            inter[j][i] = jnp.dot(shared[j // n_r][1][sl], sb[j], preferred_element_type=F32)
        for j in heads:
            cum_end = cum_c[j][end:end + 1, :]
            k_end = (shared[j // n_r][0][sl] * jnp.exp(cum_end - cum_c[j][sl])).astype(BF16)
            state[j] = jnp.exp(cum_end) * state[j] + _dot_t(k_end, v_new[j][i].astype(BF16))
    outs = []
    for j in heads:
        s_ref[j] = state[j]
        attn = (shared[j // n_r][3] * decay[j]).astype(BF16)
        y = jnp.dot(attn, jnp.concatenate(v_new[j], axis=0).astype(BF16), preferred_element_type=F32)
        y = y + jnp.concatenate(inter[j], axis=0) * jnp.exp(cum_c[j])
        if not reverse:
            y = y + yb_ref[:, j * hd:(j + 1) * hd]
            y = y * lax.rsqrt(jnp.mean(y * y, axis=-1, keepdims=True) + NORM_EPS) * ng_ref[...]
            z = z_ref[:, j * hd:(j + 1) * hd]
            y = y * (z * jax.nn.sigmoid(z))
        outs.append(y)
    o_ref[...] = jnp.concatenate(outs, axis=1).astype(o_ref.dtype)


def _gdn_mixer(u, tail, conv_w, dt_bias, a_log, norm_g, bsz, lc, seq, d_model):
    n_ctx_c, n_lat_c = _scan_geometry(bsz, lc, seq)
    n_steps = n_ctx_c + n_lat_c
    hd = GDN_HEAD_DIM
    k_heads = d_model // hd
    n_r = 2
    rows = u.shape[0]
    conv_ch = 4 * d_model
    nq = d_model // CONV_COLS
    qkv = _conv_silu(u, conv_w, jnp.zeros((1, conv_ch), F32), conv_ch, bsz, lc, seq, n_scaled=nq, n_normed=2 * nq,
                     scale=hd ** -0.5, head_dim=hd)
    n_gate = tail.shape[1] // 2
    prow = jnp.concatenate([jnp.zeros((1, n_gate), F32), dt_bias.reshape(1, n_gate)], axis=1)
    arow = jnp.concatenate([jnp.zeros((1, n_gate), F32), a_log.reshape(1, n_gate)], axis=1)
    y_b = None
    for reverse in (True, False):
        rb = functools.partial(_row_block, bsz=bsz, n_ctx_c=n_ctx_c, n_lat_c=n_lat_c, reverse=reverse)
        const = lambda b, g, c: (0, 0)
        qk_w, v_w = GDN_HEADS_PER_STEP * hd, GDN_HEADS_PER_STEP * n_r * hd
        n_groups = k_heads // GDN_HEADS_PER_STEP
        in_specs = [pl.BlockSpec((SCAN_CHUNK, qk_w), lambda b, g, c, rb=rb: (rb(b, c), g)),
                    pl.BlockSpec((SCAN_CHUNK, qk_w), lambda b, g, c, rb=rb: (rb(b, c), n_groups + g)),
                    pl.BlockSpec((SCAN_CHUNK, v_w), lambda b, g, c, rb=rb: (rb(b, c), n_groups + g)),
                    pl.BlockSpec((SCAN_CHUNK, 2 * n_gate), lambda b, g, c, rb=rb: (rb(b, c), 0)),
                    pl.BlockSpec((1, 2 * n_gate), const),
                    pl.BlockSpec((1, 2 * n_gate), const)]
        args = [qkv, qkv, qkv, tail, prow, arow]
        if not reverse:
            in_specs += [pl.BlockSpec((SCAN_CHUNK, v_w), lambda b, g, c, rb=rb: (rb(b, c), g)),
                         pl.BlockSpec((SCAN_CHUNK, v_w), lambda b, g, c, rb=rb: (rb(b, c), 2 * n_groups + g)),
                         pl.BlockSpec((1, hd), const)]
            args += [y_b, u, norm_g.reshape(1, hd)]
        out = pl.pallas_call(
            functools.partial(_gdn_kernel, reverse=reverse, n_kh=GDN_HEADS_PER_STEP, n_r=n_r, hd=hd),
            grid=(bsz, n_groups, n_steps),
            in_specs=in_specs,
            out_specs=pl.BlockSpec((SCAN_CHUNK, v_w), lambda b, g, c, rb=rb: (rb(b, c), g)),
            out_shape=jax.ShapeDtypeStruct((rows, k_heads * n_r * hd), F32 if reverse else BF16),
            scratch_shapes=[pltpu.VMEM((GDN_HEADS_PER_STEP * n_r, hd, hd), F32)],
            compiler_params=_params("parallel", "parallel", "arbitrary"),
            name="gdn_bwd" if reverse else "gdn_fwd",
        )(*args)
        if reverse:
            y_b = out
    return out


def _ssd_kernel(*refs, reverse, n_r, hd):
    if reverse:
        x_ref, b_ref, c_ref, tail_ref, dtb_ref, alog_ref, o_ref, s_ref = refs
    else:
        x_ref, b_ref, c_ref, tail_ref, dtb_ref, alog_ref, yb_ref, z_ref, d_ref, ng_ref, o_ref, s_ref = refs
    g, c = pl.program_id(1), pl.program_id(2)
    qn = SCAN_CHUNK
    n_lane = tail_ref.shape[1]
    width = n_r * hd

    @pl.when(c == 0)
    def _():
        s_ref[...] = jnp.zeros_like(s_ref)

    dt_all = _softplus(tail_ref[...] + dtb_ref[...])
    la_all = -jnp.exp(alog_ref[...]) * dt_all
    li = lax.broadcasted_iota(jnp.int32, (n_lane, n_r), 0)
    ji = lax.broadcasted_iota(jnp.int32, (n_lane, n_r), 1)
    sel = (li == ji + (n_lane // 2 if reverse else 0) + g * n_r).astype(F32)
    dt, la = _sel_cols(dt_all, sel), _sel_cols(la_all, sel)

    ti = lax.broadcasted_iota(jnp.int32, (qn, 1), 0)
    si = lax.broadcasted_iota(jnp.int32, (1, qn), 1)
    before = (si >= ti) if reverse else (si <= ti)
    cs = before.astype(F32)
    cum = _sel_rows(cs, la)
    cum_t = _sel_rows_t(la, cs)
    ei = lax.broadcasted_iota(jnp.int32, (n_r, width), 0)
    el = lax.broadcasted_iota(jnp.int32, (n_r, width), 1)
    expand = (lax.shift_right_logical(el, int(math.log2(hd))) == ei).astype(F32)
    dt_x, cum_x = _sel_cols(dt, expand), _sel_cols(cum, expand)

    xs = x_ref[...]
    v = xs * dt_x
    bm, cm = b_ref[...].astype(BF16), c_ref[...].astype(BF16)
    scores = _dot_nt(cm, bm)
    lane = lax.broadcasted_iota(jnp.int32, (1, 2 * hd), 1)
    tiles = []
    for p in range(n_r // 2):
        vt = v[:, 2 * p * hd:2 * (p + 1) * hd]
        acc = None
        for h in (2 * p, 2 * p + 1):
            decay = jnp.exp(jnp.where(before, cum[:, h:h + 1] - cum_t[h:h + 1, :], -jnp.inf))
            vh = jnp.where((lane >= hd) if h % 2 else (lane < hd), vt, 0.0).astype(BF16)
            part = jnp.dot((scores * decay).astype(BF16), vh, preferred_element_type=F32)
            acc = part if acc is None else acc + part
        tiles.append(acc)
    state = s_ref[...]
    y = jnp.concatenate(tiles, axis=1) + jnp.dot(cm, state.astype(BF16), preferred_element_type=F32) * jnp.exp(cum_x)
    end = 0 if reverse else qn - 1
    cum_end = cum_x[end:end + 1]
    s_ref[...] = state * jnp.exp(cum_end) + _dot_t(bm, (v * jnp.exp(cum_end - cum_x)).astype(BF16))
    if reverse:
        o_ref[...] = y
    else:
        y = y + yb_ref[...] + d_ref[...] * xs
        z = z_ref[...]
        y = y * (z * jax.nn.sigmoid(z))
        y = y * lax.rsqrt(jnp.mean(y * y, axis=-1, keepdims=True) + NORM_EPS) * ng_ref[...]
        o_ref[...] = y.astype(o_ref.dtype)


def _ssd_mixer(u, tail, conv_w, conv_b, dt_bias, a_log, d_skip, norm_g, bsz, lc, seq):
    n_ctx_c, n_lat_c = _scan_geometry(bsz, lc, seq)
    n_steps = n_ctx_c + n_lat_c
    d_inner = norm_g.shape[0]
    hd, st = SSD_HEAD_DIM, SSD_STATE
    heads = d_inner // hd
    n_r = heads // SSD_GROUPS
    width = n_r * hd
    rows = u.shape[0]
    conv_ch = d_inner + 2 * SSD_GROUPS * st
    xbc = _conv_silu(u, conv_w, conv_b.reshape(1, conv_ch), conv_ch, bsz, lc, seq, col0=d_inner // CONV_COLS)
    d_x = jnp.repeat(d_skip, hd).reshape(1, d_inner)
    y_b = None
    for reverse in (True, False):
        rb = functools.partial(_row_block, bsz=bsz, n_ctx_c=n_ctx_c, n_lat_c=n_lat_c, reverse=reverse)
        const = lambda b, g, c: (0, 0)
        grp = lambda b, g, c: (0, g)
        wide = lambda b, g, c, rb=rb: (rb(b, c), g)
        in_specs = [pl.BlockSpec((SCAN_CHUNK, width), wide),
                    pl.BlockSpec((SCAN_CHUNK, st), lambda b, g, c, rb=rb: (rb(b, c), d_inner // st + g)),
                    pl.BlockSpec((SCAN_CHUNK, st), lambda b, g, c, rb=rb: (rb(b, c), d_inner // st + SSD_GROUPS + g)),
                    pl.BlockSpec((SCAN_CHUNK, 2 * heads), lambda b, g, c, rb=rb: (rb(b, c), 0)),
                    pl.BlockSpec((1, 2 * heads), const),
                    pl.BlockSpec((1, 2 * heads), const)]
        args = [xbc, xbc, xbc, tail, dt_bias.reshape(1, 2 * heads), a_log.reshape(1, 2 * heads)]
        if not reverse:
            in_specs += [pl.BlockSpec((SCAN_CHUNK, width), wide), pl.BlockSpec((SCAN_CHUNK, width), wide),
                         pl.BlockSpec((1, width), grp), pl.BlockSpec((1, width), grp)]
            args += [y_b, u, d_x, norm_g.reshape(1, d_inner)]
        out = pl.pallas_call(
            functools.partial(_ssd_kernel, reverse=reverse, n_r=n_r, hd=hd),
            grid=(bsz, SSD_GROUPS, n_steps),
            in_specs=in_specs,
            out_specs=pl.BlockSpec((SCAN_CHUNK, width), wide),
            out_shape=jax.ShapeDtypeStruct((rows, d_inner), F32 if reverse else BF16),
            scratch_shapes=[pltpu.VMEM((st, width), F32)],
            compiler_params=_params("parallel", "parallel", "arbitrary"),
            name="ssd_bwd" if reverse else "ssd_fwd",
        )(*args)
        if reverse:
            y_b = out
    return out


HG_SUB = 64
HG_BLK = 16


def _log1p(x):
    return jnp.log(1.0 + x)


def _hgrn_kernel(*refs, reverse):
    if reverse:
        q_ref, f_ref, i_ref, lb_ref, o_ref, st_ref = refs
    else:
        q_ref, f_ref, i_ref, lb_ref, yb_ref, g_ref, ng_ref, o_ref, st_ref = refs
    c = pl.program_id(2)
    qn = SCAN_CHUNK
    hd = q_ref.shape[1]
    n_sub, n_blk = qn // HG_SUB, HG_SUB // HG_BLK

    @pl.when(c == 0)
    def _():
        st_ref[...] = jnp.zeros_like(st_ref)

    q, f, v, lb = q_ref[...], f_ref[...], i_ref[...], lb_ref[...]
    log_sig = jnp.minimum(f, 0.0) - _log1p(jnp.exp(-jnp.abs(f)))
    ga, gb = jnp.log(lb), _log1p(-lb) + log_sig
    log_f = jnp.maximum(ga, gb) + _log1p(jnp.exp(-jnp.abs(ga - gb)))
    k = (1.0 - lb) * jax.nn.sigmoid(-f)

    ti = lax.broadcasted_iota(jnp.int32, (qn, 1), 0)
    si = lax.broadcasted_iota(jnp.int32, (1, qn), 1)
    before = (si >= ti) if reverse else (si <= ti)
    same = lambda n: lax.shift_right_logical(ti, int(math.log2(n))) == lax.shift_right_logical(si, int(math.log2(n)))
    cum_sub = _sel_rows((same(HG_SUB) & before).astype(F32), log_f)
    cum_blk = _sel_rows((same(HG_BLK) & before).astype(F32), log_f)
    q_blk = q * jnp.exp(cum_blk)
    q_sub = (q * jnp.exp(cum_sub)).astype(BF16)

    ones = jnp.ones((hd, hd), BF16)
    lane = lax.broadcasted_iota(jnp.int32, (1, hd), 1)
    row_sub = lax.broadcasted_iota(jnp.int32, (HG_SUB, 1), 0)
    row_blk = lax.broadcasted_iota(jnp.int32, (HG_BLK, 1), 0)
    zeros_sub = jnp.zeros((hd - HG_SUB, hd), F32)
    y_intra, kvt, chunk_dec = [], [], []
    for i in range(n_sub):
        r0 = i * HG_SUB
        sl = slice(r0, r0 + HG_SUB)
        cs, ks, vs = cum_sub[sl], k[sl], v[sl]
        end = r0 if reverse else r0 + HG_SUB - 1
        cum_end = cum_sub[end:end + 1]
        kvt.append(_dot_t(vs.astype(BF16), (ks * jnp.exp(cum_end - cs)).astype(BF16)))
        chunk_dec.append(jnp.exp(cum_end))
        a_rows = []
        for a in range(n_blk):
            b0 = r0 + a * HG_BLK
            bl = slice(b0, b0 + HG_BLK)
            qb, kb, cb = q[bl], k[bl], cum_blk[bl]
            tiles = []
            for s in range(HG_BLK):
                ok = (row_blk <= s) if reverse else (row_blk >= s)
                e = jnp.exp(jnp.where(ok, cb - cb[s:s + 1], -jnp.inf))
                tiles.append(qb * e * kb[s:s + 1])
            sums = jnp.dot(jnp.concatenate(tiles, axis=0).astype(BF16), ones, preferred_element_type=F32)
            acc = jnp.zeros((HG_BLK, hd), F32)
            for s in range(HG_BLK):
                acc = acc + jnp.where(lane == a * HG_BLK + s, sums[s * HG_BLK:(s + 1) * HG_BLK], 0.0)
            has_earlier = (a < n_blk - 1) if reverse else (a > 0)
            if has_earlier:
                ref_row = b0 + HG_BLK if reverse else b0 - 1
                earlier = (row_sub >= (a + 1) * HG_BLK) if reverse else (row_sub < a * HG_BLK)
                kt = ks * jnp.exp(jnp.where(earlier, cum_sub[ref_row:ref_row + 1] - cs, -jnp.inf))
                kt = jnp.concatenate([kt, zeros_sub], axis=0).astype(BF16)
                acc = acc + _dot_nt(q_blk[bl].astype(BF16), kt)
            a_rows.append(acc)
        attn = jnp.concatenate(a_rows, axis=0).astype(BF16)
        v_pad = jnp.concatenate([vs, zeros_sub], axis=0).astype(BF16)
        y_intra.append(jnp.dot(attn, v_pad, preferred_element_type=F32))

    state = st_ref[...]
    ys = [None] * n_sub
    for i in (range(n_sub - 1, -1, -1) if reverse else range(n_sub)):
        ys[i] = y_intra[i] + _dot_nt(q_sub[i * HG_SUB:(i + 1) * HG_SUB], state.astype(BF16))
        state = state * chunk_dec[i] + kvt[i]
    st_ref[...] = state
    y = jnp.concatenate(ys, axis=0)
    if reverse:
        o_ref[...] = y
    else:
        y = y + yb_ref[...]
        y = y * lax.rsqrt(jnp.mean(y * y, axis=-1, keepdims=True) + NORM_EPS) * ng_ref[...]
        g = g_ref[...]
        o_ref[...] = (y * (g * jax.nn.sigmoid(g))).astype(o_ref.dtype)


def _hgrn_mixer(u, lb, norm_g, bsz, lc, seq):
    n_ctx_c, n_lat_c = _scan_geometry(bsz, lc, seq)
    n_steps = n_ctx_c + n_lat_c
    hd = HGRN_EXPAND
    d_model = lb.shape[0]
    heads = d_model // hd
    rows = u.shape[0]
    lb2, ng2 = lb.reshape(1, d_model), norm_g.reshape(1, d_model)
    y_b = None
    for reverse in (True, False):
        rb = functools.partial(_row_block, bsz=bsz, n_ctx_c=n_ctx_c, n_lat_c=n_lat_c, reverse=reverse)
        col = lambda seg: (lambda b, h, c, rb=rb: (rb(b, c), seg * heads + h))
        par = lambda b, h, c: (0, h)
        in_specs = [pl.BlockSpec((SCAN_CHUNK, hd), col(0)),
                    pl.BlockSpec((SCAN_CHUNK, hd), col(2 if reverse else 1)),
                    pl.BlockSpec((SCAN_CHUNK, hd), col(3)),
                    pl.BlockSpec((1, hd), par)]
        args = [u, u, u, lb2]
        if not reverse:
            in_specs += [pl.BlockSpec((SCAN_CHUNK, hd), col(0)), pl.BlockSpec((SCAN_CHUNK, hd), col(4)),
                         pl.BlockSpec((1, hd), par)]
            args += [y_b, u, ng2]
        out = pl.pallas_call(
            functools.partial(_hgrn_kernel, reverse=reverse),
            grid=(bsz, heads, n_steps),
            in_specs=in_specs,
            out_specs=pl.BlockSpec((SCAN_CHUNK, hd), col(0)),
            out_shape=jax.ShapeDtypeStruct((rows, d_model), F32 if reverse else BF16),
            scratch_shapes=[pltpu.VMEM((hd, hd), F32)],
            compiler_params=_params("parallel", "parallel", "arbitrary"),
            name="hgrn_bwd" if reverse else "hgrn_fwd",
        )(*args)
        if reverse:
            y_b = out
    return out


def _rmsnorm(x, g, eps=NORM_EPS):
    y = x * lax.rsqrt(jnp.mean(x * x, axis=-1, keepdims=True) + eps)
    return y * g


def _l2norm(x, eps=1e-6):
    return x * lax.rsqrt(jnp.sum(x * x, axis=-1, keepdims=True) + eps)


def _dwconv(u, w):
    return lax.conv_general_dilated(u, w[:, None, :], window_strides=(1,), padding=[(CONV_W // 2, CONV_W // 2)],
                                    dimension_numbers=('NWC', 'WIO', 'NWC'), feature_group_count=u.shape[-1])


def _conv_split(u, w, lc):
    return jnp.concatenate([_dwconv(u[:, :lc], w), _dwconv(u[:, lc:], w)], axis=1)


def _rev(t, lc):
    return jnp.concatenate([jnp.flip(t[:, :lc], 1), jnp.flip(t[:, lc:], 1)], axis=1)


def _to_chunks(t):
    b, n = t.shape[:2]
    return jnp.moveaxis(t.reshape((b, n // CHUNK, CHUNK) + t.shape[2:]), 1, 0)


def _from_chunks(t):
    nc, b, q = t.shape[:3]
    return jnp.moveaxis(t, 0, 1).reshape((b, nc * q) + t.shape[3:])


def _chunk_masks():
    idx = jnp.arange(CHUNK)
    return idx[:, None] >= idx[None, :], idx[:, None] > idx[None, :]


def _scalar_decay_scan(q, k, v, log_a):
    bsz, _, g, n = q.shape
    r, p = v.shape[-2:]
    incl, _ = _chunk_masks()

    def body(s, xs):
        qc, kc, vc, la = xs
        cum = jnp.cumsum(la, axis=1)
        cum_t = jnp.moveaxis(cum, 1, -1)
        seg = cum_t[..., :, None] - cum_t[..., None, :]
        scores = jnp.einsum('btgn,bsgn->bgts', qc, kc)
        attn = scores[:, :, None] * jnp.exp(jnp.where(incl, seg, -jnp.inf))
        y = jnp.einsum('bgrts,bsgrp->btgrp', attn, vc)
        y = y + jnp.einsum('btgn,bgrnp->btgrp', qc, s) * jnp.exp(cum)[..., None]
        to_end = jnp.exp(cum[:, -1:] - cum)
        s = jnp.exp(cum[:, -1])[..., None, None] * s + jnp.einsum('bsgn,bsgr,bsgrp->bgrnp', kc, to_end, vc)
        return s, y

    s0 = jnp.zeros((bsz, g, r, n, p), F32)
    _, y = lax.scan(body, s0, tuple(_to_chunks(t) for t in (q, k, v, log_a)))
    return _from_chunks(y)


def _vector_decay_scan(q, k, v, log_f):
    bsz, _, h, kd = q.shape
    vd = v.shape[-1]
    incl, _ = _chunk_masks()

    def body(s, xs):
        qc, kc, vc, lf = xs
        cum = jnp.cumsum(lf, axis=1)
        seg = cum[:, :, None] - cum[:, None, :]
        decay = jnp.exp(jnp.where(incl[:, :, None, None], seg, -jnp.inf))
        attn = jnp.einsum('bthk,bshk,btshk->bhts', qc, kc, decay)
        y = jnp.einsum('bhts,bshv->bthv', attn, vc)
        y = y + jnp.einsum('bthk,bhkv->bthv', qc * jnp.exp(cum), s)
        s = jnp.exp(cum[:, -1])[..., None] * s + jnp.einsum('bshk,bshv->bhkv', kc * jnp.exp(cum[:, -1:] - cum), vc)
        return s, y

    s0 = jnp.zeros((bsz, h, kd, vd), F32)
    _, y = lax.scan(body, s0, tuple(_to_chunks(t) for t in (q, k, v, log_f)))
    return _from_chunks(y)


def _delta_scan(q, k, v, beta, log_a):
    bsz, _, g, kd = q.shape
    r, vd = v.shape[-2:]
    incl, strict = _chunk_masks()

    def body(s, xs):
        qc, kc, vc, bc, la = xs
        cum = jnp.cumsum(la, axis=1)
        cum_t = jnp.moveaxis(cum, 1, -1)
        seg = cum_t[..., :, None] - cum_t[..., None, :]
        beta_t = jnp.moveaxis(bc, 1, -1)
        kk = jnp.einsum('btgk,bsgk->bgts', kc, kc)
        lower = beta_t[..., :, None] * kk[:, :, None] * jnp.exp(jnp.where(strict, seg, -jnp.inf))
        rhs_v = jnp.moveaxis(vc * bc[..., None], 1, 3)
        rhs_k = jnp.moveaxis(kc[:, :, :, None, :] * (bc * jnp.exp(cum))[..., None], 1, 3)
        sol = lax.linalg.triangular_solve(lower, jnp.concatenate([rhs_v, rhs_k], axis=-1),
                                          left_side=True, lower=True, unit_diagonal=True)
        u, w = sol[..., :vd], sol[..., vd:]
        v_new = u - jnp.einsum('bgrtk,bgrkv->bgrtv', w, s)
        qk = jnp.einsum('btgk,bsgk->bgts', qc, kc)
        attn = qk[:, :, None] * jnp.exp(jnp.where(incl, seg, -jnp.inf))
        y = jnp.einsum('bgrts,bgrsv->btgrv', attn, v_new)
        y = y + jnp.einsum('btgk,bgrkv->btgrv', qc, s) * jnp.exp(cum)[..., None]
        to_end = jnp.exp(cum_t[..., -1:] - cum_t)
        s = jnp.exp(cum_t[..., -1])[..., None, None] * s + jnp.einsum('bsgk,bgrs,bgrsv->bgrkv', kc, to_end, v_new)
        return s, y

    s0 = jnp.zeros((bsz, g, r, kd, vd), F32)
    _, y = lax.scan(body, s0, tuple(_to_chunks(t) for t in (q, k, v, beta, log_a)))
    return _from_chunks(y)


def _rope_tables(rows, half):
    pos = np.arange(rows * GRID_W)
    inv_freq = np.float32(ROPE_BASE) ** (-(np.arange(0, half, 2, dtype=np.float32) / np.float32(half)))
    out = []
    for p in ((pos // GRID_W).astype(np.float32), (pos % GRID_W).astype(np.float32)):
        ang = (p[:, None] * inv_freq.astype(np.float32)).astype(np.float32).astype(np.float64)
        out += [np.cos(ang).astype(np.float32), np.sin(ang).astype(np.float32)]
    return out


def _rope_2d(t, rows):
    half = t.shape[-1] // 2
    cos_r, sin_r, cos_c, sin_c = (jnp.asarray(a)[:, None, :] for a in _rope_tables(rows, half))

    def rot(u, cos, sin):
        u1, u2 = jnp.split(u, 2, axis=-1)
        return jnp.concatenate([u1 * cos - u2 * sin, u2 * cos + u1 * sin], axis=-1)

    return jnp.concatenate([rot(t[..., :half], cos_r, sin_r), rot(t[..., half:], cos_c, sin_c)], axis=-1)


def _ssd_core(u, lc, conv_w, conv_b, dt_bias, a_log, d_skip, norm_g, start):
    bsz, t = u.shape[:2]
    d_inner = norm_g.shape[0]
    heads = d_inner // SSD_HEAD_DIM
    conv_ch = d_inner + 2 * SSD_GROUPS * SSD_STATE
    z, xbc, dt = jnp.split(u, [d_inner, d_inner + conv_ch], axis=-1)
    xbc = jax.nn.silu(_conv_split(xbc, conv_w, lc) + conv_b)
    xs, bm, cm = jnp.split(xbc, [d_inner, d_inner + SSD_GROUPS * SSD_STATE], axis=-1)
    r = heads // SSD_GROUPS
    xs = xs.reshape(bsz, t, SSD_GROUPS, r, SSD_HEAD_DIM)
    bm = bm.reshape(bsz, t, SSD_GROUPS, SSD_STATE)
    cm = cm.reshape(bsz, t, SSD_GROUPS, SSD_STATE)
    dt = jax.nn.softplus(dt.reshape(bsz, t, 2, heads) + dt_bias)
    log_a = -jnp.exp(a_log) * dt
    grp = lambda a: a.reshape(bsz, t, SSD_GROUPS, r)
    y = _scalar_decay_scan(cm, bm, xs * grp(dt[:, :, 0])[..., None], grp(log_a[:, :, 0]))
    y = y + _rev(_scalar_decay_scan(_rev(cm, lc), _rev(bm, lc), _rev(xs * grp(dt[:, :, 1])[..., None], lc),
                                    _rev(grp(log_a[:, :, 1]), lc)), lc)
    y = y + d_skip.reshape(SSD_GROUPS, r)[..., None] * xs
    n = t - start
    y = y.reshape(bsz, t, d_inner)[:, start:] * jax.nn.silu(z[:, start:])
    return _rmsnorm(y.reshape(bsz, n, SSD_GROUPS, -1), norm_g.reshape(SSD_GROUPS, -1)).reshape(bsz, n, d_inner)


def _ret_core(u, lc, log_decay, d_model, rows, start):
    bsz, t = u.shape[:2]
    dv = 2 * d_model
    qk_dim = d_model // RET_HEADS
    v_dim = 2 * qk_dim
    q, k, v, g = jnp.split(u, [d_model, 2 * d_model, 2 * d_model + dv], axis=-1)
    q = q.reshape(bsz, t, RET_HEADS, qk_dim)
    k = k.reshape(bsz, t, RET_HEADS, qk_dim) * qk_dim ** -0.5
    q = jnp.concatenate([q[:, :lc], _rope_2d(q[:, lc:], rows)], axis=1)
    k = jnp.concatenate([k[:, :lc], _rope_2d(k[:, lc:], rows)], axis=1)
    v = v.reshape(bsz, t, RET_HEADS, 1, v_dim)
    ld_f = jnp.broadcast_to(log_decay[0][:, None], (bsz, t, RET_HEADS, 1))
    ld_b = jnp.broadcast_to(log_decay[1][:, None], (bsz, t, RET_HEADS, 1))
    y = _scalar_decay_scan(q, k, v, ld_f)
    y = y + _rev(_scalar_decay_scan(_rev(q, lc), _rev(k, lc), _rev(v, lc), ld_b), lc)
    n = t - start
    y = y[:, start:].reshape(bsz, n, RET_HEADS, v_dim)
    mu = jnp.mean(y, axis=-1, keepdims=True)
    var = jnp.mean(jnp.square(y - mu), axis=-1, keepdims=True)
    y = ((y - mu) * lax.rsqrt(var + NORM_EPS)).reshape(bsz, n, dv)
    return y * jax.nn.silu(g[:, start:])


def _lower_bound(lb_logits, layer):
    p = jax.nn.softmax(lb_logits.astype(F32), axis=0)
    return jnp.cumsum(p, axis=0)[layer] - p[0]


def _hgrn_core(u, lc, lb, norm_g, start):
    bsz, t = u.shape[:2]
    d_model = norm_g.shape[0]
    heads = d_model // HGRN_EXPAND
    q, f_f, f_b, i, g = jnp.split(u, 5, axis=-1)
    shp = (bsz, t, heads, HGRN_EXPAND)
    q, i = q.reshape(shp), i.reshape(shp)
    lb = lb.reshape(heads, HGRN_EXPAND)

    def gates(f):
        f = f.reshape(shp)
        log_f = jnp.logaddexp(jnp.log(lb), jnp.log1p(-lb) + jax.nn.log_sigmoid(f))
        return log_f, (1 - lb) * jax.nn.sigmoid(-f)

    lf_f, k_f = gates(f_f)
    lf_b, k_b = gates(f_b)
    y = _vector_decay_scan(q, k_f, i, lf_f)
    y = y + _rev(_vector_decay_scan(_rev(q, lc), _rev(k_b, lc), _rev(i, lc), _rev(lf_b, lc)), lc)
    n = t - start
    y = _rmsnorm(y[:, start:], norm_g.reshape(heads, HGRN_EXPAND))
    return (y * jax.nn.silu(g[:, start:].reshape(bsz, n, heads, HGRN_EXPAND))).reshape(bsz, n, d_model)


def _gdn_core(u, lc, conv_w, dt_bias, a_log, norm_g, d_model, start):
    bsz, t = u.shape[:2]
    k_heads = d_model // GDN_HEAD_DIM
    v_heads = 2 * k_heads
    dk, dv = d_model, 2 * d_model
    conv_ch = 2 * dk + dv
    qkv, z, bt, a = jnp.split(u, [conv_ch, conv_ch + dv, conv_ch + dv + 2 * v_heads], axis=-1)
    qkv = jax.nn.silu(_conv_split(qkv, conv_w, lc))
    q, k, v = jnp.split(qkv, [dk, 2 * dk], axis=-1)
    r = v_heads // k_heads
    q = _l2norm(q.reshape(bsz, t, k_heads, GDN_HEAD_DIM)) * GDN_HEAD_DIM ** -0.5
    k = _l2norm(k.reshape(bsz, t, k_heads, GDN_HEAD_DIM))
    v = v.reshape(bsz, t, k_heads, r, GDN_HEAD_DIM)
    beta = jax.nn.sigmoid(bt.reshape(bsz, t, 2, k_heads, r))
    log_a = -jnp.exp(a_log).reshape(2, k_heads, r) * jax.nn.softplus(
        a.reshape(bsz, t, 2, k_heads, r) + dt_bias.reshape(2, k_heads, r))
    y = _delta_scan(q, k, v, beta[:, :, 0], log_a[:, :, 0])
    y = y + _rev(_delta_scan(_rev(q, lc), _rev(k, lc), _rev(v, lc), _rev(beta[:, :, 1], lc),
                             _rev(log_a[:, :, 1], lc)), lc)
    n = t - start
    y = y[:, start:].reshape(bsz, n, v_heads, GDN_HEAD_DIM)
    y = _rmsnorm(y, norm_g) * jax.nn.silu(z[:, start:].reshape(bsz, n, v_heads, GDN_HEAD_DIM))
    return y.reshape(bsz, n, dv)


def _rows_to_seq(u, bsz, lc):
    n_ctx = bsz * lc
    return jnp.concatenate([u[:n_ctx].reshape(bsz, lc, -1), u[n_ctx:].reshape(bsz, -1, u.shape[-1])], axis=1)


def _seq_to_rows(y, lc, start):
    bsz = y.shape[0]
    if start:
        return y.reshape(bsz * y.shape[1], -1)
    return jnp.concatenate([y[:, :lc].reshape(bsz * lc, -1), y[:, lc:].reshape(-1, y.shape[-1])], axis=0)


def kernel(x, c, ctx, c_ctx, ada_w, ada_b, norm_g, mlp_w1, mlp_w2, final_g, ssd_w_in, ssd_conv_w, ssd_conv_b,
           ssd_dt_bias, ssd_a_log, ssd_d, ssd_norm_g, ssd_w_out, ret_w_in, ret_log_decay, ret_w_out, hgrn_w_in,
           hgrn_lb_logits, hgrn_norm_g, hgrn_w_out, gdn_w_in, gdn_conv_w, gdn_dt_bias, gdn_a_log, gdn_norm_g,
           gdn_w_out):
    bsz, seq, d = x.shape
    lc = ctx.shape[1]
    depth = ada_w.shape[0]
    rows_grid = seq // GRID_W
    n_ctx = bsz * lc
    assert lc % ROW_TILE == 0 or ROW_TILE % lc == 0 and n_ctx % ROW_TILE == 0
    assert seq % ROW_TILE == 0 and bsz + 1 <= 8

    cond_pad = jnp.concatenate([c, c_ctx[None], jnp.zeros((8 - bsz - 1, d), F32)], axis=0)
    mod = _ada_mod(cond_pad, ada_w, ada_b)

    tile_row = [bsz] * (n_ctx // ROW_TILE) + [b for b in range(bsz) for _ in range(seq // ROW_TILE)]
    tile_row = jnp.asarray(tile_row, jnp.int32)
    n_ctx_tiles = n_ctx // ROW_TILE

    xr = jnp.concatenate([ctx.reshape(n_ctx, d), x.reshape(bsz * seq, d)], axis=0)

    for i in range(depth):
        mixer, occ = i % 4, i // 4
        keep_ctx = i < depth - 1
        start = 0 if keep_ctx else lc
        mod_t = mod[i][tile_row][:, None, :]
        g0, g1 = norm_g[i, 0][None], norm_g[i, 1][None]
        if mixer == 0:
            w_in, w_out = ssd_w_in[occ], ssd_w_out[occ]
        elif mixer == 1:
            w_in, w_out = ret_w_in[occ], ret_w_out[occ]
        elif mixer == 2:
            w_in, w_out = hgrn_w_in[occ], hgrn_w_out[occ]
        else:
            w_in, w_out = gdn_w_in[occ], gdn_w_out[occ]
        n_in = w_in.shape[1]
        n_main = (n_in // 1024) * 1024 if n_in % 1024 else n_in
        w_in = w_in.astype(BF16)
        u = _ln_mm(xr, g0, mod_t, 0, 1, w_in[:, :n_main])
        tail = _ln_mm(xr, g0, mod_t, 0, 1, w_in[:, n_main:]) if n_main != n_in else None
        first_row = 0 if keep_ctx else n_ctx
        if mixer == 0:
            yr = _ssd_mixer(u, tail, ssd_conv_w[occ], ssd_conv_b[occ], ssd_dt_bias[occ], ssd_a_log[occ], ssd_d[occ],
                            ssd_norm_g[occ], bsz, lc, seq)[first_row:]
        elif mixer == 1:
            yr = _ret_mixer(u, ret_log_decay[occ], bsz, lc, seq, d)[first_row:]
        elif mixer == 2:
            yr = _hgrn_mixer(u, _lower_bound(hgrn_lb_logits, i), hgrn_norm_g[occ], bsz, lc, seq)[first_row:]
        elif mixer == 3:
            yr = _gdn_mixer(u, tail, gdn_conv_w[occ], gdn_dt_bias[occ], gdn_a_log[occ], gdn_norm_g[occ], bsz, lc,
                            seq, d)[first_row:]
        else:
            if tail is not None:
                u = jnp.concatenate([u, tail], axis=1)
            useq = _rows_to_seq(u, bsz, lc)
            if mixer == 0:
                y = _ssd_core(useq, lc, ssd_conv_w[occ], ssd_conv_b[occ], ssd_dt_bias[occ], ssd_a_log[occ],
                              ssd_d[occ], ssd_norm_g[occ], start)
            elif mixer == 2:
                y = _hgrn_core(useq, lc, _lower_bound(hgrn_lb_logits, i), hgrn_norm_g[occ], start)
            else:
                y = _gdn_core(useq, lc, gdn_conv_w[occ], gdn_dt_bias[occ], gdn_a_log[occ], gdn_norm_g[occ], d,
                              start)
            yr = _seq_to_rows(y, lc, start).astype(BF16)
        if not keep_ctx:
            xr = xr[n_ctx:]
            mod_t = mod_t[n_ctx_tiles:]
        xr = _out_proj(yr, w_out.astype(BF16), xr, mod_t, 2)
        xr = _mlp(xr, g1, mod_t, mlp_w1[i].astype(BF16), mlp_w2[i].astype(BF16), final_g[None], final=not keep_ctx)
    return xr.reshape(bsz, seq, d)
```

```python
import functools
import math

import jax
import jax.numpy as jnp
import numpy as np
from jax import lax
from jax.experimental import pallas as pl
from jax.experimental.pallas import tpu as pltpu

F32 = jnp.float32
BF16 = jnp.bfloat16

GRID_W = 64
CHUNK = 64
CONV_W = 3
NORM_EPS = 1e-6
ROPE_BASE = 10000.0
SSD_HEAD_DIM = 64
SSD_GROUPS = 8
SSD_STATE = 128
RET_HEADS = 8
HGRN_EXPAND = 128
GDN_HEAD_DIM = 128

ROW_TILE = 512
VMEM_LIMIT = 56 * 1024 * 1024


def _params(*sem):
    return pltpu.CompilerParams(dimension_semantics=sem, vmem_limit_bytes=VMEM_LIMIT)


def _col_tile(n, cap=1536):
    best = 128
    for t in range(128, cap + 1, 128):
        if n % t == 0:
            best = t
    return best


def _ada_kernel(c_ref, w_ref, b_ref, o_ref):
    c = c_ref[...]
    c = (c * jax.nn.sigmoid(c)).astype(BF16)
    o_ref[0] = jnp.dot(c, w_ref[0].astype(BF16), preferred_element_type=F32) + b_ref[0]


def _ada_mod(cond_pad, ada_w, ada_b):
    depth, d, n = ada_w.shape
    tn = 1024
    return pl.pallas_call(
        _ada_kernel,
        grid=(depth, n // tn),
        in_specs=[pl.BlockSpec((8, d), lambda l, j: (0, 0)),
                  pl.BlockSpec((1, d, tn), lambda l, j: (l, 0, j)),
                  pl.BlockSpec((1, 1, tn), lambda l, j: (l, 0, j))],
        out_specs=pl.BlockSpec((1, 8, tn), lambda l, j: (l, 0, j)),
        out_shape=jax.ShapeDtypeStruct((depth, 8, n), F32),
        compiler_params=_params("parallel", "parallel"),
        name="ada_mod",
    )(cond_pad, ada_w, ada_b.reshape(depth, 1, n))


def _adaln_rows(x, g, sh, sc):
    y = x * lax.rsqrt(jnp.mean(x * x, axis=-1, keepdims=True) + NORM_EPS)
    return y * g * (1.0 + sc) + sh


def _ln_mm_kernel(x_ref, g_ref, sh_ref, sc_ref, w_ref, o_ref, h_ref):
    @pl.when(pl.program_id(1) == 0)
    def _():
        h_ref[...] = _adaln_rows(x_ref[...], g_ref[...], sh_ref[0], sc_ref[0]).astype(BF16)

    o_ref[...] = jnp.dot(h_ref[...], w_ref[...], preferred_element_type=F32).astype(o_ref.dtype)


def _ln_mm(x, g, mod_t, sh_col, sc_col, w, col_start, n, out_dtype=F32):
    m, d = x.shape
    tm, tn = ROW_TILE, _col_tile(n)
    assert col_start % tn == 0
    col0 = col_start // tn
    return pl.pallas_call(
        _ln_mm_kernel,
        grid=(m // tm, n // tn),
        in_specs=[pl.BlockSpec((tm, d), lambda i, j: (i, 0)),
                  pl.BlockSpec((1, d), lambda i, j: (0, 0)),
                  pl.BlockSpec((1, 1, d), lambda i, j: (i, 0, sh_col)),
                  pl.BlockSpec((1, 1, d), lambda i, j: (i, 0, sc_col)),
                  pl.BlockSpec((d, tn), lambda i, j: (0, col0 + j))],
        out_specs=pl.BlockSpec((tm, tn), lambda i, j: (i, j)),
        out_shape=jax.ShapeDtypeStruct((m, n), out_dtype),
        scratch_shapes=[pltpu.VMEM((tm, d), BF16)],
        compiler_params=_params("parallel", "arbitrary"),
        name="adaln_in_proj",
    )(x, g, mod_t, mod_t, w)


def _out_kernel(y_ref, w_ref, x_ref, gate_ref, o_ref):
    o_ref[...] = x_ref[...] + gate_ref[0] * jnp.dot(y_ref[...], w_ref[...], preferred_element_type=F32)


def _out_proj(y, w, x, mod_t, gate_col, first_tile=0):
    k = y.shape[1]
    d = w.shape[1]
    tm, tn = ROW_TILE, 512
    m = y.shape[0] - first_tile * tm
    return pl.pallas_call(
        _out_kernel,
        grid=(m // tm, d // tn),
        in_specs=[pl.BlockSpec((tm, k), lambda i, j: (i + first_tile, 0)),
                  pl.BlockSpec((k, tn), lambda i, j: (0, j)),
                  pl.BlockSpec((tm, tn), lambda i, j: (i + first_tile, j)),
                  pl.BlockSpec((1, 1, tn), lambda i, j: (i + first_tile, 0, gate_col * (d // tn) + j))],
        out_specs=pl.BlockSpec((tm, tn), lambda i, j: (i, j)),
        out_shape=jax.ShapeDtypeStruct((m, d), F32),
        compiler_params=_params("parallel", "arbitrary"),
        name="out_proj",
    )(y, w, x, mod_t)


def _mlp_kernel(x_ref, g_ref, sh_ref, sc_ref, gate_ref, w1_ref, w2_ref, fg_ref, o_ref, h_ref, acc_ref, *, final):
    f = pl.program_id(1)

    @pl.when(f == 0)
    def _():
        h_ref[...] = _adaln_rows(x_ref[...], g_ref[...], sh_ref[0], sc_ref[0]).astype(BF16)
        acc_ref[...] = jnp.zeros_like(acc_ref)

    a = jnp.dot(h_ref[...], w1_ref[...], preferred_element_type=F32)
    a = jnp.square(jnp.maximum(a, 0.0)).astype(BF16)
    acc_ref[...] += jnp.dot(a, w2_ref[...], preferred_element_type=F32)

    @pl.when(f == pl.num_programs(1) - 1)
    def _():
        out = x_ref[...] + gate_ref[0] * acc_ref[...]
        if final:
            out = out * lax.rsqrt(jnp.mean(out * out, axis=-1, keepdims=True) + NORM_EPS) * fg_ref[...]
        o_ref[...] = out


def _mlp(x, g, mod_t, w1, w2, final_g, final):
    m, d = x.shape
    ff = w1.shape[1]
    tm, tf = ROW_TILE, 512
    return pl.pallas_call(
        functools.partial(_mlp_kernel, final=final),
        grid=(m // tm, ff // tf),
        in_specs=[pl.BlockSpec((tm, d), lambda i, f: (i, 0)),
                  pl.BlockSpec((1, d), lambda i, f: (0, 0)),
                  pl.BlockSpec((1, 1, d), lambda i, f: (i, 0, 3)),
                  pl.BlockSpec((1, 1, d), lambda i, f: (i, 0, 4)),
                  pl.BlockSpec((1, 1, d), lambda i, f: (i, 0, 5)),
                  pl.BlockSpec((d, tf), lambda i, f: (0, f)),
                  pl.BlockSpec((tf, d), lambda i, f: (f, 0)),
                  pl.BlockSpec((1, d), lambda i, f: (0, 0))],
        out_specs=pl.BlockSpec((tm, d), lambda i, f: (i, 0)),
        out_shape=jax.ShapeDtypeStruct((m, d), F32),
        scratch_shapes=[pltpu.VMEM((tm, d), BF16), pltpu.VMEM((tm, d), F32)],
        compiler_params=_params("parallel", "arbitrary"),
        name="adaln_mlp",
    )(x, g, mod_t, mod_t, mod_t, w1, w2, final_g)


SCAN_CHUNK = 256


def _scan_geometry(bsz, lc, seq):
    assert lc % SCAN_CHUNK == 0 and seq % SCAN_CHUNK == 0
    return lc // SCAN_CHUNK, seq // SCAN_CHUNK


def _seg_chunk(c, n_ctx_c, n_lat_c, reverse):
    if reverse:
        return c >= n_ctx_c, jnp.where(c < n_ctx_c, n_ctx_c - 1 - c, n_lat_c - 1 - (c - n_ctx_c))
    return c >= n_ctx_c, jnp.where(c < n_ctx_c, c, c - n_ctx_c)


def _row_block(b, c, bsz, n_ctx_c, n_lat_c, reverse):
    is_lat, j = _seg_chunk(c, n_ctx_c, n_lat_c, reverse)
    return jnp.where(is_lat, bsz * n_ctx_c + b * n_lat_c + j, b * n_ctx_c + j)


def _time_iotas(q):
    t = lax.broadcasted_iota(jnp.int32, (q, 1), 0).astype(F32)
    s = lax.broadcasted_iota(jnp.int32, (1, q), 1).astype(F32)
    return t, s


def _dot_t(a, b):
    return lax.dot_general(a, b, (((0,), (0,)), ((), ())), preferred_element_type=F32)


def _dot_nt(a, b):
    return lax.dot_general(a, b, (((1,), (1,)), ((), ())), preferred_element_type=F32)


def _ret_kernel(*refs, reverse, n_ctx_c, n_heads, k_scale):
    if reverse:
        ld_ref, q_ref, k_ref, v_ref, cos_ref, sin_ref, o_ref, s_ref = refs
    else:
        ld_ref, q_ref, k_ref, v_ref, cos_ref, sin_ref, yb_ref, g_ref, o_ref, s_ref = refs
    h, c = pl.program_id(1), pl.program_id(2)
    qn = SCAN_CHUNK

    @pl.when(c == 0)
    def _():
        s_ref[...] = jnp.zeros_like(s_ref)

    lg = ld_ref[(n_heads if reverse else 0) + h]
    t, s = _time_iotas(qn)
    is_lat = c >= n_ctx_c
    cos = jnp.where(is_lat, cos_ref[...], 1.0)
    sin = jnp.where(is_lat, sin_ref[...], 0.0)

    def rope(x):
        half = x.shape[1] // 2
        swapped = jnp.concatenate([pltpu.roll(x[:, :half], half // 2, 1), pltpu.roll(x[:, half:], half // 2, 1)], axis=1)
        return x * cos + swapped * sin

    q = rope(q_ref[...]).astype(BF16)
    k = rope(k_ref[...]) * k_scale
    v = v_ref[...].astype(BF16)
    if reverse:
        dmat = jnp.where(s >= t, jnp.exp((s - t) * lg), 0.0)
        q_dec, k_dec = jnp.exp((qn - t) * lg), jnp.exp(t * lg)
    else:
        dmat = jnp.where(t >= s, jnp.exp((t - s) * lg), 0.0)
        q_dec, k_dec = jnp.exp((t + 1.0) * lg), jnp.exp((qn - 1.0 - t) * lg)
    attn = (_dot_nt(q, k.astype(BF16)) * dmat).astype(BF16)
    state = s_ref[...]
    y = jnp.dot(attn, v, preferred_element_type=F32)
    y = y + jnp.dot(q, state.astype(BF16), preferred_element_type=F32) * q_dec
    chunk_dec = jnp.exp(jnp.full((1, 1), qn, F32) * lg)
    s_ref[...] = state * chunk_dec + _dot_t((k * k_dec).astype(BF16), v)
    if reverse:
        o_ref[...] = y
    else:
        y = y + yb_ref[...]
        mu = jnp.mean(y, axis=-1, keepdims=True)
        yc = y - mu
        var = jnp.mean(yc * yc, axis=-1, keepdims=True)
        g = g_ref[...]
        o_ref[...] = (yc * lax.rsqrt(var + NORM_EPS) * (g * jax.nn.sigmoid(g))).astype(o_ref.dtype)


def _ret_mixer(u, log_decay, bsz, lc, seq, d_model):
    n_ctx_c, n_lat_c = _scan_geometry(bsz, lc, seq)
    n_steps = n_ctx_c + n_lat_c
    qk_dim, v_dim = d_model // RET_HEADS, 2 * d_model // RET_HEADS
    rows = u.shape[0]
    cos_r, sin_r, cos_c, sin_c = _rope_tables(seq // GRID_W, qk_dim // 2)
    cos_t = jnp.asarray(np.concatenate([cos_r, cos_r, cos_c, cos_c], axis=1))
    sin_t = jnp.asarray(np.concatenate([-sin_r, sin_r, -sin_c, sin_c], axis=1))
    ld = log_decay.reshape(-1).astype(F32)
    y_b = None
    for reverse in (True, False):
        rb = functools.partial(_row_block, bsz=bsz, n_ctx_c=n_ctx_c, n_lat_c=n_lat_c, reverse=reverse)

        def tab(b, h, c, reverse=reverse):
            is_lat, j = _seg_chunk(c, n_ctx_c, n_lat_c, reverse)
            return (jnp.where(is_lat, j, 0), 0)

        in_specs = [pl.BlockSpec(memory_space=pltpu.SMEM),
                    pl.BlockSpec((SCAN_CHUNK, qk_dim), lambda b, h, c, rb=rb: (rb(b, c), h)),
                    pl.BlockSpec((SCAN_CHUNK, qk_dim), lambda b, h, c, rb=rb: (rb(b, c), RET_HEADS + h)),
                    pl.BlockSpec((SCAN_CHUNK, v_dim), lambda b, h, c, rb=rb: (rb(b, c), RET_HEADS + h)),
                    pl.BlockSpec((SCAN_CHUNK, qk_dim), tab),
                    pl.BlockSpec((SCAN_CHUNK, qk_dim), tab)]
        args = [ld, u, u, u, cos_t, sin_t]
        if not reverse:
            in_specs += [pl.BlockSpec((SCAN_CHUNK, v_dim), lambda b, h, c, rb=rb: (rb(b, c), h)),
                         pl.BlockSpec((SCAN_CHUNK, v_dim), lambda b, h, c, rb=rb: (rb(b, c), 2 * RET_HEADS + h))]
            args += [y_b, u]
        out = pl.pallas_call(
            functools.partial(_ret_kernel, reverse=reverse, n_ctx_c=n_ctx_c, n_heads=RET_HEADS, k_scale=qk_dim ** -0.5),
            grid=(bsz, RET_HEADS, n_steps),
            in_specs=in_specs,
            out_specs=pl.BlockSpec((SCAN_CHUNK, v_dim), lambda b, h, c, rb=rb: (rb(b, c), h)),
            out_shape=jax.ShapeDtypeStruct((rows, RET_HEADS * v_dim), F32 if reverse else BF16),
            scratch_shapes=[pltpu.VMEM((qk_dim, v_dim), F32)],
            compiler_params=_params("parallel", "parallel", "arbitrary"),
            name="retention_bwd" if reverse else "retention_fwd",
        )(*args)
        if reverse:
            y_b = out
    return out


CONV_COLS = 512


def _conv_kernel(x_ref, prev_ref, next_ref, w_ref, b_ref, o_ref, *, n_ctx_c, n_lat_c, bsz, n_scaled, n_normed, scale,
                 head_dim):
    rb, cb = pl.program_id(0), pl.program_id(1)
    is_lat = rb >= bsz * n_ctx_c
    j = jnp.where(is_lat, (rb - bsz * n_ctx_c) % n_lat_c, rb % n_ctx_c)
    last = jnp.where(is_lat, n_lat_c - 1, n_ctx_c - 1)
    x = x_ref[...]
    rows = x.shape[0]
    row = lax.broadcasted_iota(jnp.int32, (rows, 1), 0)
    halo_prev = jnp.where(j > 0, prev_ref[7:8, :], 0.0)
    halo_next = jnp.where(j < last, next_ref[0:1, :], 0.0)
    x_prev = jnp.where(row == 0, halo_prev, pltpu.roll(x, 1, 0))
    x_next = jnp.where(row == rows - 1, halo_next, pltpu.roll(x, rows - 1, 0))
    y = w_ref[0:1, :] * x_prev + w_ref[1:2, :] * x + w_ref[2:3, :] * x_next + b_ref[...]
    y = y * jax.nn.sigmoid(y)

    def normed(mult):
        parts = []
        for i in range(y.shape[1] // head_dim):
            p = y[:, i * head_dim:(i + 1) * head_dim]
            parts.append(p * (lax.rsqrt(jnp.sum(p * p, axis=-1, keepdims=True) + 1e-6) * mult))
        return jnp.concatenate(parts, axis=1)

    if n_normed == 0:
        o_ref[...] = y
    else:
        @pl.when(cb < n_scaled)
        def _():
            o_ref[...] = normed(scale)

        @pl.when(jnp.logical_and(cb >= n_scaled, cb < n_normed))
        def _():
            o_ref[...] = normed(1.0)

        @pl.when(cb >= n_normed)
        def _():
            o_ref[...] = y


def _conv_silu(u, w, b, n_cols, bsz, lc, seq, n_scaled=0, n_normed=0, scale=1.0, head_dim=128, col0=0):
    n_ctx_c, n_lat_c = _scan_geometry(bsz, lc, seq)
    rows = u.shape[0]
    n_rb = rows // SCAN_CHUNK
    sub = SCAN_CHUNK // 8
    n_halo = rows // 8
    return pl.pallas_call(
        functools.partial(_conv_kernel, n_ctx_c=n_ctx_c, n_lat_c=n_lat_c, bsz=bsz, n_scaled=n_scaled,
                          n_normed=n_normed, scale=scale, head_dim=head_dim),
        grid=(n_rb, n_cols // CONV_COLS),
        in_specs=[pl.BlockSpec((SCAN_CHUNK, CONV_COLS), lambda r, c: (r, c + col0)),
                  pl.BlockSpec((8, CONV_COLS), lambda r, c: (jnp.maximum(r * sub - 1, 0), c + col0)),
                  pl.BlockSpec((8, CONV_COLS), lambda r, c: (jnp.minimum(r * sub + sub, n_halo - 1), c + col0)),
                  pl.BlockSpec((CONV_W, CONV_COLS), lambda r, c: (0, c)),
                  pl.BlockSpec((1, CONV_COLS), lambda r, c: (0, c))],
        out_specs=pl.BlockSpec((SCAN_CHUNK, CONV_COLS), lambda r, c: (r, c)),
        out_shape=jax.ShapeDtypeStruct((rows, n_cols), F32),
        compiler_params=_params("parallel", "parallel"),
        name="conv_silu",
    )(u, u, u, w, b)


GDN_SUB = 64


GDN_BASE = 8
GDN_HEADS_PER_STEP = 4


def _split3(x):
    x1 = x.astype(BF16)
    r1 = x - x1.astype(F32)
    x2 = r1.astype(BF16)
    return x1, x2, (r1 - x2.astype(F32)).astype(BF16)


def _sel_rows(m, x):
    mb = m.astype(BF16)
    return sum(jnp.dot(mb, p, preferred_element_type=F32) for p in _split3(x))


def _sel_cols(x, m):
    mb = m.astype(BF16)
    return sum(jnp.dot(p, mb, preferred_element_type=F32) for p in _split3(x))


def _sel_rows_t(x, m):
    mb = m.astype(BF16)
    return sum(lax.dot_general(p, mb, (((0,), (1,)), ((), ())), preferred_element_type=F32) for p in _split3(x))


def _mm_bf16(a, b):
    return jnp.dot(a.astype(BF16), b.astype(BF16), preferred_element_type=F32)


def _unit_tri_inverse_minus_eye(mats, ti, si):
    def same(n):
        s = int(math.log2(n))
        return lax.shift_right_logical(ti, s) == lax.shift_right_logical(si, s)

    base = same(GDN_BASE)
    ps = [jnp.where(base, a, 0.0) for a in mats]
    ns = [-p for p in ps]
    for _ in range(int(math.log2(GDN_BASE)) - 1):
        ps = [_mm_bf16(p, p) for p in ps]
        ns = [n + p + _mm_bf16(n, p) for n, p in zip(ns, ps)]
    b = GDN_BASE
    while b < GDN_SUB:
        join = same(2 * b) & jnp.logical_not(same(b))
        cs = [jnp.where(join, a, 0.0) for a in mats]
        ms = [c + _mm_bf16(c, n) for c, n in zip(cs, ns)]
        ns = [n - (m + _mm_bf16(n, m)) for n, m in zip(ns, ms)]
        b *= 2
    return ns


def _softplus(x):
    return jnp.maximum(x, 0.0) + jnp.log(1.0 + jnp.exp(-jnp.abs(x)))


def _gdn_kernel(*refs, reverse, n_kh, n_r, hd):
    if reverse:
        q_ref, k_ref, v_ref, tail_ref, prow_ref, arow_ref, o_ref, s_ref = refs
    else:
        q_ref, k_ref, v_ref, tail_ref, prow_ref, arow_ref, yb_ref, z_ref, ng_ref, o_ref, s_ref = refs
    g, c = pl.program_id(1), pl.program_id(2)
    qn = SCAN_CHUNK
    n_sub = qn // GDN_SUB
    n_gate = tail_ref.shape[1] // 2

    @pl.when(c == 0)
    def _():
        s_ref[...] = jnp.zeros_like(s_ref)

    tail = tail_ref[...]
    lane = lax.broadcasted_iota(jnp.int32, (1, 2 * n_gate), 1)
    gates = jnp.where(lane < n_gate, jax.nn.sigmoid(tail), -jnp.exp(arow_ref[...]) * _softplus(tail + prow_ref[...]))
    nh = n_kh * n_r
    li = lax.broadcasted_iota(jnp.int32, (2 * n_gate, 2 * nh), 0)
    ji = lax.broadcasted_iota(jnp.int32, (2 * n_gate, 2 * nh), 1)
    col = jnp.where(ji < nh, ji, n_gate + ji - nh) + (n_gate // 2 if reverse else 0) + g * nh
    gsel = _sel_cols(gates, (li == col).astype(F32))
    lw = lax.broadcasted_iota(jnp.int32, (2 * n_gate, 2 * nh * hd), 0)
    jw = lax.shift_right_logical(lax.broadcasted_iota(jnp.int32, (2 * n_gate, 2 * nh * hd), 1), int(math.log2(hd)))
    colw = jnp.where(jw < nh, jw, n_gate + jw - nh) + (n_gate // 2 if reverse else 0) + g * nh
    gwide = _sel_cols(gates, (lw == colw).astype(F32))

    ti = lax.broadcasted_iota(jnp.int32, (qn, 1), 0)
    si = lax.broadcasted_iota(jnp.int32, (1, qn), 1)
    shift = int(math.log2(GDN_SUB))
    same = lax.shift_right_logical(ti, shift) == lax.shift_right_logical(si, shift)
    if reverse:
        incl, strict = same & (si >= ti), same & (si > ti)
    else:
        incl, strict = same & (si <= ti), same & (si < ti)
    cs = incl.astype(F32)
    cum_w = _sel_rows(cs, gwide[:, nh * hd:])
    cum_t = _sel_rows_t(gsel, cs)
    reps = qn // hd

    order = range(n_sub - 1, -1, -1) if reverse else range(n_sub)
    outs = []
    shared = []
    for kh in range(n_kh):
        k = k_ref[:, kh * hd:(kh + 1) * hd]
        qb, kb = q_ref[:, kh * hd:(kh + 1) * hd].astype(BF16), k.astype(BF16)
        shared.append((k, qb, _dot_nt(kb, kb), _dot_nt(qb, kb)))
    heads = range(nh)
    beta = [gwide[:, j * hd:(j + 1) * hd] for j in heads]
    cum_c = [cum_w[:, j * hd:(j + 1) * hd] for j in heads]
    decay = [jnp.exp(jnp.where(incl, jnp.concatenate([cum_c[j]] * reps, axis=1) - cum_t[nh + j:nh + j + 1, :],
                               -jnp.inf)) for j in heads]
    amat = [jnp.where(strict, jnp.concatenate([beta[j]] * reps, axis=1) * shared[j // n_r][2] * decay[j], 0.0)
            for j in heads]
    inv_off = _unit_tri_inverse_minus_eye(amat, ti, si)
    rhs = [jnp.concatenate([v_ref[:, j * hd:(j + 1) * hd] * beta[j],
                            shared[j // n_r][0] * (beta[j] * jnp.exp(cum_c[j]))], axis=1) for j in heads]
    sol = [rhs[j] + _mm_bf16(inv_off[j], rhs[j]) for j in heads]
    state = [s_ref[j] for j in heads]
    v_new = [[None] * n_sub for _ in heads]
    inter = [[None] * n_sub for _ in heads]
    for i in order:
        sl = slice(i * GDN_SUB, (i + 1) * GDN_SUB)
        end = i * GDN_SUB if reverse else (i + 1) * GDN_SUB - 1
        sb = [state[j].astype(BF16) for j in heads]
        for j in heads:
            v_new[j][i] = sol[j][sl, :hd] - jnp.dot(sol[j][sl, hd:].astype(BF16), sb[j], preferred_element_type=F32)
            inter[j][i] = jnp.dot(shared[j // n_r][1][sl], sb[j], preferred_element_type=F32)
        for j in heads:
            cum_end = cum_c[j][end:end + 1, :]
            k_end = (shared[j // n_r][0][sl] * jnp.exp(cum_end - cum_c[j][sl])).astype(BF16)
            state[j] = jnp.exp(cum_end) * state[j] + _dot_t(k_end, v_new[j][i].astype(BF16))
    outs = []
    for j in heads:
        s_ref[j] = state[j]
        attn = (shared[j // n_r][3] * decay[j]).astype(BF16)
        y = jnp.dot(attn, jnp.concatenate(v_new[j], axis=0).astype(BF16), preferred_element_type=F32)
        y = y + jnp.concatenate(inter[j], axis=0) * jnp.exp(cum_c[j])
        if not reverse:
            y = y + yb_ref[:, j * hd:(j + 1) * hd]
            y = y * lax.rsqrt(jnp.mean(y * y, axis=-1, keepdims=True) + NORM_EPS) * ng_ref[...]
            z = z_ref[:, j * hd:(j + 1) * hd]
            y = y * (z * jax.nn.sigmoid(z))
        outs.append(y)
    o_ref[...] = jnp.concatenate(outs, axis=1).astype(o_ref.dtype)


def _gdn_mixer(u, tail, conv_w, dt_bias, a_log, norm_g, bsz, lc, seq, d_model):
    n_ctx_c, n_lat_c = _scan_geometry(bsz, lc, seq)
    n_steps = n_ctx_c + n_lat_c
    hd = GDN_HEAD_DIM
    k_heads = d_model // hd
    n_r = 2
    rows = u.shape[0]
    conv_ch = 4 * d_model
    nq = d_model // CONV_COLS
    qkv = _conv_silu(u, conv_w, jnp.zeros((1, conv_ch), F32), conv_ch, bsz, lc, seq, n_scaled=nq, n_normed=2 * nq,
                     scale=hd ** -0.5, head_dim=hd)
    n_gate = tail.shape[1] // 2
    prow = jnp.concatenate([jnp.zeros((1, n_gate), F32), dt_bias.reshape(1, n_gate)], axis=1)
    arow = jnp.concatenate([jnp.zeros((1, n_gate), F32), a_log.reshape(1, n_gate)], axis=1)
    y_b = None
    for reverse in (True, False):
        rb = functools.partial(_row_block, bsz=bsz, n_ctx_c=n_ctx_c, n_lat_c=n_lat_c, reverse=reverse)
        const = lambda b, g, c: (0, 0)
        qk_w, v_w = GDN_HEADS_PER_STEP * hd, GDN_HEADS_PER_STEP * n_r * hd
        n_groups = k_heads // GDN_HEADS_PER_STEP
        in_specs = [pl.BlockSpec((SCAN_CHUNK, qk_w), lambda b, g, c, rb=rb: (rb(b, c), g)),
                    pl.BlockSpec((SCAN_CHUNK, qk_w), lambda b, g, c, rb=rb: (rb(b, c), n_groups + g)),
                    pl.BlockSpec((SCAN_CHUNK, v_w), lambda b, g, c, rb=rb: (rb(b, c), n_groups + g)),
                    pl.BlockSpec((SCAN_CHUNK, 2 * n_gate), lambda b, g, c, rb=rb: (rb(b, c), 0)),
                    pl.BlockSpec((1, 2 * n_gate), const),
                    pl.BlockSpec((1, 2 * n_gate), const)]
        args = [qkv, qkv, qkv, tail, prow, arow]
        if not reverse:
            in_specs += [pl.BlockSpec((SCAN_CHUNK, v_w), lambda b, g, c, rb=rb: (rb(b, c), g)),
                         pl.BlockSpec((SCAN_CHUNK, v_w), lambda b, g, c, rb=rb: (rb(b, c), 2 * n_groups + g)),
                         pl.BlockSpec((1, hd), const)]
            args += [y_b, u, norm_g.reshape(1, hd)]
        out = pl.pallas_call(
            functools.partial(_gdn_kernel, reverse=reverse, n_kh=GDN_HEADS_PER_STEP, n_r=n_r, hd=hd),
            grid=(bsz, n_groups, n_steps),
            in_specs=in_specs,
            out_specs=pl.BlockSpec((SCAN_CHUNK, v_w), lambda b, g, c, rb=rb: (rb(b, c), g)),
            out_shape=jax.ShapeDtypeStruct((rows, k_heads * n_r * hd), F32 if reverse else BF16),
            scratch_shapes=[pltpu.VMEM((GDN_HEADS_PER_STEP * n_r, hd, hd), F32)],
            compiler_params=_params("parallel", "parallel", "arbitrary"),
            name="gdn_bwd" if reverse else "gdn_fwd",
        )(*args)
        if reverse:
            y_b = out
    return out


def _ssd_kernel(*refs, reverse, n_r, hd):
    if reverse:
        x_ref, b_ref, c_ref, tail_ref, dtb_ref, alog_ref, o_ref, s_ref = refs
    else:
        x_ref, b_ref, c_ref, tail_ref, dtb_ref, alog_ref, yb_ref, z_ref, d_ref, ng_ref, o_ref, s_ref = refs
    g, c = pl.program_id(1), pl.program_id(2)
    qn = SCAN_CHUNK
    n_lane = tail_ref.shape[1]
    width = n_r * hd

    @pl.when(c == 0)
    def _():
        s_ref[...] = jnp.zeros_like(s_ref)

    dt_all = _softplus(tail_ref[...] + dtb_ref[...])
    la_all = -jnp.exp(alog_ref[...]) * dt_all
    li = lax.broadcasted_iota(jnp.int32, (n_lane, n_r), 0)
    ji = lax.broadcasted_iota(jnp.int32, (n_lane, n_r), 1)
    sel = (li == ji + (n_lane // 2 if reverse else 0) + g * n_r).astype(F32)
    dt, la = _sel_cols(dt_all, sel), _sel_cols(la_all, sel)

    ti = lax.broadcasted_iota(jnp.int32, (qn, 1), 0)
    si = lax.broadcasted_iota(jnp.int32, (1, qn), 1)
    before = (si >= ti) if reverse else (si <= ti)
    cs = before.astype(F32)
    cum = _sel_rows(cs, la)
    cum_t = _sel_rows_t(la, cs)
    ei = lax.broadcasted_iota(jnp.int32, (n_r, width), 0)
    el = lax.broadcasted_iota(jnp.int32, (n_r, width), 1)
    expand = (lax.shift_right_logical(el, int(math.log2(hd))) == ei).astype(F32)
    dt_x, cum_x = _sel_cols(dt, expand), _sel_cols(cum, expand)

    xs = x_ref[...]
    v = xs * dt_x
    bm, cm = b_ref[...].astype(BF16), c_ref[...].astype(BF16)
    scores = _dot_nt(cm, bm)
    lane = lax.broadcasted_iota(jnp.int32, (1, 2 * hd), 1)
    tiles = []
    for p in range(n_r // 2):
        vt = v[:, 2 * p * hd:2 * (p + 1) * hd]
        acc = None
        for h in (2 * p, 2 * p + 1):
            decay = jnp.exp(jnp.where(before, cum[:, h:h + 1] - cum_t[h:h + 1, :], -jnp.inf))
            vh = jnp.where((lane >= hd) if h % 2 else (lane < hd), vt, 0.0).astype(BF16)
            part = jnp.dot((scores * decay).astype(BF16), vh, preferred_element_type=F32)
            acc = part if acc is None else acc + part
        tiles.append(acc)
    state = s_ref[...]
    y = jnp.concatenate(tiles, axis=1) + jnp.dot(cm, state.astype(BF16), preferred_element_type=F32) * jnp.exp(cum_x)
    end = 0 if reverse else qn - 1
    cum_end = cum_x[end:end + 1]
    s_ref[...] = state * jnp.exp(cum_end) + _dot_t(bm, (v * jnp.exp(cum_end - cum_x)).astype(BF16))
    if reverse:
        o_ref[...] = y
    else:
        y = y + yb_ref[...] + d_ref[...] * xs
        z = z_ref[...]
        y = y * (z * jax.nn.sigmoid(z))
        y = y * lax.rsqrt(jnp.mean(y * y, axis=-1, keepdims=True) + NORM_EPS) * ng_ref[...]
        o_ref[...] = y.astype(o_ref.dtype)


def _ssd_mixer(u, tail, conv_w, conv_b, dt_bias, a_log, d_skip, norm_g, bsz, lc, seq):
    n_ctx_c, n_lat_c = _scan_geometry(bsz, lc, seq)
    n_steps = n_ctx_c + n_lat_c
    d_inner = norm_g.shape[0]
    hd, st = SSD_HEAD_DIM, SSD_STATE
    heads = d_inner // hd
    n_r = heads // SSD_GROUPS
    width = n_r * hd
    rows = u.shape[0]
    conv_ch = d_inner + 2 * SSD_GROUPS * st
    xbc = _conv_silu(u, conv_w, conv_b.reshape(1, conv_ch), conv_ch, bsz, lc, seq, col0=d_inner // CONV_COLS)
    d_x = jnp.repeat(d_skip, hd).reshape(1, d_inner)
    y_b = None
    for reverse in (True, False):
        rb = functools.partial(_row_block, bsz=bsz, n_ctx_c=n_ctx_c, n_lat_c=n_lat_c, reverse=reverse)
        const = lambda b, g, c: (0, 0)
        grp = lambda b, g, c: (0, g)
        wide = lambda b, g, c, rb=rb: (rb(b, c), g)
        in_specs = [pl.BlockSpec((SCAN_CHUNK, width), wide),
                    pl.BlockSpec((SCAN_CHUNK, st), lambda b, g, c, rb=rb: (rb(b, c), d_inner // st + g)),
                    pl.BlockSpec((SCAN_CHUNK, st), lambda b, g, c, rb=rb: (rb(b, c), d_inner // st + SSD_GROUPS + g)),
                    pl.BlockSpec((SCAN_CHUNK, 2 * heads), lambda b, g, c, rb=rb: (rb(b, c), 0)),
                    pl.BlockSpec((1, 2 * heads), const),
                    pl.BlockSpec((1, 2 * heads), const)]
        args = [xbc, xbc, xbc, tail, dt_bias.reshape(1, 2 * heads), a_log.reshape(1, 2 * heads)]
        if not reverse:
            in_specs += [pl.BlockSpec((SCAN_CHUNK, width), wide), pl.BlockSpec((SCAN_CHUNK, width), wide),
                         pl.BlockSpec((1, width), grp), pl.BlockSpec((1, width), grp)]
            args += [y_b, u, d_x, norm_g.reshape(1, d_inner)]
        out = pl.pallas_call(
            functools.partial(_ssd_kernel, reverse=reverse, n_r=n_r, hd=hd),
            grid=(bsz, SSD_GROUPS, n_steps),
            in_specs=in_specs,
            out_specs=pl.BlockSpec((SCAN_CHUNK, width), wide),
            out_shape=jax.ShapeDtypeStruct((rows, d_inner), F32 if reverse else BF16),
            scratch_shapes=[pltpu.VMEM((st, width), F32)],
            compiler_params=_params("parallel", "parallel", "arbitrary"),
            name="ssd_bwd" if reverse else "ssd_fwd",
        )(*args)
        if reverse:
            y_b = out
    return out


HG_SUB = 64
HG_BLK = 16


def _log1p(x):
    return jnp.log(1.0 + x)


def _hgrn_kernel(*refs, reverse):
    if reverse:
        q_ref, f_ref, i_ref, lb_ref, o_ref, st_ref = refs
    else:
        q_ref, f_ref, i_ref, lb_ref, yb_ref, g_ref, ng_ref, o_ref, st_ref = refs
    c = pl.program_id(2)
    qn = SCAN_CHUNK
    hd = q_ref.shape[1]
    n_sub, n_blk = qn // HG_SUB, HG_SUB // HG_BLK

    @pl.when(c == 0)
    def _():
        st_ref[...] = jnp.zeros_like(st_ref)

    q, f, v, lb = q_ref[...], f_ref[...], i_ref[...], lb_ref[...]
    log_sig = jnp.minimum(f, 0.0) - _log1p(jnp.exp(-jnp.abs(f)))
    ga, gb = jnp.log(lb), _log1p(-lb) + log_sig
    log_f = jnp.maximum(ga, gb) + _log1p(jnp.exp(-jnp.abs(ga - gb)))
    k = (1.0 - lb) * jax.nn.sigmoid(-f)

    ti = lax.broadcasted_iota(jnp.int32, (qn, 1), 0)
    si = lax.broadcasted_iota(jnp.int32, (1, qn), 1)
    before = (si >= ti) if reverse else (si <= ti)
    same = lambda n: lax.shift_right_logical(ti, int(math.log2(n))) == lax.shift_right_logical(si, int(math.log2(n)))
    cum_sub = _sel_rows((same(HG_SUB) & before).astype(F32), log_f)
    cum_blk = _sel_rows((same(HG_BLK) & before).astype(F32), log_f)
    q_blk = q * jnp.exp(cum_blk)
    q_sub = (q * jnp.exp(cum_sub)).astype(BF16)

    ones = jnp.ones((hd, hd), BF16)
    lane = lax.broadcasted_iota(jnp.int32, (1, hd), 1)
    row_sub = lax.broadcasted_iota(jnp.int32, (HG_SUB, 1), 0)
    row_blk = lax.broadcasted_iota(jnp.int32, (HG_BLK, 1), 0)
    zeros_sub = jnp.zeros((hd - HG_SUB, hd), F32)
    y_intra, kvt, chunk_dec = [], [], []
    for i in range(n_sub):
        r0 = i * HG_SUB
        sl = slice(r0, r0 + HG_SUB)
        cs, ks, vs = cum_sub[sl], k[sl], v[sl]
        end = r0 if reverse else r0 + HG_SUB - 1
        cum_end = cum_sub[end:end + 1]
        kvt.append(_dot_t(vs.astype(BF16), (ks * jnp.exp(cum_end - cs)).astype(BF16)))
        chunk_dec.append(jnp.exp(cum_end))
        a_rows = []
        for a in range(n_blk):
            b0 = r0 + a * HG_BLK
            bl = slice(b0, b0 + HG_BLK)
            qb, kb, cb = q[bl], k[bl], cum_blk[bl]
            tiles = []
            for s in range(HG_BLK):
                ok = (row_blk <= s) if reverse else (row_blk >= s)
                e = jnp.exp(jnp.where(ok, cb - cb[s:s + 1], -jnp.inf))
                tiles.append(qb * e * kb[s:s + 1])
            sums = jnp.dot(jnp.concatenate(tiles, axis=0).astype(BF16), ones, preferred_element_type=F32)
            acc = jnp.zeros((HG_BLK, hd), F32)
            for s in range(HG_BLK):
                acc = acc + jnp.where(lane == a * HG_BLK + s, sums[s * HG_BLK:(s + 1) * HG_BLK], 0.0)
            has_earlier = (a < n_blk - 1) if reverse else (a > 0)
            if has_earlier:
                ref_row = b0 + HG_BLK if reverse else b0 - 1
                earlier = (row_sub >= (a + 1) * HG_BLK) if reverse else (row_sub < a * HG_BLK)
                kt = ks * jnp.exp(jnp.where(earlier, cum_sub[ref_row:ref_row + 1] - cs, -jnp.inf))
                kt = jnp.concatenate([kt, zeros_sub], axis=0).astype(BF16)
                acc = acc + _dot_nt(q_blk[bl].astype(BF16), kt)
            a_rows.append(acc)
        attn = jnp.concatenate(a_rows, axis=0).astype(BF16)
        v_pad = jnp.concatenate([vs, zeros_sub], axis=0).astype(BF16)
        y_intra.append(jnp.dot(attn, v_pad, preferred_element_type=F32))

    state = st_ref[...]
    ys = [None] * n_sub
    for i in (range(n_sub - 1, -1, -1) if reverse else range(n_sub)):
        ys[i] = y_intra[i] + _dot_nt(q_sub[i * HG_SUB:(i + 1) * HG_SUB], state.astype(BF16))
        state = state * chunk_dec[i] + kvt[i]
    st_ref[...] = state
    y = jnp.concatenate(ys, axis=0)
    if reverse:
        o_ref[...] = y
    else:
        y = y + yb_ref[...]
        y = y * lax.rsqrt(jnp.mean(y * y, axis=-1, keepdims=True) + NORM_EPS) * ng_ref[...]
        g = g_ref[...]
        o_ref[...] = (y * (g * jax.nn.sigmoid(g))).astype(o_ref.dtype)


def _hgrn_mixer(u, lb, norm_g, bsz, lc, seq):
    n_ctx_c, n_lat_c = _scan_geometry(bsz, lc, seq)
    n_steps = n_ctx_c + n_lat_c
    hd = HGRN_EXPAND
    d_model = lb.shape[0]
    heads = d_model // hd
    rows = u.shape[0]
    lb2, ng2 = lb.reshape(1, d_model), norm_g.reshape(1, d_model)
    y_b = None
    for reverse in (True, False):
        rb = functools.partial(_row_block, bsz=bsz, n_ctx_c=n_ctx_c, n_lat_c=n_lat_c, reverse=reverse)
        col = lambda seg: (lambda b, h, c, rb=rb: (rb(b, c), seg * heads + h))
        par = lambda b, h, c: (0, h)
        in_specs = [pl.BlockSpec((SCAN_CHUNK, hd), col(0)),
                    pl.BlockSpec((SCAN_CHUNK, hd), col(2 if reverse else 1)),
                    pl.BlockSpec((SCAN_CHUNK, hd), col(3)),
                    pl.BlockSpec((1, hd), par)]
        args = [u, u, u, lb2]
        if not reverse:
            in_specs += [pl.BlockSpec((SCAN_CHUNK, hd), col(0)), pl.BlockSpec((SCAN_CHUNK, hd), col(4)),
                         pl.BlockSpec((1, hd), par)]
            args += [y_b, u, ng2]
        out = pl.pallas_call(
            functools.partial(_hgrn_kernel, reverse=reverse),
            grid=(bsz, heads, n_steps),
            in_specs=in_specs,
            out_specs=pl.BlockSpec((SCAN_CHUNK, hd), col(0)),
            out_shape=jax.ShapeDtypeStruct((rows, d_model), F32 if reverse else BF16),
            scratch_shapes=[pltpu.VMEM((hd, hd), F32)],
            compiler_params=_params("parallel", "parallel", "arbitrary"),
            name="hgrn_bwd" if reverse else "hgrn_fwd",
        )(*args)
        if reverse:
            y_b = out
    return out


def _rope_tables(rows, half):
    pos = np.arange(rows * GRID_W)
    inv_freq = np.float32(ROPE_BASE) ** (-(np.arange(0, half, 2, dtype=np.float32) / np.float32(half)))
    out = []
    for p in ((pos // GRID_W).astype(np.float32), (pos % GRID_W).astype(np.float32)):
        ang = (p[:, None] * inv_freq.astype(np.float32)).astype(np.float32).astype(np.float64)
        out += [np.cos(ang).astype(np.float32), np.sin(ang).astype(np.float32)]
    return out


def _lower_bound(lb_logits, layer):
    p = jax.nn.softmax(lb_logits.astype(F32), axis=0)
    return jnp.cumsum(p, axis=0)[layer] - p[0]


def kernel(x, c, ctx, c_ctx, ada_w, ada_b, norm_g, mlp_w1, mlp_w2, final_g, ssd_w_in, ssd_conv_w, ssd_conv_b,
           ssd_dt_bias, ssd_a_log, ssd_d, ssd_norm_g, ssd_w_out, ret_w_in, ret_log_decay, ret_w_out, hgrn_w_in,
           hgrn_lb_logits, hgrn_norm_g, hgrn_w_out, gdn_w_in, gdn_conv_w, gdn_dt_bias, gdn_a_log, gdn_norm_g,
           gdn_w_out):
    bsz, seq, d = x.shape
    lc = ctx.shape[1]
    depth = ada_w.shape[0]
    n_ctx = bsz * lc
    assert n_ctx % ROW_TILE == 0 and seq % ROW_TILE == 0 and bsz + 1 <= 8

    cond_pad = jnp.concatenate([c, c_ctx[None], jnp.zeros((8 - bsz - 1, d), F32)], axis=0)
    mod = _ada_mod(cond_pad, ada_w, ada_b)

    tile_row = [bsz] * (n_ctx // ROW_TILE) + [b for b in range(bsz) for _ in range(seq // ROW_TILE)]
    tile_row = jnp.asarray(tile_row, jnp.int32)
    n_ctx_tiles = n_ctx // ROW_TILE

    xr = jnp.concatenate([ctx.reshape(n_ctx, d), x.reshape(bsz * seq, d)], axis=0)

    for i in range(depth):
        mixer, occ = i % 4, i // 4
        keep_ctx = i < depth - 1
        mod_t = mod[i][tile_row][:, None, :]
        g0, g1 = norm_g[i, 0][None], norm_g[i, 1][None]
        if mixer == 0:
            w_in, w_out = ssd_w_in[occ], ssd_w_out[occ]
        elif mixer == 1:
            w_in, w_out = ret_w_in[occ], ret_w_out[occ]
        elif mixer == 2:
            w_in, w_out = hgrn_w_in[occ], hgrn_w_out[occ]
        else:
            w_in, w_out = gdn_w_in[occ], gdn_w_out[occ]
        n_in = w_in.shape[1]
        n_main = (n_in // 1024) * 1024 if n_in % 1024 else n_in
        w_in = w_in.astype(BF16)
        u = _ln_mm(xr, g0, mod_t, 0, 1, w_in, 0, n_main)
        tail = _ln_mm(xr, g0, mod_t, 0, 1, w_in, n_main, n_in - n_main) if n_main != n_in else None
        if mixer == 0:
            yr = _ssd_mixer(u, tail, ssd_conv_w[occ], ssd_conv_b[occ], ssd_dt_bias[occ], ssd_a_log[occ], ssd_d[occ],
                            ssd_norm_g[occ], bsz, lc, seq)
        elif mixer == 1:
            yr = _ret_mixer(u, ret_log_decay[occ], bsz, lc, seq, d)
        elif mixer == 2:
            yr = _hgrn_mixer(u, _lower_bound(hgrn_lb_logits, i), hgrn_norm_g[occ], bsz, lc, seq)
        else:
            yr = _gdn_mixer(u, tail, gdn_conv_w[occ], gdn_dt_bias[occ], gdn_a_log[occ], gdn_norm_g[occ], bsz, lc,
                            seq, d)
        first_tile = 0 if keep_ctx else n_ctx_tiles
        xr = _out_proj(yr, w_out.astype(BF16), xr, mod_t, 2, first_tile)
        mod_t = mod_t[first_tile:]
        xr = _mlp(xr, g1, mod_t, mlp_w1[i].astype(BF16), mlp_w2[i].astype(BF16), final_g[None], final=not keep_ctx)
    return xr.reshape(bsz, seq, d)
```

```python
import functools
import math

import jax
import jax.numpy as jnp
import numpy as np
from jax import lax
from jax.experimental import pallas as pl
from jax.experimental.pallas import tpu as pltpu

F32 = jnp.float32
BF16 = jnp.bfloat16

GRID_W = 64
CHUNK = 64
CONV_W = 3
NORM_EPS = 1e-6
ROPE_BASE = 10000.0
SSD_HEAD_DIM = 64
SSD_GROUPS = 8
SSD_STATE = 128
RET_HEADS = 8
HGRN_EXPAND = 128
GDN_HEAD_DIM = 128

ROW_TILE = 512
IN_ROW_TILE = 1024
MLP_FF_TILE = 1024
VMEM_LIMIT = 56 * 1024 * 1024


def _params(*sem):
    return pltpu.CompilerParams(dimension_semantics=sem, vmem_limit_bytes=VMEM_LIMIT)


def _col_tile(n, cap=1536):
    best = 128
    for t in range(128, cap + 1, 128):
        if n % t == 0:
            best = t
    return best


def _ada_kernel(c_ref, w_ref, b_ref, o_ref):
    c = c_ref[...]
    c = (c * jax.nn.sigmoid(c)).astype(BF16)
    o_ref[0] = jnp.dot(c, w_ref[0].astype(BF16), preferred_element_type=F32) + b_ref[0]


def _ada_mod(cond_pad, ada_w, ada_b):
    depth, d, n = ada_w.shape
    tn = 1024
    return pl.pallas_call(
        _ada_kernel,
        grid=(depth, n // tn),
        in_specs=[pl.BlockSpec((8, d), lambda l, j: (0, 0)),
                  pl.BlockSpec((1, d, tn), lambda l, j: (l, 0, j)),
                  pl.BlockSpec((1, 1, tn), lambda l, j: (l, 0, j))],
        out_specs=pl.BlockSpec((1, 8, tn), lambda l, j: (l, 0, j)),
        out_shape=jax.ShapeDtypeStruct((depth, 8, n), F32),
        compiler_params=_params("parallel", "parallel"),
        name="ada_mod",
    )(cond_pad, ada_w, ada_b.reshape(depth, 1, n))


def _adaln_rows(x, g, sh, sc):
    y = x * lax.rsqrt(jnp.mean(x * x, axis=-1, keepdims=True) + NORM_EPS)
    return y * g * (1.0 + sc) + sh


def _ln_mm_kernel(x_ref, g_ref, sh_ref, sc_ref, w_ref, o_ref, h_ref):
    @pl.when(pl.program_id(1) == 0)
    def _():
        h_ref[...] = _adaln_rows(x_ref[...], g_ref[...], sh_ref[0], sc_ref[0]).astype(BF16)

    o_ref[...] = jnp.dot(h_ref[...], w_ref[...], preferred_element_type=F32).astype(o_ref.dtype)


def _ln_mm(x, g, mod_t, sh_col, sc_col, w, col_start, n, out_dtype=F32):
    m, d = x.shape
    tm, tn = IN_ROW_TILE, _col_tile(n)
    assert col_start % tn == 0
    col0 = col_start // tn
    return pl.pallas_call(
        _ln_mm_kernel,
        grid=(m // tm, n // tn),
        in_specs=[pl.BlockSpec((tm, d), lambda i, j: (i, 0)),
                  pl.BlockSpec((1, d), lambda i, j: (0, 0)),
                  pl.BlockSpec((1, 1, d), lambda i, j: (i, 0, sh_col)),
                  pl.BlockSpec((1, 1, d), lambda i, j: (i, 0, sc_col)),
                  pl.BlockSpec((d, tn), lambda i, j: (0, col0 + j))],
        out_specs=pl.BlockSpec((tm, tn), lambda i, j: (i, j)),
        out_shape=jax.ShapeDtypeStruct((m, n), out_dtype),
        scratch_shapes=[pltpu.VMEM((tm, d), BF16)],
        compiler_params=_params("parallel", "arbitrary"),
        name="adaln_in_proj",
    )(x, g, mod_t, mod_t, w)


def _out_kernel(y_ref, w_ref, x_ref, gate_ref, o_ref):
    o_ref[...] = x_ref[...] + gate_ref[0] * jnp.dot(y_ref[...], w_ref[...], preferred_element_type=F32)


def _out_proj(y, w, x, mod_t, gate_col, first_tile=0):
    k = y.shape[1]
    d = w.shape[1]
    tm, tn = ROW_TILE, 512
    m = y.shape[0] - first_tile * tm
    return pl.pallas_call(
        _out_kernel,
        grid=(m // tm, d // tn),
        in_specs=[pl.BlockSpec((tm, k), lambda i, j: (i + first_tile, 0)),
                  pl.BlockSpec((k, tn), lambda i, j: (0, j)),
                  pl.BlockSpec((tm, tn), lambda i, j: (i + first_tile, j)),
                  pl.BlockSpec((1, 1, tn), lambda i, j: (i + first_tile, 0, gate_col * (d // tn) + j))],
        out_specs=pl.BlockSpec((tm, tn), lambda i, j: (i, j)),
        out_shape=jax.ShapeDtypeStruct((m, d), F32),
        compiler_params=_params("parallel", "arbitrary"),
        name="out_proj",
    )(y, w, x, mod_t)


def _mlp_kernel(x_ref, g_ref, sh_ref, sc_ref, gate_ref, w1_ref, w2_ref, fg_ref, o_ref, h_ref, acc_ref, *, final):
    f = pl.program_id(1)

    @pl.when(f == 0)
    def _():
        h_ref[...] = _adaln_rows(x_ref[...], g_ref[...], sh_ref[0], sc_ref[0]).astype(BF16)
        acc_ref[...] = jnp.zeros_like(acc_ref)

    a = jnp.dot(h_ref[...], w1_ref[...], preferred_element_type=F32)
    a = jnp.square(jnp.maximum(a, 0.0)).astype(BF16)
    acc_ref[...] += jnp.dot(a, w2_ref[...], preferred_element_type=F32)

    @pl.when(f == pl.num_programs(1) - 1)
    def _():
        out = x_ref[...] + gate_ref[0] * acc_ref[...]
        if final:
            out = out * lax.rsqrt(jnp.mean(out * out, axis=-1, keepdims=True) + NORM_EPS) * fg_ref[...]
        o_ref[...] = out


def _mlp(x, g, mod_t, w1, w2, final_g, final):
    m, d = x.shape
    ff = w1.shape[1]
    tm, tf = ROW_TILE, MLP_FF_TILE
    return pl.pallas_call(
        functools.partial(_mlp_kernel, final=final),
        grid=(m // tm, ff // tf),
        in_specs=[pl.BlockSpec((tm, d), lambda i, f: (i, 0)),
                  pl.BlockSpec((1, d), lambda i, f: (0, 0)),
                  pl.BlockSpec((1, 1, d), lambda i, f: (i, 0, 3)),
                  pl.BlockSpec((1, 1, d), lambda i, f: (i, 0, 4)),
                  pl.BlockSpec((1, 1, d), lambda i, f: (i, 0, 5)),
                  pl.BlockSpec((d, tf), lambda i, f: (0, f)),
                  pl.BlockSpec((tf, d), lambda i, f: (f, 0)),
                  pl.BlockSpec((1, d), lambda i, f: (0, 0))],
        out_specs=pl.BlockSpec((tm, d), lambda i, f: (i, 0)),
        out_shape=jax.ShapeDtypeStruct((m, d), F32),
        scratch_shapes=[pltpu.VMEM((tm, d), BF16), pltpu.VMEM((tm, d), F32)],
        compiler_params=_params("parallel", "arbitrary"),
        name="adaln_mlp",
    )(x, g, mod_t, mod_t, mod_t, w1, w2, final_g)


SCAN_CHUNK = 256


def _scan_geometry(bsz, lc, seq):
    assert lc % SCAN_CHUNK == 0 and seq % SCAN_CHUNK == 0
    return lc // SCAN_CHUNK, seq // SCAN_CHUNK


def _seg_chunk(c, n_ctx_c, n_lat_c, reverse):
    if reverse:
        return c >= n_ctx_c, jnp.where(c < n_ctx_c, n_ctx_c - 1 - c, n_lat_c - 1 - (c - n_ctx_c))
    return c >= n_ctx_c, jnp.where(c < n_ctx_c, c, c - n_ctx_c)


def _row_block(b, c, bsz, n_ctx_c, n_lat_c, reverse):
    is_lat, j = _seg_chunk(c, n_ctx_c, n_lat_c, reverse)
    return jnp.where(is_lat, bsz * n_ctx_c + b * n_lat_c + j, b * n_ctx_c + j)


def _time_iotas(q):
    t = lax.broadcasted_iota(jnp.int32, (q, 1), 0).astype(F32)
    s = lax.broadcasted_iota(jnp.int32, (1, q), 1).astype(F32)
    return t, s


def _dot_t(a, b):
    return lax.dot_general(a, b, (((0,), (0,)), ((), ())), preferred_element_type=F32)


def _dot_nt(a, b):
    return lax.dot_general(a, b, (((1,), (1,)), ((), ())), preferred_element_type=F32)


def _ret_kernel(*refs, reverse, n_ctx_c, n_heads, k_scale):
    if reverse:
        ld_ref, q_ref, k_ref, v_ref, cos_ref, sin_ref, o_ref, s_ref = refs
    else:
        ld_ref, q_ref, k_ref, v_ref, cos_ref, sin_ref, yb_ref, g_ref, o_ref, s_ref = refs
    h, c = pl.program_id(1), pl.program_id(2)
    qn = SCAN_CHUNK

    @pl.when(c == 0)
    def _():
        s_ref[...] = jnp.zeros_like(s_ref)

    lg = ld_ref[(n_heads if reverse else 0) + h]
    t, s = _time_iotas(qn)
    is_lat = c >= n_ctx_c
    cos = jnp.where(is_lat, cos_ref[...], 1.0)
    sin = jnp.where(is_lat, sin_ref[...], 0.0)

    def rope(x):
        half = x.shape[1] // 2
        swapped = jnp.concatenate([pltpu.roll(x[:, :half], half // 2, 1), pltpu.roll(x[:, half:], half // 2, 1)], axis=1)
        return x * cos + swapped * sin

    q = rope(q_ref[...]).astype(BF16)
    k = rope(k_ref[...]) * k_scale
    v = v_ref[...].astype(BF16)
    if reverse:
        dmat = jnp.where(s >= t, jnp.exp((s - t) * lg), 0.0)
        q_dec, k_dec = jnp.exp((qn - t) * lg), jnp.exp(t * lg)
    else:
        dmat = jnp.where(t >= s, jnp.exp((t - s) * lg), 0.0)
        q_dec, k_dec = jnp.exp((t + 1.0) * lg), jnp.exp((qn - 1.0 - t) * lg)
    attn = (_dot_nt(q, k.astype(BF16)) * dmat).astype(BF16)
    state = s_ref[...]
    y = jnp.dot(attn, v, preferred_element_type=F32)
    y = y + jnp.dot(q, state.astype(BF16), preferred_element_type=F32) * q_dec
    chunk_dec = jnp.exp(jnp.full((1, 1), qn, F32) * lg)
    s_ref[...] = state * chunk_dec + _dot_t((k * k_dec).astype(BF16), v)
    if reverse:
        o_ref[...] = y
    else:
        y = y + yb_ref[...]
        mu = jnp.mean(y, axis=-1, keepdims=True)
        yc = y - mu
        var = jnp.mean(yc * yc, axis=-1, keepdims=True)
        g = g_ref[...]
        o_ref[...] = (yc * lax.rsqrt(var + NORM_EPS) * (g * jax.nn.sigmoid(g))).astype(o_ref.dtype)


def _ret_mixer(u, log_decay, bsz, lc, seq, d_model):
    n_ctx_c, n_lat_c = _scan_geometry(bsz, lc, seq)
    n_steps = n_ctx_c + n_lat_c
    qk_dim, v_dim = d_model // RET_HEADS, 2 * d_model // RET_HEADS
    rows = u.shape[0]
    cos_r, sin_r, cos_c, sin_c = _rope_tables(seq // GRID_W, qk_dim // 2)
    cos_t = jnp.asarray(np.concatenate([cos_r, cos_r, cos_c, cos_c], axis=1))
    sin_t = jnp.asarray(np.concatenate([-sin_r, sin_r, -sin_c, sin_c], axis=1))
    ld = log_decay.reshape(-1).astype(F32)
    y_b = None
    for reverse in (True, False):
        rb = functools.partial(_row_block, bsz=bsz, n_ctx_c=n_ctx_c, n_lat_c=n_lat_c, reverse=reverse)

        def tab(b, h, c, reverse=reverse):
            is_lat, j = _seg_chunk(c, n_ctx_c, n_lat_c, reverse)
            return (jnp.where(is_lat, j, 0), 0)

        in_specs = [pl.BlockSpec(memory_space=pltpu.SMEM),
                    pl.BlockSpec((SCAN_CHUNK, qk_dim), lambda b, h, c, rb=rb: (rb(b, c), h)),
                    pl.BlockSpec((SCAN_CHUNK, qk_dim), lambda b, h, c, rb=rb: (rb(b, c), RET_HEADS + h)),
                    pl.BlockSpec((SCAN_CHUNK, v_dim), lambda b, h, c, rb=rb: (rb(b, c), RET_HEADS + h)),
                    pl.BlockSpec((SCAN_CHUNK, qk_dim), tab),
                    pl.BlockSpec((SCAN_CHUNK, qk_dim), tab)]
        args = [ld, u, u, u, cos_t, sin_t]
        if not reverse:
            in_specs += [pl.BlockSpec((SCAN_CHUNK, v_dim), lambda b, h, c, rb=rb: (rb(b, c), h)),
                         pl.BlockSpec((SCAN_CHUNK, v_dim), lambda b, h, c, rb=rb: (rb(b, c), 2 * RET_HEADS + h))]
            args += [y_b, u]
        out = pl.pallas_call(
            functools.partial(_ret_kernel, reverse=reverse, n_ctx_c=n_ctx_c, n_heads=RET_HEADS, k_scale=qk_dim ** -0.5),
            grid=(bsz, RET_HEADS, n_steps),
            in_specs=in_specs,
            out_specs=pl.BlockSpec((SCAN_CHUNK, v_dim), lambda b, h, c, rb=rb: (rb(b, c), h)),
            out_shape=jax.ShapeDtypeStruct((rows, RET_HEADS * v_dim), F32 if reverse else BF16),
            scratch_shapes=[pltpu.VMEM((qk_dim, v_dim), F32)],
            compiler_params=_params("parallel", "parallel", "arbitrary"),
            name="retention_bwd" if reverse else "retention_fwd",
        )(*args)
        if reverse:
            y_b = out
    return out


CONV_COLS = 512


def _conv_kernel(x_ref, prev_ref, next_ref, w_ref, b_ref, o_ref, *, n_ctx_c, n_lat_c, bsz, n_scaled, n_normed, scale,
                 head_dim):
    rb, cb = pl.program_id(0), pl.program_id(1)
    is_lat = rb >= bsz * n_ctx_c
    j = jnp.where(is_lat, (rb - bsz * n_ctx_c) % n_lat_c, rb % n_ctx_c)
    last = jnp.where(is_lat, n_lat_c - 1, n_ctx_c - 1)
    x = x_ref[...]
    rows = x.shape[0]
    row = lax.broadcasted_iota(jnp.int32, (rows, 1), 0)
    halo_prev = jnp.where(j > 0, prev_ref[7:8, :], 0.0)
    halo_next = jnp.where(j < last, next_ref[0:1, :], 0.0)
    x_prev = jnp.where(row == 0, halo_prev, pltpu.roll(x, 1, 0))
    x_next = jnp.where(row == rows - 1, halo_next, pltpu.roll(x, rows - 1, 0))
    y = w_ref[0:1, :] * x_prev + w_ref[1:2, :] * x + w_ref[2:3, :] * x_next + b_ref[...]
    y = y * jax.nn.sigmoid(y)

    def normed(mult):
        parts = []
        for i in range(y.shape[1] // head_dim):
            p = y[:, i * head_dim:(i + 1) * head_dim]
            parts.append(p * (lax.rsqrt(jnp.sum(p * p, axis=-1, keepdims=True) + 1e-6) * mult))
        return jnp.concatenate(parts, axis=1)

    if n_normed == 0:
        o_ref[...] = y
    else:
        @pl.when(cb < n_scaled)
        def _():
            o_ref[...] = normed(scale)

        @pl.when(jnp.logical_and(cb >= n_scaled, cb < n_normed))
        def _():
            o_ref[...] = normed(1.0)

        @pl.when(cb >= n_normed)
        def _():
            o_ref[...] = y


def _conv_silu(u, w, b, n_cols, bsz, lc, seq, n_scaled=0, n_normed=0, scale=1.0, head_dim=128, col0=0):
    n_ctx_c, n_lat_c = _scan_geometry(bsz, lc, seq)
    rows = u.shape[0]
    n_rb = rows // SCAN_CHUNK
    sub = SCAN_CHUNK // 8
    n_halo = rows // 8
    return pl.pallas_call(
        functools.partial(_conv_kernel, n_ctx_c=n_ctx_c, n_lat_c=n_lat_c, bsz=bsz, n_scaled=n_scaled,
                          n_normed=n_normed, scale=scale, head_dim=head_dim),
        grid=(n_rb, n_cols // CONV_COLS),
        in_specs=[pl.BlockSpec((SCAN_CHUNK, CONV_COLS), lambda r, c: (r, c + col0)),
                  pl.BlockSpec((8, CONV_COLS), lambda r, c: (jnp.maximum(r * sub - 1, 0), c + col0)),
                  pl.BlockSpec((8, CONV_COLS), lambda r, c: (jnp.minimum(r * sub + sub, n_halo - 1), c + col0)),
                  pl.BlockSpec((CONV_W, CONV_COLS), lambda r, c: (0, c)),
                  pl.BlockSpec((1, CONV_COLS), lambda r, c: (0, c))],
        out_specs=pl.BlockSpec((SCAN_CHUNK, CONV_COLS), lambda r, c: (r, c)),
        out_shape=jax.ShapeDtypeStruct((rows, n_cols), F32),
        compiler_params=_params("parallel", "parallel"),
        name="conv_silu",
    )(u, u, u, w, b)


GDN_SUB = 64


GDN_BASE = 8
GDN_HEADS_PER_STEP = 4


def _split3(x):
    x1 = x.astype(BF16)
    r1 = x - x1.astype(F32)
    x2 = r1.astype(BF16)
    return x1, x2, (r1 - x2.astype(F32)).astype(BF16)


def _sel_rows(m, x):
    mb = m.astype(BF16)
    return sum(jnp.dot(mb, p, preferred_element_type=F32) for p in _split3(x))


def _sel_cols(x, m):
    mb = m.astype(BF16)
    return sum(jnp.dot(p, mb, preferred_element_type=F32) for p in _split3(x))


def _sel_rows_t(x, m):
    mb = m.astype(BF16)
    return sum(lax.dot_general(p, mb, (((0,), (1,)), ((), ())), preferred_element_type=F32) for p in _split3(x))


def _mm_bf16(a, b):
    return jnp.dot(a.astype(BF16), b.astype(BF16), preferred_element_type=F32)


def _unit_tri_inverse_minus_eye(mats, ti, si):
    def same(n):
        s = int(math.log2(n))
        return lax.shift_right_logical(ti, s) == lax.shift_right_logical(si, s)

    base = same(GDN_BASE)
    ps = [jnp.where(base, a, 0.0) for a in mats]
    ns = [-p for p in ps]
    for _ in range(int(math.log2(GDN_BASE)) - 1):
        ps = [_mm_bf16(p, p) for p in ps]
        ns = [n + p + _mm_bf16(n, p) for n, p in zip(ns, ps)]
    b = GDN_BASE
    while b < GDN_SUB:
        join = same(2 * b) & jnp.logical_not(same(b))
        cs = [jnp.where(join, a, 0.0) for a in mats]
        ms = [c + _mm_bf16(c, n) for c, n in zip(cs, ns)]
        ns = [n - (m + _mm_bf16(n, m)) for n, m in zip(ns, ms)]
        b *= 2
    return ns


def _softplus(x):
    return jnp.maximum(x, 0.0) + jnp.log(1.0 + jnp.exp(-jnp.abs(x)))


def _gdn_kernel(*refs, reverse, n_kh, n_r, hd):
    if reverse:
        q_ref, k_ref, v_ref, tail_ref, prow_ref, arow_ref, o_ref, s_ref = refs
    else:
        q_ref, k_ref, v_ref, tail_ref, prow_ref, arow_ref, yb_ref, z_ref, ng_ref, o_ref, s_ref = refs
    g, c = pl.program_id(1), pl.program_id(2)
    qn = SCAN_CHUNK
    n_sub = qn // GDN_SUB
    n_gate = tail_ref.shape[1] // 2

    @pl.when(c == 0)
    def _():
        s_ref[...] = jnp.zeros_like(s_ref)

    tail = tail_ref[...]
    lane = lax.broadcasted_iota(jnp.int32, (1, 2 * n_gate), 1)
    gates = jnp.where(lane < n_gate, jax.nn.sigmoid(tail), -jnp.exp(arow_ref[...]) * _softplus(tail + prow_ref[...]))
    nh = n_kh * n_r
    li = lax.broadcasted_iota(jnp.int32, (2 * n_gate, 2 * nh), 0)
    ji = lax.broadcasted_iota(jnp.int32, (2 * n_gate, 2 * nh), 1)
    col = jnp.where(ji < nh, ji, n_gate + ji - nh) + (n_gate // 2 if reverse else 0) + g * nh
    gsel = _sel_cols(gates, (li == col).astype(F32))
    lw = lax.broadcasted_iota(jnp.int32, (2 * n_gate, 2 * nh * hd), 0)
    jw = lax.shift_right_logical(lax.broadcasted_iota(jnp.int32, (2 * n_gate, 2 * nh * hd), 1), int(math.log2(hd)))
    colw = jnp.where(jw < nh, jw, n_gate + jw - nh) + (n_gate // 2 if reverse else 0) + g * nh
    gwide = _sel_cols(gates, (lw == colw).astype(F32))

    ti = lax.broadcasted_iota(jnp.int32, (qn, 1), 0)
    si = lax.broadcasted_iota(jnp.int32, (1, qn), 1)
    shift = int(math.log2(GDN_SUB))
    same = lax.shift_right_logical(ti, shift) == lax.shift_right_logical(si, shift)
    if reverse:
        incl, strict = same & (si >= ti), same & (si > ti)
    else:
        incl, strict = same & (si <= ti), same & (si < ti)
    cs = incl.astype(F32)
    cum_w = _sel_rows(cs, gwide[:, nh * hd:])
    cum_t = _sel_rows_t(gsel, cs)
    reps = qn // hd

    order = range(n_sub - 1, -1, -1) if reverse else range(n_sub)
    outs = []
    shared = []
    for kh in range(n_kh):
        k = k_ref[:, kh * hd:(kh + 1) * hd]
        qb, kb = q_ref[:, kh * hd:(kh + 1) * hd].astype(BF16), k.astype(BF16)
        shared.append((k, qb, _dot_nt(kb, kb), _dot_nt(qb, kb)))
    heads = range(nh)
    beta = [gwide[:, j * hd:(j + 1) * hd] for j in heads]
    cum_c = [cum_w[:, j * hd:(j + 1) * hd] for j in heads]
    decay = [jnp.exp(jnp.where(incl, jnp.concatenate([cum_c[j]] * reps, axis=1) - cum_t[nh + j:nh + j + 1, :],
                               -jnp.inf)) for j in heads]
    amat = [jnp.where(strict, jnp.concatenate([beta[j]] * reps, axis=1) * shared[j // n_r][2] * decay[j], 0.0)
            for j in heads]
    inv_off = _unit_tri_inverse_minus_eye(amat, ti, si)
    rhs = [jnp.concatenate([v_ref[:, j * hd:(j + 1) * hd] * beta[j],
                            shared[j // n_r][0] * (beta[j] * jnp.exp(cum_c[j]))], axis=1) for j in heads]
    sol = [rhs[j] + _mm_bf16(inv_off[j], rhs[j]) for j in heads]
    state = [s_ref[j] for j in heads]
    v_new = [[None] * n_sub for _ in heads]
    inter = [[None] * n_sub for _ in heads]
    for i in order:
        sl = slice(i * GDN_SUB, (i + 1) * GDN_SUB)
        end = i * GDN_SUB if reverse else (i + 1) * GDN_SUB - 1
        sb = [state[j].astype(BF16) for j in heads]
        for j in heads:
            v_new[j][i] = sol[j][sl, :hd] - jnp.dot(sol[j][sl, hd:].astype(BF16), sb[j], preferred_element_type=F32)
            inter[j][i] = jnp.dot(shared[j // n_r][1][sl], sb[j], preferred_element_type=F32)
        for j in heads:
            cum_end = cum_c[j][end:end + 1, :]
            k_end = (shared[j // n_r][0][sl] * jnp.exp(cum_end - cum_c[j][sl])).astype(BF16)
            state[j] = jnp.exp(cum_end) * state[j] + _dot_t(k_end, v_new[j][i].astype(BF16))
    outs = []
    for j in heads:
        s_ref[j] = state[j]
        attn = (shared[j // n_r][3] * decay[j]).astype(BF16)
        y = jnp.dot(attn, jnp.concatenate(v_new[j], axis=0).astype(BF16), preferred_element_type=F32)
        y = y + jnp.concatenate(inter[j], axis=0) * jnp.exp(cum_c[j])
        if not reverse:
            y = y + yb_ref[:, j * hd:(j + 1) * hd]
            y = y * lax.rsqrt(jnp.mean(y * y, axis=-1, keepdims=True) + NORM_EPS) * ng_ref[...]
            z = z_ref[:, j * hd:(j + 1) * hd]
            y = y * (z * jax.nn.sigmoid(z))
        outs.append(y)
    o_ref[...] = jnp.concatenate(outs, axis=1).astype(o_ref.dtype)


def _gdn_mixer(u, tail, conv_w, dt_bias, a_log, norm_g, bsz, lc, seq, d_model):
    n_ctx_c, n_lat_c = _scan_geometry(bsz, lc, seq)
    n_steps = n_ctx_c + n_lat_c
    hd = GDN_HEAD_DIM
    k_heads = d_model // hd
    n_r = 2
    rows = u.shape[0]
    conv_ch = 4 * d_model
    nq = d_model // CONV_COLS
    qkv = _conv_silu(u, conv_w, jnp.zeros((1, conv_ch), F32), conv_ch, bsz, lc, seq, n_scaled=nq, n_normed=2 * nq,
                     scale=hd ** -0.5, head_dim=hd)
    n_gate = tail.shape[1] // 2
    prow = jnp.concatenate([jnp.zeros((1, n_gate), F32), dt_bias.reshape(1, n_gate)], axis=1)
    arow = jnp.concatenate([jnp.zeros((1, n_gate), F32), a_log.reshape(1, n_gate)], axis=1)
    y_b = None
    for reverse in (True, False):
        rb = functools.partial(_row_block, bsz=bsz, n_ctx_c=n_ctx_c, n_lat_c=n_lat_c, reverse=reverse)
        const = lambda b, g, c: (0, 0)
        qk_w, v_w = GDN_HEADS_PER_STEP * hd, GDN_HEADS_PER_STEP * n_r * hd
        n_groups = k_heads // GDN_HEADS_PER_STEP
        in_specs = [pl.BlockSpec((SCAN_CHUNK, qk_w), lambda b, g, c, rb=rb: (rb(b, c), g)),
                    pl.BlockSpec((SCAN_CHUNK, qk_w), lambda b, g, c, rb=rb: (rb(b, c), n_groups + g)),
                    pl.BlockSpec((SCAN_CHUNK, v_w), lambda b, g, c, rb=rb: (rb(b, c), n_groups + g)),
                    pl.BlockSpec((SCAN_CHUNK, 2 * n_gate), lambda b, g, c, rb=rb: (rb(b, c), 0)),
                    pl.BlockSpec((1, 2 * n_gate), const),
                    pl.BlockSpec((1, 2 * n_gate), const)]
        args = [qkv, qkv, qkv, tail, prow, arow]
        if not reverse:
            in_specs += [pl.BlockSpec((SCAN_CHUNK, v_w), lambda b, g, c, rb=rb: (rb(b, c), g)),
                         pl.BlockSpec((SCAN_CHUNK, v_w), lambda b, g, c, rb=rb: (rb(b, c), 2 * n_groups + g)),
                         pl.BlockSpec((1, hd), const)]
            args += [y_b, u, norm_g.reshape(1, hd)]
        out = pl.pallas_call(
            functools.partial(_gdn_kernel, reverse=reverse, n_kh=GDN_HEADS_PER_STEP, n_r=n_r, hd=hd),
            grid=(bsz, n_groups, n_steps),
            in_specs=in_specs,
            out_specs=pl.BlockSpec((SCAN_CHUNK, v_w), lambda b, g, c, rb=rb: (rb(b, c), g)),
            out_shape=jax.ShapeDtypeStruct((rows, k_heads * n_r * hd), F32 if reverse else BF16),
            scratch_shapes=[pltpu.VMEM((GDN_HEADS_PER_STEP * n_r, hd, hd), F32)],
            compiler_params=_params("parallel", "parallel", "arbitrary"),
            name="gdn_bwd" if reverse else "gdn_fwd",
        )(*args)
        if reverse:
            y_b = out
    return out


def _ssd_kernel(*refs, reverse, n_r, hd):
    if reverse:
        x_ref, b_ref, c_ref, tail_ref, dtb_ref, alog_ref, o_ref, s_ref = refs
    else:
        x_ref, b_ref, c_ref, tail_ref, dtb_ref, alog_ref, yb_ref, z_ref, d_ref, ng_ref, o_ref, s_ref = refs
    g, c = pl.program_id(1), pl.program_id(2)
    qn = SCAN_CHUNK
    n_lane = tail_ref.shape[1]
    width = n_r * hd

    @pl.when(c == 0)
    def _():
        s_ref[...] = jnp.zeros_like(s_ref)

    dt_all = _softplus(tail_ref[...] + dtb_ref[...])
    la_all = -jnp.exp(alog_ref[...]) * dt_all
    li = lax.broadcasted_iota(jnp.int32, (n_lane, n_r), 0)
    ji = lax.broadcasted_iota(jnp.int32, (n_lane, n_r), 1)
    sel = (li == ji + (n_lane // 2 if reverse else 0) + g * n_r).astype(F32)
    dt, la = _sel_cols(dt_all, sel), _sel_cols(la_all, sel)

    ti = lax.broadcasted_iota(jnp.int32, (qn, 1), 0)
    si = lax.broadcasted_iota(jnp.int32, (1, qn), 1)
    before = (si >= ti) if reverse else (si <= ti)
    cs = before.astype(F32)
    cum = _sel_rows(cs, la)
    cum_t = _sel_rows_t(la, cs)
    ei = lax.broadcasted_iota(jnp.int32, (n_r, width), 0)
    el = lax.broadcasted_iota(jnp.int32, (n_r, width), 1)
    expand = (lax.shift_right_logical(el, int(math.log2(hd))) == ei).astype(F32)
    dt_x, cum_x = _sel_cols(dt, expand), _sel_cols(cum, expand)

    xs = x_ref[...]
    v = xs * dt_x
    bm, cm = b_ref[...].astype(BF16), c_ref[...].astype(BF16)
    scores = _dot_nt(cm, bm)
    lane = lax.broadcasted_iota(jnp.int32, (1, 2 * hd), 1)
    tiles = []
    for p in range(n_r // 2):
        vt = v[:, 2 * p * hd:2 * (p + 1) * hd]
        acc = None
        for h in (2 * p, 2 * p + 1):
            decay = jnp.exp(jnp.where(before, cum[:, h:h + 1] - cum_t[h:h + 1, :], -jnp.inf))
            vh = jnp.where((lane >= hd) if h % 2 else (lane < hd), vt, 0.0).astype(BF16)
            part = jnp.dot((scores * decay).astype(BF16), vh, preferred_element_type=F32)
            acc = part if acc is None else acc + part
        tiles.append(acc)
    state = s_ref[...]
    y = jnp.concatenate(tiles, axis=1) + jnp.dot(cm, state.astype(BF16), preferred_element_type=F32) * jnp.exp(cum_x)
    end = 0 if reverse else qn - 1
    cum_end = cum_x[end:end + 1]
    s_ref[...] = state * jnp.exp(cum_end) + _dot_t(bm, (v * jnp.exp(cum_end - cum_x)).astype(BF16))
    if reverse:
        o_ref[...] = y
    else:
        y = y + yb_ref[...] + d_ref[...] * xs
        z = z_ref[...]
        y = y * (z * jax.nn.sigmoid(z))
        y = y * lax.rsqrt(jnp.mean(y * y, axis=-1, keepdims=True) + NORM_EPS) * ng_ref[...]
        o_ref[...] = y.astype(o_ref.dtype)


def _ssd_mixer(u, tail, conv_w, conv_b, dt_bias, a_log, d_skip, norm_g, bsz, lc, seq):
    n_ctx_c, n_lat_c = _scan_geometry(bsz, lc, seq)
    n_steps = n_ctx_c + n_lat_c
    d_inner = norm_g.shape[0]
    hd, st = SSD_HEAD_DIM, SSD_STATE
    heads = d_inner // hd
    n_r = heads // SSD_GROUPS
    width = n_r * hd
    rows = u.shape[0]
    conv_ch = d_inner + 2 * SSD_GROUPS * st
    xbc = _conv_silu(u, conv_w, conv_b.reshape(1, conv_ch), conv_ch, bsz, lc, seq, col0=d_inner // CONV_COLS)
    d_x = jnp.repeat(d_skip, hd).reshape(1, d_inner)
    y_b = None
    for reverse in (True, False):
        rb = functools.partial(_row_block, bsz=bsz, n_ctx_c=n_ctx_c, n_lat_c=n_lat_c, reverse=reverse)
        const = lambda b, g, c: (0, 0)
        grp = lambda b, g, c: (0, g)
        wide = lambda b, g, c, rb=rb: (rb(b, c), g)
        in_specs = [pl.BlockSpec((SCAN_CHUNK, width), wide),
                    pl.BlockSpec((SCAN_CHUNK, st), lambda b, g, c, rb=rb: (rb(b, c), d_inner // st + g)),
                    pl.BlockSpec((SCAN_CHUNK, st), lambda b, g, c, rb=rb: (rb(b, c), d_inner // st + SSD_GROUPS + g)),
                    pl.BlockSpec((SCAN_CHUNK, 2 * heads), lambda b, g, c, rb=rb: (rb(b, c), 0)),
                    pl.BlockSpec((1, 2 * heads), const),
                    pl.BlockSpec((1, 2 * heads), const)]
        args = [xbc, xbc, xbc, tail, dt_bias.reshape(1, 2 * heads), a_log.reshape(1, 2 * heads)]
        if not reverse:
            in_specs += [pl.BlockSpec((SCAN_CHUNK, width), wide), pl.BlockSpec((SCAN_CHUNK, width), wide),
                         pl.BlockSpec((1, width), grp), pl.BlockSpec((1, width), grp)]
            args += [y_b, u, d_x, norm_g.reshape(1, d_inner)]
        out = pl.pallas_call(
            functools.partial(_ssd_kernel, reverse=reverse, n_r=n_r, hd=hd),
            grid=(bsz, SSD_GROUPS, n_steps),
            in_specs=in_specs,
            out_specs=pl.BlockSpec((SCAN_CHUNK, width), wide),
            out_shape=jax.ShapeDtypeStruct((rows, d_inner), F32 if reverse else BF16),
            scratch_shapes=[pltpu.VMEM((st, width), F32)],
            compiler_params=_params("parallel", "parallel", "arbitrary"),
            name="ssd_bwd" if reverse else "ssd_fwd",
        )(*args)
        if reverse:
            y_b = out
    return out


HG_SUB = 64
HG_BLK = 8


def _log1p(x):
    return jnp.log(1.0 + x)


def _hgrn_kernel(*refs, reverse):
    if reverse:
        q_ref, f_ref, i_ref, lb_ref, o_ref, st_ref = refs
    else:
        q_ref, f_ref, i_ref, lb_ref, yb_ref, g_ref, ng_ref, o_ref, st_ref = refs
    c = pl.program_id(2)
    qn = SCAN_CHUNK
    hd = q_ref.shape[1]
    n_sub, n_blk = qn // HG_SUB, HG_SUB // HG_BLK

    @pl.when(c == 0)
    def _():
        st_ref[...] = jnp.zeros_like(st_ref)

    q, f, v, lb = q_ref[...], f_ref[...], i_ref[...], lb_ref[...]
    log_sig = jnp.minimum(f, 0.0) - _log1p(jnp.exp(-jnp.abs(f)))
    ga, gb = jnp.log(lb), _log1p(-lb) + log_sig
    log_f = jnp.maximum(ga, gb) + _log1p(jnp.exp(-jnp.abs(ga - gb)))
    k = (1.0 - lb) * jax.nn.sigmoid(-f)

    ti = lax.broadcasted_iota(jnp.int32, (qn, 1), 0)
    si = lax.broadcasted_iota(jnp.int32, (1, qn), 1)
    before = (si >= ti) if reverse else (si <= ti)
    same = lambda n: lax.shift_right_logical(ti, int(math.log2(n))) == lax.shift_right_logical(si, int(math.log2(n)))
    cum_sub = _sel_rows((same(HG_SUB) & before).astype(F32), log_f)
    cum_blk = _sel_rows((same(HG_BLK) & before).astype(F32), log_f)
    q_blk = q * jnp.exp(cum_blk)
    q_sub = (q * jnp.exp(cum_sub)).astype(BF16)

    ones = jnp.ones((hd, hd), BF16)
    lane = lax.broadcasted_iota(jnp.int32, (1, hd), 1)
    row_sub = lax.broadcasted_iota(jnp.int32, (HG_SUB, 1), 0)
    row_blk = lax.broadcasted_iota(jnp.int32, (HG_BLK, 1), 0)
    zeros_sub = jnp.zeros((hd - HG_SUB, hd), F32)
    y_intra, kvt, chunk_dec = [], [], []
    for i in range(n_sub):
        r0 = i * HG_SUB
        sl = slice(r0, r0 + HG_SUB)
        cs, ks, vs = cum_sub[sl], k[sl], v[sl]
        end = r0 if reverse else r0 + HG_SUB - 1
        cum_end = cum_sub[end:end + 1]
        kvt.append(_dot_t(vs.astype(BF16), (ks * jnp.exp(cum_end - cs)).astype(BF16)))
        chunk_dec.append(jnp.exp(cum_end))
        a_rows = []
        for a in range(n_blk):
            b0 = r0 + a * HG_BLK
            bl = slice(b0, b0 + HG_BLK)
            qb, kb, cb = q[bl], k[bl], cum_blk[bl]
            tiles = []
            for s in range(HG_BLK):
                ok = (row_blk <= s) if reverse else (row_blk >= s)
                e = jnp.exp(jnp.where(ok, cb - cb[s:s + 1], -jnp.inf))
                tiles.append(qb * e * kb[s:s + 1])
            sums = jnp.dot(jnp.concatenate(tiles, axis=0).astype(BF16), ones, preferred_element_type=F32)
            acc = jnp.zeros((HG_BLK, hd), F32)
            for s in range(HG_BLK):
                acc = acc + jnp.where(lane == a * HG_BLK + s, sums[s * HG_BLK:(s + 1) * HG_BLK], 0.0)
            has_earlier = (a < n_blk - 1) if reverse else (a > 0)
            if has_earlier:
                ref_row = b0 + HG_BLK if reverse else b0 - 1
                earlier = (row_sub >= (a + 1) * HG_BLK) if reverse else (row_sub < a * HG_BLK)
                kt = ks * jnp.exp(jnp.where(earlier, cum_sub[ref_row:ref_row + 1] - cs, -jnp.inf))
                kt = jnp.concatenate([kt, zeros_sub], axis=0).astype(BF16)
                acc = acc + _dot_nt(q_blk[bl].astype(BF16), kt)
            a_rows.append(acc)
        attn = jnp.concatenate(a_rows, axis=0).astype(BF16)
        v_pad = jnp.concatenate([vs, zeros_sub], axis=0).astype(BF16)
        y_intra.append(jnp.dot(attn, v_pad, preferred_element_type=F32))

    state = st_ref[...]
    ys = [None] * n_sub
    for i in (range(n_sub - 1, -1, -1) if reverse else range(n_sub)):
        ys[i] = y_intra[i] + _dot_nt(q_sub[i * HG_SUB:(i + 1) * HG_SUB], state.astype(BF16))
        state = state * chunk_dec[i] + kvt[i]
    st_ref[...] = state
    y = jnp.concatenate(ys, axis=0)
    if reverse:
        o_ref[...] = y
    else:
        y = y + yb_ref[...]
        y = y * lax.rsqrt(jnp.mean(y * y, axis=-1, keepdims=True) + NORM_EPS) * ng_ref[...]
        g = g_ref[...]
        o_ref[...] = (y * (g * jax.nn.sigmoid(g))).astype(o_ref.dtype)


def _hgrn_mixer(u, lb, norm_g, bsz, lc, seq):
    n_ctx_c, n_lat_c = _scan_geometry(bsz, lc, seq)
    n_steps = n_ctx_c + n_lat_c
    hd = HGRN_EXPAND
    d_model = lb.shape[0]
    heads = d_model // hd
    rows = u.shape[0]
    lb2, ng2 = lb.reshape(1, d_model), norm_g.reshape(1, d_model)
    y_b = None
    for reverse in (True, False):
        rb = functools.partial(_row_block, bsz=bsz, n_ctx_c=n_ctx_c, n_lat_c=n_lat_c, reverse=reverse)
        col = lambda seg: (lambda b, h, c, rb=rb: (rb(b, c), seg * heads + h))
        par = lambda b, h, c: (0, h)
        in_specs = [pl.BlockSpec((SCAN_CHUNK, hd), col(0)),
                    pl.BlockSpec((SCAN_CHUNK, hd), col(2 if reverse else 1)),
                    pl.BlockSpec((SCAN_CHUNK, hd), col(3)),
                    pl.BlockSpec((1, hd), par)]
        args = [u, u, u, lb2]
        if not reverse:
            in_specs += [pl.BlockSpec((SCAN_CHUNK, hd), col(0)), pl.BlockSpec((SCAN_CHUNK, hd), col(4)),
                         pl.BlockSpec((1, hd), par)]
            args += [y_b, u, ng2]
        out = pl.pallas_call(
            functools.partial(_hgrn_kernel, reverse=reverse),
            grid=(bsz, heads, n_steps),
            in_specs=in_specs,
            out_specs=pl.BlockSpec((SCAN_CHUNK, hd), col(0)),
            out_shape=jax.ShapeDtypeStruct((rows, d_model), F32 if reverse else BF16),
            scratch_shapes=[pltpu.VMEM((hd, hd), F32)],
            compiler_params=_params("parallel", "parallel", "arbitrary"),
            name="hgrn_bwd" if reverse else "hgrn_fwd",
        )(*args)
        if reverse:
            y_b = out
    return out


def _rope_tables(rows, half):
    pos = np.arange(rows * GRID_W)
    inv_freq = np.float32(ROPE_BASE) ** (-(np.arange(0, half, 2, dtype=np.float32) / np.float32(half)))
    out = []
    for p in ((pos // GRID_W).astype(np.float32), (pos % GRID_W).astype(np.float32)):
        ang = (p[:, None] * inv_freq.astype(np.float32)).astype(np.float32).astype(np.float64)
        out += [np.cos(ang).astype(np.float32), np.sin(ang).astype(np.float32)]
    return out


def _lower_bound(lb_logits, layer):
    p = jax.nn.softmax(lb_logits.astype(F32), axis=0)
    return jnp.cumsum(p, axis=0)[layer] - p[0]


def kernel(x, c, ctx, c_ctx, ada_w, ada_b, norm_g, mlp_w1, mlp_w2, final_g, ssd_w_in, ssd_conv_w, ssd_conv_b,
           ssd_dt_bias, ssd_a_log, ssd_d, ssd_norm_g, ssd_w_out, ret_w_in, ret_log_decay, ret_w_out, hgrn_w_in,
           hgrn_lb_logits, hgrn_norm_g, hgrn_w_out, gdn_w_in, gdn_conv_w, gdn_dt_bias, gdn_a_log, gdn_norm_g,
           gdn_w_out):
    bsz, seq, d = x.shape
    lc = ctx.shape[1]
    depth = ada_w.shape[0]
    n_ctx = bsz * lc
    assert n_ctx % IN_ROW_TILE == 0 and seq % IN_ROW_TILE == 0 and IN_ROW_TILE % ROW_TILE == 0 and bsz + 1 <= 8

    cond_pad = jnp.concatenate([c, c_ctx[None], jnp.zeros((8 - bsz - 1, d), F32)], axis=0)
    mod = _ada_mod(cond_pad, ada_w, ada_b)

    def tile_rows(tile):
        return jnp.asarray([bsz] * (n_ctx // tile) + [b for b in range(bsz) for _ in range(seq // tile)], jnp.int32)

    tile_row, in_tile_row = tile_rows(ROW_TILE), tile_rows(IN_ROW_TILE)
    n_ctx_tiles = n_ctx // ROW_TILE

    xr = jnp.concatenate([ctx.reshape(n_ctx, d), x.reshape(bsz * seq, d)], axis=0)

    for i in range(depth):
        mixer, occ = i % 4, i // 4
        keep_ctx = i < depth - 1
        mod_t = mod[i][tile_row][:, None, :]
        g0, g1 = norm_g[i, 0][None], norm_g[i, 1][None]
        if mixer == 0:
            w_in, w_out = ssd_w_in[occ], ssd_w_out[occ]
        elif mixer == 1:
            w_in, w_out = ret_w_in[occ], ret_w_out[occ]
        elif mixer == 2:
            w_in, w_out = hgrn_w_in[occ], hgrn_w_out[occ]
        else:
            w_in, w_out = gdn_w_in[occ], gdn_w_out[occ]
        n_in = w_in.shape[1]
        n_main = (n_in // 1024) * 1024 if n_in % 1024 else n_in
        w_in = w_in.astype(BF16)
        mod_in = mod[i][in_tile_row][:, None, :]
        u = _ln_mm(xr, g0, mod_in, 0, 1, w_in, 0, n_main)
        tail = _ln_mm(xr, g0, mod_in, 0, 1, w_in, n_main, n_in - n_main) if n_main != n_in else None
        if mixer == 0:
            yr = _ssd_mixer(u, tail, ssd_conv_w[occ], ssd_conv_b[occ], ssd_dt_bias[occ], ssd_a_log[occ], ssd_d[occ],
                            ssd_norm_g[occ], bsz, lc, seq)
        elif mixer == 1:
            yr = _ret_mixer(u, ret_log_decay[occ], bsz, lc, seq, d)
        elif mixer == 2:
            yr = _hgrn_mixer(u, _lower_bound(hgrn_lb_logits, i), hgrn_norm_g[occ], bsz, lc, seq)
        else:
            yr = _gdn_mixer(u, tail, gdn_conv_w[occ], gdn_dt_bias[occ], gdn_a_log[occ], gdn_norm_g[occ], bsz, lc,
                            seq, d)
        first_tile = 0 if keep_ctx else n_ctx_tiles
        xr = _out_proj(yr, w_out.astype(BF16), xr, mod_t, 2, first_tile)
        mod_t = mod_t[first_tile:]
        xr = _mlp(xr, g1, mod_t, mlp_w1[i].astype(BF16), mlp_w2[i].astype(BF16), final_g[None], final=not keep_ctx)
    return xr.reshape(bsz, seq, d)
```

```python
import functools
import math

import jax
import jax.numpy as jnp
import numpy as np
from jax import lax
from jax.experimental import pallas as pl
from jax.experimental.pallas import tpu as pltpu

F32 = jnp.float32
BF16 = jnp.bfloat16

GRID_W = 64
CHUNK = 64
CONV_W = 3
NORM_EPS = 1e-6
ROPE_BASE = 10000.0
SSD_HEAD_DIM = 64
SSD_GROUPS = 8
SSD_STATE = 128
RET_HEADS = 8
HGRN_EXPAND = 128
GDN_HEAD_DIM = 128

ROW_TILE = 512
IN_ROW_TILE = 1024
MLP_FF_TILE = 1024
OUT_COL_TILE = 1024
VMEM_LIMIT = 56 * 1024 * 1024


def _params(*sem):
    return pltpu.CompilerParams(dimension_semantics=sem, vmem_limit_bytes=VMEM_LIMIT)


def _col_tile(n, cap=1536):
    best = 128
    for t in range(128, cap + 1, 128):
        if n % t == 0:
            best = t
    return best


def _ada_kernel(c_ref, w_ref, b_ref, o_ref):
    c = c_ref[...]
    c = (c * jax.nn.sigmoid(c)).astype(BF16)
    o_ref[0] = jnp.dot(c, w_ref[0].astype(BF16), preferred_element_type=F32) + b_ref[0]


def _ada_mod(cond_pad, ada_w, ada_b):
    depth, d, n = ada_w.shape
    tn = 1024
    return pl.pallas_call(
        _ada_kernel,
        grid=(depth, n // tn),
        in_specs=[pl.BlockSpec((8, d), lambda l, j: (0, 0)),
                  pl.BlockSpec((1, d, tn), lambda l, j: (l, 0, j)),
                  pl.BlockSpec((1, 1, tn), lambda l, j: (l, 0, j))],
        out_specs=pl.BlockSpec((1, 8, tn), lambda l, j: (l, 0, j)),
        out_shape=jax.ShapeDtypeStruct((depth, 8, n), F32),
        compiler_params=_params("parallel", "parallel"),
        name="ada_mod",
    )(cond_pad, ada_w, ada_b.reshape(depth, 1, n))


def _adaln_rows(x, g, sh, sc):
    y = x * lax.rsqrt(jnp.mean(x * x, axis=-1, keepdims=True) + NORM_EPS)
    return y * g * (1.0 + sc) + sh


def _ln_mm_kernel(x_ref, g_ref, sh_ref, sc_ref, w_ref, o_ref, h_ref):
    @pl.when(pl.program_id(1) == 0)
    def _():
        h_ref[...] = _adaln_rows(x_ref[...], g_ref[...], sh_ref[0], sc_ref[0]).astype(BF16)

    o_ref[...] = jnp.dot(h_ref[...], w_ref[...], preferred_element_type=F32).astype(o_ref.dtype)


def _ln_mm(x, g, mod_t, sh_col, sc_col, w, col_start, n, out_dtype=F32):
    m, d = x.shape
    tm, tn = IN_ROW_TILE, _col_tile(n)
    assert col_start % tn == 0
    col0 = col_start // tn
    return pl.pallas_call(
        _ln_mm_kernel,
        grid=(m // tm, n // tn),
        in_specs=[pl.BlockSpec((tm, d), lambda i, j: (i, 0)),
                  pl.BlockSpec((1, d), lambda i, j: (0, 0)),
                  pl.BlockSpec((1, 1, d), lambda i, j: (i, 0, sh_col)),
                  pl.BlockSpec((1, 1, d), lambda i, j: (i, 0, sc_col)),
                  pl.BlockSpec((d, tn), lambda i, j: (0, col0 + j))],
        out_specs=pl.BlockSpec((tm, tn), lambda i, j: (i, j)),
        out_shape=jax.ShapeDtypeStruct((m, n), out_dtype),
        scratch_shapes=[pltpu.VMEM((tm, d), BF16)],
        compiler_params=_params("parallel", "arbitrary"),
        name="adaln_in_proj",
    )(x, g, mod_t, mod_t, w)


def _out_kernel(y_ref, w_ref, x_ref, gate_ref, o_ref):
    o_ref[...] = x_ref[...] + gate_ref[0] * jnp.dot(y_ref[...], w_ref[...], preferred_element_type=F32)


def _out_proj(y, w, x, mod_t, gate_col, first_tile=0):
    k = y.shape[1]
    d = w.shape[1]
    tm, tn = ROW_TILE, OUT_COL_TILE
    m = y.shape[0] - first_tile * tm
    return pl.pallas_call(
        _out_kernel,
        grid=(m // tm, d // tn),
        in_specs=[pl.BlockSpec((tm, k), lambda i, j: (i + first_tile, 0)),
                  pl.BlockSpec((k, tn), lambda i, j: (0, j)),
                  pl.BlockSpec((tm, tn), lambda i, j: (i + first_tile, j)),
                  pl.BlockSpec((1, 1, tn), lambda i, j: (i + first_tile, 0, gate_col * (d // tn) + j))],
        out_specs=pl.BlockSpec((tm, tn), lambda i, j: (i, j)),
        out_shape=jax.ShapeDtypeStruct((m, d), F32),
        compiler_params=_params("parallel", "arbitrary"),
        name="out_proj",
    )(y, w, x, mod_t)


def _mlp_kernel(x_ref, g_ref, sh_ref, sc_ref, gate_ref, w1_ref, w2_ref, fg_ref, o_ref, h_ref, acc_ref, *, final):
    f = pl.program_id(1)

    @pl.when(f == 0)
    def _():
        h_ref[...] = _adaln_rows(x_ref[...], g_ref[...], sh_ref[0], sc_ref[0]).astype(BF16)
        acc_ref[...] = jnp.zeros_like(acc_ref)

    a = jnp.dot(h_ref[...], w1_ref[...], preferred_element_type=F32)
    a = jnp.square(jnp.maximum(a, 0.0)).astype(BF16)
    acc_ref[...] += jnp.dot(a, w2_ref[...], preferred_element_type=F32)

    @pl.when(f == pl.num_programs(1) - 1)
    def _():
        out = x_ref[...] + gate_ref[0] * acc_ref[...]
        if final:
            out = out * lax.rsqrt(jnp.mean(out * out, axis=-1, keepdims=True) + NORM_EPS) * fg_ref[...]
        o_ref[...] = out


def _mlp(x, g, mod_t, w1, w2, final_g, final):
    m, d = x.shape
    ff = w1.shape[1]
    tm, tf = ROW_TILE, MLP_FF_TILE
    return pl.pallas_call(
        functools.partial(_mlp_kernel, final=final),
        grid=(m // tm, ff // tf),
        in_specs=[pl.BlockSpec((tm, d), lambda i, f: (i, 0)),
                  pl.BlockSpec((1, d), lambda i, f: (0, 0)),
                  pl.BlockSpec((1, 1, d), lambda i, f: (i, 0, 3)),
                  pl.BlockSpec((1, 1, d), lambda i, f: (i, 0, 4)),
                  pl.BlockSpec((1, 1, d), lambda i, f: (i, 0, 5)),
                  pl.BlockSpec((d, tf), lambda i, f: (0, f)),
                  pl.BlockSpec((tf, d), lambda i, f: (f, 0)),
                  pl.BlockSpec((1, d), lambda i, f: (0, 0))],
        out_specs=pl.BlockSpec((tm, d), lambda i, f: (i, 0)),
        out_shape=jax.ShapeDtypeStruct((m, d), F32),
        scratch_shapes=[pltpu.VMEM((tm, d), BF16), pltpu.VMEM((tm, d), F32)],
        compiler_params=_params("parallel", "arbitrary"),
        name="adaln_mlp",
    )(x, g, mod_t, mod_t, mod_t, w1, w2, final_g)


SCAN_CHUNK = 256


def _scan_geometry(bsz, lc, seq):
    assert lc % SCAN_CHUNK == 0 and seq % SCAN_CHUNK == 0
    return lc // SCAN_CHUNK, seq // SCAN_CHUNK


def _seg_chunk(c, n_ctx_c, n_lat_c, reverse):
    if reverse:
        return c >= n_ctx_c, jnp.where(c < n_ctx_c, n_ctx_c - 1 - c, n_lat_c - 1 - (c - n_ctx_c))
    return c >= n_ctx_c, jnp.where(c < n_ctx_c, c, c - n_ctx_c)


def _row_block(b, c, bsz, n_ctx_c, n_lat_c, reverse):
    is_lat, j = _seg_chunk(c, n_ctx_c, n_lat_c, reverse)
    return jnp.where(is_lat, bsz * n_ctx_c + b * n_lat_c + j, b * n_ctx_c + j)


def _time_iotas(q):
    t = lax.broadcasted_iota(jnp.int32, (q, 1), 0).astype(F32)
    s = lax.broadcasted_iota(jnp.int32, (1, q), 1).astype(F32)
    return t, s


def _dot_t(a, b):
    return lax.dot_general(a, b, (((0,), (0,)), ((), ())), preferred_element_type=F32)


def _dot_nt(a, b):
    return lax.dot_general(a, b, (((1,), (1,)), ((), ())), preferred_element_type=F32)


def _ret_kernel(*refs, reverse, n_ctx_c, n_heads, k_scale):
    if reverse:
        ld_ref, q_ref, k_ref, v_ref, cos_ref, sin_ref, o_ref, s_ref = refs
    else:
        ld_ref, q_ref, k_ref, v_ref, cos_ref, sin_ref, yb_ref, g_ref, o_ref, s_ref = refs
    h, c = pl.program_id(1), pl.program_id(2)
    qn = SCAN_CHUNK

    @pl.when(c == 0)
    def _():
        s_ref[...] = jnp.zeros_like(s_ref)

    lg = ld_ref[(n_heads if reverse else 0) + h]
    t, s = _time_iotas(qn)
    is_lat = c >= n_ctx_c
    cos = jnp.where(is_lat, cos_ref[...], 1.0)
    sin = jnp.where(is_lat, sin_ref[...], 0.0)

    def rope(x):
        half = x.shape[1] // 2
        swapped = jnp.concatenate([pltpu.roll(x[:, :half], half // 2, 1), pltpu.roll(x[:, half:], half // 2, 1)], axis=1)
        return x * cos + swapped * sin

    q = rope(q_ref[...]).astype(BF16)
    k = rope(k_ref[...]) * k_scale
    v = v_ref[...].astype(BF16)
    if reverse:
        dmat = jnp.where(s >= t, jnp.exp((s - t) * lg), 0.0)
        q_dec, k_dec = jnp.exp((qn - t) * lg), jnp.exp(t * lg)
    else:
        dmat = jnp.where(t >= s, jnp.exp((t - s) * lg), 0.0)
        q_dec, k_dec = jnp.exp((t + 1.0) * lg), jnp.exp((qn - 1.0 - t) * lg)
    attn = (_dot_nt(q, k.astype(BF16)) * dmat).astype(BF16)
    state = s_ref[...]
    y = jnp.dot(attn, v, preferred_element_type=F32)
    y = y + jnp.dot(q, state.astype(BF16), preferred_element_type=F32) * q_dec
    chunk_dec = jnp.exp(jnp.full((1, 1), qn, F32) * lg)
    s_ref[...] = state * chunk_dec + _dot_t((k * k_dec).astype(BF16), v)
    if reverse:
        o_ref[...] = y
    else:
        y = y + yb_ref[...]
        mu = jnp.mean(y, axis=-1, keepdims=True)
        yc = y - mu
        var = jnp.mean(yc * yc, axis=-1, keepdims=True)
        g = g_ref[...]
        o_ref[...] = (yc * lax.rsqrt(var + NORM_EPS) * (g * jax.nn.sigmoid(g))).astype(o_ref.dtype)


def _ret_mixer(u, log_decay, bsz, lc, seq, d_model):
    n_ctx_c, n_lat_c = _scan_geometry(bsz, lc, seq)
    n_steps = n_ctx_c + n_lat_c
    qk_dim, v_dim = d_model // RET_HEADS, 2 * d_model // RET_HEADS
    rows = u.shape[0]
    cos_r, sin_r, cos_c, sin_c = _rope_tables(seq // GRID_W, qk_dim // 2)
    cos_t = jnp.asarray(np.concatenate([cos_r, cos_r, cos_c, cos_c], axis=1))
    sin_t = jnp.asarray(np.concatenate([-sin_r, sin_r, -sin_c, sin_c], axis=1))
    ld = log_decay.reshape(-1).astype(F32)
    y_b = None
    for reverse in (True, False):
        rb = functools.partial(_row_block, bsz=bsz, n_ctx_c=n_ctx_c, n_lat_c=n_lat_c, reverse=reverse)

        def tab(b, h, c, reverse=reverse):
            is_lat, j = _seg_chunk(c, n_ctx_c, n_lat_c, reverse)
            return (jnp.where(is_lat, j, 0), 0)

        in_specs = [pl.BlockSpec(memory_space=pltpu.SMEM),
                    pl.BlockSpec((SCAN_CHUNK, qk_dim), lambda b, h, c, rb=rb: (rb(b, c), h)),
                    pl.BlockSpec((SCAN_CHUNK, qk_dim), lambda b, h, c, rb=rb: (rb(b, c), RET_HEADS + h)),
                    pl.BlockSpec((SCAN_CHUNK, v_dim), lambda b, h, c, rb=rb: (rb(b, c), RET_HEADS + h)),
                    pl.BlockSpec((SCAN_CHUNK, qk_dim), tab),
                    pl.BlockSpec((SCAN_CHUNK, qk_dim), tab)]
        args = [ld, u, u, u, cos_t, sin_t]
        if not reverse:
            in_specs += [pl.BlockSpec((SCAN_CHUNK, v_dim), lambda b, h, c, rb=rb: (rb(b, c), h)),
                         pl.BlockSpec((SCAN_CHUNK, v_dim), lambda b, h, c, rb=rb: (rb(b, c), 2 * RET_HEADS + h))]
            args += [y_b, u]
        out = pl.pallas_call(
            functools.partial(_ret_kernel, reverse=reverse, n_ctx_c=n_ctx_c, n_heads=RET_HEADS, k_scale=qk_dim ** -0.5),
            grid=(bsz, RET_HEADS, n_steps),
            in_specs=in_specs,
            out_specs=pl.BlockSpec((SCAN_CHUNK, v_dim), lambda b, h, c, rb=rb: (rb(b, c), h)),
            out_shape=jax.ShapeDtypeStruct((rows, RET_HEADS * v_dim), F32 if reverse else BF16),
            scratch_shapes=[pltpu.VMEM((qk_dim, v_dim), F32)],
            compiler_params=_params("parallel", "parallel", "arbitrary"),
            name="retention_bwd" if reverse else "retention_fwd",
        )(*args)
        if reverse:
            y_b = out
    return out


CONV_COLS = 1024
CONV_ROWS = 8


def _conv_kernel(x_ref, prev_ref, next_ref, w_ref, b_ref, o_ref, *, n_ctx_c, n_lat_c, bsz, n_scaled, n_normed, scale,
                 head_dim):
    rb, cb = pl.program_id(0), pl.program_id(1)
    is_lat = rb >= bsz * n_ctx_c
    j = jnp.where(is_lat, (rb - bsz * n_ctx_c) % n_lat_c, rb % n_ctx_c)
    last = jnp.where(is_lat, n_lat_c - 1, n_ctx_c - 1)
    rows, cols = x_ref.shape
    halo_prev = jnp.where(j > 0, prev_ref[7:8, :], 0.0)
    halo_next = jnp.where(j < last, next_ref[0:1, :], 0.0)
    w0, w1, w2, bias = w_ref[0:1, :], w_ref[1:2, :], w_ref[2:3, :], b_ref[...]
    row = lax.broadcasted_iota(jnp.int32, (CONV_ROWS, 1), 0)

    def run(mult):
        before = halo_prev
        for r0 in range(0, rows, CONV_ROWS):
            x = x_ref[r0:r0 + CONV_ROWS, :]
            after = x_ref[r0 + CONV_ROWS:r0 + CONV_ROWS + 1, :] if r0 + CONV_ROWS < rows else halo_next
            x_prev = jnp.where(row == 0, before, pltpu.roll(x, 1, 0))
            x_next = jnp.where(row == CONV_ROWS - 1, after, pltpu.roll(x, CONV_ROWS - 1, 0))
            y = w0 * x_prev + w1 * x + w2 * x_next + bias
            y = y * jax.nn.sigmoid(y)
            if mult is not None:
                parts = []
                for i in range(cols // head_dim):
                    p = y[:, i * head_dim:(i + 1) * head_dim]
                    parts.append(p * (lax.rsqrt(jnp.sum(p * p, axis=-1, keepdims=True) + 1e-6) * mult))
                y = jnp.concatenate(parts, axis=1)
            o_ref[r0:r0 + CONV_ROWS, :] = y
            before = x[CONV_ROWS - 1:CONV_ROWS, :]

    if n_normed == 0:
        run(None)
    else:
        @pl.when(cb < n_normed)
        def _():
            run(jnp.where(cb < n_scaled, scale, 1.0))

        @pl.when(cb >= n_normed)
        def _():
            run(None)


def _conv_silu(u, w, b, n_cols, bsz, lc, seq, n_scaled=0, n_normed=0, scale=1.0, head_dim=128, col0=0):
    n_ctx_c, n_lat_c = _scan_geometry(bsz, lc, seq)
    rows = u.shape[0]
    n_rb = rows // SCAN_CHUNK
    sub = SCAN_CHUNK // 8
    n_halo = rows // 8
    return pl.pallas_call(
        functools.partial(_conv_kernel, n_ctx_c=n_ctx_c, n_lat_c=n_lat_c, bsz=bsz, n_scaled=n_scaled,
                          n_normed=n_normed, scale=scale, head_dim=head_dim),
        grid=(n_rb, n_cols // CONV_COLS),
        in_specs=[pl.BlockSpec((SCAN_CHUNK, CONV_COLS), lambda r, c: (r, c + col0)),
                  pl.BlockSpec((8, CONV_COLS), lambda r, c: (jnp.maximum(r * sub - 1, 0), c + col0)),
                  pl.BlockSpec((8, CONV_COLS), lambda r, c: (jnp.minimum(r * sub + sub, n_halo - 1), c + col0)),
                  pl.BlockSpec((CONV_W, CONV_COLS), lambda r, c: (0, c)),
                  pl.BlockSpec((1, CONV_COLS), lambda r, c: (0, c))],
        out_specs=pl.BlockSpec((SCAN_CHUNK, CONV_COLS), lambda r, c: (r, c)),
        out_shape=jax.ShapeDtypeStruct((rows, n_cols), F32),
        compiler_params=_params("parallel", "parallel"),
        name="conv_silu",
    )(u, u, u, w, b)


GDN_SUB = 64


GDN_BASE = 8
GDN_HEADS_PER_STEP = 4


def _split3(x):
    x1 = x.astype(BF16)
    r1 = x - x1.astype(F32)
    x2 = r1.astype(BF16)
    return x1, x2, (r1 - x2.astype(F32)).astype(BF16)


def _sel_rows(m, x):
    mb = m.astype(BF16)
    return sum(jnp.dot(mb, p, preferred_element_type=F32) for p in _split3(x))


def _sel_cols(x, m):
    mb = m.astype(BF16)
    return sum(jnp.dot(p, mb, preferred_element_type=F32) for p in _split3(x))


def _sel_rows_t(x, m):
    mb = m.astype(BF16)
    return sum(lax.dot_general(p, mb, (((0,), (1,)), ((), ())), preferred_element_type=F32) for p in _split3(x))


def _mm_bf16(a, b):
    return jnp.dot(a.astype(BF16), b.astype(BF16), preferred_element_type=F32)


def _unit_tri_inverse_minus_eye(mats, ti, si):
    def same(n):
        s = int(math.log2(n))
        return lax.shift_right_logical(ti, s) == lax.shift_right_logical(si, s)

    base = same(GDN_BASE)
    ps = [jnp.where(base, a, 0.0) for a in mats]
    ns = [-p for p in ps]
    for _ in range(int(math.log2(GDN_BASE)) - 1):
        ps = [_mm_bf16(p, p) for p in ps]
        ns = [n + p + _mm_bf16(n, p) for n, p in zip(ns, ps)]
    b = GDN_BASE
    while b < GDN_SUB:
        join = same(2 * b) & jnp.logical_not(same(b))
        cs = [jnp.where(join, a, 0.0) for a in mats]
        ms = [c + _mm_bf16(c, n) for c, n in zip(cs, ns)]
        ns = [n - (m + _mm_bf16(n, m)) for n, m in zip(ns, ms)]
        b *= 2
    return ns


def _softplus(x):
    return jnp.maximum(x, 0.0) + jnp.log(1.0 + jnp.exp(-jnp.abs(x)))


def _gdn_kernel(*refs, reverse, n_kh, n_r, hd):
    if reverse:
        q_ref, k_ref, v_ref, tail_ref, prow_ref, arow_ref, o_ref, s_ref = refs
    else:
        q_ref, k_ref, v_ref, tail_ref, prow_ref, arow_ref, yb_ref, z_ref, ng_ref, o_ref, s_ref = refs
    g, c = pl.program_id(1), pl.program_id(2)
    qn = SCAN_CHUNK
    n_sub = qn // GDN_SUB
    n_gate = tail_ref.shape[1] // 2

    @pl.when(c == 0)
    def _():
        s_ref[...] = jnp.zeros_like(s_ref)

    tail = tail_ref[...]
    lane = lax.broadcasted_iota(jnp.int32, (1, 2 * n_gate), 1)
    gates = jnp.where(lane < n_gate, jax.nn.sigmoid(tail), -jnp.exp(arow_ref[...]) * _softplus(tail + prow_ref[...]))
    nh = n_kh * n_r
    li = lax.broadcasted_iota(jnp.int32, (2 * n_gate, 2 * nh), 0)
    ji = lax.broadcasted_iota(jnp.int32, (2 * n_gate, 2 * nh), 1)
    col = jnp.where(ji < nh, ji, n_gate + ji - nh) + (n_gate // 2 if reverse else 0) + g * nh
    gsel = _sel_cols(gates, (li == col).astype(F32))
    lw = lax.broadcasted_iota(jnp.int32, (2 * n_gate, 2 * nh * hd), 0)
    jw = lax.shift_right_logical(lax.broadcasted_iota(jnp.int32, (2 * n_gate, 2 * nh * hd), 1), int(math.log2(hd)))
    colw = jnp.where(jw < nh, jw, n_gate + jw - nh) + (n_gate // 2 if reverse else 0) + g * nh
    gwide = _sel_cols(gates, (lw == colw).astype(F32))

    ti = lax.broadcasted_iota(jnp.int32, (qn, 1), 0)
    si = lax.broadcasted_iota(jnp.int32, (1, qn), 1)
    shift = int(math.log2(GDN_SUB))
    same = lax.shift_right_logical(ti, shift) == lax.shift_right_logical(si, shift)
    if reverse:
        incl, strict = same & (si >= ti), same & (si > ti)
    else:
        incl, strict = same & (si <= ti), same & (si < ti)
    cs = incl.astype(F32)
    cum_w = _sel_rows(cs, gwide[:, nh * hd:])
    cum_t = _sel_rows_t(gsel, cs)
    reps = qn // hd

    order = range(n_sub - 1, -1, -1) if reverse else range(n_sub)
    outs = []
    shared = []
    for kh in range(n_kh):
        k = k_ref[:, kh * hd:(kh + 1) * hd]
        qb, kb = q_ref[:, kh * hd:(kh + 1) * hd].astype(BF16), k.astype(BF16)
        shared.append((k, qb, _dot_nt(kb, kb), _dot_nt(qb, kb)))
    heads = range(nh)
    beta = [gwide[:, j * hd:(j + 1) * hd] for j in heads]
    cum_c = [cum_w[:, j * hd:(j + 1) * hd] for j in heads]
    decay = [jnp.exp(jnp.where(incl, jnp.concatenate([cum_c[j]] * reps, axis=1) - cum_t[nh + j:nh + j + 1, :],
                               -jnp.inf)) for j in heads]
    amat = [jnp.where(strict, jnp.concatenate([beta[j]] * reps, axis=1) * shared[j // n_r][2] * decay[j], 0.0)
            for j in heads]
    inv_off = _unit_tri_inverse_minus_eye(amat, ti, si)
    rhs = [jnp.concatenate([v_ref[:, j * hd:(j + 1) * hd] * beta[j],
                            shared[j // n_r][0] * (beta[j] * jnp.exp(cum_c[j]))], axis=1) for j in heads]
    sol = [rhs[j] + _mm_bf16(inv_off[j], rhs[j]) for j in heads]
    state = [s_ref[j] for j in heads]
    v_new = [[None] * n_sub for _ in heads]
    inter = [[None] * n_sub for _ in heads]
    for i in order:
        sl = slice(i * GDN_SUB, (i + 1) * GDN_SUB)
        end = i * GDN_SUB if reverse else (i + 1) * GDN_SUB - 1
        sb = [state[j].astype(BF16) for j in heads]
        for j in heads:
            v_new[j][i] = sol[j][sl, :hd] - jnp.dot(sol[j][sl, hd:].astype(BF16), sb[j], preferred_element_type=F32)
            inter[j][i] = jnp.dot(shared[j // n_r][1][sl], sb[j], preferred_element_type=F32)
        for j in heads:
            cum_end = cum_c[j][end:end + 1, :]
            k_end = (shared[j // n_r][0][sl] * jnp.exp(cum_end - cum_c[j][sl])).astype(BF16)
            state[j] = jnp.exp(cum_end) * state[j] + _dot_t(k_end, v_new[j][i].astype(BF16))
    outs = []
    for j in heads:
        s_ref[j] = state[j]
        attn = (shared[j // n_r][3] * decay[j]).astype(BF16)
        y = jnp.dot(attn, jnp.concatenate(v_new[j], axis=0).astype(BF16), preferred_element_type=F32)
        y = y + jnp.concatenate(inter[j], axis=0) * jnp.exp(cum_c[j])
        if not reverse:
            y = y + yb_ref[:, j * hd:(j + 1) * hd]
            y = y * lax.rsqrt(jnp.mean(y * y, axis=-1, keepdims=True) + NORM_EPS) * ng_ref[...]
            z = z_ref[:, j * hd:(j + 1) * hd]
            y = y * (z * jax.nn.sigmoid(z))
        outs.append(y)
    o_ref[...] = jnp.concatenate(outs, axis=1).astype(o_ref.dtype)


def _gdn_mixer(u, tail, conv_w, dt_bias, a_log, norm_g, bsz, lc, seq, d_model):
    n_ctx_c, n_lat_c = _scan_geometry(bsz, lc, seq)
    n_steps = n_ctx_c + n_lat_c
    hd = GDN_HEAD_DIM
    k_heads = d_model // hd
    n_r = 2
    rows = u.shape[0]
    conv_ch = 4 * d_model
    nq = d_model // CONV_COLS
    qkv = _conv_silu(u, conv_w, jnp.zeros((1, conv_ch), F32), conv_ch, bsz, lc, seq, n_scaled=nq, n_normed=2 * nq,
                     scale=hd ** -0.5, head_dim=hd)
    n_gate = tail.shape[1] // 2
    prow = jnp.concatenate([jnp.zeros((1, n_gate), F32), dt_bias.reshape(1, n_gate)], axis=1)
    arow = jnp.concatenate([jnp.zeros((1, n_gate), F32), a_log.reshape(1, n_gate)], axis=1)
    y_b = None
    for reverse in (True, False):
        rb = functools.partial(_row_block, bsz=bsz, n_ctx_c=n_ctx_c, n_lat_c=n_lat_c, reverse=reverse)
        const = lambda b, g, c: (0, 0)
        qk_w, v_w = GDN_HEADS_PER_STEP * hd, GDN_HEADS_PER_STEP * n_r * hd
        n_groups = k_heads // GDN_HEADS_PER_STEP
        in_specs = [pl.BlockSpec((SCAN_CHUNK, qk_w), lambda b, g, c, rb=rb: (rb(b, c), g)),
                    pl.BlockSpec((SCAN_CHUNK, qk_w), lambda b, g, c, rb=rb: (rb(b, c), n_groups + g)),
                    pl.BlockSpec((SCAN_CHUNK, v_w), lambda b, g, c, rb=rb: (rb(b, c), n_groups + g)),
                    pl.BlockSpec((SCAN_CHUNK, 2 * n_gate), lambda b, g, c, rb=rb: (rb(b, c), 0)),
                    pl.BlockSpec((1, 2 * n_gate), const),
                    pl.BlockSpec((1, 2 * n_gate), const)]
        args = [qkv, qkv, qkv, tail, prow, arow]
        if not reverse:
            in_specs += [pl.BlockSpec((SCAN_CHUNK, v_w), lambda b, g, c, rb=rb: (rb(b, c), g)),
                         pl.BlockSpec((SCAN_CHUNK, v_w), lambda b, g, c, rb=rb: (rb(b, c), 2 * n_groups + g)),
                         pl.BlockSpec((1, hd), const)]
            args += [y_b, u, norm_g.reshape(1, hd)]
        out = pl.pallas_call(
            functools.partial(_gdn_kernel, reverse=reverse, n_kh=GDN_HEADS_PER_STEP, n_r=n_r, hd=hd),
            grid=(bsz, n_groups, n_steps),
            in_specs=in_specs,
            out_specs=pl.BlockSpec((SCAN_CHUNK, v_w), lambda b, g, c, rb=rb: (rb(b, c), g)),
            out_shape=jax.ShapeDtypeStruct((rows, k_heads * n_r * hd), F32 if reverse else BF16),
            scratch_shapes=[pltpu.VMEM((GDN_HEADS_PER_STEP * n_r, hd, hd), F32)],
            compiler_params=_params("parallel", "parallel", "arbitrary"),
            name="gdn_bwd" if reverse else "gdn_fwd",
        )(*args)
        if reverse:
            y_b = out
    return out


def _ssd_kernel(*refs, reverse, n_r, hd):
    if reverse:
        x_ref, b_ref, c_ref, tail_ref, dtb_ref, alog_ref, o_ref, s_ref = refs
    else:
        x_ref, b_ref, c_ref, tail_ref, dtb_ref, alog_ref, yb_ref, z_ref, d_ref, ng_ref, o_ref, s_ref = refs
    g, c = pl.program_id(1), pl.program_id(2)
    qn = SCAN_CHUNK
    n_lane = tail_ref.shape[1]
    width = n_r * hd

    @pl.when(c == 0)
    def _():
        s_ref[...] = jnp.zeros_like(s_ref)

    dt_all = _softplus(tail_ref[...] + dtb_ref[...])
    la_all = -jnp.exp(alog_ref[...]) * dt_all
    li = lax.broadcasted_iota(jnp.int32, (n_lane, n_r), 0)
    ji = lax.broadcasted_iota(jnp.int32, (n_lane, n_r), 1)
    sel = (li == ji + (n_lane // 2 if reverse else 0) + g * n_r).astype(F32)
    dt, la = _sel_cols(dt_all, sel), _sel_cols(la_all, sel)

    ti = lax.broadcasted_iota(jnp.int32, (qn, 1), 0)
    si = lax.broadcasted_iota(jnp.int32, (1, qn), 1)
    before = (si >= ti) if reverse else (si <= ti)
    cs = before.astype(F32)
    cum = _sel_rows(cs, la)
    cum_t = _sel_rows_t(la, cs)
    ei = lax.broadcasted_iota(jnp.int32, (n_r, width), 0)
    el = lax.broadcasted_iota(jnp.int32, (n_r, width), 1)
    expand = (lax.shift_right_logical(el, int(math.log2(hd))) == ei).astype(F32)
    dt_x, cum_x = _sel_cols(dt, expand), _sel_cols(cum, expand)

    xs = x_ref[...]
    v = xs * dt_x
    bm, cm = b_ref[...].astype(BF16), c_ref[...].astype(BF16)
    scores = _dot_nt(cm, bm)
    lane = lax.broadcasted_iota(jnp.int32, (1, 2 * hd), 1)
    tiles = []
    for p in range(n_r // 2):
        vt = v[:, 2 * p * hd:2 * (p + 1) * hd]
        acc = None
        for h in (2 * p, 2 * p + 1):
            decay = jnp.exp(jnp.where(before, cum[:, h:h + 1] - cum_t[h:h + 1, :], -jnp.inf))
            vh = jnp.where((lane >= hd) if h % 2 else (lane < hd), vt, 0.0).astype(BF16)
            part = jnp.dot((scores * decay).astype(BF16), vh, preferred_element_type=F32)
            acc = part if acc is None else acc + part
        tiles.append(acc)
    state = s_ref[...]
    y = jnp.concatenate(tiles, axis=1) + jnp.dot(cm, state.astype(BF16), preferred_element_type=F32) * jnp.exp(cum_x)
    end = 0 if reverse else qn - 1
    cum_end = cum_x[end:end + 1]
    s_ref[...] = state * jnp.exp(cum_end) + _dot_t(bm, (v * jnp.exp(cum_end - cum_x)).astype(BF16))
    if reverse:
        o_ref[...] = y
    else:
        y = y + yb_ref[...] + d_ref[...] * xs
        z = z_ref[...]
        y = y * (z * jax.nn.sigmoid(z))
        y = y * lax.rsqrt(jnp.mean(y * y, axis=-1, keepdims=True) + NORM_EPS) * ng_ref[...]
        o_ref[...] = y.astype(o_ref.dtype)


def _ssd_mixer(u, tail, conv_w, conv_b, dt_bias, a_log, d_skip, norm_g, bsz, lc, seq):
    n_ctx_c, n_lat_c = _scan_geometry(bsz, lc, seq)
    n_steps = n_ctx_c + n_lat_c
    d_inner = norm_g.shape[0]
    hd, st = SSD_HEAD_DIM, SSD_STATE
    heads = d_inner // hd
    n_r = heads // SSD_GROUPS
    width = n_r * hd
    rows = u.shape[0]
    conv_ch = d_inner + 2 * SSD_GROUPS * st
    xbc = _conv_silu(u, conv_w, conv_b.reshape(1, conv_ch), conv_ch, bsz, lc, seq, col0=d_inner // CONV_COLS)
    d_x = jnp.repeat(d_skip, hd).reshape(1, d_inner)
    y_b = None
    for reverse in (True, False):
        rb = functools.partial(_row_block, bsz=bsz, n_ctx_c=n_ctx_c, n_lat_c=n_lat_c, reverse=reverse)
        const = lambda b, g, c: (0, 0)
        grp = lambda b, g, c: (0, g)
        wide = lambda b, g, c, rb=rb: (rb(b, c), g)
        in_specs = [pl.BlockSpec((SCAN_CHUNK, width), wide),
                    pl.BlockSpec((SCAN_CHUNK, st), lambda b, g, c, rb=rb: (rb(b, c), d_inner // st + g)),
                    pl.BlockSpec((SCAN_CHUNK, st), lambda b, g, c, rb=rb: (rb(b, c), d_inner // st + SSD_GROUPS + g)),
                    pl.BlockSpec((SCAN_CHUNK, 2 * heads), lambda b, g, c, rb=rb: (rb(b, c), 0)),
                    pl.BlockSpec((1, 2 * heads), const),
                    pl.BlockSpec((1, 2 * heads), const)]
        args = [xbc, xbc, xbc, tail, dt_bias.reshape(1, 2 * heads), a_log.reshape(1, 2 * heads)]
        if not reverse:
            in_specs += [pl.BlockSpec((SCAN_CHUNK, width), wide), pl.BlockSpec((SCAN_CHUNK, width), wide),
                         pl.BlockSpec((1, width), grp), pl.BlockSpec((1, width), grp)]
            args += [y_b, u, d_x, norm_g.reshape(1, d_inner)]
        out = pl.pallas_call(
            functools.partial(_ssd_kernel, reverse=reverse, n_r=n_r, hd=hd),
            grid=(bsz, SSD_GROUPS, n_steps),
            in_specs=in_specs,
            out_specs=pl.BlockSpec((SCAN_CHUNK, width), wide),
            out_shape=jax.ShapeDtypeStruct((rows, d_inner), F32 if reverse else BF16),
            scratch_shapes=[pltpu.VMEM((st, width), F32)],
            compiler_params=_params("parallel", "parallel", "arbitrary"),
            name="ssd_bwd" if reverse else "ssd_fwd",
        )(*args)
        if reverse:
            y_b = out
    return out


HG_SUB = 64
HG_BLK = 8


def _log1p(x):
    return jnp.log(1.0 + x)


def _hgrn_kernel(*refs, reverse):
    if reverse:
        q_ref, f_ref, i_ref, lb_ref, o_ref, st_ref = refs
    else:
        q_ref, f_ref, i_ref, lb_ref, yb_ref, g_ref, ng_ref, o_ref, st_ref = refs
    c = pl.program_id(2)
    qn = SCAN_CHUNK
    hd = q_ref.shape[1]
    n_sub, n_blk = qn // HG_SUB, HG_SUB // HG_BLK

    @pl.when(c == 0)
    def _():
        st_ref[...] = jnp.zeros_like(st_ref)

    q, f, v, lb = q_ref[...], f_ref[...], i_ref[...], lb_ref[...]
    log_sig = jnp.minimum(f, 0.0) - _log1p(jnp.exp(-jnp.abs(f)))
    ga, gb = jnp.log(lb), _log1p(-lb) + log_sig
    log_f = jnp.maximum(ga, gb) + _log1p(jnp.exp(-jnp.abs(ga - gb)))
    k = (1.0 - lb) * jax.nn.sigmoid(-f)

    ti = lax.broadcasted_iota(jnp.int32, (qn, 1), 0)
    si = lax.broadcasted_iota(jnp.int32, (1, qn), 1)
    before = (si >= ti) if reverse else (si <= ti)
    same = lambda n: lax.shift_right_logical(ti, int(math.log2(n))) == lax.shift_right_logical(si, int(math.log2(n)))
    cum_sub = _sel_rows((same(HG_SUB) & before).astype(F32), log_f)
    cum_blk = _sel_rows((same(HG_BLK) & before).astype(F32), log_f)
    q_blk = q * jnp.exp(cum_blk)
    q_sub = (q * jnp.exp(cum_sub)).astype(BF16)

    ones = jnp.ones((hd, hd), BF16)
    lane = lax.broadcasted_iota(jnp.int32, (1, hd), 1)
    row_sub = lax.broadcasted_iota(jnp.int32, (HG_SUB, 1), 0)
    row_blk = lax.broadcasted_iota(jnp.int32, (HG_BLK, 1), 0)
    zeros_sub = jnp.zeros((hd - HG_SUB, hd), F32)
    y_intra, kvt, chunk_dec = [], [], []
    for i in range(n_sub):
        r0 = i * HG_SUB
        sl = slice(r0, r0 + HG_SUB)
        cs, ks, vs = cum_sub[sl], k[sl], v[sl]
        end = r0 if reverse else r0 + HG_SUB - 1
        cum_end = cum_sub[end:end + 1]
        kvt.append(_dot_t(vs.astype(BF16), (ks * jnp.exp(cum_end - cs)).astype(BF16)))
        chunk_dec.append(jnp.exp(cum_end))
        a_rows = []
        for a in range(n_blk):
            b0 = r0 + a * HG_BLK
            bl = slice(b0, b0 + HG_BLK)
            qb, kb, cb = q[bl], k[bl], cum_blk[bl]
            tiles = []
            for s in range(HG_BLK):
                ok = (row_blk <= s) if reverse else (row_blk >= s)
                e = jnp.exp(jnp.where(ok, cb - cb[s:s + 1], -jnp.inf))
                tiles.append(qb * e * kb[s:s + 1])
            sums = jnp.dot(jnp.concatenate(tiles, axis=0).astype(BF16), ones, preferred_element_type=F32)
            acc = jnp.zeros((HG_BLK, hd), F32)
            for s in range(HG_BLK):
                acc = acc + jnp.where(lane == a * HG_BLK + s, sums[s * HG_BLK:(s + 1) * HG_BLK], 0.0)
            has_earlier = (a < n_blk - 1) if reverse else (a > 0)
            if has_earlier:
                ref_row = b0 + HG_BLK if reverse else b0 - 1
                earlier = (row_sub >= (a + 1) * HG_BLK) if reverse else (row_sub < a * HG_BLK)
                kt = ks * jnp.exp(jnp.where(earlier, cum_sub[ref_row:ref_row + 1] - cs, -jnp.inf))
                kt = jnp.concatenate([kt, zeros_sub], axis=0).astype(BF16)
                acc = acc + _dot_nt(q_blk[bl].astype(BF16), kt)
            a_rows.append(acc)
        attn = jnp.concatenate(a_rows, axis=0).astype(BF16)
        v_pad = jnp.concatenate([vs, zeros_sub], axis=0).astype(BF16)
        y_intra.append(jnp.dot(attn, v_pad, preferred_element_type=F32))

    state = st_ref[...]
    ys = [None] * n_sub
    for i in (range(n_sub - 1, -1, -1) if reverse else range(n_sub)):
        ys[i] = y_intra[i] + _dot_nt(q_sub[i * HG_SUB:(i + 1) * HG_SUB], state.astype(BF16))
        state = state * chunk_dec[i] + kvt[i]
    st_ref[...] = state
    y = jnp.concatenate(ys, axis=0)
    if reverse:
        o_ref[...] = y
    else:
        y = y + yb_ref[...]
        y = y * lax.rsqrt(jnp.mean(y * y, axis=-1, keepdims=True) + NORM_EPS) * ng_ref[...]
        g = g_ref[...]
        o_ref[...] = (y * (g * jax.nn.sigmoid(g))).astype(o_ref.dtype)


def _hgrn_mixer(u, lb, norm_g, bsz, lc, seq):
    n_ctx_c, n_lat_c = _scan_geometry(bsz, lc, seq)
    n_steps = n_ctx_c + n_lat_c
    hd = HGRN_EXPAND
    d_model = lb.shape[0]
    heads = d_model // hd
    rows = u.shape[0]
    lb2, ng2 = lb.reshape(1, d_model), norm_g.reshape(1, d_model)
    y_b = None
    for reverse in (True, False):
        rb = functools.partial(_row_block, bsz=bsz, n_ctx_c=n_ctx_c, n_lat_c=n_lat_c, reverse=reverse)
        col = lambda seg: (lambda b, h, c, rb=rb: (rb(b, c), seg * heads + h))
        par = lambda b, h, c: (0, h)
        in_specs = [pl.BlockSpec((SCAN_CHUNK, hd), col(0)),
                    pl.BlockSpec((SCAN_CHUNK, hd), col(2 if reverse else 1)),
                    pl.BlockSpec((SCAN_CHUNK, hd), col(3)),
                    pl.BlockSpec((1, hd), par)]
        args = [u, u, u, lb2]
        if not reverse:
            in_specs += [pl.BlockSpec((SCAN_CHUNK, hd), col(0)), pl.BlockSpec((SCAN_CHUNK, hd), col(4)),
                         pl.BlockSpec((1, hd), par)]
            args += [y_b, u, ng2]
        out = pl.pallas_call(
            functools.partial(_hgrn_kernel, reverse=reverse),
            grid=(bsz, heads, n_steps),
            in_specs=in_specs,
            out_specs=pl.BlockSpec((SCAN_CHUNK, hd), col(0)),
            out_shape=jax.ShapeDtypeStruct((rows, d_model), F32 if reverse else BF16),
            scratch_shapes=[pltpu.VMEM((hd, hd), F32)],
            compiler_params=_params("parallel", "parallel", "arbitrary"),
            name="hgrn_bwd" if reverse else "hgrn_fwd",
        )(*args)
        if reverse:
            y_b = out
    return out


def _rope_tables(rows, half):
    pos = np.arange(rows * GRID_W)
    inv_freq = np.float32(ROPE_BASE) ** (-(np.arange(0, half, 2, dtype=np.float32) / np.float32(half)))
    out = []
    for p in ((pos // GRID_W).astype(np.float32), (pos % GRID_W).astype(np.float32)):
        ang = (p[:, None] * inv_freq.astype(np.float32)).astype(np.float32).astype(np.float64)
        out += [np.cos(ang).astype(np.float32), np.sin(ang).astype(np.float32)]
    return out


def _lower_bound(lb_logits, layer):
    p = jax.nn.softmax(lb_logits.astype(F32), axis=0)
    return jnp.cumsum(p, axis=0)[layer] - p[0]


def kernel(x, c, ctx, c_ctx, ada_w, ada_b, norm_g, mlp_w1, mlp_w2, final_g, ssd_w_in, ssd_conv_w, ssd_conv_b,
           ssd_dt_bias, ssd_a_log, ssd_d, ssd_norm_g, ssd_w_out, ret_w_in, ret_log_decay, ret_w_out, hgrn_w_in,
           hgrn_lb_logits, hgrn_norm_g, hgrn_w_out, gdn_w_in, gdn_conv_w, gdn_dt_bias, gdn_a_log, gdn_norm_g,
           gdn_w_out):
    bsz, seq, d = x.shape
    lc = ctx.shape[1]
    depth = ada_w.shape[0]
    n_ctx = bsz * lc
    assert n_ctx % IN_ROW_TILE == 0 and seq % IN_ROW_TILE == 0 and IN_ROW_TILE % ROW_TILE == 0 and bsz + 1 <= 8

    cond_pad = jnp.concatenate([c, c_ctx[None], jnp.zeros((8 - bsz - 1, d), F32)], axis=0)
    mod = _ada_mod(cond_pad, ada_w, ada_b)

    def tile_rows(tile):
        return jnp.asarray([bsz] * (n_ctx // tile) + [b for b in range(bsz) for _ in range(seq // tile)], jnp.int32)

    tile_row, in_tile_row = tile_rows(ROW_TILE), tile_rows(IN_ROW_TILE)
    n_ctx_tiles = n_ctx // ROW_TILE

    xr = jnp.concatenate([ctx.reshape(n_ctx, d), x.reshape(bsz * seq, d)], axis=0)

    for i in range(depth):
        mixer, occ = i % 4, i // 4
        keep_ctx = i < depth - 1
        mod_t = mod[i][tile_row][:, None, :]
        g0, g1 = norm_g[i, 0][None], norm_g[i, 1][None]
        if mixer == 0:
            w_in, w_out = ssd_w_in[occ], ssd_w_out[occ]
        elif mixer == 1:
            w_in, w_out = ret_w_in[occ], ret_w_out[occ]
        elif mixer == 2:
            w_in, w_out = hgrn_w_in[occ], hgrn_w_out[occ]
        else:
            w_in, w_out = gdn_w_in[occ], gdn_w_out[occ]
        n_in = w_in.shape[1]
        n_main = (n_in // 1024) * 1024 if n_in % 1024 else n_in
        w_in = w_in.astype(BF16)
        mod_in = mod[i][in_tile_row][:, None, :]
        u = _ln_mm(xr, g0, mod_in, 0, 1, w_in, 0, n_main)
        tail = _ln_mm(xr, g0, mod_in, 0, 1, w_in, n_main, n_in - n_main) if n_main != n_in else None
        if mixer == 0:
            yr = _ssd_mixer(u, tail, ssd_conv_w[occ], ssd_conv_b[occ], ssd_dt_bias[occ], ssd_a_log[occ], ssd_d[occ],
                            ssd_norm_g[occ], bsz, lc, seq)
        elif mixer == 1:
            yr = _ret_mixer(u, ret_log_decay[occ], bsz, lc, seq, d)
        elif mixer == 2:
            yr = _hgrn_mixer(u, _lower_bound(hgrn_lb_logits, i), hgrn_norm_g[occ], bsz, lc, seq)
        else:
            yr = _gdn_mixer(u, tail, gdn_conv_w[occ], gdn_dt_bias[occ], gdn_a_log[occ], gdn_norm_g[occ], bsz, lc,
                            seq, d)
        first_tile = 0 if keep_ctx else n_ctx_tiles
        xr = _out_proj(yr, w_out.astype(BF16), xr, mod_t, 2, first_tile)
        mod_t = mod_t[first_tile:]
        xr = _mlp(xr, g1, mod_t, mlp_w1[i].astype(BF16), mlp_w2[i].astype(BF16), final_g[None], final=not keep_ctx)
    return xr.reshape(bsz, seq, d)
```

```python
import functools
import math

import jax
import jax.numpy as jnp
import numpy as np
from jax import lax
from jax.experimental import pallas as pl
from jax.experimental.pallas import tpu as pltpu

F32 = jnp.float32
BF16 = jnp.bfloat16

GRID_W = 64
CHUNK = 64
CONV_W = 3
NORM_EPS = 1e-6
ROPE_BASE = 10000.0
SSD_HEAD_DIM = 64
SSD_GROUPS = 8
SSD_STATE = 128
RET_HEADS = 8
HGRN_EXPAND = 128
GDN_HEAD_DIM = 128

ROW_TILE = 512
IN_ROW_TILE = 1024
MLP_FF_TILE = 1024
OUT_COL_TILE = 1024
VMEM_LIMIT = 56 * 1024 * 1024


def _params(*sem):
    return pltpu.CompilerParams(dimension_semantics=sem, vmem_limit_bytes=VMEM_LIMIT)


def _col_tile(n, cap=1536):
    best = 128
    for t in range(128, cap + 1, 128):
        if n % t == 0:
            best = t
    return best


def _ada_kernel(c_ref, w_ref, b_ref, o_ref):
    c = c_ref[...]
    c = (c * jax.nn.sigmoid(c)).astype(BF16)
    o_ref[0] = jnp.dot(c, w_ref[0].astype(BF16), preferred_element_type=F32) + b_ref[0]


def _ada_mod(cond_pad, ada_w, ada_b):
    depth, d, n = ada_w.shape
    tn = 1024
    return pl.pallas_call(
        _ada_kernel,
        grid=(depth, n // tn),
        in_specs=[pl.BlockSpec((8, d), lambda l, j: (0, 0)),
                  pl.BlockSpec((1, d, tn), lambda l, j: (l, 0, j)),
                  pl.BlockSpec((1, 1, tn), lambda l, j: (l, 0, j))],
        out_specs=pl.BlockSpec((1, 8, tn), lambda l, j: (l, 0, j)),
        out_shape=jax.ShapeDtypeStruct((depth, 8, n), F32),
        compiler_params=_params("parallel", "parallel"),
        name="ada_mod",
    )(cond_pad, ada_w, ada_b.reshape(depth, 1, n))


def _adaln_rows(x, g, sh, sc):
    y = x * lax.rsqrt(jnp.mean(x * x, axis=-1, keepdims=True) + NORM_EPS)
    return y * g * (1.0 + sc) + sh


def _ln_mm_kernel(x_ref, g_ref, sh_ref, sc_ref, w_ref, o_ref, h_ref):
    @pl.when(pl.program_id(1) == 0)
    def _():
        h_ref[...] = _adaln_rows(x_ref[...], g_ref[...], sh_ref[0], sc_ref[0]).astype(BF16)

    o_ref[...] = jnp.dot(h_ref[...], w_ref[...], preferred_element_type=F32).astype(o_ref.dtype)


def _ln_mm(x, g, mod_t, sh_col, sc_col, w, col_start, n, out_dtype=F32):
    m, d = x.shape
    tm, tn = IN_ROW_TILE, _col_tile(n)
    assert col_start % tn == 0
    col0 = col_start // tn
    return pl.pallas_call(
        _ln_mm_kernel,
        grid=(m // tm, n // tn),
        in_specs=[pl.BlockSpec((tm, d), lambda i, j: (i, 0)),
                  pl.BlockSpec((1, d), lambda i, j: (0, 0)),
                  pl.BlockSpec((1, 1, d), lambda i, j: (i, 0, sh_col)),
                  pl.BlockSpec((1, 1, d), lambda i, j: (i, 0, sc_col)),
                  pl.BlockSpec((d, tn), lambda i, j: (0, col0 + j))],
        out_specs=pl.BlockSpec((tm, tn), lambda i, j: (i, j)),
        out_shape=jax.ShapeDtypeStruct((m, n), out_dtype),
        scratch_shapes=[pltpu.VMEM((tm, d), BF16)],
        compiler_params=_params("parallel", "arbitrary"),
        name="adaln_in_proj",
    )(x, g, mod_t, mod_t, w)


def _out_kernel(y_ref, w_ref, x_ref, gate_ref, o_ref):
    o_ref[...] = x_ref[...] + gate_ref[0] * jnp.dot(y_ref[...], w_ref[...], preferred_element_type=F32)


def _out_proj(y, w, x, mod_t, gate_col, first_tile=0):
    k = y.shape[1]
    d = w.shape[1]
    tm, tn = ROW_TILE, OUT_COL_TILE
    m = y.shape[0] - first_tile * tm
    return pl.pallas_call(
        _out_kernel,
        grid=(m // tm, d // tn),
        in_specs=[pl.BlockSpec((tm, k), lambda i, j: (i + first_tile, 0)),
                  pl.BlockSpec((k, tn), lambda i, j: (0, j)),
                  pl.BlockSpec((tm, tn), lambda i, j: (i + first_tile, j)),
                  pl.BlockSpec((1, 1, tn), lambda i, j: (i + first_tile, 0, gate_col * (d // tn) + j))],
        out_specs=pl.BlockSpec((tm, tn), lambda i, j: (i, j)),
        out_shape=jax.ShapeDtypeStruct((m, d), F32),
        compiler_params=_params("parallel", "arbitrary"),
        name="out_proj",
    )(y, w, x, mod_t)


def _mlp_kernel(x_ref, g_ref, sh_ref, sc_ref, gate_ref, w1_ref, w2_ref, fg_ref, o_ref, h_ref, acc_ref, *, final):
    f = pl.program_id(1)

    @pl.when(f == 0)
    def _():
        h_ref[...] = _adaln_rows(x_ref[...], g_ref[...], sh_ref[0], sc_ref[0]).astype(BF16)
        acc_ref[...] = jnp.zeros_like(acc_ref)

    a = jnp.dot(h_ref[...], w1_ref[...], preferred_element_type=F32)
    a = jnp.square(jnp.maximum(a, 0.0)).astype(BF16)
    acc_ref[...] += jnp.dot(a, w2_ref[...], preferred_element_type=F32)

    @pl.when(f == pl.num_programs(1) - 1)
    def _():
        out = x_ref[...] + gate_ref[0] * acc_ref[...]
        if final:
            out = out * lax.rsqrt(jnp.mean(out * out, axis=-1, keepdims=True) + NORM_EPS) * fg_ref[...]
        o_ref[...] = out


def _mlp(x, g, mod_t, w1, w2, final_g, final):
    m, d = x.shape
    ff = w1.shape[1]
    tm, tf = ROW_TILE, MLP_FF_TILE
    return pl.pallas_call(
        functools.partial(_mlp_kernel, final=final),
        grid=(m // tm, ff // tf),
        in_specs=[pl.BlockSpec((tm, d), lambda i, f: (i, 0)),
                  pl.BlockSpec((1, d), lambda i, f: (0, 0)),
                  pl.BlockSpec((1, 1, d), lambda i, f: (i, 0, 3)),
                  pl.BlockSpec((1, 1, d), lambda i, f: (i, 0, 4)),
                  pl.BlockSpec((1, 1, d), lambda i, f: (i, 0, 5)),
                  pl.BlockSpec((d, tf), lambda i, f: (0, f)),
                  pl.BlockSpec((tf, d), lambda i, f: (f, 0)),
                  pl.BlockSpec((1, d), lambda i, f: (0, 0))],
        out_specs=pl.BlockSpec((tm, d), lambda i, f: (i, 0)),
        out_shape=jax.ShapeDtypeStruct((m, d), F32),
        scratch_shapes=[pltpu.VMEM((tm, d), BF16), pltpu.VMEM((tm, d), F32)],
        compiler_params=_params("parallel", "arbitrary"),
        name="adaln_mlp",
    )(x, g, mod_t, mod_t, mod_t, w1, w2, final_g)


SCAN_CHUNK = 256


def _scan_geometry(bsz, lc, seq):
    assert lc % SCAN_CHUNK == 0 and seq % SCAN_CHUNK == 0
    return lc // SCAN_CHUNK, seq // SCAN_CHUNK


def _seg_chunk(c, n_ctx_c, n_lat_c, reverse):
    if reverse:
        return c >= n_ctx_c, jnp.where(c < n_ctx_c, n_ctx_c - 1 - c, n_lat_c - 1 - (c - n_ctx_c))
    return c >= n_ctx_c, jnp.where(c < n_ctx_c, c, c - n_ctx_c)


def _row_block(b, c, bsz, n_ctx_c, n_lat_c, reverse):
    is_lat, j = _seg_chunk(c, n_ctx_c, n_lat_c, reverse)
    return jnp.where(is_lat, bsz * n_ctx_c + b * n_lat_c + j, b * n_ctx_c + j)


def _time_iotas(q):
    t = lax.broadcasted_iota(jnp.int32, (q, 1), 0).astype(F32)
    s = lax.broadcasted_iota(jnp.int32, (1, q), 1).astype(F32)
    return t, s


def _dot_t(a, b):
    return lax.dot_general(a, b, (((0,), (0,)), ((), ())), preferred_element_type=F32)


def _dot_nt(a, b):
    return lax.dot_general(a, b, (((1,), (1,)), ((), ())), preferred_element_type=F32)


def _ret_kernel(*refs, reverse, n_ctx_c, n_heads, k_scale):
    if reverse:
        ld_ref, q_ref, k_ref, v_ref, cos_ref, sin_ref, o_ref, s_ref = refs
    else:
        ld_ref, q_ref, k_ref, v_ref, cos_ref, sin_ref, yb_ref, g_ref, o_ref, s_ref = refs
    c, h = pl.program_id(1), pl.program_id(2)
    qn = SCAN_CHUNK

    @pl.when(c == 0)
    def _():
        s_ref[h] = jnp.zeros(s_ref.shape[1:], F32)

    lg = ld_ref[(n_heads if reverse else 0) + h]
    t, s = _time_iotas(qn)
    is_lat = c >= n_ctx_c
    cos = jnp.where(is_lat, cos_ref[...], 1.0)
    sin = jnp.where(is_lat, sin_ref[...], 0.0)

    def rope(x):
        half = x.shape[1] // 2
        swapped = jnp.concatenate([pltpu.roll(x[:, :half], half // 2, 1), pltpu.roll(x[:, half:], half // 2, 1)], axis=1)
        return x * cos + swapped * sin

    q = rope(q_ref[...]).astype(BF16)
    k = rope(k_ref[...]) * k_scale
    v = v_ref[...].astype(BF16)
    if reverse:
        dmat = jnp.where(s >= t, jnp.exp((s - t) * lg), 0.0)
        q_dec, k_dec = jnp.exp((qn - t) * lg), jnp.exp(t * lg)
    else:
        dmat = jnp.where(t >= s, jnp.exp((t - s) * lg), 0.0)
        q_dec, k_dec = jnp.exp((t + 1.0) * lg), jnp.exp((qn - 1.0 - t) * lg)
    attn = (_dot_nt(q, k.astype(BF16)) * dmat).astype(BF16)
    state = s_ref[h]
    y = jnp.dot(attn, v, preferred_element_type=F32)
    y = y + jnp.dot(q, state.astype(BF16), preferred_element_type=F32) * q_dec
    chunk_dec = jnp.exp(jnp.full((1, 1), qn, F32) * lg)
    s_ref[h] = state * chunk_dec + _dot_t((k * k_dec).astype(BF16), v)
    if reverse:
        o_ref[...] = y
    else:
        y = y + yb_ref[...]
        mu = jnp.mean(y, axis=-1, keepdims=True)
        yc = y - mu
        var = jnp.mean(yc * yc, axis=-1, keepdims=True)
        g = g_ref[...]
        o_ref[...] = (yc * lax.rsqrt(var + NORM_EPS) * (g * jax.nn.sigmoid(g))).astype(o_ref.dtype)


def _ret_mixer(u, log_decay, bsz, lc, seq, d_model):
    n_ctx_c, n_lat_c = _scan_geometry(bsz, lc, seq)
    n_steps = n_ctx_c + n_lat_c
    qk_dim, v_dim = d_model // RET_HEADS, 2 * d_model // RET_HEADS
    rows = u.shape[0]
    cos_r, sin_r, cos_c, sin_c = _rope_tables(seq // GRID_W, qk_dim // 2)
    cos_t = jnp.asarray(np.concatenate([cos_r, cos_r, cos_c, cos_c], axis=1))
    sin_t = jnp.asarray(np.concatenate([-sin_r, sin_r, -sin_c, sin_c], axis=1))
    ld = log_decay.reshape(-1).astype(F32)
    y_b = None
    for reverse in (True, False):
        rb = functools.partial(_row_block, bsz=bsz, n_ctx_c=n_ctx_c, n_lat_c=n_lat_c, reverse=reverse)

        def tab(b, c, h, reverse=reverse):
            is_lat, j = _seg_chunk(c, n_ctx_c, n_lat_c, reverse)
            return (jnp.where(is_lat, j, 0), 0)

        in_specs = [pl.BlockSpec(memory_space=pltpu.SMEM),
                    pl.BlockSpec((SCAN_CHUNK, qk_dim), lambda b, c, h, rb=rb: (rb(b, c), h)),
                    pl.BlockSpec((SCAN_CHUNK, qk_dim), lambda b, c, h, rb=rb: (rb(b, c), RET_HEADS + h)),
                    pl.BlockSpec((SCAN_CHUNK, v_dim), lambda b, c, h, rb=rb: (rb(b, c), RET_HEADS + h)),
                    pl.BlockSpec((SCAN_CHUNK, qk_dim), tab),
                    pl.BlockSpec((SCAN_CHUNK, qk_dim), tab)]
        args = [ld, u, u, u, cos_t, sin_t]
        if not reverse:
            in_specs += [pl.BlockSpec((SCAN_CHUNK, v_dim), lambda b, c, h, rb=rb: (rb(b, c), h)),
                         pl.BlockSpec((SCAN_CHUNK, v_dim), lambda b, c, h, rb=rb: (rb(b, c), 2 * RET_HEADS + h))]
            args += [y_b, u]
        out = pl.pallas_call(
            functools.partial(_ret_kernel, reverse=reverse, n_ctx_c=n_ctx_c, n_heads=RET_HEADS, k_scale=qk_dim ** -0.5),
            grid=(bsz, n_steps, RET_HEADS),
            in_specs=in_specs,
            out_specs=pl.BlockSpec((SCAN_CHUNK, v_dim), lambda b, c, h, rb=rb: (rb(b, c), h)),
            out_shape=jax.ShapeDtypeStruct((rows, RET_HEADS * v_dim), F32 if reverse else BF16),
            scratch_shapes=[pltpu.VMEM((RET_HEADS, qk_dim, v_dim), F32)],
            compiler_params=_params("parallel", "arbitrary", "arbitrary"),
            name="retention_bwd" if reverse else "retention_fwd",
        )(*args)
        if reverse:
            y_b = out
    return out


CONV_COLS = 1024
CONV_ROWS = 8


def _conv_kernel(x_ref, prev_ref, next_ref, w_ref, b_ref, o_ref, *, n_ctx_c, n_lat_c, bsz, n_scaled, n_normed, scale,
                 head_dim):
    rb, cb = pl.program_id(0), pl.program_id(1)
    is_lat = rb >= bsz * n_ctx_c
    j = jnp.where(is_lat, (rb - bsz * n_ctx_c) % n_lat_c, rb % n_ctx_c)
    last = jnp.where(is_lat, n_lat_c - 1, n_ctx_c - 1)
    rows, cols = x_ref.shape
    halo_prev = jnp.where(j > 0, prev_ref[7:8, :], 0.0)
    halo_next = jnp.where(j < last, next_ref[0:1, :], 0.0)
    w0, w1, w2, bias = w_ref[0:1, :], w_ref[1:2, :], w_ref[2:3, :], b_ref[...]
    row = lax.broadcasted_iota(jnp.int32, (CONV_ROWS, 1), 0)

    def run(mult):
        before = halo_prev
        for r0 in range(0, rows, CONV_ROWS):
            x = x_ref[r0:r0 + CONV_ROWS, :]
            after = x_ref[r0 + CONV_ROWS:r0 + CONV_ROWS + 1, :] if r0 + CONV_ROWS < rows else halo_next
            x_prev = jnp.where(row == 0, before, pltpu.roll(x, 1, 0))
            x_next = jnp.where(row == CONV_ROWS - 1, after, pltpu.roll(x, CONV_ROWS - 1, 0))
            y = w0 * x_prev + w1 * x + w2 * x_next + bias
            y = y * jax.nn.sigmoid(y)
            if mult is not None:
                parts = []
                for i in range(cols // head_dim):
                    p = y[:, i * head_dim:(i + 1) * head_dim]
                    parts.append(p * (lax.rsqrt(jnp.sum(p * p, axis=-1, keepdims=True) + 1e-6) * mult))
                y = jnp.concatenate(parts, axis=1)
            o_ref[r0:r0 + CONV_ROWS, :] = y
            before = x[CONV_ROWS - 1:CONV_ROWS, :]

    if n_normed == 0:
        run(None)
    else:
        @pl.when(cb < n_normed)
        def _():
            run(jnp.where(cb < n_scaled, scale, 1.0))

        @pl.when(cb >= n_normed)
        def _():
            run(None)


def _conv_silu(u, w, b, n_cols, bsz, lc, seq, n_scaled=0, n_normed=0, scale=1.0, head_dim=128, col0=0):
    n_ctx_c, n_lat_c = _scan_geometry(bsz, lc, seq)
    rows = u.shape[0]
    n_rb = rows // SCAN_CHUNK
    sub = SCAN_CHUNK // 8
    n_halo = rows // 8
    return pl.pallas_call(
        functools.partial(_conv_kernel, n_ctx_c=n_ctx_c, n_lat_c=n_lat_c, bsz=bsz, n_scaled=n_scaled,
                          n_normed=n_normed, scale=scale, head_dim=head_dim),
        grid=(n_rb, n_cols // CONV_COLS),
        in_specs=[pl.BlockSpec((SCAN_CHUNK, CONV_COLS), lambda r, c: (r, c + col0)),
                  pl.BlockSpec((8, CONV_COLS), lambda r, c: (jnp.maximum(r * sub - 1, 0), c + col0)),
                  pl.BlockSpec((8, CONV_COLS), lambda r, c: (jnp.minimum(r * sub + sub, n_halo - 1), c + col0)),
                  pl.BlockSpec((CONV_W, CONV_COLS), lambda r, c: (0, c)),
                  pl.BlockSpec((1, CONV_COLS), lambda r, c: (0, c))],
        out_specs=pl.BlockSpec((SCAN_CHUNK, CONV_COLS), lambda r, c: (r, c)),
        out_shape=jax.ShapeDtypeStruct((rows, n_cols), F32),
        compiler_params=_params("parallel", "parallel"),
        name="conv_silu",
    )(u, u, u, w, b)


GDN_SUB = 64


GDN_BASE = 8
GDN_HEADS_PER_STEP = 8


def _split3(x):
    x1 = x.astype(BF16)
    r1 = x - x1.astype(F32)
    x2 = r1.astype(BF16)
    return x1, x2, (r1 - x2.astype(F32)).astype(BF16)


def _sel_rows(m, x):
    mb = m.astype(BF16)
    return sum(jnp.dot(mb, p, preferred_element_type=F32) for p in _split3(x))


def _sel_cols(x, m):
    mb = m.astype(BF16)
    return sum(jnp.dot(p, mb, preferred_element_type=F32) for p in _split3(x))


def _sel_rows_t(x, m):
    mb = m.astype(BF16)
    return sum(lax.dot_general(p, mb, (((0,), (1,)), ((), ())), preferred_element_type=F32) for p in _split3(x))


def _mm_bf16(a, b):
    return jnp.dot(a.astype(BF16), b.astype(BF16), preferred_element_type=F32)


def _unit_tri_inverse_minus_eye(mats, ti, si):
    def same(n):
        s = int(math.log2(n))
        return lax.shift_right_logical(ti, s) == lax.shift_right_logical(si, s)

    base = same(GDN_BASE)
    ps = [jnp.where(base, a, 0.0) for a in mats]
    ns = [-p for p in ps]
    for _ in range(int(math.log2(GDN_BASE)) - 1):
        ps = [_mm_bf16(p, p) for p in ps]
        ns = [n + p + _mm_bf16(n, p) for n, p in zip(ns, ps)]
    b = GDN_BASE
    while b < GDN_SUB:
        join = same(2 * b) & jnp.logical_not(same(b))
        cs = [jnp.where(join, a, 0.0) for a in mats]
        ms = [c + _mm_bf16(c, n) for c, n in zip(cs, ns)]
        ns = [n - (m + _mm_bf16(n, m)) for n, m in zip(ns, ms)]
        b *= 2
    return ns


def _softplus(x):
    return jnp.maximum(x, 0.0) + jnp.log(1.0 + jnp.exp(-jnp.abs(x)))


def _gdn_kernel(*refs, reverse, n_kh, n_r, hd):
    if reverse:
        q_ref, k_ref, v_ref, tail_ref, prow_ref, arow_ref, o_ref, s_ref = refs
    else:
        q_ref, k_ref, v_ref, tail_ref, prow_ref, arow_ref, yb_ref, z_ref, ng_ref, o_ref, s_ref = refs
    g, c = pl.program_id(1), pl.program_id(2)
    qn = SCAN_CHUNK
    n_sub = qn // GDN_SUB
    n_gate = tail_ref.shape[1] // 2

    @pl.when(c == 0)
    def _():
        s_ref[...] = jnp.zeros_like(s_ref)

    tail = tail_ref[...]
    lane = lax.broadcasted_iota(jnp.int32, (1, 2 * n_gate), 1)
    gates = jnp.where(lane < n_gate, jax.nn.sigmoid(tail), -jnp.exp(arow_ref[...]) * _softplus(tail + prow_ref[...]))
    nh = n_kh * n_r
    li = lax.broadcasted_iota(jnp.int32, (2 * n_gate, 2 * nh), 0)
    ji = lax.broadcasted_iota(jnp.int32, (2 * n_gate, 2 * nh), 1)
    col = jnp.where(ji < nh, ji, n_gate + ji - nh) + (n_gate // 2 if reverse else 0) + g * nh
    gsel = _sel_cols(gates, (li == col).astype(F32))
    lw = lax.broadcasted_iota(jnp.int32, (2 * n_gate, 2 * nh * hd), 0)
    jw = lax.shift_right_logical(lax.broadcasted_iota(jnp.int32, (2 * n_gate, 2 * nh * hd), 1), int(math.log2(hd)))
    colw = jnp.where(jw < nh, jw, n_gate + jw - nh) + (n_gate // 2 if reverse else 0) + g * nh
    gwide = _sel_cols(gates, (lw == colw).astype(F32))

    ti = lax.broadcasted_iota(jnp.int32, (qn, 1), 0)
    si = lax.broadcasted_iota(jnp.int32, (1, qn), 1)
    shift = int(math.log2(GDN_SUB))
    same = lax.shift_right_logical(ti, shift) == lax.shift_right_logical(si, shift)
    if reverse:
        incl, strict = same & (si >= ti), same & (si > ti)
    else:
        incl, strict = same & (si <= ti), same & (si < ti)
    cs = incl.astype(F32)
    cum_w = _sel_rows(cs, gwide[:, nh * hd:])
    cum_t = _sel_rows_t(gsel, cs)
    reps = qn // hd

    order = range(n_sub - 1, -1, -1) if reverse else range(n_sub)
    outs = []
    shared = []
    for kh in range(n_kh):
        k = k_ref[:, kh * hd:(kh + 1) * hd]
        qb, kb = q_ref[:, kh * hd:(kh + 1) * hd].astype(BF16), k.astype(BF16)
        shared.append((k, qb, _dot_nt(kb, kb), _dot_nt(qb, kb)))
    heads = range(nh)
    beta = [gwide[:, j * hd:(j + 1) * hd] for j in heads]
    cum_c = [cum_w[:, j * hd:(j + 1) * hd] for j in heads]
    decay = [jnp.exp(jnp.where(incl, jnp.concatenate([cum_c[j]] * reps, axis=1) - cum_t[nh + j:nh + j + 1, :],
                               -jnp.inf)) for j in heads]
    amat = [jnp.where(strict, jnp.concatenate([beta[j]] * reps, axis=1) * shared[j // n_r][2] * decay[j], 0.0)
            for j in heads]
    inv_off = _unit_tri_inverse_minus_eye(amat, ti, si)
    rhs = [jnp.concatenate([v_ref[:, j * hd:(j + 1) * hd] * beta[j],
                            shared[j // n_r][0] * (beta[j] * jnp.exp(cum_c[j]))], axis=1) for j in heads]
    sol = [rhs[j] + _mm_bf16(inv_off[j], rhs[j]) for j in heads]
    state = [s_ref[j] for j in heads]
    v_new = [[None] * n_sub for _ in heads]
    inter = [[None] * n_sub for _ in heads]
    for i in order:
        sl = slice(i * GDN_SUB, (i + 1) * GDN_SUB)
        end = i * GDN_SUB if reverse else (i + 1) * GDN_SUB - 1
        sb = [state[j].astype(BF16) for j in heads]
        for j in heads:
            v_new[j][i] = sol[j][sl, :hd] - jnp.dot(sol[j][sl, hd:].astype(BF16), sb[j], preferred_element_type=F32)
            inter[j][i] = jnp.dot(shared[j // n_r][1][sl], sb[j], preferred_element_type=F32)
        for j in heads:
            cum_end = cum_c[j][end:end + 1, :]
            k_end = (shared[j // n_r][0][sl] * jnp.exp(cum_end - cum_c[j][sl])).astype(BF16)
            state[j] = jnp.exp(cum_end) * state[j] + _dot_t(k_end, v_new[j][i].astype(BF16))
    outs = []
    for j in heads:
        s_ref[j] = state[j]
        attn = (shared[j // n_r][3] * decay[j]).astype(BF16)
        y = jnp.dot(attn, jnp.concatenate(v_new[j], axis=0).astype(BF16), preferred_element_type=F32)
        y = y + jnp.concatenate(inter[j], axis=0) * jnp.exp(cum_c[j])
        if not reverse:
            y = y + yb_ref[:, j * hd:(j + 1) * hd]
            y = y * lax.rsqrt(jnp.mean(y * y, axis=-1, keepdims=True) + NORM_EPS) * ng_ref[...]
            z = z_ref[:, j * hd:(j + 1) * hd]
            y = y * (z * jax.nn.sigmoid(z))
        outs.append(y)
    o_ref[...] = jnp.concatenate(outs, axis=1).astype(o_ref.dtype)


def _gdn_mixer(u, tail, conv_w, dt_bias, a_log, norm_g, bsz, lc, seq, d_model):
    n_ctx_c, n_lat_c = _scan_geometry(bsz, lc, seq)
    n_steps = n_ctx_c + n_lat_c
    hd = GDN_HEAD_DIM
    k_heads = d_model // hd
    n_r = 2
    rows = u.shape[0]
    conv_ch = 4 * d_model
    nq = d_model // CONV_COLS
    qkv = _conv_silu(u, conv_w, jnp.zeros((1, conv_ch), F32), conv_ch, bsz, lc, seq, n_scaled=nq, n_normed=2 * nq,
                     scale=hd ** -0.5, head_dim=hd)
    n_gate = tail.shape[1] // 2
    prow = jnp.concatenate([jnp.zeros((1, n_gate), F32), dt_bias.reshape(1, n_gate)], axis=1)
    arow = jnp.concatenate([jnp.zeros((1, n_gate), F32), a_log.reshape(1, n_gate)], axis=1)
    y_b = None
    for reverse in (True, False):
        rb = functools.partial(_row_block, bsz=bsz, n_ctx_c=n_ctx_c, n_lat_c=n_lat_c, reverse=reverse)
        const = lambda b, g, c: (0, 0)
        qk_w, v_w = GDN_HEADS_PER_STEP * hd, GDN_HEADS_PER_STEP * n_r * hd
        n_groups = k_heads // GDN_HEADS_PER_STEP
        in_specs = [pl.BlockSpec((SCAN_CHUNK, qk_w), lambda b, g, c, rb=rb: (rb(b, c), g)),
                    pl.BlockSpec((SCAN_CHUNK, qk_w), lambda b, g, c, rb=rb: (rb(b, c), n_groups + g)),
                    pl.BlockSpec((SCAN_CHUNK, v_w), lambda b, g, c, rb=rb: (rb(b, c), n_groups + g)),
                    pl.BlockSpec((SCAN_CHUNK, 2 * n_gate), lambda b, g, c, rb=rb: (rb(b, c), 0)),
                    pl.BlockSpec((1, 2 * n_gate), const),
                    pl.BlockSpec((1, 2 * n_gate), const)]
        args = [qkv, qkv, qkv, tail, prow, arow]
        if not reverse:
            in_specs += [pl.BlockSpec((SCAN_CHUNK, v_w), lambda b, g, c, rb=rb: (rb(b, c), g)),
                         pl.BlockSpec((SCAN_CHUNK, v_w), lambda b, g, c, rb=rb: (rb(b, c), 2 * n_groups + g)),
                         pl.BlockSpec((1, hd), const)]
            args += [y_b, u, norm_g.reshape(1, hd)]
        out = pl.pallas_call(
            functools.partial(_gdn_kernel, reverse=reverse, n_kh=GDN_HEADS_PER_STEP, n_r=n_r, hd=hd),
            grid=(bsz, n_groups, n_steps),
            in_specs=in_specs,
            out_specs=pl.BlockSpec((SCAN_CHUNK, v_w), lambda b, g, c, rb=rb: (rb(b, c), g)),
            out_shape=jax.ShapeDtypeStruct((rows, k_heads * n_r * hd), F32 if reverse else BF16),
            scratch_shapes=[pltpu.VMEM((GDN_HEADS_PER_STEP * n_r, hd, hd), F32)],
            compiler_params=_params("parallel", "parallel", "arbitrary"),
            name="gdn_bwd" if reverse else "gdn_fwd",
        )(*args)
        if reverse:
            y_b = out
    return out


def _ssd_kernel(*refs, reverse, n_r, hd):
    if reverse:
        x_ref, b_ref, c_ref, tail_ref, dtb_ref, alog_ref, o_ref, s_ref = refs
    else:
        x_ref, b_ref, c_ref, tail_ref, dtb_ref, alog_ref, yb_ref, z_ref, d_ref, ng_ref, o_ref, s_ref = refs
    g, c = pl.program_id(1), pl.program_id(2)
    qn = SCAN_CHUNK
    n_lane = tail_ref.shape[1]
    width = n_r * hd

    @pl.when(c == 0)
    def _():
        s_ref[...] = jnp.zeros_like(s_ref)

    dt_all = _softplus(tail_ref[...] + dtb_ref[...])
    la_all = -jnp.exp(alog_ref[...]) * dt_all
    li = lax.broadcasted_iota(jnp.int32, (n_lane, n_r), 0)
    ji = lax.broadcasted_iota(jnp.int32, (n_lane, n_r), 1)
    sel = (li == ji + (n_lane // 2 if reverse else 0) + g * n_r).astype(F32)
    dt, la = _sel_cols(dt_all, sel), _sel_cols(la_all, sel)

    ti = lax.broadcasted_iota(jnp.int32, (qn, 1), 0)
    si = lax.broadcasted_iota(jnp.int32, (1, qn), 1)
    before = (si >= ti) if reverse else (si <= ti)
    cs = before.astype(F32)
    cum = _sel_rows(cs, la)
    cum_t = _sel_rows_t(la, cs)
    ei = lax.broadcasted_iota(jnp.int32, (n_r, width), 0)
    el = lax.broadcasted_iota(jnp.int32, (n_r, width), 1)
    expand = (lax.shift_right_logical(el, int(math.log2(hd))) == ei).astype(F32)
    dt_x, cum_x = _sel_cols(dt, expand), _sel_cols(cum, expand)

    xs = x_ref[...]
    v = xs * dt_x
    bm, cm = b_ref[...].astype(BF16), c_ref[...].astype(BF16)
    scores = _dot_nt(cm, bm)
    lane = lax.broadcasted_iota(jnp.int32, (1, 2 * hd), 1)
    tiles = []
    for p in range(n_r // 2):
        vt = v[:, 2 * p * hd:2 * (p + 1) * hd]
        acc = None
        for h in (2 * p, 2 * p + 1):
            decay = jnp.exp(jnp.where(before, cum[:, h:h + 1] - cum_t[h:h + 1, :], -jnp.inf))
            vh = jnp.where((lane >= hd) if h % 2 else (lane < hd), vt, 0.0).astype(BF16)
            part = jnp.dot((scores * decay).astype(BF16), vh, preferred_element_type=F32)
            acc = part if acc is None else acc + part
        tiles.append(acc)
    state = s_ref[...]
    y = jnp.concatenate(tiles, axis=1) + jnp.dot(cm, state.astype(BF16), preferred_element_type=F32) * jnp.exp(cum_x)
    end = 0 if reverse else qn - 1
    cum_end = cum_x[end:end + 1]
    s_ref[...] = state * jnp.exp(cum_end) + _dot_t(bm, (v * jnp.exp(cum_end - cum_x)).astype(BF16))
    if reverse:
        o_ref[...] = y
    else:
        y = y + yb_ref[...] + d_ref[...] * xs
        z = z_ref[...]
        y = y * (z * jax.nn.sigmoid(z))
        y = y * lax.rsqrt(jnp.mean(y * y, axis=-1, keepdims=True) + NORM_EPS) * ng_ref[...]
        o_ref[...] = y.astype(o_ref.dtype)


def _ssd_mixer(u, tail, conv_w, conv_b, dt_bias, a_log, d_skip, norm_g, bsz, lc, seq):
    n_ctx_c, n_lat_c = _scan_geometry(bsz, lc, seq)
    n_steps = n_ctx_c + n_lat_c
    d_inner = norm_g.shape[0]
    hd, st = SSD_HEAD_DIM, SSD_STATE
    heads = d_inner // hd
    n_r = heads // SSD_GROUPS
    width = n_r * hd
    rows = u.shape[0]
    conv_ch = d_inner + 2 * SSD_GROUPS * st
    xbc = _conv_silu(u, conv_w, conv_b.reshape(1, conv_ch), conv_ch, bsz, lc, seq, col0=d_inner // CONV_COLS)
    d_x = jnp.repeat(d_skip, hd).reshape(1, d_inner)
    y_b = None
    for reverse in (True, False):
        rb = functools.partial(_row_block, bsz=bsz, n_ctx_c=n_ctx_c, n_lat_c=n_lat_c, reverse=reverse)
        const = lambda b, g, c: (0, 0)
        grp = lambda b, g, c: (0, g)
        wide = lambda b, g, c, rb=rb: (rb(b, c), g)
        in_specs = [pl.BlockSpec((SCAN_CHUNK, width), wide),
                    pl.BlockSpec((SCAN_CHUNK, st), lambda b, g, c, rb=rb: (rb(b, c), d_inner // st + g)),
                    pl.BlockSpec((SCAN_CHUNK, st), lambda b, g, c, rb=rb: (rb(b, c), d_inner // st + SSD_GROUPS + g)),
                    pl.BlockSpec((SCAN_CHUNK, 2 * heads), lambda b, g, c, rb=rb: (rb(b, c), 0)),
                    pl.BlockSpec((1, 2 * heads), const),
                    pl.BlockSpec((1, 2 * heads), const)]
        args = [xbc, xbc, xbc, tail, dt_bias.reshape(1, 2 * heads), a_log.reshape(1, 2 * heads)]
        if not reverse:
            in_specs += [pl.BlockSpec((SCAN_CHUNK, width), wide), pl.BlockSpec((SCAN_CHUNK, width), wide),
                         pl.BlockSpec((1, width), grp), pl.BlockSpec((1, width), grp)]
            args += [y_b, u, d_x, norm_g.reshape(1, d_inner)]
        out = pl.pallas_call(
            functools.partial(_ssd_kernel, reverse=reverse, n_r=n_r, hd=hd),
            grid=(bsz, SSD_GROUPS, n_steps),
            in_specs=in_specs,
            out_specs=pl.BlockSpec((SCAN_CHUNK, width), wide),
            out_shape=jax.ShapeDtypeStruct((rows, d_inner), F32 if reverse else BF16),
            scratch_shapes=[pltpu.VMEM((st, width), F32)],
            compiler_params=_params("parallel", "parallel", "arbitrary"),
            name="ssd_bwd" if reverse else "ssd_fwd",
        )(*args)
        if reverse:
            y_b = out
    return out


HG_SUB = 64
HG_BLK = 8


def _log1p(x):
    return jnp.log(1.0 + x)


def _hgrn_kernel(*refs, reverse):
    if reverse:
        q_ref, f_ref, i_ref, lb_ref, o_ref, st_ref = refs
    else:
        q_ref, f_ref, i_ref, lb_ref, yb_ref, g_ref, ng_ref, o_ref, st_ref = refs
    c = pl.program_id(2)
    qn = SCAN_CHUNK
    hd = q_ref.shape[1]
    n_sub, n_blk = qn // HG_SUB, HG_SUB // HG_BLK

    @pl.when(c == 0)
    def _():
        st_ref[...] = jnp.zeros_like(st_ref)

    q, f, v, lb = q_ref[...], f_ref[...], i_ref[...], lb_ref[...]
    log_sig = jnp.minimum(f, 0.0) - _log1p(jnp.exp(-jnp.abs(f)))
    ga, gb = jnp.log(lb), _log1p(-lb) + log_sig
    log_f = jnp.maximum(ga, gb) + _log1p(jnp.exp(-jnp.abs(ga - gb)))
    k = (1.0 - lb) * jax.nn.sigmoid(-f)

    ti = lax.broadcasted_iota(jnp.int32, (qn, 1), 0)
    si = lax.broadcasted_iota(jnp.int32, (1, qn), 1)
    before = (si >= ti) if reverse else (si <= ti)
    same = lambda n: lax.shift_right_logical(ti, int(math.log2(n))) == lax.shift_right_logical(si, int(math.log2(n)))
    cum_sub = _sel_rows((same(HG_SUB) & before).astype(F32), log_f)
    cum_blk = _sel_rows((same(HG_BLK) & before).astype(F32), log_f)
    q_blk = q * jnp.exp(cum_blk)
    q_sub = (q * jnp.exp(cum_sub)).astype(BF16)

    ones = jnp.ones((hd, hd), BF16)
    lane = lax.broadcasted_iota(jnp.int32, (1, hd), 1)
    row_sub = lax.broadcasted_iota(jnp.int32, (HG_SUB, 1), 0)
    row_blk = lax.broadcasted_iota(jnp.int32, (HG_BLK, 1), 0)
    zeros_sub = jnp.zeros((hd - HG_SUB, hd), F32)
    y_intra, kvt, chunk_dec = [], [], []
    for i in range(n_sub):
        r0 = i * HG_SUB
        sl = slice(r0, r0 + HG_SUB)
        cs, ks, vs = cum_sub[sl], k[sl], v[sl]
        end = r0 if reverse else r0 + HG_SUB - 1
        cum_end = cum_sub[end:end + 1]
        kvt.append(_dot_t(vs.astype(BF16), (ks * jnp.exp(cum_end - cs)).astype(BF16)))
        chunk_dec.append(jnp.exp(cum_end))
        a_rows = []
        for a in range(n_blk):
            b0 = r0 + a * HG_BLK
            bl = slice(b0, b0 + HG_BLK)
            qb, kb, cb = q[bl], k[bl], cum_blk[bl]
            tiles = []
            for s in range(HG_BLK):
                ok = (row_blk <= s) if reverse else (row_blk >= s)
                e = jnp.exp(jnp.where(ok, cb - cb[s:s + 1], -jnp.inf))
                tiles.append(qb * e * kb[s:s + 1])
            sums = jnp.dot(jnp.concatenate(tiles, axis=0).astype(BF16), ones, preferred_element_type=F32)
            acc = jnp.zeros((HG_BLK, hd), F32)
            for s in range(HG_BLK):
                acc = acc + jnp.where(lane == a * HG_BLK + s, sums[s * HG_BLK:(s + 1) * HG_BLK], 0.0)
            has_earlier = (a < n_blk - 1) if reverse else (a > 0)
            if has_earlier:
                ref_row = b0 + HG_BLK if reverse else b0 - 1
                earlier = (row_sub >= (a + 1) * HG_BLK) if reverse else (row_sub < a * HG_BLK)
                kt = ks * jnp.exp(jnp.where(earlier, cum_sub[ref_row:ref_row + 1] - cs, -jnp.inf))
                kt = jnp.concatenate([kt, zeros_sub], axis=0).astype(BF16)
                acc = acc + _dot_nt(q_blk[bl].astype(BF16), kt)
            a_rows.append(acc)
        attn = jnp.concatenate(a_rows, axis=0).astype(BF16)
        v_pad = jnp.concatenate([vs, zeros_sub], axis=0).astype(BF16)
        y_intra.append(jnp.dot(attn, v_pad, preferred_element_type=F32))

    state = st_ref[...]
    ys = [None] * n_sub
    for i in (range(n_sub - 1, -1, -1) if reverse else range(n_sub)):
        ys[i] = y_intra[i] + _dot_nt(q_sub[i * HG_SUB:(i + 1) * HG_SUB], state.astype(BF16))
        state = state * chunk_dec[i] + kvt[i]
    st_ref[...] = state
    y = jnp.concatenate(ys, axis=0)
    if reverse:
        o_ref[...] = y
    else:
        y = y + yb_ref[...]
        y = y * lax.rsqrt(jnp.mean(y * y, axis=-1, keepdims=True) + NORM_EPS) * ng_ref[...]
        g = g_ref[...]
        o_ref[...] = (y * (g * jax.nn.sigmoid(g))).astype(o_ref.dtype)


def _hgrn_mixer(u, lb, norm_g, bsz, lc, seq):
    n_ctx_c, n_lat_c = _scan_geometry(bsz, lc, seq)
    n_steps = n_ctx_c + n_lat_c
    hd = HGRN_EXPAND
    d_model = lb.shape[0]
    heads = d_model // hd
    rows = u.shape[0]
    lb2, ng2 = lb.reshape(1, d_model), norm_g.reshape(1, d_model)
    y_b = None
    for reverse in (True, False):
        rb = functools.partial(_row_block, bsz=bsz, n_ctx_c=n_ctx_c, n_lat_c=n_lat_c, reverse=reverse)
        col = lambda seg: (lambda b, h, c, rb=rb: (rb(b, c), seg * heads + h))
        par = lambda b, h, c: (0, h)
        in_specs = [pl.BlockSpec((SCAN_CHUNK, hd), col(0)),
                    pl.BlockSpec((SCAN_CHUNK, hd), col(2 if reverse else 1)),
                    pl.BlockSpec((SCAN_CHUNK, hd), col(3)),
                    pl.BlockSpec((1, hd), par)]
        args = [u, u, u, lb2]
        if not reverse:
            in_specs += [pl.BlockSpec((SCAN_CHUNK, hd), col(0)), pl.BlockSpec((SCAN_CHUNK, hd), col(4)),
                         pl.BlockSpec((1, hd), par)]
            args += [y_b, u, ng2]
        out = pl.pallas_call(
            functools.partial(_hgrn_kernel, reverse=reverse),
            grid=(bsz, heads, n_steps),
            in_specs=in_specs,
            out_specs=pl.BlockSpec((SCAN_CHUNK, hd), col(0)),
            out_shape=jax.ShapeDtypeStruct((rows, d_model), F32 if reverse else BF16),
            scratch_shapes=[pltpu.VMEM((hd, hd), F32)],
            compiler_params=_params("parallel", "parallel", "arbitrary"),
            name="hgrn_bwd" if reverse else "hgrn_fwd",
        )(*args)
        if reverse:
            y_b = out
    return out


def _rope_tables(rows, half):
    pos = np.arange(rows * GRID_W)
    inv_freq = np.float32(ROPE_BASE) ** (-(np.arange(0, half, 2, dtype=np.float32) / np.float32(half)))
    out = []
    for p in ((pos // GRID_W).astype(np.float32), (pos % GRID_W).astype(np.float32)):
        ang = (p[:, None] * inv_freq.astype(np.float32)).astype(np.float32).astype(np.float64)
        out += [np.cos(ang).astype(np.float32), np.sin(ang).astype(np.float32)]
    return out


def _lower_bound(lb_logits, layer):
    p = jax.nn.softmax(lb_logits.astype(F32), axis=0)
    return jnp.cumsum(p, axis=0)[layer] - p[0]


def kernel(x, c, ctx, c_ctx, ada_w, ada_b, norm_g, mlp_w1, mlp_w2, final_g, ssd_w_in, ssd_conv_w, ssd_conv_b,
           ssd_dt_bias, ssd_a_log, ssd_d, ssd_norm_g, ssd_w_out, ret_w_in, ret_log_decay, ret_w_out, hgrn_w_in,
           hgrn_lb_logits, hgrn_norm_g, hgrn_w_out, gdn_w_in, gdn_conv_w, gdn_dt_bias, gdn_a_log, gdn_norm_g,
           gdn_w_out):
    bsz, seq, d = x.shape
    lc = ctx.shape[1]
    depth = ada_w.shape[0]
    n_ctx = bsz * lc
    assert n_ctx % IN_ROW_TILE == 0 and seq % IN_ROW_TILE == 0 and IN_ROW_TILE % ROW_TILE == 0 and bsz + 1 <= 8

    cond_pad = jnp.concatenate([c, c_ctx[None], jnp.zeros((8 - bsz - 1, d), F32)], axis=0)
    mod = _ada_mod(cond_pad, ada_w, ada_b)

    def tile_rows(tile):
        return jnp.asarray([bsz] * (n_ctx // tile) + [b for b in range(bsz) for _ in range(seq // tile)], jnp.int32)

    tile_row, in_tile_row = tile_rows(ROW_TILE), tile_rows(IN_ROW_TILE)
    n_ctx_tiles = n_ctx // ROW_TILE

    xr = jnp.concatenate([ctx.reshape(n_ctx, d), x.reshape(bsz * seq, d)], axis=0)

    for i in range(depth):
        mixer, occ = i % 4, i // 4
        keep_ctx = i < depth - 1
        mod_t = mod[i][tile_row][:, None, :]
        g0, g1 = norm_g[i, 0][None], norm_g[i, 1][None]
        if mixer == 0:
            w_in, w_out = ssd_w_in[occ], ssd_w_out[occ]
        elif mixer == 1:
            w_in, w_out = ret_w_in[occ], ret_w_out[occ]
        elif mixer == 2:
            w_in, w_out = hgrn_w_in[occ], hgrn_w_out[occ]
        else:
            w_in, w_out = gdn_w_in[occ], gdn_w_out[occ]
        n_in = w_in.shape[1]
        n_main = (n_in // 1024) * 1024 if n_in % 1024 else n_in
        w_in = w_in.astype(BF16)
        mod_in = mod[i][in_tile_row][:, None, :]
        u = _ln_mm(xr, g0, mod_in, 0, 1, w_in, 0, n_main)
        tail = _ln_mm(xr, g0, mod_in, 0, 1, w_in, n_main, n_in - n_main) if n_main != n_in else None
        if mixer == 0:
            yr = _ssd_mixer(u, tail, ssd_conv_w[occ], ssd_conv_b[occ], ssd_dt_bias[occ], ssd_a_log[occ], ssd_d[occ],
                            ssd_norm_g[occ], bsz, lc, seq)
        elif mixer == 1:
            yr = _ret_mixer(u, ret_log_decay[occ], bsz, lc, seq, d)
        elif mixer == 2:
            yr = _hgrn_mixer(u, _lower_bound(hgrn_lb_logits, i), hgrn_norm_g[occ], bsz, lc, seq)
        else:
            yr = _gdn_mixer(u, tail, gdn_conv_w[occ], gdn_dt_bias[occ], gdn_a_log[occ], gdn_norm_g[occ], bsz, lc,
                            seq, d)
        first_tile = 0 if keep_ctx else n_ctx_tiles
        xr = _out_proj(yr, w_out.astype(BF16), xr, mod_t, 2, first_tile)
        mod_t = mod_t[first_tile:]
        xr = _mlp(xr, g1, mod_t, mlp_w1[i].astype(BF16), mlp_w2[i].astype(BF16), final_g[None], final=not keep_ctx)
    return xr.reshape(bsz, seq, d)
```

```python
import functools
import math

import jax
import jax.numpy as jnp
import numpy as np
from jax import lax
from jax.experimental import pallas as pl
from jax.experimental.pallas import tpu as pltpu

F32 = jnp.float32
BF16 = jnp.bfloat16

GRID_W = 64
CHUNK = 64
CONV_W = 3
NORM_EPS = 1e-6
ROPE_BASE = 10000.0
SSD_HEAD_DIM = 64
SSD_GROUPS = 8
SSD_STATE = 128
RET_HEADS = 8
HGRN_EXPAND = 128
GDN_HEAD_DIM = 128

ROW_TILE = 512
IN_ROW_TILE = 1024
MLP_FF_TILE = 1024
OUT_COL_TILE = 1024
VMEM_LIMIT = 56 * 1024 * 1024


def _params(*sem):
    return pltpu.CompilerParams(dimension_semantics=sem, vmem_limit_bytes=VMEM_LIMIT)


def _col_tile(n, cap=1536):
    best = 128
    for t in range(128, cap + 1, 128):
        if n % t == 0:
            best = t
    return best


def _ada_kernel(c_ref, w_ref, b_ref, o_ref):
    c = c_ref[...]
    c = (c * jax.nn.sigmoid(c)).astype(BF16)
    o_ref[0] = jnp.dot(c, w_ref[0].astype(BF16), preferred_element_type=F32) + b_ref[0]


def _ada_mod(cond_pad, ada_w, ada_b):
    depth, d, n = ada_w.shape
    tn = 1024
    return pl.pallas_call(
        _ada_kernel,
        grid=(depth, n // tn),
        in_specs=[pl.BlockSpec((8, d), lambda l, j: (0, 0)),
                  pl.BlockSpec((1, d, tn), lambda l, j: (l, 0, j)),
                  pl.BlockSpec((1, 1, tn), lambda l, j: (l, 0, j))],
        out_specs=pl.BlockSpec((1, 8, tn), lambda l, j: (l, 0, j)),
        out_shape=jax.ShapeDtypeStruct((depth, 8, n), F32),
        compiler_params=_params("parallel", "parallel"),
        name="ada_mod",
    )(cond_pad, ada_w, ada_b.reshape(depth, 1, n))


def _adaln_rows(x, g, sh, sc):
    y = x * lax.rsqrt(jnp.mean(x * x, axis=-1, keepdims=True) + NORM_EPS)
    return y * g * (1.0 + sc) + sh


def _ln_mm_kernel(x_ref, g_ref, sh_ref, sc_ref, w_ref, o_ref, h_ref):
    @pl.when(pl.program_id(1) == 0)
    def _():
        h_ref[...] = _adaln_rows(x_ref[...], g_ref[...], sh_ref[0], sc_ref[0]).astype(BF16)

    o_ref[...] = jnp.dot(h_ref[...], w_ref[...], preferred_element_type=F32).astype(o_ref.dtype)


def _ln_mm(x, g, mod_t, sh_col, sc_col, w, col_start, n, out_dtype=F32):
    m, d = x.shape
    tm, tn = IN_ROW_TILE, _col_tile(n)
    assert col_start % tn == 0
    col0 = col_start // tn
    return pl.pallas_call(
        _ln_mm_kernel,
        grid=(m // tm, n // tn),
        in_specs=[pl.BlockSpec((tm, d), lambda i, j: (i, 0)),
                  pl.BlockSpec((1, d), lambda i, j: (0, 0)),
                  pl.BlockSpec((1, 1, d), lambda i, j: (i, 0, sh_col)),
                  pl.BlockSpec((1, 1, d), lambda i, j: (i, 0, sc_col)),
                  pl.BlockSpec((d, tn), lambda i, j: (0, col0 + j))],
        out_specs=pl.BlockSpec((tm, tn), lambda i, j: (i, j)),
        out_shape=jax.ShapeDtypeStruct((m, n), out_dtype),
        scratch_shapes=[pltpu.VMEM((tm, d), BF16)],
        compiler_params=_params("parallel", "arbitrary"),
        name="adaln_in_proj",
    )(x, g, mod_t, mod_t, w)


def _out_kernel(y_ref, w_ref, x_ref, gate_ref, o_ref):
    o_ref[...] = x_ref[...] + gate_ref[0] * jnp.dot(y_ref[...], w_ref[...], preferred_element_type=F32)


def _out_proj(y, w, x, mod_t, gate_col, first_tile=0):
    k = y.shape[1]
    d = w.shape[1]
    tm, tn = ROW_TILE, OUT_COL_TILE
    m = y.shape[0] - first_tile * tm
    return pl.pallas_call(
        _out_kernel,
        grid=(m // tm, d // tn),
        in_specs=[pl.BlockSpec((tm, k), lambda i, j: (i + first_tile, 0)),
                  pl.BlockSpec((k, tn), lambda i, j: (0, j)),
                  pl.BlockSpec((tm, tn), lambda i, j: (i + first_tile, j)),
                  pl.BlockSpec((1, 1, tn), lambda i, j: (i + first_tile, 0, gate_col * (d // tn) + j))],
        out_specs=pl.BlockSpec((tm, tn), lambda i, j: (i, j)),
        out_shape=jax.ShapeDtypeStruct((m, d), F32),
        compiler_params=_params("parallel", "arbitrary"),
        name="out_proj",
    )(y, w, x, mod_t)


def _mlp_kernel(x_ref, g_ref, sh_ref, sc_ref, gate_ref, w1_ref, w2_ref, fg_ref, o_ref, h_ref, acc_ref, *, final):
    f = pl.program_id(1)

    @pl.when(f == 0)
    def _():
        h_ref[...] = _adaln_rows(x_ref[...], g_ref[...], sh_ref[0], sc_ref[0]).astype(BF16)
        acc_ref[...] = jnp.zeros_like(acc_ref)

    a = jnp.dot(h_ref[...], w1_ref[0], preferred_element_type=F32)
    a = jnp.square(jnp.maximum(a, 0.0)).astype(BF16)
    acc_ref[...] += jnp.dot(a, w2_ref[0], preferred_element_type=F32)

    @pl.when(f == pl.num_programs(1) - 1)
    def _():
        out = x_ref[...] + gate_ref[0] * acc_ref[...]
        if final:
            out = out * lax.rsqrt(jnp.mean(out * out, axis=-1, keepdims=True) + NORM_EPS) * fg_ref[...]
        o_ref[...] = out


def _mlp(x, g, mod_t, w1, w2, layer, final_g, final):
    m, d = x.shape
    ff = w1.shape[2]
    tm, tf = ROW_TILE, MLP_FF_TILE
    return pl.pallas_call(
        functools.partial(_mlp_kernel, final=final),
        grid=(m // tm, ff // tf),
        in_specs=[pl.BlockSpec((tm, d), lambda i, f: (i, 0)),
                  pl.BlockSpec((1, d), lambda i, f: (0, 0)),
                  pl.BlockSpec((1, 1, d), lambda i, f: (i, 0, 3)),
                  pl.BlockSpec((1, 1, d), lambda i, f: (i, 0, 4)),
                  pl.BlockSpec((1, 1, d), lambda i, f: (i, 0, 5)),
                  pl.BlockSpec((1, d, tf), lambda i, f: (layer, 0, f)),
                  pl.BlockSpec((1, tf, d), lambda i, f: (layer, f, 0)),
                  pl.BlockSpec((1, d), lambda i, f: (0, 0))],
        out_specs=pl.BlockSpec((tm, d), lambda i, f: (i, 0)),
        out_shape=jax.ShapeDtypeStruct((m, d), F32),
        scratch_shapes=[pltpu.VMEM((tm, d), BF16), pltpu.VMEM((tm, d), F32)],
        compiler_params=_params("parallel", "arbitrary"),
        name="adaln_mlp",
    )(x, g, mod_t, mod_t, mod_t, w1, w2, final_g)


SCAN_CHUNK = 256


def _scan_geometry(bsz, lc, seq):
    assert lc % SCAN_CHUNK == 0 and seq % SCAN_CHUNK == 0
    return lc // SCAN_CHUNK, seq // SCAN_CHUNK


def _seg_chunk(c, n_ctx_c, n_lat_c, reverse):
    if reverse:
        return c >= n_ctx_c, jnp.where(c < n_ctx_c, n_ctx_c - 1 - c, n_lat_c - 1 - (c - n_ctx_c))
    return c >= n_ctx_c, jnp.where(c < n_ctx_c, c, c - n_ctx_c)


def _row_block(b, c, bsz, n_ctx_c, n_lat_c, reverse):
    is_lat, j = _seg_chunk(c, n_ctx_c, n_lat_c, reverse)
    return jnp.where(is_lat, bsz * n_ctx_c + b * n_lat_c + j, b * n_ctx_c + j)


def _time_iotas(q):
    t = lax.broadcasted_iota(jnp.int32, (q, 1), 0).astype(F32)
    s = lax.broadcasted_iota(jnp.int32, (1, q), 1).astype(F32)
    return t, s


def _dot_t(a, b):
    return lax.dot_general(a, b, (((0,), (0,)), ((), ())), preferred_element_type=F32)


def _dot_nt(a, b):
    return lax.dot_general(a, b, (((1,), (1,)), ((), ())), preferred_element_type=F32)


def _ret_kernel(*refs, reverse, n_ctx_c, n_heads, k_scale):
    if reverse:
        ld_ref, q_ref, k_ref, v_ref, cos_ref, sin_ref, o_ref, s_ref = refs
    else:
        ld_ref, q_ref, k_ref, v_ref, cos_ref, sin_ref, yb_ref, g_ref, o_ref, s_ref = refs
    c, h = pl.program_id(1), pl.program_id(2)
    qn = SCAN_CHUNK

    @pl.when(c == 0)
    def _():
        s_ref[h] = jnp.zeros(s_ref.shape[1:], F32)

    lg = ld_ref[(n_heads if reverse else 0) + h]
    t, s = _time_iotas(qn)
    is_lat = c >= n_ctx_c
    cos = jnp.where(is_lat, cos_ref[...], 1.0)
    sin = jnp.where(is_lat, sin_ref[...], 0.0)

    def rope(x):
        half = x.shape[1] // 2
        swapped = jnp.concatenate([pltpu.roll(x[:, :half], half // 2, 1), pltpu.roll(x[:, half:], half // 2, 1)], axis=1)
        return x * cos + swapped * sin

    q = rope(q_ref[...]).astype(BF16)
    k = rope(k_ref[...]) * k_scale
    v = v_ref[...].astype(BF16)
    if reverse:
        dmat = jnp.where(s >= t, jnp.exp((s - t) * lg), 0.0)
        q_dec, k_dec = jnp.exp((qn - t) * lg), jnp.exp(t * lg)
    else:
        dmat = jnp.where(t >= s, jnp.exp((t - s) * lg), 0.0)
        q_dec, k_dec = jnp.exp((t + 1.0) * lg), jnp.exp((qn - 1.0 - t) * lg)
    attn = (_dot_nt(q, k.astype(BF16)) * dmat).astype(BF16)
    state = s_ref[h]
    y = jnp.dot(attn, v, preferred_element_type=F32)
    y = y + jnp.dot(q, state.astype(BF16), preferred_element_type=F32) * q_dec
    chunk_dec = jnp.exp(jnp.full((1, 1), qn, F32) * lg)
    s_ref[h] = state * chunk_dec + _dot_t((k * k_dec).astype(BF16), v)
    if reverse:
        o_ref[...] = y
    else:
        y = y + yb_ref[...]
        mu = jnp.mean(y, axis=-1, keepdims=True)
        yc = y - mu
        var = jnp.mean(yc * yc, axis=-1, keepdims=True)
        g = g_ref[...]
        o_ref[...] = (yc * lax.rsqrt(var + NORM_EPS) * (g * jax.nn.sigmoid(g))).astype(o_ref.dtype)


def _ret_mixer(u, log_decay, bsz, lc, seq, d_model):
    n_ctx_c, n_lat_c = _scan_geometry(bsz, lc, seq)
    n_steps = n_ctx_c + n_lat_c
    qk_dim, v_dim = d_model // RET_HEADS, 2 * d_model // RET_HEADS
    rows = u.shape[0]
    cos_r, sin_r, cos_c, sin_c = _rope_tables(seq // GRID_W, qk_dim // 2)
    cos_t = jnp.asarray(np.concatenate([cos_r, cos_r, cos_c, cos_c], axis=1))
    sin_t = jnp.asarray(np.concatenate([-sin_r, sin_r, -sin_c, sin_c], axis=1))
    ld = log_decay.reshape(-1).astype(F32)
    y_b = None
    for reverse in (True, False):
        rb = functools.partial(_row_block, bsz=bsz, n_ctx_c=n_ctx_c, n_lat_c=n_lat_c, reverse=reverse)

        def tab(b, c, h, reverse=reverse):
            is_lat, j = _seg_chunk(c, n_ctx_c, n_lat_c, reverse)
            return (jnp.where(is_lat, j, 0), 0)

        in_specs = [pl.BlockSpec(memory_space=pltpu.SMEM),
                    pl.BlockSpec((SCAN_CHUNK, qk_dim), lambda b, c, h, rb=rb: (rb(b, c), h)),
                    pl.BlockSpec((SCAN_CHUNK, qk_dim), lambda b, c, h, rb=rb: (rb(b, c), RET_HEADS + h)),
                    pl.BlockSpec((SCAN_CHUNK, v_dim), lambda b, c, h, rb=rb: (rb(b, c), RET_HEADS + h)),
                    pl.BlockSpec((SCAN_CHUNK, qk_dim), tab),
                    pl.BlockSpec((SCAN_CHUNK, qk_dim), tab)]
        args = [ld, u, u, u, cos_t, sin_t]
        if not reverse:
            in_specs += [pl.BlockSpec((SCAN_CHUNK, v_dim), lambda b, c, h, rb=rb: (rb(b, c), h)),
                         pl.BlockSpec((SCAN_CHUNK, v_dim), lambda b, c, h, rb=rb: (rb(b, c), 2 * RET_HEADS + h))]
            args += [y_b, u]
        out = pl.pallas_call(
            functools.partial(_ret_kernel, reverse=reverse, n_ctx_c=n_ctx_c, n_heads=RET_HEADS, k_scale=qk_dim ** -0.5),
            grid=(bsz, n_steps, RET_HEADS),
            in_specs=in_specs,
            out_specs=pl.BlockSpec((SCAN_CHUNK, v_dim), lambda b, c, h, rb=rb: (rb(b, c), h)),
            out_shape=jax.ShapeDtypeStruct((rows, RET_HEADS * v_dim), F32 if reverse else BF16),
            scratch_shapes=[pltpu.VMEM((RET_HEADS, qk_dim, v_dim), F32)],
            compiler_params=_params("parallel", "arbitrary", "arbitrary"),
            name="retention_bwd" if reverse else "retention_fwd",
        )(*args)
        if reverse:
            y_b = out
    return out


CONV_COLS = 1024
CONV_ROWS = 8


def _conv_kernel(x_ref, prev_ref, next_ref, w_ref, b_ref, o_ref, *, n_ctx_c, n_lat_c, bsz, n_scaled, n_normed, scale,
                 head_dim):
    rb, cb = pl.program_id(0), pl.program_id(1)
    is_lat = rb >= bsz * n_ctx_c
    j = jnp.where(is_lat, (rb - bsz * n_ctx_c) % n_lat_c, rb % n_ctx_c)
    last = jnp.where(is_lat, n_lat_c - 1, n_ctx_c - 1)
    rows, cols = x_ref.shape
    halo_prev = jnp.where(j > 0, prev_ref[7:8, :], 0.0)
    halo_next = jnp.where(j < last, next_ref[0:1, :], 0.0)
    w0, w1, w2, bias = w_ref[0:1, :], w_ref[1:2, :], w_ref[2:3, :], b_ref[...]
    row = lax.broadcasted_iota(jnp.int32, (CONV_ROWS, 1), 0)

    def run(mult):
        before = halo_prev
        for r0 in range(0, rows, CONV_ROWS):
            x = x_ref[r0:r0 + CONV_ROWS, :]
            after = x_ref[r0 + CONV_ROWS:r0 + CONV_ROWS + 1, :] if r0 + CONV_ROWS < rows else halo_next
            x_prev = jnp.where(row == 0, before, pltpu.roll(x, 1, 0))
            x_next = jnp.where(row == CONV_ROWS - 1, after, pltpu.roll(x, CONV_ROWS - 1, 0))
            y = w0 * x_prev + w1 * x + w2 * x_next + bias
            y = y * jax.nn.sigmoid(y)
            if mult is not None:
                parts = []
                for i in range(cols // head_dim):
                    p = y[:, i * head_dim:(i + 1) * head_dim]
                    parts.append(p * (lax.rsqrt(jnp.sum(p * p, axis=-1, keepdims=True) + 1e-6) * mult))
                y = jnp.concatenate(parts, axis=1)
            o_ref[r0:r0 + CONV_ROWS, :] = y
            before = x[CONV_ROWS - 1:CONV_ROWS, :]

    if n_normed == 0:
        run(None)
    else:
        @pl.when(cb < n_normed)
        def _():
            run(jnp.where(cb < n_scaled, scale, 1.0))

        @pl.when(cb >= n_normed)
        def _():
            run(None)


def _conv_silu(u, w, b, n_cols, bsz, lc, seq, n_scaled=0, n_normed=0, scale=1.0, head_dim=128, col0=0):
    n_ctx_c, n_lat_c = _scan_geometry(bsz, lc, seq)
    rows = u.shape[0]
    n_rb = rows // SCAN_CHUNK
    sub = SCAN_CHUNK // 8
    n_halo = rows // 8
    return pl.pallas_call(
        functools.partial(_conv_kernel, n_ctx_c=n_ctx_c, n_lat_c=n_lat_c, bsz=bsz, n_scaled=n_scaled,
                          n_normed=n_normed, scale=scale, head_dim=head_dim),
        grid=(n_rb, n_cols // CONV_COLS),
        in_specs=[pl.BlockSpec((SCAN_CHUNK, CONV_COLS), lambda r, c: (r, c + col0)),
                  pl.BlockSpec((8, CONV_COLS), lambda r, c: (jnp.maximum(r * sub - 1, 0), c + col0)),
                  pl.BlockSpec((8, CONV_COLS), lambda r, c: (jnp.minimum(r * sub + sub, n_halo - 1), c + col0)),
                  pl.BlockSpec((CONV_W, CONV_COLS), lambda r, c: (0, c)),
                  pl.BlockSpec((1, CONV_COLS), lambda r, c: (0, c))],
        out_specs=pl.BlockSpec((SCAN_CHUNK, CONV_COLS), lambda r, c: (r, c)),
        out_shape=jax.ShapeDtypeStruct((rows, n_cols), F32),
        compiler_params=_params("parallel", "parallel"),
        name="conv_silu",
    )(u, u, u, w, b)


GDN_SUB = 64


GDN_BASE = 8
GDN_HEADS_PER_STEP = 8


def _split3(x):
    x1 = x.astype(BF16)
    r1 = x - x1.astype(F32)
    x2 = r1.astype(BF16)
    return x1, x2, (r1 - x2.astype(F32)).astype(BF16)


def _sel_rows(m, x):
    mb = m.astype(BF16)
    return sum(jnp.dot(mb, p, preferred_element_type=F32) for p in _split3(x))


def _sel_cols(x, m):
    mb = m.astype(BF16)
    return sum(jnp.dot(p, mb, preferred_element_type=F32) for p in _split3(x))


def _sel_rows_t(x, m):
    mb = m.astype(BF16)
    return sum(lax.dot_general(p, mb, (((0,), (1,)), ((), ())), preferred_element_type=F32) for p in _split3(x))


def _mm_bf16(a, b):
    return jnp.dot(a.astype(BF16), b.astype(BF16), preferred_element_type=F32)


def _unit_tri_inverse_minus_eye(mats, ti, si):
    def same(n):
        s = int(math.log2(n))
        return lax.shift_right_logical(ti, s) == lax.shift_right_logical(si, s)

    base = same(GDN_BASE)
    ps = [jnp.where(base, a, 0.0) for a in mats]
    ns = [-p for p in ps]
    for _ in range(int(math.log2(GDN_BASE)) - 1):
        ps = [_mm_bf16(p, p) for p in ps]
        ns = [n + p + _mm_bf16(n, p) for n, p in zip(ns, ps)]
    b = GDN_BASE
    while b < GDN_SUB:
        join = same(2 * b) & jnp.logical_not(same(b))
        cs = [jnp.where(join, a, 0.0) for a in mats]
        ms = [c + _mm_bf16(c, n) for c, n in zip(cs, ns)]
        ns = [n - (m + _mm_bf16(n, m)) for n, m in zip(ns, ms)]
        b *= 2
    return ns


def _softplus(x):
    return jnp.maximum(x, 0.0) + jnp.log(1.0 + jnp.exp(-jnp.abs(x)))


def _gdn_kernel(*refs, reverse, n_kh, n_r, hd):
    if reverse:
        q_ref, k_ref, v_ref, tail_ref, prow_ref, arow_ref, o_ref, s_ref = refs
    else:
        q_ref, k_ref, v_ref, tail_ref, prow_ref, arow_ref, yb_ref, z_ref, ng_ref, o_ref, s_ref = refs
    g, c = pl.program_id(1), pl.program_id(2)
    qn = SCAN_CHUNK
    n_sub = qn // GDN_SUB
    n_gate = tail_ref.shape[1] // 2

    @pl.when(c == 0)
    def _():
        s_ref[...] = jnp.zeros_like(s_ref)

    tail = tail_ref[...]
    lane = lax.broadcasted_iota(jnp.int32, (1, 2 * n_gate), 1)
    gates = jnp.where(lane < n_gate, jax.nn.sigmoid(tail), -jnp.exp(arow_ref[...]) * _softplus(tail + prow_ref[...]))
    nh = n_kh * n_r
    li = lax.broadcasted_iota(jnp.int32, (2 * n_gate, 2 * nh), 0)
    ji = lax.broadcasted_iota(jnp.int32, (2 * n_gate, 2 * nh), 1)
    col = jnp.where(ji < nh, ji, n_gate + ji - nh) + (n_gate // 2 if reverse else 0) + g * nh
    gsel = _sel_cols(gates, (li == col).astype(F32))
    lw = lax.broadcasted_iota(jnp.int32, (2 * n_gate, 2 * nh * hd), 0)
    jw = lax.shift_right_logical(lax.broadcasted_iota(jnp.int32, (2 * n_gate, 2 * nh * hd), 1), int(math.log2(hd)))
    colw = jnp.where(jw < nh, jw, n_gate + jw - nh) + (n_gate // 2 if reverse else 0) + g * nh
    gwide = _sel_cols(gates, (lw == colw).astype(F32))

    ti = lax.broadcasted_iota(jnp.int32, (qn, 1), 0)
    si = lax.broadcasted_iota(jnp.int32, (1, qn), 1)
    shift = int(math.log2(GDN_SUB))
    same = lax.shift_right_logical(ti, shift) == lax.shift_right_logical(si, shift)
    if reverse:
        incl, strict = same & (si >= ti), same & (si > ti)
    else:
        incl, strict = same & (si <= ti), same & (si < ti)
    cs = incl.astype(F32)
    cum_w = _sel_rows(cs, gwide[:, nh * hd:])
    cum_t = _sel_rows_t(gsel, cs)
    reps = qn // hd

    order = range(n_sub - 1, -1, -1) if reverse else range(n_sub)
    outs = []
    shared = []
    for kh in range(n_kh):
        k = k_ref[:, kh * hd:(kh + 1) * hd]
        qb, kb = q_ref[:, kh * hd:(kh + 1) * hd].astype(BF16), k.astype(BF16)
        shared.append((k, qb, _dot_nt(kb, kb), _dot_nt(qb, kb)))
    heads = range(nh)
    beta = [gwide[:, j * hd:(j + 1) * hd] for j in heads]
    cum_c = [cum_w[:, j * hd:(j + 1) * hd] for j in heads]
    decay = [jnp.exp(jnp.where(incl, jnp.concatenate([cum_c[j]] * reps, axis=1) - cum_t[nh + j:nh + j + 1, :],
                               -jnp.inf)) for j in heads]
    amat = [jnp.where(strict, jnp.concatenate([beta[j]] * reps, axis=1) * shared[j // n_r][2] * decay[j], 0.0)
            for j in heads]
    inv_off = _unit_tri_inverse_minus_eye(amat, ti, si)
    rhs = [jnp.concatenate([v_ref[:, j * hd:(j + 1) * hd] * beta[j],
                            shared[j // n_r][0] * (beta[j] * jnp.exp(cum_c[j]))], axis=1) for j in heads]
    sol = [rhs[j] + _mm_bf16(inv_off[j], rhs[j]) for j in heads]
    state = [s_ref[j] for j in heads]
    v_new = [[None] * n_sub for _ in heads]
    inter = [[None] * n_sub for _ in heads]
    for i in order:
        sl = slice(i * GDN_SUB, (i + 1) * GDN_SUB)
        end = i * GDN_SUB if reverse else (i + 1) * GDN_SUB - 1
        sb = [state[j].astype(BF16) for j in heads]
        for j in heads:
            v_new[j][i] = sol[j][sl, :hd] - jnp.dot(sol[j][sl, hd:].astype(BF16), sb[j], preferred_element_type=F32)
            inter[j][i] = jnp.dot(shared[j // n_r][1][sl], sb[j], preferred_element_type=F32)
        for j in heads:
            cum_end = cum_c[j][end:end + 1, :]
            k_end = (shared[j // n_r][0][sl] * jnp.exp(cum_end - cum_c[j][sl])).astype(BF16)
            state[j] = jnp.exp(cum_end) * state[j] + _dot_t(k_end, v_new[j][i].astype(BF16))
    outs = []
    for j in heads:
        s_ref[j] = state[j]
        attn = (shared[j // n_r][3] * decay[j]).astype(BF16)
        y = jnp.dot(attn, jnp.concatenate(v_new[j], axis=0).astype(BF16), preferred_element_type=F32)
        y = y + jnp.concatenate(inter[j], axis=0) * jnp.exp(cum_c[j])
        if not reverse:
            y = y + yb_ref[:, j * hd:(j + 1) * hd]
            y = y * lax.rsqrt(jnp.mean(y * y, axis=-1, keepdims=True) + NORM_EPS) * ng_ref[...]
            z = z_ref[:, j * hd:(j + 1) * hd]
            y = y * (z * jax.nn.sigmoid(z))
        outs.append(y)
    o_ref[...] = jnp.concatenate(outs, axis=1).astype(o_ref.dtype)


def _gdn_mixer(u, tail, conv_w, dt_bias, a_log, norm_g, bsz, lc, seq, d_model):
    n_ctx_c, n_lat_c = _scan_geometry(bsz, lc, seq)
    n_steps = n_ctx_c + n_lat_c
    hd = GDN_HEAD_DIM
    k_heads = d_model // hd
    n_r = 2
    rows = u.shape[0]
    conv_ch = 4 * d_model
    nq = d_model // CONV_COLS
    qkv = _conv_silu(u, conv_w, jnp.zeros((1, conv_ch), F32), conv_ch, bsz, lc, seq, n_scaled=nq, n_normed=2 * nq,
                     scale=hd ** -0.5, head_dim=hd)
    n_gate = tail.shape[1] // 2
    prow = jnp.concatenate([jnp.zeros((1, n_gate), F32), dt_bias.reshape(1, n_gate)], axis=1)
    arow = jnp.concatenate([jnp.zeros((1, n_gate), F32), a_log.reshape(1, n_gate)], axis=1)
    y_b = None
    for reverse in (True, False):
        rb = functools.partial(_row_block, bsz=bsz, n_ctx_c=n_ctx_c, n_lat_c=n_lat_c, reverse=reverse)
        const = lambda b, g, c: (0, 0)
        qk_w, v_w = GDN_HEADS_PER_STEP * hd, GDN_HEADS_PER_STEP * n_r * hd
        n_groups = k_heads // GDN_HEADS_PER_STEP
        in_specs = [pl.BlockSpec((SCAN_CHUNK, qk_w), lambda b, g, c, rb=rb: (rb(b, c), g)),
                    pl.BlockSpec((SCAN_CHUNK, qk_w), lambda b, g, c, rb=rb: (rb(b, c), n_groups + g)),
                    pl.BlockSpec((SCAN_CHUNK, v_w), lambda b, g, c, rb=rb: (rb(b, c), n_groups + g)),
                    pl.BlockSpec((SCAN_CHUNK, 2 * n_gate), lambda b, g, c, rb=rb: (rb(b, c), 0)),
                    pl.BlockSpec((1, 2 * n_gate), const),
                    pl.BlockSpec((1, 2 * n_gate), const)]
        args = [qkv, qkv, qkv, tail, prow, arow]
        if not reverse:
            in_specs += [pl.BlockSpec((SCAN_CHUNK, v_w), lambda b, g, c, rb=rb: (rb(b, c), g)),
                         pl.BlockSpec((SCAN_CHUNK, v_w), lambda b, g, c, rb=rb: (rb(b, c), 2 * n_groups + g)),
                         pl.BlockSpec((1, hd), const)]
            args += [y_b, u, norm_g.reshape(1, hd)]
        out = pl.pallas_call(
            functools.partial(_gdn_kernel, reverse=reverse, n_kh=GDN_HEADS_PER_STEP, n_r=n_r, hd=hd),
            grid=(bsz, n_groups, n_steps),
            in_specs=in_specs,
            out_specs=pl.BlockSpec((SCAN_CHUNK, v_w), lambda b, g, c, rb=rb: (rb(b, c), g)),
            out_shape=jax.ShapeDtypeStruct((rows, k_heads * n_r * hd), F32 if reverse else BF16),
            scratch_shapes=[pltpu.VMEM((GDN_HEADS_PER_STEP * n_r, hd, hd), F32)],
            compiler_params=_params("parallel", "parallel", "arbitrary"),
            name="gdn_bwd" if reverse else "gdn_fwd",
        )(*args)
        if reverse:
            y_b = out
    return out


def _ssd_kernel(*refs, reverse, n_r, hd):
    if reverse:
        x_ref, b_ref, c_ref, tail_ref, dtb_ref, alog_ref, o_ref, s_ref = refs
    else:
        x_ref, b_ref, c_ref, tail_ref, dtb_ref, alog_ref, yb_ref, z_ref, d_ref, ng_ref, o_ref, s_ref = refs
    g, c = pl.program_id(1), pl.program_id(2)
    qn = SCAN_CHUNK
    n_lane = tail_ref.shape[1]
    width = n_r * hd

    @pl.when(c == 0)
    def _():
        s_ref[...] = jnp.zeros_like(s_ref)

    dt_all = _softplus(tail_ref[...] + dtb_ref[...])
    la_all = -jnp.exp(alog_ref[...]) * dt_all
    li = lax.broadcasted_iota(jnp.int32, (n_lane, n_r), 0)
    ji = lax.broadcasted_iota(jnp.int32, (n_lane, n_r), 1)
    sel = (li == ji + (n_lane // 2 if reverse else 0) + g * n_r).astype(F32)
    dt, la = _sel_cols(dt_all, sel), _sel_cols(la_all, sel)

    ti = lax.broadcasted_iota(jnp.int32, (qn, 1), 0)
    si = lax.broadcasted_iota(jnp.int32, (1, qn), 1)
    before = (si >= ti) if reverse else (si <= ti)
    cs = before.astype(F32)
    cum = _sel_rows(cs, la)
    cum_t = _sel_rows_t(la, cs)
    ei = lax.broadcasted_iota(jnp.int32, (n_r, width), 0)
    el = lax.broadcasted_iota(jnp.int32, (n_r, width), 1)
    expand = (lax.shift_right_logical(el, int(math.log2(hd))) == ei).astype(F32)
    dt_x, cum_x = _sel_cols(dt, expand), _sel_cols(cum, expand)

    xs = x_ref[...]
    v = xs * dt_x
    bm, cm = b_ref[...].astype(BF16), c_ref[...].astype(BF16)
    scores = _dot_nt(cm, bm)
    lane = lax.broadcasted_iota(jnp.int32, (1, 2 * hd), 1)
    tiles = []
    for p in range(n_r // 2):
        vt = v[:, 2 * p * hd:2 * (p + 1) * hd]
        acc = None
        for h in (2 * p, 2 * p + 1):
            decay = jnp.exp(jnp.where(before, cum[:, h:h + 1] - cum_t[h:h + 1, :], -jnp.inf))
            vh = jnp.where((lane >= hd) if h % 2 else (lane < hd), vt, 0.0).astype(BF16)
            part = jnp.dot((scores * decay).astype(BF16), vh, preferred_element_type=F32)
            acc = part if acc is None else acc + part
        tiles.append(acc)
    state = s_ref[...]
    y = jnp.concatenate(tiles, axis=1) + jnp.dot(cm, state.astype(BF16), preferred_element_type=F32) * jnp.exp(cum_x)
    end = 0 if reverse else qn - 1
    cum_end = cum_x[end:end + 1]
    s_ref[...] = state * jnp.exp(cum_end) + _dot_t(bm, (v * jnp.exp(cum_end - cum_x)).astype(BF16))
    if reverse:
        o_ref[...] = y
    else:
        y = y + yb_ref[...] + d_ref[...] * xs
        z = z_ref[...]
        y = y * (z * jax.nn.sigmoid(z))
        y = y * lax.rsqrt(jnp.mean(y * y, axis=-1, keepdims=True) + NORM_EPS) * ng_ref[...]
        o_ref[...] = y.astype(o_ref.dtype)


def _ssd_mixer(u, tail, conv_w, conv_b, dt_bias, a_log, d_skip, norm_g, bsz, lc, seq):
    n_ctx_c, n_lat_c = _scan_geometry(bsz, lc, seq)
    n_steps = n_ctx_c + n_lat_c
    d_inner = norm_g.shape[0]
    hd, st = SSD_HEAD_DIM, SSD_STATE
    heads = d_inner // hd
    n_r = heads // SSD_GROUPS
    width = n_r * hd
    rows = u.shape[0]
    conv_ch = d_inner + 2 * SSD_GROUPS * st
    xbc = _conv_silu(u, conv_w, conv_b.reshape(1, conv_ch), conv_ch, bsz, lc, seq, col0=d_inner // CONV_COLS)
    d_x = jnp.repeat(d_skip, hd).reshape(1, d_inner)
    y_b = None
    for reverse in (True, False):
        rb = functools.partial(_row_block, bsz=bsz, n_ctx_c=n_ctx_c, n_lat_c=n_lat_c, reverse=reverse)
        const = lambda b, g, c: (0, 0)
        grp = lambda b, g, c: (0, g)
        wide = lambda b, g, c, rb=rb: (rb(b, c), g)
        in_specs = [pl.BlockSpec((SCAN_CHUNK, width), wide),
                    pl.BlockSpec((SCAN_CHUNK, st), lambda b, g, c, rb=rb: (rb(b, c), d_inner // st + g)),
                    pl.BlockSpec((SCAN_CHUNK, st), lambda b, g, c, rb=rb: (rb(b, c), d_inner // st + SSD_GROUPS + g)),
                    pl.BlockSpec((SCAN_CHUNK, 2 * heads), lambda b, g, c, rb=rb: (rb(b, c), 0)),
                    pl.BlockSpec((1, 2 * heads), const),
                    pl.BlockSpec((1, 2 * heads), const)]
        args = [xbc, xbc, xbc, tail, dt_bias.reshape(1, 2 * heads), a_log.reshape(1, 2 * heads)]
        if not reverse:
            in_specs += [pl.BlockSpec((SCAN_CHUNK, width), wide), pl.BlockSpec((SCAN_CHUNK, width), wide),
                         pl.BlockSpec((1, width), grp), pl.BlockSpec((1, width), grp)]
            args += [y_b, u, d_x, norm_g.reshape(1, d_inner)]
        out = pl.pallas_call(
            functools.partial(_ssd_kernel, reverse=reverse, n_r=n_r, hd=hd),
            grid=(bsz, SSD_GROUPS, n_steps),
            in_specs=in_specs,
            out_specs=pl.BlockSpec((SCAN_CHUNK, width), wide),
            out_shape=jax.ShapeDtypeStruct((rows, d_inner), F32 if reverse else BF16),
            scratch_shapes=[pltpu.VMEM((st, width), F32)],
            compiler_params=_params("parallel", "parallel", "arbitrary"),
            name="ssd_bwd" if reverse else "ssd_fwd",
        )(*args)
        if reverse:
            y_b = out
    return out


HG_SUB = 64
HG_BLK = 8


def _log1p(x):
    return jnp.log(1.0 + x)


def _hgrn_kernel(*refs, reverse):
    if reverse:
        q_ref, f_ref, i_ref, lb_ref, o_ref, st_ref = refs
    else:
        q_ref, f_ref, i_ref, lb_ref, yb_ref, g_ref, ng_ref, o_ref, st_ref = refs
    c = pl.program_id(2)
    qn = SCAN_CHUNK
    hd = q_ref.shape[1]
    n_sub, n_blk = qn // HG_SUB, HG_SUB // HG_BLK

    @pl.when(c == 0)
    def _():
        st_ref[...] = jnp.zeros_like(st_ref)

    q, f, v, lb = q_ref[...], f_ref[...], i_ref[...], lb_ref[...]
    log_sig = jnp.minimum(f, 0.0) - _log1p(jnp.exp(-jnp.abs(f)))
    ga, gb = jnp.log(lb), _log1p(-lb) + log_sig
    log_f = jnp.maximum(ga, gb) + _log1p(jnp.exp(-jnp.abs(ga - gb)))
    k = (1.0 - lb) * jax.nn.sigmoid(-f)

    ti = lax.broadcasted_iota(jnp.int32, (qn, 1), 0)
    si = lax.broadcasted_iota(jnp.int32, (1, qn), 1)
    before = (si >= ti) if reverse else (si <= ti)
    same = lambda n: lax.shift_right_logical(ti, int(math.log2(n))) == lax.shift_right_logical(si, int(math.log2(n)))
    cum_sub = _sel_rows((same(HG_SUB) & before).astype(F32), log_f)
    cum_blk = _sel_rows((same(HG_BLK) & before).astype(F32), log_f)
    q_blk = q * jnp.exp(cum_blk)
    q_sub = (q * jnp.exp(cum_sub)).astype(BF16)

    ones = jnp.ones((hd, hd), BF16)
    lane = lax.broadcasted_iota(jnp.int32, (1, hd), 1)
    row_sub = lax.broadcasted_iota(jnp.int32, (HG_SUB, 1), 0)
    row_blk = lax.broadcasted_iota(jnp.int32, (HG_BLK, 1), 0)
    zeros_sub = jnp.zeros((hd - HG_SUB, hd), F32)
    y_intra, kvt, chunk_dec = [], [], []
    for i in range(n_sub):
        r0 = i * HG_SUB
        sl = slice(r0, r0 + HG_SUB)
        cs, ks, vs = cum_sub[sl], k[sl], v[sl]
        end = r0 if reverse else r0 + HG_SUB - 1
        cum_end = cum_sub[end:end + 1]
        kvt.append(_dot_t(vs.astype(BF16), (ks * jnp.exp(cum_end - cs)).astype(BF16)))
        chunk_dec.append(jnp.exp(cum_end))
        a_rows = []
        for a in range(n_blk):
            b0 = r0 + a * HG_BLK
            bl = slice(b0, b0 + HG_BLK)
            qb, kb, cb = q[bl], k[bl], cum_blk[bl]
            tiles = []
            for s in range(HG_BLK):
                ok = (row_blk <= s) if reverse else (row_blk >= s)
                e = jnp.exp(jnp.where(ok, cb - cb[s:s + 1], -jnp.inf))
                tiles.append(qb * e * kb[s:s + 1])
            sums = jnp.dot(jnp.concatenate(tiles, axis=0).astype(BF16), ones, preferred_element_type=F32)
            acc = jnp.zeros((HG_BLK, hd), F32)
            for s in range(HG_BLK):
                acc = acc + jnp.where(lane == a * HG_BLK + s, sums[s * HG_BLK:(s + 1) * HG_BLK], 0.0)
            has_earlier = (a < n_blk - 1) if reverse else (a > 0)
            if has_earlier:
                ref_row = b0 + HG_BLK if reverse else b0 - 1
                earlier = (row_sub >= (a + 1) * HG_BLK) if reverse else (row_sub < a * HG_BLK)
                kt = ks * jnp.exp(jnp.where(earlier, cum_sub[ref_row:ref_row + 1] - cs, -jnp.inf))
                kt = jnp.concatenate([kt, zeros_sub], axis=0).astype(BF16)
                acc = acc + _dot_nt(q_blk[bl].astype(BF16), kt)
            a_rows.append(acc)
        attn = jnp.concatenate(a_rows, axis=0).astype(BF16)
        v_pad = jnp.concatenate([vs, zeros_sub], axis=0).astype(BF16)
        y_intra.append(jnp.dot(attn, v_pad, preferred_element_type=F32))

    state = st_ref[...]
    ys = [None] * n_sub
    for i in (range(n_sub - 1, -1, -1) if reverse else range(n_sub)):
        ys[i] = y_intra[i] + _dot_nt(q_sub[i * HG_SUB:(i + 1) * HG_SUB], state.astype(BF16))
        state = state * chunk_dec[i] + kvt[i]
    st_ref[...] = state
    y = jnp.concatenate(ys, axis=0)
    if reverse:
        o_ref[...] = y
    else:
        y = y + yb_ref[...]
        y = y * lax.rsqrt(jnp.mean(y * y, axis=-1, keepdims=True) + NORM_EPS) * ng_ref[...]
        g = g_ref[...]
        o_ref[...] = (y * (g * jax.nn.sigmoid(g))).astype(o_ref.dtype)


def _hgrn_mixer(u, lb, norm_g, bsz, lc, seq):
    n_ctx_c, n_lat_c = _scan_geometry(bsz, lc, seq)
    n_steps = n_ctx_c + n_lat_c
    hd = HGRN_EXPAND
    d_model = lb.shape[0]
    heads = d_model // hd
    rows = u.shape[0]
    lb2, ng2 = lb.reshape(1, d_model), norm_g.reshape(1, d_model)
    y_b = None
    for reverse in (True, False):
        rb = functools.partial(_row_block, bsz=bsz, n_ctx_c=n_ctx_c, n_lat_c=n_lat_c, reverse=reverse)
        col = lambda seg: (lambda b, h, c, rb=rb: (rb(b, c), seg * heads + h))
        par = lambda b, h, c: (0, h)
        in_specs = [pl.BlockSpec((SCAN_CHUNK, hd), col(0)),
                    pl.BlockSpec((SCAN_CHUNK, hd), col(2 if reverse else 1)),
                    pl.BlockSpec((SCAN_CHUNK, hd), col(3)),
                    pl.BlockSpec((1, hd), par)]
        args = [u, u, u, lb2]
        if not reverse:
            in_specs += [pl.BlockSpec((SCAN_CHUNK, hd), col(0)), pl.BlockSpec((SCAN_CHUNK, hd), col(4)),
                         pl.BlockSpec((1, hd), par)]
            args += [y_b, u, ng2]
        out = pl.pallas_call(
            functools.partial(_hgrn_kernel, reverse=reverse),
            grid=(bsz, heads, n_steps),
            in_specs=in_specs,
            out_specs=pl.BlockSpec((SCAN_CHUNK, hd), col(0)),
            out_shape=jax.ShapeDtypeStruct((rows, d_model), F32 if reverse else BF16),
            scratch_shapes=[pltpu.VMEM((hd, hd), F32)],
            compiler_params=_params("parallel", "parallel", "arbitrary"),
            name="hgrn_bwd" if reverse else "hgrn_fwd",
        )(*args)
        if reverse:
            y_b = out
    return out


def _rope_tables(rows, half):
    pos = np.arange(rows * GRID_W)
    inv_freq = np.float32(ROPE_BASE) ** (-(np.arange(0, half, 2, dtype=np.float32) / np.float32(half)))
    out = []
    for p in ((pos // GRID_W).astype(np.float32), (pos % GRID_W).astype(np.float32)):
        ang = (p[:, None] * inv_freq.astype(np.float32)).astype(np.float32).astype(np.float64)
        out += [np.cos(ang).astype(np.float32), np.sin(ang).astype(np.float32)]
    return out


def _lower_bound(lb_logits, layer):
    p = jax.nn.softmax(lb_logits.astype(F32), axis=0)
    return jnp.cumsum(p, axis=0)[layer] - p[0]


def kernel(x, c, ctx, c_ctx, ada_w, ada_b, norm_g, mlp_w1, mlp_w2, final_g, ssd_w_in, ssd_conv_w, ssd_conv_b,
           ssd_dt_bias, ssd_a_log, ssd_d, ssd_norm_g, ssd_w_out, ret_w_in, ret_log_decay, ret_w_out, hgrn_w_in,
           hgrn_lb_logits, hgrn_norm_g, hgrn_w_out, gdn_w_in, gdn_conv_w, gdn_dt_bias, gdn_a_log, gdn_norm_g,
           gdn_w_out):
    bsz, seq, d = x.shape
    lc = ctx.shape[1]
    depth = ada_w.shape[0]
    n_ctx = bsz * lc
    assert n_ctx % IN_ROW_TILE == 0 and seq % IN_ROW_TILE == 0 and IN_ROW_TILE % ROW_TILE == 0 and bsz + 1 <= 8

    cond_pad = jnp.concatenate([c, c_ctx[None], jnp.zeros((8 - bsz - 1, d), F32)], axis=0)
    mod = _ada_mod(cond_pad, ada_w, ada_b)

    def tile_rows(tile):
        return jnp.asarray([bsz] * (n_ctx // tile) + [b for b in range(bsz) for _ in range(seq // tile)], jnp.int32)

    tile_row, in_tile_row = tile_rows(ROW_TILE), tile_rows(IN_ROW_TILE)
    n_ctx_tiles = n_ctx // ROW_TILE

    xr = jnp.concatenate([ctx.reshape(n_ctx, d), x.reshape(bsz * seq, d)], axis=0)
    w1_all, w2_all = mlp_w1.astype(BF16), mlp_w2.astype(BF16)

    for i in range(depth):
        mixer, occ = i % 4, i // 4
        keep_ctx = i < depth - 1
        mod_t = mod[i][tile_row][:, None, :]
        g0, g1 = norm_g[i, 0][None], norm_g[i, 1][None]
        if mixer == 0:
            w_in, w_out = ssd_w_in[occ], ssd_w_out[occ]
        elif mixer == 1:
            w_in, w_out = ret_w_in[occ], ret_w_out[occ]
        elif mixer == 2:
            w_in, w_out = hgrn_w_in[occ], hgrn_w_out[occ]
        else:
            w_in, w_out = gdn_w_in[occ], gdn_w_out[occ]
        n_in = w_in.shape[1]
        n_main = (n_in // 1024) * 1024 if n_in % 1024 else n_in
        w_in = w_in.astype(BF16)
        mod_in = mod[i][in_tile_row][:, None, :]
        u = _ln_mm(xr, g0, mod_in, 0, 1, w_in, 0, n_main)
        tail = _ln_mm(xr, g0, mod_in, 0, 1, w_in, n_main, n_in - n_main) if n_main != n_in else None
        if mixer == 0:
            yr = _ssd_mixer(u, tail, ssd_conv_w[occ], ssd_conv_b[occ], ssd_dt_bias[occ], ssd_a_log[occ], ssd_d[occ],
                            ssd_norm_g[occ], bsz, lc, seq)
        elif mixer == 1:
            yr = _ret_mixer(u, ret_log_decay[occ], bsz, lc, seq, d)
        elif mixer == 2:
            yr = _hgrn_mixer(u, _lower_bound(hgrn_lb_logits, i), hgrn_norm_g[occ], bsz, lc, seq)
        else:
            yr = _gdn_mixer(u, tail, gdn_conv_w[occ], gdn_dt_bias[occ], gdn_a_log[occ], gdn_norm_g[occ], bsz, lc,
                            seq, d)
        first_tile = 0 if keep_ctx else n_ctx_tiles
        xr = _out_proj(yr, w_out.astype(BF16), xr, mod_t, 2, first_tile)
        mod_t = mod_t[first_tile:]
        xr = _mlp(xr, g1, mod_t, w1_all, w2_all, i, final_g[None], final=not keep_ctx)
    return xr.reshape(bsz, seq, d)
```

```python
import functools
import math

import jax
import jax.numpy as jnp
import numpy as np
from jax import lax
from jax.experimental import pallas as pl
from jax.experimental.pallas import tpu as pltpu

F32 = jnp.float32
BF16 = jnp.bfloat16

LANES = 128
SUBLANES = 8

GRID_W = 64
CONV_W = 3
NORM_EPS = 1e-6
L2_EPS = 1e-6
ROPE_BASE = 10000.0
SSD_HEAD_DIM = 64
SSD_GROUPS = 8
SSD_STATE = 128
RET_HEADS = 8
HGRN_EXPAND = 128
GDN_HEAD_DIM = 128

ROW_TILE = 512
IN_ROW_TILE = 1024
MLP_FF_TILE = 1024
OUT_COL_TILE = 1024
ADA_COL_TILE = 1024
VMEM_LIMIT = 56 * 1024 * 1024


def _params(*sem):
    return pltpu.CompilerParams(dimension_semantics=sem, vmem_limit_bytes=VMEM_LIMIT)


def _col_tile(n, cap=1536):
    best = LANES
    for t in range(LANES, cap + 1, LANES):
        if n % t == 0:
            best = t
    return best


def _ada_kernel(c_ref, w_ref, b_ref, o_ref):
    c = c_ref[...]
    c = (c * jax.nn.sigmoid(c)).astype(BF16)
    o_ref[0] = jnp.dot(c, w_ref[0].astype(BF16), preferred_element_type=F32) + b_ref[0]


def _ada_mod(cond_pad, ada_w, ada_b):
    depth, d, n = ada_w.shape
    tn = ADA_COL_TILE
    return pl.pallas_call(
        _ada_kernel,
        grid=(depth, n // tn),
        in_specs=[pl.BlockSpec((SUBLANES, d), lambda l, j: (0, 0)),
                  pl.BlockSpec((1, d, tn), lambda l, j: (l, 0, j)),
                  pl.BlockSpec((1, 1, tn), lambda l, j: (l, 0, j))],
        out_specs=pl.BlockSpec((1, SUBLANES, tn), lambda l, j: (l, 0, j)),
        out_shape=jax.ShapeDtypeStruct((depth, SUBLANES, n), F32),
        compiler_params=_params("parallel", "parallel"),
        name="ada_mod",
    )(cond_pad, ada_w, ada_b.reshape(depth, 1, n))


def _adaln_rows(x, g, sh, sc):
    y = x * lax.rsqrt(jnp.mean(x * x, axis=-1, keepdims=True) + NORM_EPS)
    return y * g * (1.0 + sc) + sh


def _ln_mm_kernel(x_ref, g_ref, sh_ref, sc_ref, w_ref, o_ref, h_ref):
    @pl.when(pl.program_id(1) == 0)
    def _():
        h_ref[...] = _adaln_rows(x_ref[...], g_ref[...], sh_ref[0], sc_ref[0]).astype(BF16)

    o_ref[...] = jnp.dot(h_ref[...], w_ref[...], preferred_element_type=F32).astype(o_ref.dtype)


def _ln_mm(x, g, mod_t, sh_col, sc_col, w, col_start, n, out_dtype=F32):
    m, d = x.shape
    tm, tn = IN_ROW_TILE, _col_tile(n)
    assert col_start % tn == 0
    col0 = col_start // tn
    return pl.pallas_call(
        _ln_mm_kernel,
        grid=(m // tm, n // tn),
        in_specs=[pl.BlockSpec((tm, d), lambda i, j: (i, 0)),
                  pl.BlockSpec((1, d), lambda i, j: (0, 0)),
                  pl.BlockSpec((1, 1, d), lambda i, j: (i, 0, sh_col)),
                  pl.BlockSpec((1, 1, d), lambda i, j: (i, 0, sc_col)),
                  pl.BlockSpec((d, tn), lambda i, j: (0, col0 + j))],
        out_specs=pl.BlockSpec((tm, tn), lambda i, j: (i, j)),
        out_shape=jax.ShapeDtypeStruct((m, n), out_dtype),
        scratch_shapes=[pltpu.VMEM((tm, d), BF16)],
        compiler_params=_params("parallel", "arbitrary"),
        name="adaln_in_proj",
    )(x, g, mod_t, mod_t, w)


def _out_kernel(y_ref, w_ref, x_ref, gate_ref, o_ref):
    o_ref[...] = x_ref[...] + gate_ref[0] * jnp.dot(y_ref[...], w_ref[...], preferred_element_type=F32)


def _out_proj(y, w, x, mod_t, gate_col, first_tile=0):
    k = y.shape[1]
    d = w.shape[1]
    tm, tn = ROW_TILE, OUT_COL_TILE
    m = y.shape[0] - first_tile * tm
    return pl.pallas_call(
        _out_kernel,
        grid=(m // tm, d // tn),
        in_specs=[pl.BlockSpec((tm, k), lambda i, j: (i + first_tile, 0)),
                  pl.BlockSpec((k, tn), lambda i, j: (0, j)),
                  pl.BlockSpec((tm, tn), lambda i, j: (i + first_tile, j)),
                  pl.BlockSpec((1, 1, tn), lambda i, j: (i + first_tile, 0, gate_col * (d // tn) + j))],
        out_specs=pl.BlockSpec((tm, tn), lambda i, j: (i, j)),
        out_shape=jax.ShapeDtypeStruct((m, d), F32),
        compiler_params=_params("parallel", "arbitrary"),
        name="out_proj",
    )(y, w, x, mod_t)


def _mlp_kernel(x_ref, g_ref, sh_ref, sc_ref, gate_ref, w1_ref, w2_ref, fg_ref, o_ref, h_ref, acc_ref, *, final):
    f = pl.program_id(1)

    @pl.when(f == 0)
    def _():
        h_ref[...] = _adaln_rows(x_ref[...], g_ref[...], sh_ref[0], sc_ref[0]).astype(BF16)
        acc_ref[...] = jnp.zeros_like(acc_ref)

    a = jnp.dot(h_ref[...], w1_ref[0], preferred_element_type=F32)
    a = jnp.square(jnp.maximum(a, 0.0)).astype(BF16)
    acc_ref[...] += jnp.dot(a, w2_ref[0], preferred_element_type=F32)

    @pl.when(f == pl.num_programs(1) - 1)
    def _():
        out = x_ref[...] + gate_ref[0] * acc_ref[...]
        if final:
            out = out * lax.rsqrt(jnp.mean(out * out, axis=-1, keepdims=True) + NORM_EPS) * fg_ref[...]
        o_ref[...] = out


def _mlp(x, g, mod_t, w1, w2, layer, final_g, final):
    m, d = x.shape
    ff = w1.shape[2]
    tm, tf = ROW_TILE, MLP_FF_TILE
    return pl.pallas_call(
        functools.partial(_mlp_kernel, final=final),
        grid=(m // tm, ff // tf),
        in_specs=[pl.BlockSpec((tm, d), lambda i, f: (i, 0)),
                  pl.BlockSpec((1, d), lambda i, f: (0, 0)),
                  pl.BlockSpec((1, 1, d), lambda i, f: (i, 0, 3)),
                  pl.BlockSpec((1, 1, d), lambda i, f: (i, 0, 4)),
                  pl.BlockSpec((1, 1, d), lambda i, f: (i, 0, 5)),
                  pl.BlockSpec((1, d, tf), lambda i, f: (layer, 0, f)),
                  pl.BlockSpec((1, tf, d), lambda i, f: (layer, f, 0)),
                  pl.BlockSpec((1, d), lambda i, f: (0, 0))],
        out_specs=pl.BlockSpec((tm, d), lambda i, f: (i, 0)),
        out_shape=jax.ShapeDtypeStruct((m, d), F32),
        scratch_shapes=[pltpu.VMEM((tm, d), BF16), pltpu.VMEM((tm, d), F32)],
        compiler_params=_params("parallel", "arbitrary"),
        name="adaln_mlp",
    )(x, g, mod_t, mod_t, mod_t, w1, w2, final_g)


SCAN_CHUNK = 256


def _scan_geometry(bsz, lc, seq):
    assert lc % SCAN_CHUNK == 0 and seq % SCAN_CHUNK == 0
    return lc // SCAN_CHUNK, seq // SCAN_CHUNK


def _seg_chunk(c, n_ctx_c, n_lat_c, reverse):
    if reverse:
        return c >= n_ctx_c, jnp.where(c < n_ctx_c, n_ctx_c - 1 - c, n_lat_c - 1 - (c - n_ctx_c))
    return c >= n_ctx_c, jnp.where(c < n_ctx_c, c, c - n_ctx_c)


def _row_block(b, c, bsz, n_ctx_c, n_lat_c, reverse):
    is_lat, j = _seg_chunk(c, n_ctx_c, n_lat_c, reverse)
    return jnp.where(is_lat, bsz * n_ctx_c + b * n_lat_c + j, b * n_ctx_c + j)


def _time_iotas(q):
    t = lax.broadcasted_iota(jnp.int32, (q, 1), 0).astype(F32)
    s = lax.broadcasted_iota(jnp.int32, (1, q), 1).astype(F32)
    return t, s


def _dot_t(a, b):
    return lax.dot_general(a, b, (((0,), (0,)), ((), ())), preferred_element_type=F32)


def _dot_nt(a, b):
    return lax.dot_general(a, b, (((1,), (1,)), ((), ())), preferred_element_type=F32)


def _ret_kernel(*refs, reverse, n_ctx_c, n_heads, k_scale):
    if reverse:
        ld_ref, q_ref, k_ref, v_ref, cos_ref, sin_ref, o_ref, s_ref = refs
    else:
        ld_ref, q_ref, k_ref, v_ref, cos_ref, sin_ref, yb_ref, g_ref, o_ref, s_ref = refs
    c, h = pl.program_id(1), pl.program_id(2)
    qn = SCAN_CHUNK

    @pl.when(c == 0)
    def _():
        s_ref[h] = jnp.zeros(s_ref.shape[1:], F32)

    lg = ld_ref[(n_heads if reverse else 0) + h]
    t, s = _time_iotas(qn)
    is_lat = c >= n_ctx_c
    cos = jnp.where(is_lat, cos_ref[...], 1.0)
    sin = jnp.where(is_lat, sin_ref[...], 0.0)

    def rope(x):
        half = x.shape[1] // 2
        swapped = jnp.concatenate([pltpu.roll(x[:, :half], half // 2, 1), pltpu.roll(x[:, half:], half // 2, 1)], axis=1)
        return x * cos + swapped * sin

    q = rope(q_ref[...]).astype(BF16)
    k = rope(k_ref[...]) * k_scale
    v = v_ref[...].astype(BF16)
    if reverse:
        dmat = jnp.where(s >= t, jnp.exp((s - t) * lg), 0.0)
        q_dec, k_dec = jnp.exp((qn - t) * lg), jnp.exp(t * lg)
    else:
        dmat = jnp.where(t >= s, jnp.exp((t - s) * lg), 0.0)
        q_dec, k_dec = jnp.exp((t + 1.0) * lg), jnp.exp((qn - 1.0 - t) * lg)
    attn = (_dot_nt(q, k.astype(BF16)) * dmat).astype(BF16)
    state = s_ref[h]
    y = jnp.dot(attn, v, preferred_element_type=F32)
    y = y + jnp.dot(q, state.astype(BF16), preferred_element_type=F32) * q_dec
    chunk_dec = jnp.exp(jnp.full((1, 1), qn, F32) * lg)
    s_ref[h] = state * chunk_dec + _dot_t((k * k_dec).astype(BF16), v)
    if reverse:
        o_ref[...] = y
    else:
        y = y + yb_ref[...]
        mu = jnp.mean(y, axis=-1, keepdims=True)
        yc = y - mu
        var = jnp.mean(yc * yc, axis=-1, keepdims=True)
        g = g_ref[...]
        o_ref[...] = (yc * lax.rsqrt(var + NORM_EPS) * (g * jax.nn.sigmoid(g))).astype(o_ref.dtype)


def _ret_mixer(u, log_decay, bsz, lc, seq, d_model):
    n_ctx_c, n_lat_c = _scan_geometry(bsz, lc, seq)
    n_steps = n_ctx_c + n_lat_c
    qk_dim, v_dim = d_model // RET_HEADS, 2 * d_model // RET_HEADS
    rows = u.shape[0]
    cos_r, sin_r, cos_c, sin_c = _rope_tables(seq // GRID_W, qk_dim // 2)
    cos_t = jnp.asarray(np.concatenate([cos_r, cos_r, cos_c, cos_c], axis=1))
    sin_t = jnp.asarray(np.concatenate([-sin_r, sin_r, -sin_c, sin_c], axis=1))
    ld = log_decay.reshape(-1).astype(F32)
    y_b = None
    for reverse in (True, False):
        rb = functools.partial(_row_block, bsz=bsz, n_ctx_c=n_ctx_c, n_lat_c=n_lat_c, reverse=reverse)

        def tab(b, c, h, reverse=reverse):
            is_lat, j = _seg_chunk(c, n_ctx_c, n_lat_c, reverse)
            return (jnp.where(is_lat, j, 0), 0)

        in_specs = [pl.BlockSpec(memory_space=pltpu.SMEM),
                    pl.BlockSpec((SCAN_CHUNK, qk_dim), lambda b, c, h, rb=rb: (rb(b, c), h)),
                    pl.BlockSpec((SCAN_CHUNK, qk_dim), lambda b, c, h, rb=rb: (rb(b, c), RET_HEADS + h)),
                    pl.BlockSpec((SCAN_CHUNK, v_dim), lambda b, c, h, rb=rb: (rb(b, c), RET_HEADS + h)),
                    pl.BlockSpec((SCAN_CHUNK, qk_dim), tab),
                    pl.BlockSpec((SCAN_CHUNK, qk_dim), tab)]
        args = [ld, u, u, u, cos_t, sin_t]
        if not reverse:
            in_specs += [pl.BlockSpec((SCAN_CHUNK, v_dim), lambda b, c, h, rb=rb: (rb(b, c), h)),
                         pl.BlockSpec((SCAN_CHUNK, v_dim), lambda b, c, h, rb=rb: (rb(b, c), 2 * RET_HEADS + h))]
            args += [y_b, u]
        out = pl.pallas_call(
            functools.partial(_ret_kernel, reverse=reverse, n_ctx_c=n_ctx_c, n_heads=RET_HEADS, k_scale=qk_dim ** -0.5),
            grid=(bsz, n_steps, RET_HEADS),
            in_specs=in_specs,
            out_specs=pl.BlockSpec((SCAN_CHUNK, v_dim), lambda b, c, h, rb=rb: (rb(b, c), h)),
            out_shape=jax.ShapeDtypeStruct((rows, RET_HEADS * v_dim), F32 if reverse else BF16),
            scratch_shapes=[pltpu.VMEM((RET_HEADS, qk_dim, v_dim), F32)],
            compiler_params=_params("parallel", "arbitrary", "arbitrary"),
            name="retention_bwd" if reverse else "retention_fwd",
        )(*args)
        if reverse:
            y_b = out
    return out


CONV_COLS = 1024
CONV_ROWS = 8


def _conv_kernel(x_ref, prev_ref, next_ref, w_ref, b_ref, o_ref, *, n_ctx_c, n_lat_c, bsz, n_scaled, n_normed, scale,
                 head_dim):
    rb, cb = pl.program_id(0), pl.program_id(1)
    is_lat = rb >= bsz * n_ctx_c
    j = jnp.where(is_lat, (rb - bsz * n_ctx_c) % n_lat_c, rb % n_ctx_c)
    last = jnp.where(is_lat, n_lat_c - 1, n_ctx_c - 1)
    rows, cols = x_ref.shape
    halo_prev = jnp.where(j > 0, prev_ref[SUBLANES - 1:SUBLANES, :], 0.0)
    halo_next = jnp.where(j < last, next_ref[0:1, :], 0.0)
    w0, w1, w2, bias = w_ref[0:1, :], w_ref[1:2, :], w_ref[2:3, :], b_ref[...]
    row = lax.broadcasted_iota(jnp.int32, (CONV_ROWS, 1), 0)

    def run(mult):
        before = halo_prev
        for r0 in range(0, rows, CONV_ROWS):
            x = x_ref[r0:r0 + CONV_ROWS, :]
            after = x_ref[r0 + CONV_ROWS:r0 + CONV_ROWS + 1, :] if r0 + CONV_ROWS < rows else halo_next
            x_prev = jnp.where(row == 0, before, pltpu.roll(x, 1, 0))
            x_next = jnp.where(row == CONV_ROWS - 1, after, pltpu.roll(x, CONV_ROWS - 1, 0))
            y = w0 * x_prev + w1 * x + w2 * x_next + bias
            y = y * jax.nn.sigmoid(y)
            if mult is not None:
                parts = []
                for i in range(cols // head_dim):
                    p = y[:, i * head_dim:(i + 1) * head_dim]
                    parts.append(p * (lax.rsqrt(jnp.sum(p * p, axis=-1, keepdims=True) + L2_EPS) * mult))
                y = jnp.concatenate(parts, axis=1)
            o_ref[r0:r0 + CONV_ROWS, :] = y
            before = x[CONV_ROWS - 1:CONV_ROWS, :]

    if n_normed == 0:
        run(None)
    else:
        @pl.when(cb < n_normed)
        def _():
            run(jnp.where(cb < n_scaled, scale, 1.0))

        @pl.when(cb >= n_normed)
        def _():
            run(None)


def _conv_silu(u, w, b, n_cols, bsz, lc, seq, n_scaled=0, n_normed=0, scale=1.0, head_dim=LANES, col0=0):
    n_ctx_c, n_lat_c = _scan_geometry(bsz, lc, seq)
    rows = u.shape[0]
    n_rb = rows // SCAN_CHUNK
    sub = SCAN_CHUNK // SUBLANES
    n_halo = rows // SUBLANES
    return pl.pallas_call(
        functools.partial(_conv_kernel, n_ctx_c=n_ctx_c, n_lat_c=n_lat_c, bsz=bsz, n_scaled=n_scaled,
                          n_normed=n_normed, scale=scale, head_dim=head_dim),
        grid=(n_rb, n_cols // CONV_COLS),
        in_specs=[pl.BlockSpec((SCAN_CHUNK, CONV_COLS), lambda r, c: (r, c + col0)),
                  pl.BlockSpec((SUBLANES, CONV_COLS), lambda r, c: (jnp.maximum(r * sub - 1, 0), c + col0)),
                  pl.BlockSpec((SUBLANES, CONV_COLS), lambda r, c: (jnp.minimum(r * sub + sub, n_halo - 1), c + col0)),
                  pl.BlockSpec((CONV_W, CONV_COLS), lambda r, c: (0, c)),
                  pl.BlockSpec((1, CONV_COLS), lambda r, c: (0, c))],
        out_specs=pl.BlockSpec((SCAN_CHUNK, CONV_COLS), lambda r, c: (r, c)),
        out_shape=jax.ShapeDtypeStruct((rows, n_cols), F32),
        compiler_params=_params("parallel", "parallel"),
        name="conv_silu",
    )(u, u, u, w, b)


GDN_SUB = 64


GDN_BASE = 8
GDN_HEADS_PER_STEP = 8


def _split3(x):
    x1 = x.astype(BF16)
    r1 = x - x1.astype(F32)
    x2 = r1.astype(BF16)
    return x1, x2, (r1 - x2.astype(F32)).astype(BF16)


def _sel_rows(m, x):
    mb = m.astype(BF16)
    return sum(jnp.dot(mb, p, preferred_element_type=F32) for p in _split3(x))


def _sel_cols(x, m):
    mb = m.astype(BF16)
    return sum(jnp.dot(p, mb, preferred_element_type=F32) for p in _split3(x))


def _sel_rows_t(x, m):
    mb = m.astype(BF16)
    return sum(lax.dot_general(p, mb, (((0,), (1,)), ((), ())), preferred_element_type=F32) for p in _split3(x))


def _mm_bf16(a, b):
    return jnp.dot(a.astype(BF16), b.astype(BF16), preferred_element_type=F32)


def _unit_tri_inverse_minus_eye(mats, ti, si):
    def same(n):
        s = int(math.log2(n))
        return lax.shift_right_logical(ti, s) == lax.shift_right_logical(si, s)

    base = same(GDN_BASE)
    ps = [jnp.where(base, a, 0.0) for a in mats]
    ns = [-p for p in ps]
    for _ in range(int(math.log2(GDN_BASE)) - 1):
        ps = [_mm_bf16(p, p) for p in ps]
        ns = [n + p + _mm_bf16(n, p) for n, p in zip(ns, ps)]
    b = GDN_BASE
    while b < GDN_SUB:
        join = same(2 * b) & jnp.logical_not(same(b))
        cs = [jnp.where(join, a, 0.0) for a in mats]
        ms = [c + _mm_bf16(c, n) for c, n in zip(cs, ns)]
        ns = [n - (m + _mm_bf16(n, m)) for n, m in zip(ns, ms)]
        b *= 2
    return ns


def _softplus(x):
    return jnp.maximum(x, 0.0) + jnp.log(1.0 + jnp.exp(-jnp.abs(x)))


def _gdn_kernel(*refs, reverse, n_kh, n_r, hd):
    if reverse:
        q_ref, k_ref, v_ref, tail_ref, prow_ref, arow_ref, o_ref, s_ref = refs
    else:
        q_ref, k_ref, v_ref, tail_ref, prow_ref, arow_ref, yb_ref, z_ref, ng_ref, o_ref, s_ref = refs
    g, c = pl.program_id(1), pl.program_id(2)
    qn = SCAN_CHUNK
    n_sub = qn // GDN_SUB
    n_gate = tail_ref.shape[1] // 2

    @pl.when(c == 0)
    def _():
        s_ref[...] = jnp.zeros_like(s_ref)

    tail = tail_ref[...]
    lane = lax.broadcasted_iota(jnp.int32, (1, 2 * n_gate), 1)
    gates = jnp.where(lane < n_gate, jax.nn.sigmoid(tail), -jnp.exp(arow_ref[...]) * _softplus(tail + prow_ref[...]))
    nh = n_kh * n_r
    li = lax.broadcasted_iota(jnp.int32, (2 * n_gate, 2 * nh), 0)
    ji = lax.broadcasted_iota(jnp.int32, (2 * n_gate, 2 * nh), 1)
    col = jnp.where(ji < nh, ji, n_gate + ji - nh) + (n_gate // 2 if reverse else 0) + g * nh
    gsel = _sel_cols(gates, (li == col).astype(F32))
    lw = lax.broadcasted_iota(jnp.int32, (2 * n_gate, 2 * nh * hd), 0)
    jw = lax.shift_right_logical(lax.broadcasted_iota(jnp.int32, (2 * n_gate, 2 * nh * hd), 1), int(math.log2(hd)))
    colw = jnp.where(jw < nh, jw, n_gate + jw - nh) + (n_gate // 2 if reverse else 0) + g * nh
    gwide = _sel_cols(gates, (lw == colw).astype(F32))

    ti = lax.broadcasted_iota(jnp.int32, (qn, 1), 0)
    si = lax.broadcasted_iota(jnp.int32, (1, qn), 1)
    shift = int(math.log2(GDN_SUB))
    same = lax.shift_right_logical(ti, shift) == lax.shift_right_logical(si, shift)
    if reverse:
        incl, strict = same & (si >= ti), same & (si > ti)
    else:
        incl, strict = same & (si <= ti), same & (si < ti)
    cs = incl.astype(F32)
    cum_w = _sel_rows(cs, gwide[:, nh * hd:])
    cum_t = _sel_rows_t(gsel, cs)
    reps = qn // hd

    order = range(n_sub - 1, -1, -1) if reverse else range(n_sub)
    shared = []
    for kh in range(n_kh):
        k = k_ref[:, kh * hd:(kh + 1) * hd]
        qb, kb = q_ref[:, kh * hd:(kh + 1) * hd].astype(BF16), k.astype(BF16)
        shared.append((k, qb, _dot_nt(kb, kb), _dot_nt(qb, kb)))
    heads = range(nh)
    beta = [gwide[:, j * hd:(j + 1) * hd] for j in heads]
    cum_c = [cum_w[:, j * hd:(j + 1) * hd] for j in heads]
    decay = [jnp.exp(jnp.where(incl, jnp.concatenate([cum_c[j]] * reps, axis=1) - cum_t[nh + j:nh + j + 1, :],
                               -jnp.inf)) for j in heads]
    amat = [jnp.where(strict, jnp.concatenate([beta[j]] * reps, axis=1) * shared[j // n_r][2] * decay[j], 0.0)
            for j in heads]
    inv_off = _unit_tri_inverse_minus_eye(amat, ti, si)
    rhs = [jnp.concatenate([v_ref[:, j * hd:(j + 1) * hd] * beta[j],
                            shared[j // n_r][0] * (beta[j] * jnp.exp(cum_c[j]))], axis=1) for j in heads]
    sol = [rhs[j] + _mm_bf16(inv_off[j], rhs[j]) for j in heads]
    state = [s_ref[j] for j in heads]
    v_new = [[None] * n_sub for _ in heads]
    inter = [[None] * n_sub for _ in heads]
    for i in order:
        sl = slice(i * GDN_SUB, (i + 1) * GDN_SUB)
        end = i * GDN_SUB if reverse else (i + 1) * GDN_SUB - 1
        sb = [state[j].astype(BF16) for j in heads]
        for j in heads:
            v_new[j][i] = sol[j][sl, :hd] - jnp.dot(sol[j][sl, hd:].astype(BF16), sb[j], preferred_element_type=F32)
            inter[j][i] = jnp.dot(shared[j // n_r][1][sl], sb[j], preferred_element_type=F32)
        for j in heads:
            cum_end = cum_c[j][end:end + 1, :]
            k_end = (shared[j // n_r][0][sl] * jnp.exp(cum_end - cum_c[j][sl])).astype(BF16)
            state[j] = jnp.exp(cum_end) * state[j] + _dot_t(k_end, v_new[j][i].astype(BF16))
    outs = []
    for j in heads:
        s_ref[j] = state[j]
        attn = (shared[j // n_r][3] * decay[j]).astype(BF16)
        y = jnp.dot(attn, jnp.concatenate(v_new[j], axis=0).astype(BF16), preferred_element_type=F32)
        y = y + jnp.concatenate(inter[j], axis=0) * jnp.exp(cum_c[j])
        if not reverse:
            y = y + yb_ref[:, j * hd:(j + 1) * hd]
            y = y * lax.rsqrt(jnp.mean(y * y, axis=-1, keepdims=True) + NORM_EPS) * ng_ref[...]
            z = z_ref[:, j * hd:(j + 1) * hd]
            y = y * (z * jax.nn.sigmoid(z))
        outs.append(y)
    o_ref[...] = jnp.concatenate(outs, axis=1).astype(o_ref.dtype)


def _gdn_mixer(u, tail, conv_w, dt_bias, a_log, norm_g, bsz, lc, seq, d_model):
    n_ctx_c, n_lat_c = _scan_geometry(bsz, lc, seq)
    n_steps = n_ctx_c + n_lat_c
    hd = GDN_HEAD_DIM
    k_heads = d_model // hd
    n_r = 2
    rows = u.shape[0]
    conv_ch = 4 * d_model
    nq = d_model // CONV_COLS
    qkv = _conv_silu(u, conv_w, jnp.zeros((1, conv_ch), F32), conv_ch, bsz, lc, seq, n_scaled=nq, n_normed=2 * nq,
                     scale=hd ** -0.5, head_dim=hd)
    n_gate = tail.shape[1] // 2
    prow = jnp.concatenate([jnp.zeros((1, n_gate), F32), dt_bias.reshape(1, n_gate)], axis=1)
    arow = jnp.concatenate([jnp.zeros((1, n_gate), F32), a_log.reshape(1, n_gate)], axis=1)
    y_b = None
    for reverse in (True, False):
        rb = functools.partial(_row_block, bsz=bsz, n_ctx_c=n_ctx_c, n_lat_c=n_lat_c, reverse=reverse)
        const = lambda b, g, c: (0, 0)
        qk_w, v_w = GDN_HEADS_PER_STEP * hd, GDN_HEADS_PER_STEP * n_r * hd
        n_groups = k_heads // GDN_HEADS_PER_STEP
        in_specs = [pl.BlockSpec((SCAN_CHUNK, qk_w), lambda b, g, c, rb=rb: (rb(b, c), g)),
                    pl.BlockSpec((SCAN_CHUNK, qk_w), lambda b, g, c, rb=rb: (rb(b, c), n_groups + g)),
                    pl.BlockSpec((SCAN_CHUNK, v_w), lambda b, g, c, rb=rb: (rb(b, c), n_groups + g)),
                    pl.BlockSpec((SCAN_CHUNK, 2 * n_gate), lambda b, g, c, rb=rb: (rb(b, c), 0)),
                    pl.BlockSpec((1, 2 * n_gate), const),
                    pl.BlockSpec((1, 2 * n_gate), const)]
        args = [qkv, qkv, qkv, tail, prow, arow]
        if not reverse:
            in_specs += [pl.BlockSpec((SCAN_CHUNK, v_w), lambda b, g, c, rb=rb: (rb(b, c), g)),
                         pl.BlockSpec((SCAN_CHUNK, v_w), lambda b, g, c, rb=rb: (rb(b, c), 2 * n_groups + g)),
                         pl.BlockSpec((1, hd), const)]
            args += [y_b, u, norm_g.reshape(1, hd)]
        out = pl.pallas_call(
            functools.partial(_gdn_kernel, reverse=reverse, n_kh=GDN_HEADS_PER_STEP, n_r=n_r, hd=hd),
            grid=(bsz, n_groups, n_steps),
            in_specs=in_specs,
            out_specs=pl.BlockSpec((SCAN_CHUNK, v_w), lambda b, g, c, rb=rb: (rb(b, c), g)),
            out_shape=jax.ShapeDtypeStruct((rows, k_heads * n_r * hd), F32 if reverse else BF16),
            scratch_shapes=[pltpu.VMEM((GDN_HEADS_PER_STEP * n_r, hd, hd), F32)],
            compiler_params=_params("parallel", "parallel", "arbitrary"),
            name="gdn_bwd" if reverse else "gdn_fwd",
        )(*args)
        if reverse:
            y_b = out
    return out


def _ssd_kernel(*refs, reverse, n_r, hd):
    if reverse:
        x_ref, b_ref, c_ref, tail_ref, dtb_ref, alog_ref, o_ref, s_ref = refs
    else:
        x_ref, b_ref, c_ref, tail_ref, dtb_ref, alog_ref, yb_ref, z_ref, d_ref, ng_ref, o_ref, s_ref = refs
    g, c = pl.program_id(1), pl.program_id(2)
    qn = SCAN_CHUNK
    n_lane = tail_ref.shape[1]
    width = n_r * hd

    @pl.when(c == 0)
    def _():
        s_ref[...] = jnp.zeros_like(s_ref)

    dt_all = _softplus(tail_ref[...] + dtb_ref[...])
    la_all = -jnp.exp(alog_ref[...]) * dt_all
    li = lax.broadcasted_iota(jnp.int32, (n_lane, n_r), 0)
    ji = lax.broadcasted_iota(jnp.int32, (n_lane, n_r), 1)
    sel = (li == ji + (n_lane // 2 if reverse else 0) + g * n_r).astype(F32)
    dt, la = _sel_cols(dt_all, sel), _sel_cols(la_all, sel)

    ti = lax.broadcasted_iota(jnp.int32, (qn, 1), 0)
    si = lax.broadcasted_iota(jnp.int32, (1, qn), 1)
    before = (si >= ti) if reverse else (si <= ti)
    cs = before.astype(F32)
    cum = _sel_rows(cs, la)
    cum_t = _sel_rows_t(la, cs)
    ei = lax.broadcasted_iota(jnp.int32, (n_r, width), 0)
    el = lax.broadcasted_iota(jnp.int32, (n_r, width), 1)
    expand = (lax.shift_right_logical(el, int(math.log2(hd))) == ei).astype(F32)
    dt_x, cum_x = _sel_cols(dt, expand), _sel_cols(cum, expand)

    xs = x_ref[...]
    v = xs * dt_x
    bm, cm = b_ref[...].astype(BF16), c_ref[...].astype(BF16)
    scores = _dot_nt(cm, bm)
    lane = lax.broadcasted_iota(jnp.int32, (1, 2 * hd), 1)
    tiles = []
    for p in range(n_r // 2):
        vt = v[:, 2 * p * hd:2 * (p + 1) * hd]
        acc = None
        for h in (2 * p, 2 * p + 1):
            decay = jnp.exp(jnp.where(before, cum[:, h:h + 1] - cum_t[h:h + 1, :], -jnp.inf))
            vh = jnp.where((lane >= hd) if h % 2 else (lane < hd), vt, 0.0).astype(BF16)
            part = jnp.dot((scores * decay).astype(BF16), vh, preferred_element_type=F32)
            acc = part if acc is None else acc + part
        tiles.append(acc)
    state = s_ref[...]
    y = jnp.concatenate(tiles, axis=1) + jnp.dot(cm, state.astype(BF16), preferred_element_type=F32) * jnp.exp(cum_x)
    end = 0 if reverse else qn - 1
    cum_end = cum_x[end:end + 1]
    s_ref[...] = state * jnp.exp(cum_end) + _dot_t(bm, (v * jnp.exp(cum_end - cum_x)).astype(BF16))
    if reverse:
        o_ref[...] = y
    else:
        y = y + yb_ref[...] + d_ref[...] * xs
        z = z_ref[...]
        y = y * (z * jax.nn.sigmoid(z))
        y = y * lax.rsqrt(jnp.mean(y * y, axis=-1, keepdims=True) + NORM_EPS) * ng_ref[...]
        o_ref[...] = y.astype(o_ref.dtype)


def _ssd_mixer(u, tail, conv_w, conv_b, dt_bias, a_log, d_skip, norm_g, bsz, lc, seq):
    n_ctx_c, n_lat_c = _scan_geometry(bsz, lc, seq)
    n_steps = n_ctx_c + n_lat_c
    d_inner = norm_g.shape[0]
    hd, st = SSD_HEAD_DIM, SSD_STATE
    heads = d_inner // hd
    n_r = heads // SSD_GROUPS
    width = n_r * hd
    rows = u.shape[0]
    conv_ch = d_inner + 2 * SSD_GROUPS * st
    xbc = _conv_silu(u, conv_w, conv_b.reshape(1, conv_ch), conv_ch, bsz, lc, seq, col0=d_inner // CONV_COLS)
    d_x = jnp.repeat(d_skip, hd).reshape(1, d_inner)
    y_b = None
    for reverse in (True, False):
        rb = functools.partial(_row_block, bsz=bsz, n_ctx_c=n_ctx_c, n_lat_c=n_lat_c, reverse=reverse)
        const = lambda b, g, c: (0, 0)
        grp = lambda b, g, c: (0, g)
        wide = lambda b, g, c, rb=rb: (rb(b, c), g)
        in_specs = [pl.BlockSpec((SCAN_CHUNK, width), wide),
                    pl.BlockSpec((SCAN_CHUNK, st), lambda b, g, c, rb=rb: (rb(b, c), d_inner // st + g)),
                    pl.BlockSpec((SCAN_CHUNK, st), lambda b, g, c, rb=rb: (rb(b, c), d_inner // st + SSD_GROUPS + g)),
                    pl.BlockSpec((SCAN_CHUNK, 2 * heads), lambda b, g, c, rb=rb: (rb(b, c), 0)),
                    pl.BlockSpec((1, 2 * heads), const),
                    pl.BlockSpec((1, 2 * heads), const)]
        args = [xbc, xbc, xbc, tail, dt_bias.reshape(1, 2 * heads), a_log.reshape(1, 2 * heads)]
        if not reverse:
            in_specs += [pl.BlockSpec((SCAN_CHUNK, width), wide), pl.BlockSpec((SCAN_CHUNK, width), wide),
                         pl.BlockSpec((1, width), grp), pl.BlockSpec((1, width), grp)]
            args += [y_b, u, d_x, norm_g.reshape(1, d_inner)]
        out = pl.pallas_call(
            functools.partial(_ssd_kernel, reverse=reverse, n_r=n_r, hd=hd),
            grid=(bsz, SSD_GROUPS, n_steps),
            in_specs=in_specs,
            out_specs=pl.BlockSpec((SCAN_CHUNK, width), wide),
            out_shape=jax.ShapeDtypeStruct((rows, d_inner), F32 if reverse else BF16),
            scratch_shapes=[pltpu.VMEM((st, width), F32)],
            compiler_params=_params("parallel", "parallel", "arbitrary"),
            name="ssd_bwd" if reverse else "ssd_fwd",
        )(*args)
        if reverse:
            y_b = out
    return out


HG_SUB = 64
HG_BLK = 8


def _log1p(x):
    return jnp.log(1.0 + x)


def _hgrn_kernel(*refs, reverse):
    if reverse:
        q_ref, f_ref, i_ref, lb_ref, o_ref, st_ref = refs
    else:
        q_ref, f_ref, i_ref, lb_ref, yb_ref, g_ref, ng_ref, o_ref, st_ref = refs
    c = pl.program_id(2)
    qn = SCAN_CHUNK
    hd = q_ref.shape[1]
    n_sub, n_blk = qn // HG_SUB, HG_SUB // HG_BLK

    @pl.when(c == 0)
    def _():
        st_ref[...] = jnp.zeros_like(st_ref)

    q, f, v, lb = q_ref[...], f_ref[...], i_ref[...], lb_ref[...]
    log_sig = jnp.minimum(f, 0.0) - _log1p(jnp.exp(-jnp.abs(f)))
    ga, gb = jnp.log(lb), _log1p(-lb) + log_sig
    log_f = jnp.maximum(ga, gb) + _log1p(jnp.exp(-jnp.abs(ga - gb)))
    k = (1.0 - lb) * jax.nn.sigmoid(-f)

    ti = lax.broadcasted_iota(jnp.int32, (qn, 1), 0)
    si = lax.broadcasted_iota(jnp.int32, (1, qn), 1)
    before = (si >= ti) if reverse else (si <= ti)
    same = lambda n: lax.shift_right_logical(ti, int(math.log2(n))) == lax.shift_right_logical(si, int(math.log2(n)))
    cum_sub = _sel_rows((same(HG_SUB) & before).astype(F32), log_f)
    cum_blk = _sel_rows((same(HG_BLK) & before).astype(F32), log_f)
    q_blk = q * jnp.exp(cum_blk)
    q_sub = (q * jnp.exp(cum_sub)).astype(BF16)

    ones = jnp.ones((hd, hd), BF16)
    lane = lax.broadcasted_iota(jnp.int32, (1, hd), 1)
    row_sub = lax.broadcasted_iota(jnp.int32, (HG_SUB, 1), 0)
    row_blk = lax.broadcasted_iota(jnp.int32, (HG_BLK, 1), 0)
    zeros_sub = jnp.zeros((hd - HG_SUB, hd), F32)
    y_intra, kvt, chunk_dec = [], [], []
    for i in range(n_sub):
        r0 = i * HG_SUB
        sl = slice(r0, r0 + HG_SUB)
        cs, ks, vs = cum_sub[sl], k[sl], v[sl]
        end = r0 if reverse else r0 + HG_SUB - 1
        cum_end = cum_sub[end:end + 1]
        kvt.append(_dot_t(vs.astype(BF16), (ks * jnp.exp(cum_end - cs)).astype(BF16)))
        chunk_dec.append(jnp.exp(cum_end))
        a_rows = []
        for a in range(n_blk):
            b0 = r0 + a * HG_BLK
            bl = slice(b0, b0 + HG_BLK)
            qb, kb, cb = q[bl], k[bl], cum_blk[bl]
            tiles = []
            for s in range(HG_BLK):
                ok = (row_blk <= s) if reverse else (row_blk >= s)
                e = jnp.exp(jnp.where(ok, cb - cb[s:s + 1], -jnp.inf))
                tiles.append(qb * e * kb[s:s + 1])
            sums = jnp.dot(jnp.concatenate(tiles, axis=0).astype(BF16), ones, preferred_element_type=F32)
            acc = jnp.zeros((HG_BLK, hd), F32)
            for s in range(HG_BLK):
                acc = acc + jnp.where(lane == a * HG_BLK + s, sums[s * HG_BLK:(s + 1) * HG_BLK], 0.0)
            has_earlier = (a < n_blk - 1) if reverse else (a > 0)
            if has_earlier:
                ref_row = b0 + HG_BLK if reverse else b0 - 1
                earlier = (row_sub >= (a + 1) * HG_BLK) if reverse else (row_sub < a * HG_BLK)
                kt = ks * jnp.exp(jnp.where(earlier, cum_sub[ref_row:ref_row + 1] - cs, -jnp.inf))
                kt = jnp.concatenate([kt, zeros_sub], axis=0).astype(BF16)
                acc = acc + _dot_nt(q_blk[bl].astype(BF16), kt)
            a_rows.append(acc)
        attn = jnp.concatenate(a_rows, axis=0).astype(BF16)
        v_pad = jnp.concatenate([vs, zeros_sub], axis=0).astype(BF16)
        y_intra.append(jnp.dot(attn, v_pad, preferred_element_type=F32))

    state = st_ref[...]
    ys = [None] * n_sub
    for i in (range(n_sub - 1, -1, -1) if reverse else range(n_sub)):
        ys[i] = y_intra[i] + _dot_nt(q_sub[i * HG_SUB:(i + 1) * HG_SUB], state.astype(BF16))
        state = state * chunk_dec[i] + kvt[i]
    st_ref[...] = state
    y = jnp.concatenate(ys, axis=0)
    if reverse:
        o_ref[...] = y
    else:
        y = y + yb_ref[...]
        y = y * lax.rsqrt(jnp.mean(y * y, axis=-1, keepdims=True) + NORM_EPS) * ng_ref[...]
        g = g_ref[...]
        o_ref[...] = (y * (g * jax.nn.sigmoid(g))).astype(o_ref.dtype)


def _hgrn_mixer(u, lb, norm_g, bsz, lc, seq):
    n_ctx_c, n_lat_c = _scan_geometry(bsz, lc, seq)
    n_steps = n_ctx_c + n_lat_c
    hd = HGRN_EXPAND
    d_model = lb.shape[0]
    heads = d_model // hd
    rows = u.shape[0]
    lb2, ng2 = lb.reshape(1, d_model), norm_g.reshape(1, d_model)
    y_b = None
    for reverse in (True, False):
        rb = functools.partial(_row_block, bsz=bsz, n_ctx_c=n_ctx_c, n_lat_c=n_lat_c, reverse=reverse)
        col = lambda seg: (lambda b, h, c, rb=rb: (rb(b, c), seg * heads + h))
        par = lambda b, h, c: (0, h)
        in_specs = [pl.BlockSpec((SCAN_CHUNK, hd), col(0)),
                    pl.BlockSpec((SCAN_CHUNK, hd), col(2 if reverse else 1)),
                    pl.BlockSpec((SCAN_CHUNK, hd), col(3)),
                    pl.BlockSpec((1, hd), par)]
        args = [u, u, u, lb2]
        if not reverse:
            in_specs += [pl.BlockSpec((SCAN_CHUNK, hd), col(0)), pl.BlockSpec((SCAN_CHUNK, hd), col(4)),
                         pl.BlockSpec((1, hd), par)]
            args += [y_b, u, ng2]
        out = pl.pallas_call(
            functools.partial(_hgrn_kernel, reverse=reverse),
            grid=(bsz, heads, n_steps),
            in_specs=in_specs,
            out_specs=pl.BlockSpec((SCAN_CHUNK, hd), col(0)),
            out_shape=jax.ShapeDtypeStruct((rows, d_model), F32 if reverse else BF16),
            scratch_shapes=[pltpu.VMEM((hd, hd), F32)],
            compiler_params=_params("parallel", "parallel", "arbitrary"),
            name="hgrn_bwd" if reverse else "hgrn_fwd",
        )(*args)
        if reverse:
            y_b = out
    return out


def _rope_tables(rows, half):
    pos = np.arange(rows * GRID_W)
    inv_freq = np.float32(ROPE_BASE) ** (-(np.arange(0, half, 2, dtype=np.float32) / np.float32(half)))
    out = []
    for p in ((pos // GRID_W).astype(np.float32), (pos % GRID_W).astype(np.float32)):
        ang = (p[:, None] * inv_freq.astype(np.float32)).astype(np.float32).astype(np.float64)
        out += [np.cos(ang).astype(np.float32), np.sin(ang).astype(np.float32)]
    return out


def _lower_bound(lb_logits, layer):
    p = jax.nn.softmax(lb_logits.astype(F32), axis=0)
    return jnp.cumsum(p, axis=0)[layer] - p[0]


def kernel(x, c, ctx, c_ctx, ada_w, ada_b, norm_g, mlp_w1, mlp_w2, final_g, ssd_w_in, ssd_conv_w, ssd_conv_b,
           ssd_dt_bias, ssd_a_log, ssd_d, ssd_norm_g, ssd_w_out, ret_w_in, ret_log_decay, ret_w_out, hgrn_w_in,
           hgrn_lb_logits, hgrn_norm_g, hgrn_w_out, gdn_w_in, gdn_conv_w, gdn_dt_bias, gdn_a_log, gdn_norm_g,
           gdn_w_out):
    bsz, seq, d = x.shape
    lc = ctx.shape[1]
    depth = ada_w.shape[0]
    n_ctx = bsz * lc
    assert n_ctx % IN_ROW_TILE == 0 and seq % IN_ROW_TILE == 0 and IN_ROW_TILE % ROW_TILE == 0
    assert bsz + 1 <= SUBLANES

    cond_pad = jnp.concatenate([c, c_ctx[None], jnp.zeros((SUBLANES - bsz - 1, d), F32)], axis=0)
    mod = _ada_mod(cond_pad, ada_w, ada_b)

    def tile_rows(tile):
        return jnp.asarray([bsz] * (n_ctx // tile) + [b for b in range(bsz) for _ in range(seq // tile)], jnp.int32)

    tile_row, in_tile_row = tile_rows(ROW_TILE), tile_rows(IN_ROW_TILE)
    n_ctx_tiles = n_ctx // ROW_TILE

    xr = jnp.concatenate([ctx.reshape(n_ctx, d), x.reshape(bsz * seq, d)], axis=0)
    w1_all, w2_all = mlp_w1.astype(BF16), mlp_w2.astype(BF16)

    for i in range(depth):
        mixer, occ = i % 4, i // 4
        keep_ctx = i < depth - 1
        mod_t = mod[i][tile_row][:, None, :]
        g0, g1 = norm_g[i, 0][None], norm_g[i, 1][None]
        if mixer == 0:
            w_in, w_out = ssd_w_in[occ], ssd_w_out[occ]
        elif mixer == 1:
            w_in, w_out = ret_w_in[occ], ret_w_out[occ]
        elif mixer == 2:
            w_in, w_out = hgrn_w_in[occ], hgrn_w_out[occ]
        else:
            w_in, w_out = gdn_w_in[occ], gdn_w_out[occ]
        n_in = w_in.shape[1]
        n_main = (n_in // 1024) * 1024 if n_in % 1024 else n_in
        w_in = w_in.astype(BF16)
        mod_in = mod[i][in_tile_row][:, None, :]
        u = _ln_mm(xr, g0, mod_in, 0, 1, w_in, 0, n_main)
        tail = _ln_mm(xr, g0, mod_in, 0, 1, w_in, n_main, n_in - n_main) if n_main != n_in else None
        if mixer == 0:
            yr = _ssd_mixer(u, tail, ssd_conv_w[occ], ssd_conv_b[occ], ssd_dt_bias[occ], ssd_a_log[occ], ssd_d[occ],
                            ssd_norm_g[occ], bsz, lc, seq)
        elif mixer == 1:
            yr = _ret_mixer(u, ret_log_decay[occ], bsz, lc, seq, d)
        elif mixer == 2:
            yr = _hgrn_mixer(u, _lower_bound(hgrn_lb_logits, i), hgrn_norm_g[occ], bsz, lc, seq)
        else:
            yr = _gdn_mixer(u, tail, gdn_conv_w[occ], gdn_dt_bias[occ], gdn_a_log[occ], gdn_norm_g[occ], bsz, lc,
                            seq, d)
        first_tile = 0 if keep_ctx else n_ctx_tiles
        xr = _out_proj(yr, w_out.astype(BF16), xr, mod_t, 2, first_tile)
        mod_t = mod_t[first_tile:]
        xr = _mlp(xr, g1, mod_t, w1_all, w2_all, i, final_g[None], final=not keep_ctx)
    return xr.reshape(bsz, seq, d)
```

```python
import functools
import math

import jax
import jax.numpy as jnp
import numpy as np
from jax import lax
from jax.experimental import pallas as pl
from jax.experimental.pallas import tpu as pltpu

F32 = jnp.float32
BF16 = jnp.bfloat16

LANES = 128
SUBLANES = 8

GRID_W = 64
CONV_W = 3
NORM_EPS = 1e-6
L2_EPS = 1e-6
ROPE_BASE = 10000.0
SSD_HEAD_DIM = 64
SSD_GROUPS = 8
SSD_STATE = 128
RET_HEADS = 8
HGRN_EXPAND = 128
GDN_HEAD_DIM = 128

ROW_TILE = 512
IN_ROW_TILE = 1024
MLP_FF_TILE = 1024
OUT_COL_TILE = 1024
ADA_COL_TILE = 1024
VMEM_LIMIT = 56 * 1024 * 1024


def _params(*sem):
    return pltpu.CompilerParams(dimension_semantics=sem, vmem_limit_bytes=VMEM_LIMIT)


def _col_tile(n, cap=1536):
    best = LANES
    for t in range(LANES, cap + 1, LANES):
        if n % t == 0:
            best = t
    return best


def _ada_kernel(c_ref, w_ref, b_ref, o_ref):
    c = c_ref[...]
    c = (c * jax.nn.sigmoid(c)).astype(BF16)
    o_ref[0] = jnp.dot(c, w_ref[0].astype(BF16), preferred_element_type=F32) + b_ref[0]


def _ada_mod(cond_pad, ada_w, ada_b):
    depth, d, n = ada_w.shape
    tn = ADA_COL_TILE
    return pl.pallas_call(
        _ada_kernel,
        grid=(depth, n // tn),
        in_specs=[pl.BlockSpec((SUBLANES, d), lambda l, j: (0, 0)),
                  pl.BlockSpec((1, d, tn), lambda l, j: (l, 0, j)),
                  pl.BlockSpec((1, 1, tn), lambda l, j: (l, 0, j))],
        out_specs=pl.BlockSpec((1, SUBLANES, tn), lambda l, j: (l, 0, j)),
        out_shape=jax.ShapeDtypeStruct((depth, SUBLANES, n), F32),
        compiler_params=_params("parallel", "parallel"),
        name="ada_mod",
    )(cond_pad, ada_w, ada_b.reshape(depth, 1, n))


def _adaln_rows(x, g, sh, sc):
    y = x * lax.rsqrt(jnp.mean(x * x, axis=-1, keepdims=True) + NORM_EPS)
    return y * g * (1.0 + sc) + sh


def _ln_mm_kernel(x_ref, g_ref, sh_ref, sc_ref, w_ref, o_ref, h_ref):
    @pl.when(pl.program_id(1) == 0)
    def _():
        h_ref[...] = _adaln_rows(x_ref[...], g_ref[...], sh_ref[0], sc_ref[0]).astype(BF16)

    o_ref[...] = jnp.dot(h_ref[...], w_ref[...], preferred_element_type=F32).astype(o_ref.dtype)


def _ln_mm(x, g, mod_t, sh_col, sc_col, w, col_start, n, out_dtype=F32):
    m, d = x.shape
    tm, tn = IN_ROW_TILE, _col_tile(n)
    assert col_start % tn == 0
    col0 = col_start // tn
    return pl.pallas_call(
        _ln_mm_kernel,
        grid=(m // tm, n // tn),
        in_specs=[pl.BlockSpec((tm, d), lambda i, j: (i, 0)),
                  pl.BlockSpec((1, d), lambda i, j: (0, 0)),
                  pl.BlockSpec((1, 1, d), lambda i, j: (i, 0, sh_col)),
                  pl.BlockSpec((1, 1, d), lambda i, j: (i, 0, sc_col)),
                  pl.BlockSpec((d, tn), lambda i, j: (0, col0 + j))],
        out_specs=pl.BlockSpec((tm, tn), lambda i, j: (i, j)),
        out_shape=jax.ShapeDtypeStruct((m, n), out_dtype),
        scratch_shapes=[pltpu.VMEM((tm, d), BF16)],
        compiler_params=_params("parallel", "arbitrary"),
        name="adaln_in_proj",
    )(x, g, mod_t, mod_t, w)


def _out_kernel(y_ref, w_ref, x_ref, gate_ref, o_ref):
    o_ref[...] = x_ref[...] + gate_ref[0] * jnp.dot(y_ref[...], w_ref[...], preferred_element_type=F32)


def _out_proj(y, w, x, mod_t, gate_col, first_tile=0):
    k = y.shape[1]
    d = w.shape[1]
    tm, tn = ROW_TILE, OUT_COL_TILE
    m = y.shape[0] - first_tile * tm
    return pl.pallas_call(
        _out_kernel,
        grid=(m // tm, d // tn),
        in_specs=[pl.BlockSpec((tm, k), lambda i, j: (i + first_tile, 0)),
                  pl.BlockSpec((k, tn), lambda i, j: (0, j)),
                  pl.BlockSpec((tm, tn), lambda i, j: (i + first_tile, j)),
                  pl.BlockSpec((1, 1, tn), lambda i, j: (i + first_tile, 0, gate_col * (d // tn) + j))],
        out_specs=pl.BlockSpec((tm, tn), lambda i, j: (i, j)),
        out_shape=jax.ShapeDtypeStruct((m, d), F32),
        compiler_params=_params("parallel", "arbitrary"),
        name="out_proj",
    )(y, w, x, mod_t)


def _mlp_kernel(x_ref, g_ref, sh_ref, sc_ref, gate_ref, w1_ref, w2_ref, fg_ref, o_ref, h_ref, acc_ref, *, final):
    f = pl.program_id(1)

    @pl.when(f == 0)
    def _():
        h_ref[...] = _adaln_rows(x_ref[...], g_ref[...], sh_ref[0], sc_ref[0]).astype(BF16)
        acc_ref[...] = jnp.zeros_like(acc_ref)

    a = jnp.dot(h_ref[...], w1_ref[0], preferred_element_type=F32)
    a = jnp.square(jnp.maximum(a, 0.0)).astype(BF16)
    acc_ref[...] += jnp.dot(a, w2_ref[0], preferred_element_type=F32)

    @pl.when(f == pl.num_programs(1) - 1)
    def _():
        out = x_ref[...] + gate_ref[0] * acc_ref[...]
        if final:
            out = out * lax.rsqrt(jnp.mean(out * out, axis=-1, keepdims=True) + NORM_EPS) * fg_ref[...]
        o_ref[...] = out


def _mlp(x, g, mod_t, w1, w2, layer, final_g, final):
    m, d = x.shape
    ff = w1.shape[2]
    tm, tf = ROW_TILE, MLP_FF_TILE
    return pl.pallas_call(
        functools.partial(_mlp_kernel, final=final),
        grid=(m // tm, ff // tf),
        in_specs=[pl.BlockSpec((tm, d), lambda i, f: (i, 0)),
                  pl.BlockSpec((1, d), lambda i, f: (0, 0)),
                  pl.BlockSpec((1, 1, d), lambda i, f: (i, 0, 3)),
                  pl.BlockSpec((1, 1, d), lambda i, f: (i, 0, 4)),
                  pl.BlockSpec((1, 1, d), lambda i, f: (i, 0, 5)),
                  pl.BlockSpec((1, d, tf), lambda i, f: (layer, 0, f)),
                  pl.BlockSpec((1, tf, d), lambda i, f: (layer, f, 0)),
                  pl.BlockSpec((1, d), lambda i, f: (0, 0))],
        out_specs=pl.BlockSpec((tm, d), lambda i, f: (i, 0)),
        out_shape=jax.ShapeDtypeStruct((m, d), F32),
        scratch_shapes=[pltpu.VMEM((tm, d), BF16), pltpu.VMEM((tm, d), F32)],
        compiler_params=_params("parallel", "arbitrary"),
        name="adaln_mlp",
    )(x, g, mod_t, mod_t, mod_t, w1, w2, final_g)


SCAN_CHUNK = 256


def _scan_geometry(bsz, lc, seq):
    assert lc % SCAN_CHUNK == 0 and seq % SCAN_CHUNK == 0
    return lc // SCAN_CHUNK, seq // SCAN_CHUNK


def _seg_chunk(c, n_ctx_c, n_lat_c, reverse):
    if reverse:
        return c >= n_ctx_c, jnp.where(c < n_ctx_c, n_ctx_c - 1 - c, n_lat_c - 1 - (c - n_ctx_c))
    return c >= n_ctx_c, jnp.where(c < n_ctx_c, c, c - n_ctx_c)


def _row_block(b, c, bsz, n_ctx_c, n_lat_c, reverse):
    is_lat, j = _seg_chunk(c, n_ctx_c, n_lat_c, reverse)
    return jnp.where(is_lat, bsz * n_ctx_c + b * n_lat_c + j, b * n_ctx_c + j)


def _time_iotas(q):
    t = lax.broadcasted_iota(jnp.int32, (q, 1), 0).astype(F32)
    s = lax.broadcasted_iota(jnp.int32, (1, q), 1).astype(F32)
    return t, s


def _dot_t(a, b):
    return lax.dot_general(a, b, (((0,), (0,)), ((), ())), preferred_element_type=F32)


def _dot_nt(a, b):
    return lax.dot_general(a, b, (((1,), (1,)), ((), ())), preferred_element_type=F32)


def _ret_kernel(*refs, reverse, n_ctx_c, n_heads, k_scale):
    if reverse:
        ld_ref, q_ref, k_ref, v_ref, cos_ref, sin_ref, o_ref, s_ref = refs
    else:
        ld_ref, q_ref, k_ref, v_ref, cos_ref, sin_ref, yb_ref, g_ref, o_ref, s_ref = refs
    c, h = pl.program_id(1), pl.program_id(2)
    qn = SCAN_CHUNK

    @pl.when(c == 0)
    def _():
        s_ref[h] = jnp.zeros(s_ref.shape[1:], F32)

    lg = ld_ref[(n_heads if reverse else 0) + h]
    t, s = _time_iotas(qn)
    is_lat = c >= n_ctx_c
    cos = jnp.where(is_lat, cos_ref[...], 1.0)
    sin = jnp.where(is_lat, sin_ref[...], 0.0)

    def rope(x):
        half = x.shape[1] // 2
        swapped = jnp.concatenate([pltpu.roll(x[:, :half], half // 2, 1), pltpu.roll(x[:, half:], half // 2, 1)], axis=1)
        return x * cos + swapped * sin

    q = rope(q_ref[...]).astype(BF16)
    k = rope(k_ref[...]) * k_scale
    v = v_ref[...].astype(BF16)
    if reverse:
        dmat = jnp.where(s >= t, jnp.exp((s - t) * lg), 0.0)
        q_dec, k_dec = jnp.exp((qn - t) * lg), jnp.exp(t * lg)
    else:
        dmat = jnp.where(t >= s, jnp.exp((t - s) * lg), 0.0)
        q_dec, k_dec = jnp.exp((t + 1.0) * lg), jnp.exp((qn - 1.0 - t) * lg)
    attn = (_dot_nt(q, k.astype(BF16)) * dmat).astype(BF16)
    state = s_ref[h]
    y = jnp.dot(attn, v, preferred_element_type=F32)
    y = y + jnp.dot(q, state.astype(BF16), preferred_element_type=F32) * q_dec
    chunk_dec = jnp.exp(jnp.full((1, 1), qn, F32) * lg)
    s_ref[h] = state * chunk_dec + _dot_t((k * k_dec).astype(BF16), v)
    if reverse:
        o_ref[...] = y
    else:
        y = y + yb_ref[...]
        mu = jnp.mean(y, axis=-1, keepdims=True)
        yc = y - mu
        var = jnp.mean(yc * yc, axis=-1, keepdims=True)
        g = g_ref[...]
        o_ref[...] = (yc * lax.rsqrt(var + NORM_EPS) * (g * jax.nn.sigmoid(g))).astype(o_ref.dtype)


def _ret_mixer(u, log_decay, bsz, lc, seq, d_model):
    n_ctx_c, n_lat_c = _scan_geometry(bsz, lc, seq)
    n_steps = n_ctx_c + n_lat_c
    qk_dim, v_dim = d_model // RET_HEADS, 2 * d_model // RET_HEADS
    rows = u.shape[0]
    cos_r, sin_r, cos_c, sin_c = _rope_tables(seq // GRID_W, qk_dim // 2)
    cos_t = jnp.asarray(np.concatenate([cos_r, cos_r, cos_c, cos_c], axis=1))
    sin_t = jnp.asarray(np.concatenate([-sin_r, sin_r, -sin_c, sin_c], axis=1))
    ld = log_decay.reshape(-1).astype(F32)
    y_b = None
    for reverse in (True, False):
        rb = functools.partial(_row_block, bsz=bsz, n_ctx_c=n_ctx_c, n_lat_c=n_lat_c, reverse=reverse)

        def tab(b, c, h, reverse=reverse):
            is_lat, j = _seg_chunk(c, n_ctx_c, n_lat_c, reverse)
            return (jnp.where(is_lat, j, 0), 0)

        in_specs = [pl.BlockSpec(memory_space=pltpu.SMEM),
                    pl.BlockSpec((SCAN_CHUNK, qk_dim), lambda b, c, h, rb=rb: (rb(b, c), h)),
                    pl.BlockSpec((SCAN_CHUNK, qk_dim), lambda b, c, h, rb=rb: (rb(b, c), RET_HEADS + h)),
                    pl.BlockSpec((SCAN_CHUNK, v_dim), lambda b, c, h, rb=rb: (rb(b, c), RET_HEADS + h)),
                    pl.BlockSpec((SCAN_CHUNK, qk_dim), tab),
                    pl.BlockSpec((SCAN_CHUNK, qk_dim), tab)]
        args = [ld, u, u, u, cos_t, sin_t]
        if not reverse:
            in_specs += [pl.BlockSpec((SCAN_CHUNK, v_dim), lambda b, c, h, rb=rb: (rb(b, c), h)),
                         pl.BlockSpec((SCAN_CHUNK, v_dim), lambda b, c, h, rb=rb: (rb(b, c), 2 * RET_HEADS + h))]
            args += [y_b, u]
        out = pl.pallas_call(
            functools.partial(_ret_kernel, reverse=reverse, n_ctx_c=n_ctx_c, n_heads=RET_HEADS, k_scale=qk_dim ** -0.5),
            grid=(bsz, n_steps, RET_HEADS),
            in_specs=in_specs,
            out_specs=pl.BlockSpec((SCAN_CHUNK, v_dim), lambda b, c, h, rb=rb: (rb(b, c), h)),
            out_shape=jax.ShapeDtypeStruct((rows, RET_HEADS * v_dim), F32 if reverse else BF16),
            scratch_shapes=[pltpu.VMEM((RET_HEADS, qk_dim, v_dim), F32)],
            compiler_params=_params("parallel", "arbitrary", "arbitrary"),
            name="retention_bwd" if reverse else "retention_fwd",
        )(*args)
        if reverse:
            y_b = out
    return out


CONV_COLS = 1024
CONV_ROWS = 8


def _conv_kernel(x_ref, prev_ref, next_ref, w_ref, b_ref, o_ref, *, n_ctx_c, n_lat_c, bsz, n_scaled, n_normed, scale,
                 head_dim):
    rb, cb = pl.program_id(0), pl.program_id(1)
    is_lat = rb >= bsz * n_ctx_c
    j = jnp.where(is_lat, (rb - bsz * n_ctx_c) % n_lat_c, rb % n_ctx_c)
    last = jnp.where(is_lat, n_lat_c - 1, n_ctx_c - 1)
    rows, cols = x_ref.shape
    halo_prev = jnp.where(j > 0, prev_ref[SUBLANES - 1:SUBLANES, :], 0.0)
    halo_next = jnp.where(j < last, next_ref[0:1, :], 0.0)
    w0, w1, w2, bias = w_ref[0:1, :], w_ref[1:2, :], w_ref[2:3, :], b_ref[...]
    row = lax.broadcasted_iota(jnp.int32, (CONV_ROWS, 1), 0)

    def run(mult):
        before = halo_prev
        for r0 in range(0, rows, CONV_ROWS):
            x = x_ref[r0:r0 + CONV_ROWS, :]
            after = x_ref[r0 + CONV_ROWS:r0 + CONV_ROWS + 1, :] if r0 + CONV_ROWS < rows else halo_next
            x_prev = jnp.where(row == 0, before, pltpu.roll(x, 1, 0))
            x_next = jnp.where(row == CONV_ROWS - 1, after, pltpu.roll(x, CONV_ROWS - 1, 0))
            y = w0 * x_prev + w1 * x + w2 * x_next + bias
            y = y * jax.nn.sigmoid(y)
            if mult is not None:
                parts = []
                for i in range(cols // head_dim):
                    p = y[:, i * head_dim:(i + 1) * head_dim]
                    parts.append(p * (lax.rsqrt(jnp.sum(p * p, axis=-1, keepdims=True) + L2_EPS) * mult))
                y = jnp.concatenate(parts, axis=1)
            o_ref[r0:r0 + CONV_ROWS, :] = y
            before = x[CONV_ROWS - 1:CONV_ROWS, :]

    if n_normed == 0:
        run(None)
    else:
        @pl.when(cb < n_normed)
        def _():
            run(jnp.where(cb < n_scaled, scale, 1.0))

        @pl.when(cb >= n_normed)
        def _():
            run(None)


def _conv_silu(u, w, b, n_cols, bsz, lc, seq, n_scaled=0, n_normed=0, scale=1.0, head_dim=LANES, col0=0):
    n_ctx_c, n_lat_c = _scan_geometry(bsz, lc, seq)
    rows = u.shape[0]
    n_rb = rows // SCAN_CHUNK
    sub = SCAN_CHUNK // SUBLANES
    n_halo = rows // SUBLANES
    return pl.pallas_call(
        functools.partial(_conv_kernel, n_ctx_c=n_ctx_c, n_lat_c=n_lat_c, bsz=bsz, n_scaled=n_scaled,
                          n_normed=n_normed, scale=scale, head_dim=head_dim),
        grid=(n_rb, n_cols // CONV_COLS),
        in_specs=[pl.BlockSpec((SCAN_CHUNK, CONV_COLS), lambda r, c: (r, c + col0)),
                  pl.BlockSpec((SUBLANES, CONV_COLS), lambda r, c: (jnp.maximum(r * sub - 1, 0), c + col0)),
                  pl.BlockSpec((SUBLANES, CONV_COLS), lambda r, c: (jnp.minimum(r * sub + sub, n_halo - 1), c + col0)),
                  pl.BlockSpec((CONV_W, CONV_COLS), lambda r, c: (0, c)),
                  pl.BlockSpec((1, CONV_COLS), lambda r, c: (0, c))],
        out_specs=pl.BlockSpec((SCAN_CHUNK, CONV_COLS), lambda r, c: (r, c)),
        out_shape=jax.ShapeDtypeStruct((rows, n_cols), F32),
        compiler_params=_params("parallel", "parallel"),
        name="conv_silu",
    )(u, u, u, w, b)


GDN_SUB = 64


GDN_BASE = 8
GDN_HEADS_PER_STEP = 8


def _split3(x):
    x1 = x.astype(BF16)
    r1 = x - x1.astype(F32)
    x2 = r1.astype(BF16)
    return x1, x2, (r1 - x2.astype(F32)).astype(BF16)


def _sel_rows(m, x):
    mb = m.astype(BF16)
    return sum(jnp.dot(mb, p, preferred_element_type=F32) for p in _split3(x))


def _sel_cols(x, m):
    mb = m.astype(BF16)
    return sum(jnp.dot(p, mb, preferred_element_type=F32) for p in _split3(x))


def _sel_rows_t(x, m):
    mb = m.astype(BF16)
    return sum(lax.dot_general(p, mb, (((0,), (1,)), ((), ())), preferred_element_type=F32) for p in _split3(x))


def _mm_bf16(a, b):
    return jnp.dot(a.astype(BF16), b.astype(BF16), preferred_element_type=F32)


def _unit_tri_inverse_minus_eye(mats, ti, si):
    def same(n):
        s = int(math.log2(n))
        return lax.shift_right_logical(ti, s) == lax.shift_right_logical(si, s)

    base = same(GDN_BASE)
    ps = [jnp.where(base, a, 0.0) for a in mats]
    ns = [-p for p in ps]
    for _ in range(int(math.log2(GDN_BASE)) - 1):
        ps = [_mm_bf16(p, p) for p in ps]
        ns = [n + p + _mm_bf16(n, p) for n, p in zip(ns, ps)]
    b = GDN_BASE
    while b < GDN_SUB:
        join = same(2 * b) & jnp.logical_not(same(b))
        cs = [jnp.where(join, a, 0.0) for a in mats]
        ms = [c + _mm_bf16(c, n) for c, n in zip(cs, ns)]
        ns = [n - (m + _mm_bf16(n, m)) for n, m in zip(ns, ms)]
        b *= 2
    return ns


def _softplus(x):
    return jnp.maximum(x, 0.0) + jnp.log(1.0 + jnp.exp(-jnp.abs(x)))


def _gdn_kernel(*refs, reverse, n_kh, n_r, hd):
    if reverse:
        q_ref, k_ref, v_ref, tail_ref, prow_ref, arow_ref, o_ref, s_ref = refs
    else:
        q_ref, k_ref, v_ref, tail_ref, prow_ref, arow_ref, yb_ref, z_ref, ng_ref, o_ref, s_ref = refs
    g, c = pl.program_id(1), pl.program_id(2)
    qn = SCAN_CHUNK
    n_sub = qn // GDN_SUB
    n_gate = tail_ref.shape[1] // 2

    @pl.when(c == 0)
    def _():
        s_ref[...] = jnp.zeros_like(s_ref)

    tail = tail_ref[...]
    lane = lax.broadcasted_iota(jnp.int32, (1, 2 * n_gate), 1)
    gates = jnp.where(lane < n_gate, jax.nn.sigmoid(tail), -jnp.exp(arow_ref[...]) * _softplus(tail + prow_ref[...]))
    nh = n_kh * n_r
    li = lax.broadcasted_iota(jnp.int32, (2 * n_gate, 2 * nh), 0)
    ji = lax.broadcasted_iota(jnp.int32, (2 * n_gate, 2 * nh), 1)
    col = jnp.where(ji < nh, ji, n_gate + ji - nh) + (n_gate // 2 if reverse else 0) + g * nh
    gsel = _sel_cols(gates, (li == col).astype(F32))
    lw = lax.broadcasted_iota(jnp.int32, (2 * n_gate, 2 * nh * hd), 0)
    jw = lax.shift_right_logical(lax.broadcasted_iota(jnp.int32, (2 * n_gate, 2 * nh * hd), 1), int(math.log2(hd)))
    colw = jnp.where(jw < nh, jw, n_gate + jw - nh) + (n_gate // 2 if reverse else 0) + g * nh
    gwide = _sel_cols(gates, (lw == colw).astype(F32))

    ti = lax.broadcasted_iota(jnp.int32, (qn, 1), 0)
    si = lax.broadcasted_iota(jnp.int32, (1, qn), 1)
    shift = int(math.log2(GDN_SUB))
    same = lax.shift_right_logical(ti, shift) == lax.shift_right_logical(si, shift)
    if reverse:
        incl, strict = same & (si >= ti), same & (si > ti)
    else:
        incl, strict = same & (si <= ti), same & (si < ti)
    cs = incl.astype(F32)
    cum_w = _sel_rows(cs, gwide[:, nh * hd:])
    cum_t = _sel_rows_t(gsel, cs)
    reps = qn // hd

    order = range(n_sub - 1, -1, -1) if reverse else range(n_sub)
    shared = []
    for kh in range(n_kh):
        k = k_ref[:, kh * hd:(kh + 1) * hd]
        qb, kb = q_ref[:, kh * hd:(kh + 1) * hd].astype(BF16), k.astype(BF16)
        shared.append((k, qb, _dot_nt(kb, kb), _dot_nt(qb, kb)))
    heads = range(nh)
    beta = [gwide[:, j * hd:(j + 1) * hd] for j in heads]
    cum_c = [cum_w[:, j * hd:(j + 1) * hd] for j in heads]
    decay = [jnp.exp(jnp.where(incl, jnp.concatenate([cum_c[j]] * reps, axis=1) - cum_t[nh + j:nh + j + 1, :],
                               -jnp.inf)) for j in heads]
    amat = [jnp.where(strict, jnp.concatenate([beta[j]] * reps, axis=1) * shared[j // n_r][2] * decay[j], 0.0)
            for j in heads]
    inv_off = _unit_tri_inverse_minus_eye(amat, ti, si)
    rhs = [jnp.concatenate([v_ref[:, j * hd:(j + 1) * hd] * beta[j],
                            shared[j // n_r][0] * (beta[j] * jnp.exp(cum_c[j]))], axis=1) for j in heads]
    sol = [rhs[j] + _mm_bf16(inv_off[j], rhs[j]) for j in heads]
    state = [s_ref[j] for j in heads]
    v_new = [[None] * n_sub for _ in heads]
    inter = [[None] * n_sub for _ in heads]
    for i in order:
        sl = slice(i * GDN_SUB, (i + 1) * GDN_SUB)
        end = i * GDN_SUB if reverse else (i + 1) * GDN_SUB - 1
        sb = [state[j].astype(BF16) for j in heads]
        for j in heads:
            v_new[j][i] = sol[j][sl, :hd] - jnp.dot(sol[j][sl, hd:].astype(BF16), sb[j], preferred_element_type=F32)
            inter[j][i] = jnp.dot(shared[j // n_r][1][sl], sb[j], preferred_element_type=F32)
        for j in heads:
            cum_end = cum_c[j][end:end + 1, :]
            k_end = (shared[j // n_r][0][sl] * jnp.exp(cum_end - cum_c[j][sl])).astype(BF16)
            state[j] = jnp.exp(cum_end) * state[j] + _dot_t(k_end, v_new[j][i].astype(BF16))
    outs = []
    for j in heads:
        s_ref[j] = state[j]
        attn = (shared[j // n_r][3] * decay[j]).astype(BF16)
        y = jnp.dot(attn, jnp.concatenate(v_new[j], axis=0).astype(BF16), preferred_element_type=F32)
        y = y + jnp.concatenate(inter[j], axis=0) * jnp.exp(cum_c[j])
        if not reverse:
            y = y + yb_ref[:, j * hd:(j + 1) * hd]
            y = y * lax.rsqrt(jnp.mean(y * y, axis=-1, keepdims=True) + NORM_EPS) * ng_ref[...]
            z = z_ref[:, j * hd:(j + 1) * hd]
            y = y * (z * jax.nn.sigmoid(z))
        outs.append(y)
    o_ref[...] = jnp.concatenate(outs, axis=1).astype(o_ref.dtype)


def _gdn_mixer(u, tail, conv_w, dt_bias, a_log, norm_g, bsz, lc, seq, d_model):
    n_ctx_c, n_lat_c = _scan_geometry(bsz, lc, seq)
    n_steps = n_ctx_c + n_lat_c
    hd = GDN_HEAD_DIM
    k_heads = d_model // hd
    n_r = 2
    rows = u.shape[0]
    conv_ch = 4 * d_model
    nq = d_model // CONV_COLS
    qkv = _conv_silu(u, conv_w, jnp.zeros((1, conv_ch), F32), conv_ch, bsz, lc, seq, n_scaled=nq, n_normed=2 * nq,
                     scale=hd ** -0.5, head_dim=hd)
    n_gate = tail.shape[1] // 2
    prow = jnp.concatenate([jnp.zeros((1, n_gate), F32), dt_bias.reshape(1, n_gate)], axis=1)
    arow = jnp.concatenate([jnp.zeros((1, n_gate), F32), a_log.reshape(1, n_gate)], axis=1)
    y_b = None
    for reverse in (True, False):
        rb = functools.partial(_row_block, bsz=bsz, n_ctx_c=n_ctx_c, n_lat_c=n_lat_c, reverse=reverse)
        const = lambda b, g, c: (0, 0)
        qk_w, v_w = GDN_HEADS_PER_STEP * hd, GDN_HEADS_PER_STEP * n_r * hd
        n_groups = k_heads // GDN_HEADS_PER_STEP
        in_specs = [pl.BlockSpec((SCAN_CHUNK, qk_w), lambda b, g, c, rb=rb: (rb(b, c), g)),
                    pl.BlockSpec((SCAN_CHUNK, qk_w), lambda b, g, c, rb=rb: (rb(b, c), n_groups + g)),
                    pl.BlockSpec((SCAN_CHUNK, v_w), lambda b, g, c, rb=rb: (rb(b, c), n_groups + g)),
                    pl.BlockSpec((SCAN_CHUNK, 2 * n_gate), lambda b, g, c, rb=rb: (rb(b, c), 0)),
                    pl.BlockSpec((1, 2 * n_gate), const),
                    pl.BlockSpec((1, 2 * n_gate), const)]
        args = [qkv, qkv, qkv, tail, prow, arow]
        if not reverse:
            in_specs += [pl.BlockSpec((SCAN_CHUNK, v_w), lambda b, g, c, rb=rb: (rb(b, c), g)),
                         pl.BlockSpec((SCAN_CHUNK, v_w), lambda b, g, c, rb=rb: (rb(b, c), 2 * n_groups + g)),
                         pl.BlockSpec((1, hd), const)]
            args += [y_b, u, norm_g.reshape(1, hd)]
        out = pl.pallas_call(
            functools.partial(_gdn_kernel, reverse=reverse, n_kh=GDN_HEADS_PER_STEP, n_r=n_r, hd=hd),
            grid=(bsz, n_groups, n_steps),
            in_specs=in_specs,
            out_specs=pl.BlockSpec((SCAN_CHUNK, v_w), lambda b, g, c, rb=rb: (rb(b, c), g)),
            out_shape=jax.ShapeDtypeStruct((rows, k_heads * n_r * hd), F32 if reverse else BF16),
            scratch_shapes=[pltpu.VMEM((GDN_HEADS_PER_STEP * n_r, hd, hd), F32)],
            compiler_params=_params("parallel", "parallel", "arbitrary"),
            name="gdn_bwd" if reverse else "gdn_fwd",
        )(*args)
        if reverse:
            y_b = out
    return out


def _ssd_kernel(*refs, reverse, n_r, hd):
    if reverse:
        x_ref, b_ref, c_ref, tail_ref, dtb_ref, alog_ref, o_ref, s_ref = refs
    else:
        x_ref, b_ref, c_ref, tail_ref, dtb_ref, alog_ref, yb_ref, z_ref, d_ref, ng_ref, o_ref, s_ref = refs
    g, c = pl.program_id(1), pl.program_id(2)
    qn = SCAN_CHUNK
    n_lane = tail_ref.shape[1]
    width = n_r * hd

    @pl.when(c == 0)
    def _():
        s_ref[...] = jnp.zeros_like(s_ref)

    dt_all = _softplus(tail_ref[...] + dtb_ref[...])
    la_all = -jnp.exp(alog_ref[...]) * dt_all
    li = lax.broadcasted_iota(jnp.int32, (n_lane, n_r), 0)
    ji = lax.broadcasted_iota(jnp.int32, (n_lane, n_r), 1)
    sel = (li == ji + (n_lane // 2 if reverse else 0) + g * n_r).astype(F32)
    dt, la = _sel_cols(dt_all, sel), _sel_cols(la_all, sel)

    ti = lax.broadcasted_iota(jnp.int32, (qn, 1), 0)
    si = lax.broadcasted_iota(jnp.int32, (1, qn), 1)
    before = (si >= ti) if reverse else (si <= ti)
    cs = before.astype(F32)
    cum = _sel_rows(cs, la)
    cum_t = _sel_rows_t(la, cs)
    ei = lax.broadcasted_iota(jnp.int32, (n_r, width), 0)
    el = lax.broadcasted_iota(jnp.int32, (n_r, width), 1)
    expand = (lax.shift_right_logical(el, int(math.log2(hd))) == ei).astype(F32)
    dt_x, cum_x = _sel_cols(dt, expand), _sel_cols(cum, expand)

    xs = x_ref[...]
    v = xs * dt_x
    bm, cm = b_ref[...].astype(BF16), c_ref[...].astype(BF16)
    scores = _dot_nt(cm, bm)
    lane = lax.broadcasted_iota(jnp.int32, (1, 2 * hd), 1)
    tiles = []
    for p in range(n_r // 2):
        vt = v[:, 2 * p * hd:2 * (p + 1) * hd]
        acc = None
        for h in (2 * p, 2 * p + 1):
            decay = jnp.exp(jnp.where(before, cum[:, h:h + 1] - cum_t[h:h + 1, :], -jnp.inf))
            vh = jnp.where((lane >= hd) if h % 2 else (lane < hd), vt, 0.0).astype(BF16)
            part = jnp.dot((scores * decay).astype(BF16), vh, preferred_element_type=F32)
            acc = part if acc is None else acc + part
        tiles.append(acc)
    state = s_ref[...]
    y = jnp.concatenate(tiles, axis=1) + jnp.dot(cm, state.astype(BF16), preferred_element_type=F32) * jnp.exp(cum_x)
    end = 0 if reverse else qn - 1
    cum_end = cum_x[end:end + 1]
    s_ref[...] = state * jnp.exp(cum_end) + _dot_t(bm, (v * jnp.exp(cum_end - cum_x)).astype(BF16))
    if reverse:
        o_ref[...] = y
    else:
        y = y + yb_ref[...] + d_ref[...] * xs
        z = z_ref[...]
        y = y * (z * jax.nn.sigmoid(z))
        y = y * lax.rsqrt(jnp.mean(y * y, axis=-1, keepdims=True) + NORM_EPS) * ng_ref[...]
        o_ref[...] = y.astype(o_ref.dtype)


def _ssd_mixer(u, tail, conv_w, conv_b, dt_bias, a_log, d_skip, norm_g, bsz, lc, seq):
    n_ctx_c, n_lat_c = _scan_geometry(bsz, lc, seq)
    n_steps = n_ctx_c + n_lat_c
    d_inner = norm_g.shape[0]
    hd, st = SSD_HEAD_DIM, SSD_STATE
    heads = d_inner // hd
    n_r = heads // SSD_GROUPS
    width = n_r * hd
    rows = u.shape[0]
    conv_ch = d_inner + 2 * SSD_GROUPS * st
    xbc = _conv_silu(u, conv_w, conv_b.reshape(1, conv_ch), conv_ch, bsz, lc, seq, col0=d_inner // CONV_COLS)
    d_x = jnp.repeat(d_skip, hd).reshape(1, d_inner)
    y_b = None
    for reverse in (True, False):
        rb = functools.partial(_row_block, bsz=bsz, n_ctx_c=n_ctx_c, n_lat_c=n_lat_c, reverse=reverse)
        const = lambda b, g, c: (0, 0)
        grp = lambda b, g, c: (0, g)
        wide = lambda b, g, c, rb=rb: (rb(b, c), g)
        in_specs = [pl.BlockSpec((SCAN_CHUNK, width), wide),
                    pl.BlockSpec((SCAN_CHUNK, st), lambda b, g, c, rb=rb: (rb(b, c), d_inner // st + g)),
                    pl.BlockSpec((SCAN_CHUNK, st), lambda b, g, c, rb=rb: (rb(b, c), d_inner // st + SSD_GROUPS + g)),
                    pl.BlockSpec((SCAN_CHUNK, 2 * heads), lambda b, g, c, rb=rb: (rb(b, c), 0)),
                    pl.BlockSpec((1, 2 * heads), const),
                    pl.BlockSpec((1, 2 * heads), const)]
        args = [xbc, xbc, xbc, tail, dt_bias.reshape(1, 2 * heads), a_log.reshape(1, 2 * heads)]
        if not reverse:
            in_specs += [pl.BlockSpec((SCAN_CHUNK, width), wide), pl.BlockSpec((SCAN_CHUNK, width), wide),
                         pl.BlockSpec((1, width), grp), pl.BlockSpec((1, width), grp)]
            args += [y_b, u, d_x, norm_g.reshape(1, d_inner)]
        out = pl.pallas_call(
            functools.partial(_ssd_kernel, reverse=reverse, n_r=n_r, hd=hd),
            grid=(bsz, SSD_GROUPS, n_steps),
            in_specs=in_specs,
            out_specs=pl.BlockSpec((SCAN_CHUNK, width), wide),
            out_shape=jax.ShapeDtypeStruct((rows, d_inner), F32 if reverse else BF16),
            scratch_shapes=[pltpu.VMEM((st, width), F32)],
            compiler_params=_params("parallel", "parallel", "arbitrary"),
            name="ssd_bwd" if reverse else "ssd_fwd",
        )(*args)
        if reverse:
            y_b = out
    return out


HG_SUB = 64
HG_BLK = 8


def _log1p(x):
    return jnp.log(1.0 + x)


def _hgrn_kernel(*refs, reverse):
    if reverse:
        q_ref, f_ref, i_ref, lb_ref, o_ref, st_ref = refs
    else:
        q_ref, f_ref, i_ref, lb_ref, yb_ref, g_ref, ng_ref, o_ref, st_ref = refs
    c = pl.program_id(2)
    qn = SCAN_CHUNK
    hd = q_ref.shape[1]
    n_sub, n_blk = qn // HG_SUB, HG_SUB // HG_BLK

    @pl.when(c == 0)
    def _():
        st_ref[...] = jnp.zeros_like(st_ref)

    q, f, v, lb = q_ref[...], f_ref[...], i_ref[...], lb_ref[...]
    log_sig = jnp.minimum(f, 0.0) - _log1p(jnp.exp(-jnp.abs(f)))
    ga, gb = jnp.log(lb), _log1p(-lb) + log_sig
    log_f = jnp.maximum(ga, gb) + _log1p(jnp.exp(-jnp.abs(ga - gb)))
    k = (1.0 - lb) * jax.nn.sigmoid(-f)

    ti = lax.broadcasted_iota(jnp.int32, (qn, 1), 0)
    si = lax.broadcasted_iota(jnp.int32, (1, qn), 1)
    before = (si >= ti) if reverse else (si <= ti)
    same = lambda n: lax.shift_right_logical(ti, int(math.log2(n))) == lax.shift_right_logical(si, int(math.log2(n)))
    cum_sub = _sel_rows((same(HG_SUB) & before).astype(F32), log_f)
    cum_blk = _sel_rows((same(HG_BLK) & before).astype(F32), log_f)
    q_blk = q * jnp.exp(cum_blk)
    q_sub = (q * jnp.exp(cum_sub)).astype(BF16)

    ones = jnp.ones((hd, hd), BF16)
    lane = lax.broadcasted_iota(jnp.int32, (1, hd), 1)
    row_sub = lax.broadcasted_iota(jnp.int32, (HG_SUB, 1), 0)
    row_blk = lax.broadcasted_iota(jnp.int32, (HG_BLK, 1), 0)
    zeros_sub = jnp.zeros((hd - HG_SUB, hd), F32)
    subs = range(n_sub)
    blocks = [(i, a) for i in subs for a in range(n_blk)]
    cum_end = [cum_sub[(i * HG_SUB if reverse else (i + 1) * HG_SUB - 1):][:1] for i in subs]
    chunk_dec = [jnp.exp(cum_end[i]) for i in subs]
    kvt = [_dot_t(v[i * HG_SUB:(i + 1) * HG_SUB].astype(BF16),
                  (k[i * HG_SUB:(i + 1) * HG_SUB] * jnp.exp(cum_end[i] - cum_sub[i * HG_SUB:(i + 1) * HG_SUB])).astype(BF16))
           for i in subs]
    tiles = []
    for i, a in blocks:
        b0 = i * HG_SUB + a * HG_BLK
        qb, kb, cb = q[b0:b0 + HG_BLK], k[b0:b0 + HG_BLK], cum_blk[b0:b0 + HG_BLK]
        for s in range(HG_BLK):
            ok = (row_blk <= s) if reverse else (row_blk >= s)
            tiles.append(qb * jnp.exp(jnp.where(ok, cb - cb[s:s + 1], -jnp.inf)) * kb[s:s + 1])
    sums = jnp.dot(jnp.concatenate(tiles, axis=0).astype(BF16), ones, preferred_element_type=F32)
    acc = []
    for n, (i, a) in enumerate(blocks):
        base = n * HG_BLK * HG_BLK
        x = jnp.zeros((HG_BLK, hd), F32)
        for s in range(HG_BLK):
            x = x + jnp.where(lane == a * HG_BLK + s, sums[base + s * HG_BLK:base + (s + 1) * HG_BLK], 0.0)
        acc.append(x)
    kts = {}
    for n, (i, a) in enumerate(blocks):
        if (a < n_blk - 1) if reverse else (a > 0):
            r0 = i * HG_SUB
            b0 = r0 + a * HG_BLK
            ref_row = b0 + HG_BLK if reverse else b0 - 1
            earlier = (row_sub >= (a + 1) * HG_BLK) if reverse else (row_sub < a * HG_BLK)
            kt = k[r0:r0 + HG_SUB] * jnp.exp(jnp.where(earlier, cum_sub[ref_row:ref_row + 1] - cum_sub[r0:r0 + HG_SUB],
                                                         -jnp.inf))
            kts[n] = jnp.concatenate([kt, zeros_sub], axis=0).astype(BF16)
    for n, (i, a) in enumerate(blocks):
        if n in kts:
            b0 = i * HG_SUB + a * HG_BLK
            acc[n] = acc[n] + _dot_nt(q_blk[b0:b0 + HG_BLK].astype(BF16), kts[n])
    y_intra = []
    for i in subs:
        attn = jnp.concatenate(acc[i * n_blk:(i + 1) * n_blk], axis=0).astype(BF16)
        v_pad = jnp.concatenate([v[i * HG_SUB:(i + 1) * HG_SUB], zeros_sub], axis=0).astype(BF16)
        y_intra.append(jnp.dot(attn, v_pad, preferred_element_type=F32))

    state = st_ref[...]
    ys = [None] * n_sub
    for i in (range(n_sub - 1, -1, -1) if reverse else range(n_sub)):
        ys[i] = y_intra[i] + _dot_nt(q_sub[i * HG_SUB:(i + 1) * HG_SUB], state.astype(BF16))
        state = state * chunk_dec[i] + kvt[i]
    st_ref[...] = state
    y = jnp.concatenate(ys, axis=0)
    if reverse:
        o_ref[...] = y
    else:
        y = y + yb_ref[...]
        y = y * lax.rsqrt(jnp.mean(y * y, axis=-1, keepdims=True) + NORM_EPS) * ng_ref[...]
        g = g_ref[...]
        o_ref[...] = (y * (g * jax.nn.sigmoid(g))).astype(o_ref.dtype)


def _hgrn_mixer(u, lb, norm_g, bsz, lc, seq):
    n_ctx_c, n_lat_c = _scan_geometry(bsz, lc, seq)
    n_steps = n_ctx_c + n_lat_c
    hd = HGRN_EXPAND
    d_model = lb.shape[0]
    heads = d_model // hd
    rows = u.shape[0]
    lb2, ng2 = lb.reshape(1, d_model), norm_g.reshape(1, d_model)
    y_b = None
    for reverse in (True, False):
        rb = functools.partial(_row_block, bsz=bsz, n_ctx_c=n_ctx_c, n_lat_c=n_lat_c, reverse=reverse)
        col = lambda seg: (lambda b, h, c, rb=rb: (rb(b, c), seg * heads + h))
        par = lambda b, h, c: (0, h)
        in_specs = [pl.BlockSpec((SCAN_CHUNK, hd), col(0)),
                    pl.BlockSpec((SCAN_CHUNK, hd), col(2 if reverse else 1)),
                    pl.BlockSpec((SCAN_CHUNK, hd), col(3)),
                    pl.BlockSpec((1, hd), par)]
        args = [u, u, u, lb2]
        if not reverse:
            in_specs += [pl.BlockSpec((SCAN_CHUNK, hd), col(0)), pl.BlockSpec((SCAN_CHUNK, hd), col(4)),
                         pl.BlockSpec((1, hd), par)]
            args += [y_b, u, ng2]
        out = pl.pallas_call(
            functools.partial(_hgrn_kernel, reverse=reverse),
            grid=(bsz, heads, n_steps),
            in_specs=in_specs,
            out_specs=pl.BlockSpec((SCAN_CHUNK, hd), col(0)),
            out_shape=jax.ShapeDtypeStruct((rows, d_model), F32 if reverse else BF16),
            scratch_shapes=[pltpu.VMEM((hd, hd), F32)],
            compiler_params=_params("parallel", "parallel", "arbitrary"),
            name="hgrn_bwd" if reverse else "hgrn_fwd",
        )(*args)
        if reverse:
            y_b = out
    return out


def _rope_tables(rows, half):
    pos = np.arange(rows * GRID_W)
    inv_freq = np.float32(ROPE_BASE) ** (-(np.arange(0, half, 2, dtype=np.float32) / np.float32(half)))
    out = []
    for p in ((pos // GRID_W).astype(np.float32), (pos % GRID_W).astype(np.float32)):
        ang = (p[:, None] * inv_freq.astype(np.float32)).astype(np.float32).astype(np.float64)
        out += [np.cos(ang).astype(np.float32), np.sin(ang).astype(np.float32)]
    return out


def _lower_bound(lb_logits, layer):
    p = jax.nn.softmax(lb_logits.astype(F32), axis=0)
    return jnp.cumsum(p, axis=0)[layer] - p[0]


def kernel(x, c, ctx, c_ctx, ada_w, ada_b, norm_g, mlp_w1, mlp_w2, final_g, ssd_w_in, ssd_conv_w, ssd_conv_b,
           ssd_dt_bias, ssd_a_log, ssd_d, ssd_norm_g, ssd_w_out, ret_w_in, ret_log_decay, ret_w_out, hgrn_w_in,
           hgrn_lb_logits, hgrn_norm_g, hgrn_w_out, gdn_w_in, gdn_conv_w, gdn_dt_bias, gdn_a_log, gdn_norm_g,
           gdn_w_out):
    bsz, seq, d = x.shape
    lc = ctx.shape[1]
    depth = ada_w.shape[0]
    n_ctx = bsz * lc
    assert n_ctx % IN_ROW_TILE == 0 and seq % IN_ROW_TILE == 0 and IN_ROW_TILE % ROW_TILE == 0
    assert bsz + 1 <= SUBLANES

    cond_pad = jnp.concatenate([c, c_ctx[None], jnp.zeros((SUBLANES - bsz - 1, d), F32)], axis=0)
    mod = _ada_mod(cond_pad, ada_w, ada_b)

    def tile_rows(tile):
        return jnp.asarray([bsz] * (n_ctx // tile) + [b for b in range(bsz) for _ in range(seq // tile)], jnp.int32)

    tile_row, in_tile_row = tile_rows(ROW_TILE), tile_rows(IN_ROW_TILE)
    n_ctx_tiles = n_ctx // ROW_TILE

    xr = jnp.concatenate([ctx.reshape(n_ctx, d), x.reshape(bsz * seq, d)], axis=0)
    w1_all, w2_all = mlp_w1.astype(BF16), mlp_w2.astype(BF16)

    for i in range(depth):
        mixer, occ = i % 4, i // 4
        keep_ctx = i < depth - 1
        mod_t = mod[i][tile_row][:, None, :]
        g0, g1 = norm_g[i, 0][None], norm_g[i, 1][None]
        if mixer == 0:
            w_in, w_out = ssd_w_in[occ], ssd_w_out[occ]
        elif mixer == 1:
            w_in, w_out = ret_w_in[occ], ret_w_out[occ]
        elif mixer == 2:
            w_in, w_out = hgrn_w_in[occ], hgrn_w_out[occ]
        else:
            w_in, w_out = gdn_w_in[occ], gdn_w_out[occ]
        n_in = w_in.shape[1]
        n_main = (n_in // 1024) * 1024 if n_in % 1024 else n_in
        w_in = w_in.astype(BF16)
        mod_in = mod[i][in_tile_row][:, None, :]
        u = _ln_mm(xr, g0, mod_in, 0, 1, w_in, 0, n_main)
        tail = _ln_mm(xr, g0, mod_in, 0, 1, w_in, n_main, n_in - n_main) if n_main != n_in else None
        if mixer == 0:
            yr = _ssd_mixer(u, tail, ssd_conv_w[occ], ssd_conv_b[occ], ssd_dt_bias[occ], ssd_a_log[occ], ssd_d[occ],
                            ssd_norm_g[occ], bsz, lc, seq)
        elif mixer == 1:
            yr = _ret_mixer(u, ret_log_decay[occ], bsz, lc, seq, d)
        elif mixer == 2:
            yr = _hgrn_mixer(u, _lower_bound(hgrn_lb_logits, i), hgrn_norm_g[occ], bsz, lc, seq)
        else:
            yr = _gdn_mixer(u, tail, gdn_conv_w[occ], gdn_dt_bias[occ], gdn_a_log[occ], gdn_norm_g[occ], bsz, lc,
                            seq, d)
        first_tile = 0 if keep_ctx else n_ctx_tiles
        xr = _out_proj(yr, w_out.astype(BF16), xr, mod_t, 2, first_tile)
        mod_t = mod_t[first_tile:]
        xr = _mlp(xr, g1, mod_t, w1_all, w2_all, i, final_g[None], final=not keep_ctx)
    return xr.reshape(bsz, seq, d)
```
